```python
import math
import jax
import jax.numpy as jnp
from jax import lax
import numpy as np

D_MODEL = 1024
BATCH = 1
SEQ = 16384
DEPTH = 2
DEC_BATCH = 8
DEC_SEQ = 16
PAST_LEN = 1024

CHUNK = 64
N_META = 16
QUERY_BLOCK = 128
SB_DIM = 64
SB_HEADS = (D_MODEL // 2) // SB_DIM
SB_W = SB_HEADS * SB_DIM
GDN_DK = 128
GDN_DV = 128
GDN_HEADS = (D_MODEL // 2) // GDN_DV
GDN_QK = GDN_HEADS * GDN_DK
GDN_V = GDN_HEADS * GDN_DV
GDN_CONV = 4
GDN_CONV_DIM = 2 * GDN_QK + GDN_V
EVEN_IN = 3 * SB_W + GDN_CONV_DIM + GDN_V + 2 * GDN_HEADS
EVEN_MIX = SB_W + GDN_V
FOX_DIM = 64
FOX_HEADS = D_MODEL // FOX_DIM
FOX_W = FOX_HEADS * FOX_DIM
ODD_IN = 3 * FOX_W + FOX_HEADS
PEER_HEADS = 8
PEER_NKEYS = 128
PEER_TOPK = 16
PEER_QDIM = 256
N_EXPERTS = PEER_NKEYS ** 2
PEER_BLOCK = 256
N_EVEN = (DEPTH + 1) // 2
N_ODD = DEPTH // 2
RMS_EPS = 1e-6
F32 = jnp.float32

kernel_name = 'streaming_sb_gdn_fox_peer'


def rms_norm(x, g):
    xf = x.astype(F32)
    y = xf * lax.rsqrt(jnp.mean(xf * xf, axis=-1, keepdims=True) + RMS_EPS)
    return (y * g.astype(F32)).astype(x.dtype)


def l2_normalize(x):
    xf = x.astype(F32)
    return xf * lax.rsqrt(jnp.sum(xf * xf, axis=-1, keepdims=True) + RMS_EPS)


def _sweep(block_fn, q_parts, q_offset):
    tq = q_parts[0].shape[1]
    blk = min(QUERY_BLOCK, tq)
    nb = -(-tq // blk)
    pad = nb * blk - tq

    def split(a):
        a = jnp.pad(a, [(0, 0), (0, pad)] + [(0, 0)] * (a.ndim - 2))
        return jnp.moveaxis(a.reshape(a.shape[0], nb, blk, *a.shape[2:]), 1, 0)

    pos = (q_offset + jnp.arange(nb * blk, dtype=jnp.int32)).reshape(nb, blk)
    out = lax.map(lambda args: block_fn(args[0], args[1]), (tuple(split(a) for a in q_parts), pos))
    out = jnp.moveaxis(out, 0, 1)
    return out.reshape(out.shape[0], nb * blk, *out.shape[3:])[:, :tq]


def _sb_block(q, k, v, q_pos, k_pos):
    z = jnp.einsum('bqhd,bkhd->bhqk', q, k, preferred_element_type=F32) * (SB_DIM ** -0.5)
    mask = k_pos[None, :] < q_pos[:, None]
    log_1mb = jnp.where(mask, jax.nn.log_sigmoid(-z), 0.0)
    later = lax.cumsum(log_1mb, axis=3, reverse=True) - log_1mb
    w = jnp.where(mask, jnp.exp(jax.nn.log_sigmoid(z) + later), 0.0)
    return jnp.einsum('bhqk,bkhd->bqhd', w.astype(v.dtype), v)


def _fox_block(q, fq, k, v, f_key, q_pos, k_pos):
    s = jnp.einsum('bqhd,bkhd->bhqk', q, k, preferred_element_type=F32) * (FOX_DIM ** -0.5)
    s = s + jnp.transpose(fq, (0, 2, 1))[..., :, None] - f_key[:, :, None, :]
    s = jnp.where(k_pos[None, :] <= q_pos[:, None], s, -jnp.inf)
    p = jax.nn.softmax(s, axis=-1)
    return jnp.einsum('bhqk,bkhd->bqhd', p.astype(v.dtype), v)


def gated_delta_rule(q, k, v, g, beta, s0):
    b_, t_ = q.shape[:2]
    n = -(-t_ // CHUNK)
    pad = n * CHUNK - t_

    def split(a):
        a = jnp.pad(a.astype(F32), [(0, 0), (0, pad)] + [(0, 0)] * (a.ndim - 2))
        a = a.reshape(b_, n, CHUNK, *a.shape[2:])
        return jnp.transpose(a, (1, 0, 3, 2) + tuple(range(4, a.ndim)))

    tri = jnp.tril(jnp.ones((CHUNK, CHUNK), dtype=bool))
    strict = jnp.tril(jnp.ones((CHUNK, CHUNK), dtype=bool), -1)
    eye = jnp.eye(CHUNK, dtype=F32)

    def step(s, blk):
        qc, kc, vc, gc, bc = blk
        gcum = jnp.cumsum(gc, axis=-1)
        decay = jnp.exp(jnp.where(tri, gcum[..., :, None] - gcum[..., None, :], -jnp.inf))
        kb = kc * bc[..., None]
        m = jnp.where(strict, jnp.einsum('bhik,bhjk->bhij', kb, kc) * decay, 0.0)
        tinv = lax.linalg.triangular_solve(eye + m, jnp.broadcast_to(eye, m.shape),
                                           left_side=True, lower=True, unit_diagonal=True)
        eg = jnp.exp(gcum)[..., None]
        u = tinv @ (vc * bc[..., None])
        w = tinv @ (kb * eg)
        v_new = u - w @ s
        attn = jnp.einsum('bhik,bhjk->bhij', qc, kc) * decay
        o = (qc * eg) @ s + attn @ v_new
        g_last = gcum[..., -1]
        s = s * jnp.exp(g_last)[..., None, None] + jnp.einsum(
            'bhck,bhcv->bhkv', kc * jnp.exp(g_last[..., None] - gcum)[..., None], v_new)
        return s, o

    s_fin, o = lax.scan(step, s0.astype(F32), tuple(split(a) for a in (q, k, v, g, beta)))
    o = jnp.transpose(o, (1, 0, 3, 2, 4)).reshape(b_, n * CHUNK, q.shape[2], v.shape[-1])[:, :t_]
    return o, s_fin


def gdn_heads(qkv_pre, z, a, b, conv_buf, s0, conv_w, a_log, dt_bias, gnorm):
    bsz, t_ = qkv_pre.shape[:2]
    xp = jnp.concatenate([conv_buf.astype(qkv_pre.dtype), qkv_pre], axis=1)
    conv = sum(xp[:, i:i + t_] * conv_w[i] for i in range(GDN_CONV))
    act = jax.nn.silu(conv)
    q = l2_normalize(act[..., :GDN_QK].reshape(bsz, t_, GDN_HEADS, GDN_DK)) * (GDN_DK ** -0.5)
    k = l2_normalize(act[..., GDN_QK:2 * GDN_QK].reshape(bsz, t_, GDN_HEADS, GDN_DK))
    v = act[..., 2 * GDN_QK:].reshape(bsz, t_, GDN_HEADS, GDN_DV)
    beta = jax.nn.sigmoid(b.astype(F32))
    g = -jnp.exp(a_log.astype(F32)) * jax.nn.softplus(a.astype(F32) + dt_bias.astype(F32))
    o, s_fin = gated_delta_rule(q, k, v, g, beta, s0)
    o = rms_norm(o, gnorm) * jax.nn.silu(z.astype(F32).reshape(bsz, t_, GDN_HEADS, GDN_DV))
    return (o.reshape(bsz, t_, GDN_V).astype(qkv_pre.dtype), s_fin.astype(s0.dtype),
            xp[:, -(GDN_CONV - 1):])


def even_mixer(xn, sb_k_past, sb_v_past, gdn_s0, conv_buf, w_in, w_out, conv_w, a_log, dt_bias, gnorm):
    bsz, t_ = xn.shape[:2]
    p = sb_k_past.shape[1]
    proj = xn @ w_in
    sq, sk, sv = [proj[..., i * SB_W:(i + 1) * SB_W].reshape(bsz, t_, SB_HEADS, SB_DIM) for i in range(3)]
    o0 = 3 * SB_W
    qkv_pre = proj[..., o0:o0 + GDN_CONV_DIM]
    z = proj[..., o0 + GDN_CONV_DIM:o0 + GDN_CONV_DIM + GDN_V]
    a = proj[..., EVEN_IN - 2 * GDN_HEADS:EVEN_IN - GDN_HEADS]
    b = proj[..., EVEN_IN - GDN_HEADS:]
    k_all = jnp.concatenate([sb_k_past.astype(sk.dtype), sk], axis=1)
    v_all = jnp.concatenate([sb_v_past.astype(sv.dtype), sv], axis=1)
    k_pos = jnp.arange(p + t_, dtype=jnp.int32)
    o_sb = _sweep(lambda qp, pos: _sb_block(qp[0], k_all, v_all, pos, k_pos), (sq,), p)
    o_gdn, s_fin, new_buf = gdn_heads(qkv_pre, z, a, b, conv_buf, gdn_s0, conv_w, a_log, dt_bias, gnorm)
    y = jnp.concatenate([o_sb.reshape(bsz, t_, SB_W).astype(xn.dtype), o_gdn], axis=-1) @ w_out
    return y, sk, sv, s_fin, new_buf


def fox_attention(xn, k_past, v_past, logf_past, w_in, b_f, w_out):
    bsz, t_ = xn.shape[:2]
    p = k_past.shape[1]
    proj = xn @ w_in
    q, k, v = [proj[..., i * FOX_W:(i + 1) * FOX_W].reshape(bsz, t_, FOX_HEADS, FOX_DIM) for i in range(3)]
    logf = jax.nn.log_sigmoid(proj[..., 3 * FOX_W:].astype(F32) + b_f.astype(F32))
    k_all = jnp.concatenate([k_past.astype(k.dtype), k], axis=1)
    v_all = jnp.concatenate([v_past.astype(v.dtype), v], axis=1)
    f_cum = jnp.cumsum(jnp.concatenate([logf_past.astype(F32), logf], axis=1), axis=1)
    f_key = jnp.transpose(f_cum, (0, 2, 1))
    k_pos = jnp.arange(p + t_, dtype=jnp.int32)
    o = _sweep(lambda qp, pos: _fox_block(qp[0], qp[1], k_all, v_all, f_key, pos, k_pos),
               (q, f_cum[:, p:]), p)
    y = o.reshape(bsz, t_, FOX_W).astype(xn.dtype) @ w_out
    return y, k, v, logf.astype(k.dtype)


def _peer_block(xt, wq, k1, k2, u_tab, v_tab):
    n = xt.shape[0]
    half = PEER_QDIM // 2
    q = (xt @ wq).astype(F32).reshape(n, PEER_HEADS, PEER_QDIM)
    s1 = jnp.einsum('nhd,hkd->nhk', q[..., :half], k1.astype(F32))
    s2 = jnp.einsum('nhd,hkd->nhk', q[..., half:], k2.astype(F32))
    t1, i1 = lax.top_k(s1, PEER_TOPK)
    t2, i2 = lax.top_k(s2, PEER_TOPK)
    cand_s = (t1[..., :, None] + t2[..., None, :]).reshape(n, PEER_HEADS, PEER_TOPK * PEER_TOPK)
    cand_i = (i1[..., :, None] * PEER_NKEYS + i2[..., None, :]).reshape(n, PEER_HEADS, PEER_TOPK * PEER_TOPK)
    best_s, j = lax.top_k(cand_s, PEER_TOPK)
    idx = jnp.take_along_axis(cand_i, j, axis=-1)
    gate = jax.nn.softmax(best_s, axis=-1)
    u = jnp.take(u_tab, idx, axis=0)
    act = jax.nn.gelu(jnp.einsum('nd,nhkd->nhk', xt, u, preferred_element_type=F32))
    v = jnp.take(v_tab, idx, axis=0)
    return jnp.einsum('nhk,nhkd->nd', (gate * act).astype(xt.dtype), v)


def peer_ffn(x, wq, k1, k2, u_tab, v_tab):
    bsz, t_, d = x.shape
    n = bsz * t_
    blk = min(PEER_BLOCK, n)
    nb = -(-n // blk)
    xb = jnp.pad(x.reshape(n, d), ((0, nb * blk - n), (0, 0))).reshape(nb, blk, d)
    out = lax.map(lambda xt: _peer_block(xt, wq, k1, k2, u_tab, v_tab), xb)
    return out.reshape(nb * blk, d)[:n].reshape(bsz, t_, d)


def setup_inputs(seed: int = 0) -> dict:
    key = jax.random.key(seed)
    ks = jax.random.split(key, 28)
    d = D_MODEL

    def nrm(k, shape, s):
        return jax.random.normal(k, shape, F32) * s

    dt = jnp.exp(jax.random.uniform(ks[17], (N_EVEN, GDN_HEADS), F32, math.log(1e-3), math.log(1e-1)))
    return {
        'x_prompt': nrm(ks[0], (BATCH, SEQ, d), 1.0),
        'x_sample': nrm(ks[1], (DEC_BATCH, DEC_SEQ, d), 1.0),
        'cache_sb_k': nrm(ks[2], (N_EVEN, DEC_BATCH, PAST_LEN, SB_HEADS, SB_DIM), 1.0),
        'cache_sb_v': nrm(ks[3], (N_EVEN, DEC_BATCH, PAST_LEN, SB_HEADS, SB_DIM), 1.0),
        'state_gdn': nrm(ks[4], (N_EVEN, DEC_BATCH, GDN_HEADS, GDN_DK, GDN_DV), GDN_DK ** -0.5),
        'state_gdn_conv': nrm(ks[5], (N_EVEN, DEC_BATCH, GDN_CONV - 1, GDN_CONV_DIM), 1.0),
        'cache_fox_k': nrm(ks[6], (N_ODD, DEC_BATCH, PAST_LEN, FOX_HEADS, FOX_DIM), 1.0),
        'cache_fox_v': nrm(ks[7], (N_ODD, DEC_BATCH, PAST_LEN, FOX_HEADS, FOX_DIM), 1.0),
        'cache_fox_logf': jax.nn.log_sigmoid(3.0 + nrm(ks[8], (N_ODD, DEC_BATCH, PAST_LEN, FOX_HEADS), 1.0)),
        'meta_tokens': nrm(ks[9], (N_META, d), 1.0),
        'norm_mix': 1.0 + nrm(ks[10], (DEPTH, d), 0.02),
        'norm_ffn': 1.0 + nrm(ks[11], (DEPTH, d), 0.02),
        'norm_final': 1.0 + nrm(ks[12], (d,), 0.02),
        'w_in_even': nrm(ks[13], (N_EVEN, d, EVEN_IN), d ** -0.5),
        'w_out_even': nrm(ks[14], (N_EVEN, EVEN_MIX, d), EVEN_MIX ** -0.5),
        'gdn_conv_w': nrm(ks[15], (N_EVEN, GDN_CONV, GDN_CONV_DIM), GDN_CONV ** -0.5),
        'gdn_a_log': jnp.log(jax.random.uniform(ks[16], (N_EVEN, GDN_HEADS), F32, 1.0, 16.0)),
        'gdn_dt_bias': dt + jnp.log(-jnp.expm1(-dt)),
        'gdn_norm': 1.0 + nrm(ks[18], (N_EVEN, GDN_DV), 0.02),
        'w_in_odd': jnp.concatenate([nrm(ks[19], (N_ODD, d, 3 * FOX_W), d ** -0.5),
                                     nrm(ks[20], (N_ODD, d, FOX_HEADS), 0.5 * d ** -0.5)], axis=-1),
        'b_forget': 3.0 + nrm(ks[21], (N_ODD, FOX_HEADS), 1.0),
        'w_out_odd': nrm(ks[22], (N_ODD, FOX_W, d), FOX_W ** -0.5),
        'peer_wq': nrm(ks[23], (DEPTH, d, PEER_HEADS * PEER_QDIM), d ** -0.5),
        'peer_k1': nrm(ks[24], (DEPTH, PEER_HEADS, PEER_NKEYS, PEER_QDIM // 2), (PEER_QDIM // 2) ** -0.5),
        'peer_k2': nrm(ks[25], (DEPTH, PEER_HEADS, PEER_NKEYS, PEER_QDIM // 2), (PEER_QDIM // 2) ** -0.5),
        'peer_u': nrm(ks[26], (DEPTH, N_EXPERTS, d), d ** -0.5),
        'peer_v': nrm(ks[27], (DEPTH, N_EXPERTS, d), PEER_HEADS ** -0.5),
    }


def reference(x_prompt, x_sample, cache_sb_k, cache_sb_v, state_gdn, state_gdn_conv,
              cache_fox_k, cache_fox_v, cache_fox_logf, meta_tokens, norm_mix, norm_ffn, norm_final,
              w_in_even, w_out_even, gdn_conv_w, gdn_a_log, gdn_dt_bias, gdn_norm,
              w_in_odd, b_forget, w_out_odd, peer_wq, peer_k1, peer_k2, peer_u, peer_v):
    bsz = x_prompt.shape[0]
    dt = x_prompt.dtype
    empty_sb = jnp.zeros((bsz, 0, SB_HEADS, SB_DIM), dt)
    zero_s = jnp.zeros((bsz, GDN_HEADS, GDN_DK, GDN_DV), dt)
    zero_buf = jnp.zeros((bsz, GDN_CONV - 1, GDN_CONV_DIM), dt)
    empty_fox = jnp.zeros((bsz, 0, FOX_HEADS, FOX_DIM), dt)
    empty_logf = jnp.zeros((bsz, 0, FOX_HEADS), dt)

    meta = jnp.broadcast_to(meta_tokens.astype(dt)[None], (bsz, N_META, D_MODEL))
    hp = jnp.concatenate([meta, x_prompt], axis=1)
    hs = x_sample

    sbk_p, sbv_p, sbk_s, sbv_s = [], [], [], []
    gs_p, gs_s, gc_p, gc_s = [], [], [], []
    fk_p, fv_p, ff_p, fk_s, fv_s, ff_s = [], [], [], [], [], []

    for layer in range(DEPTH):
        if layer % 2 == 0:
            e = layer // 2
            w = (w_in_even[e], w_out_even[e], gdn_conv_w[e], gdn_a_log[e], gdn_dt_bias[e], gdn_norm[e])
            dp, kp, vp, sp, bp = even_mixer(rms_norm(hp, norm_mix[layer]), empty_sb, empty_sb, zero_s, zero_buf, *w)
            ds, ks_, vs_, ss, bs = even_mixer(rms_norm(hs, norm_mix[layer]), cache_sb_k[e], cache_sb_v[e],
                                              state_gdn[e], state_gdn_conv[e], *w)
            sbk_p.append(kp); sbv_p.append(vp); sbk_s.append(ks_); sbv_s.append(vs_)
            gs_p.append(sp); gs_s.append(ss); gc_p.append(bp); gc_s.append(bs)
        else:
            o = layer // 2
            w = (w_in_odd[o], b_forget[o], w_out_odd[o])
            dp, kp, vp, fp = fox_attention(rms_norm(hp, norm_mix[layer]), empty_fox, empty_fox, empty_logf, *w)
            ds, ks_, vs_, fs = fox_attention(rms_norm(hs, norm_mix[layer]), cache_fox_k[o], cache_fox_v[o],
                                             cache_fox_logf[o], *w)
            fk_p.append(kp); fv_p.append(vp); ff_p.append(fp)
            fk_s.append(ks_); fv_s.append(vs_); ff_s.append(fs)
        hp = hp + dp
        hs = hs + ds
        pw = (peer_wq[layer], peer_k1[layer], peer_k2[layer], peer_u[layer], peer_v[layer])
        hp = hp + peer_ffn(rms_norm(hp, norm_ffn[layer]), *pw)
        hs = hs + peer_ffn(rms_norm(hs, norm_ffn[layer]), *pw)

    y_prompt = rms_norm(hp, norm_final)[:, N_META:]
    y_sample = rms_norm(hs, norm_final)
    return (y_prompt, y_sample,
            jnp.stack(sbk_p), jnp.stack(sbv_p), jnp.stack(sbk_s), jnp.stack(sbv_s),
            jnp.stack(gs_p), jnp.stack(gs_s), jnp.stack(gc_p), jnp.stack(gc_s),
            jnp.stack(fk_p), jnp.stack(fv_p), jnp.stack(ff_p),
            jnp.stack(fk_s), jnp.stack(fv_s), jnp.stack(ff_s))
```

```python
import functools

import jax
import jax.numpy as jnp
from jax import lax
from jax.experimental import pallas as pl
from jax.experimental.pallas import tpu as pltpu

F32 = jnp.float32
BF16 = jnp.bfloat16

D_MODEL = 1024
N_META = 16
CHUNK = 64
SB_DIM = 64
SB_HEADS = 8
SB_W = SB_HEADS * SB_DIM
GDN_DK = 128
GDN_DV = 128
GDN_HEADS = 4
GDN_QK = GDN_HEADS * GDN_DK
GDN_V = GDN_HEADS * GDN_DV
GDN_CONV = 4
GDN_CONV_DIM = 2 * GDN_QK + GDN_V
FOX_DIM = 64
FOX_HEADS = 16
FOX_W = FOX_HEADS * FOX_DIM
PEER_HEADS = 8
PEER_NKEYS = 128
PEER_TOPK = 16
PEER_QDIM = 256
N_EXPERTS = PEER_NKEYS ** 2
RMS_EPS = 1e-6

LANES = 128
ROW_TILE = 256
VMEM_LIMIT = 56 * 1024 * 1024
NEG_BIG = -1e30

_NT = (((1,), (1,)), ((), ()))


def _cparams(*sem):
    return pltpu.CompilerParams(dimension_semantics=sem, vmem_limit_bytes=VMEM_LIMIT)


def _round_up(n, m):
    return -(-n // m) * m


def _pad_rows(a, rows, axis=0):
    pad = [(0, 0)] * a.ndim
    pad[axis] = (0, rows - a.shape[axis])
    return jnp.pad(a, pad)


def _norm_proj_body(x_ref, g_ref, w_ref, b_ref, *out_refs, offs, logsig_last):
    x = x_ref[...]
    xn = x * lax.rsqrt(jnp.mean(x * x, axis=-1, keepdims=True) + RMS_EPS) * g_ref[...]
    xb = xn.astype(BF16)
    n_out = len(out_refs)
    for i, o_ref in enumerate(out_refs):
        y = jnp.dot(xb, w_ref[:, offs[i]:offs[i + 1]], preferred_element_type=F32)
        if logsig_last and i == n_out - 1:
            y = jax.nn.log_sigmoid(y + b_ref[...])
        o_ref[...] = y.astype(o_ref.dtype)


def norm_proj(x, g, w_bf, splits, bias=None, name="norm_proj"):
    m, d = x.shape
    n = w_bf.shape[1]
    offs = [0]
    for s in splits:
        offs.append(offs[-1] + s)
    tm = min(ROW_TILE, m)
    assert offs[-1] == n and m % tm == 0
    logsig_last = bias is not None
    if bias is None:
        bias = jnp.zeros((1, splits[-1]), F32)
    return pl.pallas_call(
        functools.partial(_norm_proj_body, offs=tuple(offs), logsig_last=logsig_last),
        grid=(m // tm,),
        in_specs=[
            pl.BlockSpec((tm, d), lambda i: (i, 0)),
            pl.BlockSpec((1, d), lambda i: (0, 0)),
            pl.BlockSpec((d, n), lambda i: (0, 0)),
            pl.BlockSpec((1, splits[-1]), lambda i: (0, 0)),
        ],
        out_specs=[pl.BlockSpec((tm, s), lambda i: (i, 0)) for s in splits],
        out_shape=[jax.ShapeDtypeStruct((m, s), F32) for s in splits],
        compiler_params=_cparams("parallel"),
        name=name,
    )(x, g.reshape(1, d), w_bf, bias)


def _out_proj_body(*refs, n_in, offs):
    a_refs = refs[:n_in]
    h_ref, w_ref, o_ref = refs[n_in:]
    acc = h_ref[...]
    for i, a_ref in enumerate(a_refs):
        acc = acc + jnp.dot(a_ref[...].astype(BF16), w_ref[offs[i]:offs[i + 1], :], preferred_element_type=F32)
    o_ref[...] = acc


def out_proj_residual(parts, h, w_bf, name="out_proj"):
    m, d = h.shape
    offs = [0]
    for a in parts:
        offs.append(offs[-1] + a.shape[1])
    tm = min(ROW_TILE, m)
    assert offs[-1] == w_bf.shape[0] and m % tm == 0
    return pl.pallas_call(
        functools.partial(_out_proj_body, n_in=len(parts), offs=tuple(offs)),
        grid=(m // tm,),
        in_specs=[pl.BlockSpec((tm, a.shape[1]), lambda i: (i, 0)) for a in parts] + [
            pl.BlockSpec((tm, d), lambda i: (i, 0)),
            pl.BlockSpec(w_bf.shape, lambda i: (0, 0)),
        ],
        out_specs=pl.BlockSpec((tm, d), lambda i: (i, 0)),
        out_shape=jax.ShapeDtypeStruct((m, d), F32),
        compiler_params=_cparams("parallel"),
        name=name,
    )(*parts, h, w_bf)


def _cumsum_body(x_ref, o_ref, carry_ref):
    @pl.when(pl.program_id(1) == 0)
    def _():
        carry_ref[...] = jnp.zeros_like(carry_ref)

    x = x_ref[0]
    t = x.shape[0]
    tri = (lax.broadcasted_iota(jnp.int32, (t, t), 0) >= lax.broadcasted_iota(jnp.int32, (t, t), 1)).astype(F32)
    c = jnp.dot(tri, x, preferred_element_type=F32, precision=lax.Precision.HIGHEST) + carry_ref[...]
    o_ref[0] = c
    carry_ref[...] = c[t - 1:t, :]


def cumsum_rows(x, name="cumsum_rows"):
    b, l, c = x.shape
    tm = ROW_TILE
    assert l % tm == 0
    return pl.pallas_call(
        _cumsum_body,
        grid=(b, l // tm),
        in_specs=[pl.BlockSpec((1, tm, c), lambda i, j: (i, j, 0))],
        out_specs=pl.BlockSpec((1, tm, c), lambda i, j: (i, j, 0)),
        out_shape=jax.ShapeDtypeStruct((b, l, c), F32),
        scratch_shapes=[pltpu.VMEM((1, c), F32)],
        compiler_params=_cparams("parallel", "arbitrary"),
        name=name,
    )(x)


def _fox_body(q_ref, k_ref, vt_ref, fk_ref, fqt_ref, o_ref, acc_ref, *, q_offset, tq, tk):
    hp = pl.program_id(1)
    qi = pl.program_id(2)
    q = q_ref[0] * jnp.asarray(FOX_DIM ** -0.5, BF16)
    lane = lax.broadcasted_iota(jnp.int32, (1, LANES), 1)
    q_heads = (jnp.where(lane < FOX_DIM, q, jnp.zeros_like(q)), jnp.where(lane >= FOX_DIM, q, jnp.zeros_like(q)))
    q0 = q_offset + qi * tq
    q_pos = q0 + lax.broadcasted_iota(jnp.int32, (1, tq), 1)
    n_kb = (q0 + tq - 1) // tk + 1
    fq = [fqt_ref[0, pl.ds(2 * hp + h, 1), :] for h in range(2)]
    acc_ref[...] = jnp.zeros_like(acc_ref)

    def step(kb, carry):
        k0 = pl.multiple_of(kb * tk, tk)
        k = k_ref[0, pl.ds(k0, tk), :]
        vt = vt_ref[0, kb]
        k_pos = k0 + lax.broadcasted_iota(jnp.int32, (tk, 1), 0)
        mask = k_pos <= q_pos
        fk_blk = fk_ref[0, pl.ds(k0, tk), :]
        out = []
        for h in range(2):
            m_prev, l_prev = carry[2 * h], carry[2 * h + 1]
            lane_h = lax.broadcasted_iota(jnp.int32, (1, LANES), 1) == (2 * hp + h)
            fk = jnp.sum(jnp.where(lane_h, fk_blk, 0.0), axis=1, keepdims=True)
            s = lax.dot_general(k, q_heads[h], _NT, preferred_element_type=F32)
            s = jnp.where(mask, s + fq[h] - fk, NEG_BIG)
            m_new = jnp.maximum(m_prev, jnp.max(s, axis=0, keepdims=True))
            p = jnp.exp(s - m_new)
            alpha = jnp.exp(m_prev - m_new)
            l_new = alpha * l_prev + jnp.sum(p, axis=0, keepdims=True)
            pv = jnp.dot(vt, p.astype(BF16), preferred_element_type=F32)
            acc_ref[h] = alpha * acc_ref[h] + pv
            out += [m_new, l_new]
        return tuple(out)

    init = tuple(jnp.full((1, tq), NEG_BIG, F32) if i % 2 == 0 else jnp.zeros((1, tq), F32) for i in range(4))
    fin = lax.fori_loop(0, n_kb, step, init)
    row = lax.broadcasted_iota(jnp.int32, (LANES, 1), 0)
    ot = jnp.where(row < FOX_DIM, acc_ref[0] / fin[1], acc_ref[1] / fin[3])
    o_ref[0] = ot.T


def fox_attention(q_bf, k_bf, vt_bf, f_keys, f_q_t, q_offset, name="fox_attention"):
    b, tq_all, w = q_bf.shape
    tk_all = k_bf.shape[1]
    tq = min(ROW_TILE, tq_all)
    tk = ROW_TILE
    assert tq_all % tq == 0 and tk_all % tk == 0
    return pl.pallas_call(
        functools.partial(_fox_body, q_offset=q_offset, tq=tq, tk=tk),
        grid=(b, w // LANES, tq_all // tq),
        in_specs=[
            pl.BlockSpec((1, tq, LANES), lambda i, h, j: (i, j, h)),
            pl.BlockSpec((1, tk_all, LANES), lambda i, h, j: (i, 0, h)),
            pl.BlockSpec((1, tk_all // tk, LANES, tk), lambda i, h, j: (i, 0, h, 0)),
            pl.BlockSpec((1, tk_all, LANES), lambda i, h, j: (i, 0, 0)),
            pl.BlockSpec((1, FOX_HEADS, tq), lambda i, h, j: (i, 0, j)),
        ],
        out_specs=pl.BlockSpec((1, tq, LANES), lambda i, h, j: (i, j, h)),
        out_shape=jax.ShapeDtypeStruct((b, tq_all, w), F32),
        scratch_shapes=[pltpu.VMEM((2, LANES, tq), F32)],
        compiler_params=_cparams("parallel", "parallel", "arbitrary"),
        name=name,
    )(q_bf, k_bf, vt_bf, f_keys, f_q_t)


def _sb_body(q_ref, k_ref, vt_ref, o_ref, acc_ref, *, q_offset, tq, tk):
    qi = pl.program_id(2)
    q = q_ref[0] * jnp.asarray(SB_DIM ** -0.5, BF16)
    lane = lax.broadcasted_iota(jnp.int32, (1, LANES), 1)
    q_heads = (jnp.where(lane < SB_DIM, q, jnp.zeros_like(q)), jnp.where(lane >= SB_DIM, q, jnp.zeros_like(q)))
    q0 = q_offset + qi * tq
    q_pos = q0 + lax.broadcasted_iota(jnp.int32, (1, tq), 1)
    n_kb = jnp.maximum(q0 + tq - 2, 0) // tk + 1
    upper = (lax.broadcasted_iota(jnp.int32, (tk, tk), 1) > lax.broadcasted_iota(jnp.int32, (tk, tk), 0)).astype(BF16)
    acc_ref[...] = jnp.zeros_like(acc_ref)

    def step(j, carry):
        kb = n_kb - 1 - j
        k0 = pl.multiple_of(kb * tk, tk)
        k = k_ref[0, pl.ds(k0, tk), :]
        vt = vt_ref[0, kb]
        k_pos = k0 + lax.broadcasted_iota(jnp.int32, (tk, 1), 0)
        mask = k_pos < q_pos
        out = []
        for h in range(2):
            r_prev = carry[h]
            z = lax.dot_general(k, q_heads[h], _NT, preferred_element_type=F32)
            sp = jnp.maximum(z, 0.0) + jnp.log1p(jnp.exp(-jnp.abs(z)))
            l = jnp.where(mask, -sp, 0.0)
            l_hi = l.astype(BF16)
            l_lo = (l - l_hi.astype(F32)).astype(BF16)
            later = (jnp.dot(upper, l_hi, preferred_element_type=F32)
                     + jnp.dot(upper, l_lo, preferred_element_type=F32))
            w = jnp.where(mask, jnp.exp((z - sp) + later + r_prev), 0.0)
            acc_ref[h] += jnp.dot(vt, w.astype(BF16), preferred_element_type=F32)
            out.append(r_prev + later[0:1, :] + l[0:1, :])
        return tuple(out)

    lax.fori_loop(0, n_kb, step, (jnp.zeros((1, tq), F32), jnp.zeros((1, tq), F32)))
    row = lax.broadcasted_iota(jnp.int32, (LANES, 1), 0)
    o_ref[0] = jnp.where(row < SB_DIM, acc_ref[0], acc_ref[1]).T


def sb_attention(q_bf, k_bf, vt_bf, q_offset, name="sb_attention"):
    b, tq_all, w = q_bf.shape
    tk_all = k_bf.shape[1]
    tq = min(ROW_TILE, tq_all)
    tk = ROW_TILE
    assert tq_all % tq == 0 and tk_all % tk == 0
    return pl.pallas_call(
        functools.partial(_sb_body, q_offset=q_offset, tq=tq, tk=tk),
        grid=(b, w // LANES, tq_all // tq),
        in_specs=[
            pl.BlockSpec((1, tq, LANES), lambda i, h, j: (i, j, h)),
            pl.BlockSpec((1, tk_all, LANES), lambda i, h, j: (i, 0, h)),
            pl.BlockSpec((1, tk_all // tk, LANES, tk), lambda i, h, j: (i, 0, h, 0)),
        ],
        out_specs=pl.BlockSpec((1, tq, LANES), lambda i, h, j: (i, j, h)),
        out_shape=jax.ShapeDtypeStruct((b, tq_all, w), F32),
        scratch_shapes=[pltpu.VMEM((2, LANES, tq), F32)],
        compiler_params=_cparams("parallel", "parallel", "arbitrary"),
        name=name,
    )(q_bf, k_bf, vt_bf)


_HI = lax.Precision.HIGHEST
_TN = (((0,), (0,)), ((), ()))
_CONV_PAD = 8


def _dot_hi(a, b):
    return jnp.dot(a, b, preferred_element_type=F32, precision=_HI)


def _softplus(x):
    return jnp.maximum(x, 0.0) + jnp.log1p(jnp.exp(-jnp.abs(x)))


def _silu(x):
    return x / (1.0 + jnp.exp(-x))


def _unit_lower_inverse(m):
    c_len = m.shape[0]
    ri = lax.broadcasted_iota(jnp.int32, (c_len, c_len), 0)
    ci = lax.broadcasted_iota(jnp.int32, (c_len, c_len), 1)
    d = (ri == ci).astype(F32)
    s = 1
    while s < c_len:
        join = (ri // (2 * s) == ci // (2 * s)) & (ri % (2 * s) >= s) & (ci % (2 * s) < s)
        c = jnp.where(join, m, 0.0)
        d = d - (c if s == 1 else _dot_hi(_dot_hi(d, c), d))
        s *= 2
    return d


def _gdn_body(x_ref, z_ref, ab_ref, buf_ref, s0_ref, cw_ref, alog_ref, dt_ref, gn_ref,
              o_ref, sfin_ref, xwin_ref, s_ref, *, t_valid):
    c = pl.program_id(1)
    n_c = pl.num_programs(1)
    hist = GDN_CONV - 1

    @pl.when(c == 0)
    def _():
        xwin_ref[_CONV_PAD - hist:_CONV_PAD, :] = buf_ref[0]
        s_ref[...] = s0_ref[0]

    xwin_ref[_CONV_PAD:_CONV_PAD + CHUNK, :] = x_ref[0]
    conv = xwin_ref[_CONV_PAD - hist:_CONV_PAD - hist + CHUNK, :] * cw_ref[0:1, :]
    for i in range(1, GDN_CONV):
        conv = conv + xwin_ref[_CONV_PAD - hist + i:_CONV_PAD - hist + i + CHUNK, :] * cw_ref[i:i + 1, :]
    tail = xwin_ref[_CONV_PAD + CHUNK - hist:_CONV_PAD + CHUNK, :]
    xwin_ref[_CONV_PAD - hist:_CONV_PAD, :] = tail
    act = _silu(conv)

    ab = ab_ref[0]
    row_ok = (c * CHUNK + lax.broadcasted_iota(jnp.int32, (CHUNK, 1), 0)) < t_valid
    g_all = jnp.where(row_ok, -jnp.exp(alog_ref[...]) * _softplus(ab + dt_ref[...]), 0.0)
    beta_all = jnp.where(row_ok, 1.0 / (1.0 + jnp.exp(-ab)), 0.0)
    ri = lax.broadcasted_iota(jnp.int32, (CHUNK, CHUNK), 0)
    ci = lax.broadcasted_iota(jnp.int32, (CHUNK, CHUNK), 1)
    tri = ri >= ci
    strict = ri > ci
    gcum_all = _dot_hi(tri.astype(F32), g_all)
    sel = (lax.broadcasted_iota(jnp.int32, (8, LANES), 0) == lax.broadcasted_iota(jnp.int32, (8, LANES), 1)).astype(F32)
    gcum_rows = lax.dot_general(sel, gcum_all, _NT, preferred_element_type=F32, precision=_HI)

    for h in range(GDN_HEADS):
        lo = h * GDN_DK
        qh = act[:, lo:lo + GDN_DK]
        kh = act[:, GDN_QK + lo:GDN_QK + lo + GDN_DK]
        vh = act[:, 2 * GDN_QK + h * GDN_DV:2 * GDN_QK + (h + 1) * GDN_DV]
        qh = qh * lax.rsqrt(jnp.sum(qh * qh, axis=-1, keepdims=True) + RMS_EPS) * (GDN_DK ** -0.5)
        kh = kh * lax.rsqrt(jnp.sum(kh * kh, axis=-1, keepdims=True) + RMS_EPS)
        beta = beta_all[:, GDN_HEADS + h:GDN_HEADS + h + 1]
        gc = gcum_all[:, h:h + 1]
        gr = gcum_rows[h:h + 1, :]
        decay = jnp.exp(jnp.where(tri, gc - gr, NEG_BIG))
        kb = kh * beta
        m = jnp.where(strict, lax.dot_general(kb, kh, _NT, preferred_element_type=F32, precision=_HI) * decay, 0.0)
        tinv = _unit_lower_inverse(m)
        eg = jnp.exp(gc)
        s_h = s_ref[h]
        u = _dot_hi(tinv, vh * beta)
        w = _dot_hi(tinv, kb * eg)
        v_new = u - _dot_hi(w, s_h)
        attn = lax.dot_general(qh, kh, _NT, preferred_element_type=F32, precision=_HI) * decay
        o = _dot_hi(qh * eg, s_h) + _dot_hi(attn, v_new)
        g_last = gc[CHUNK - 1:CHUNK, :]
        k_dec = kh * jnp.exp(g_last - gc)
        s_ref[h] = s_h * jnp.exp(g_last) + lax.dot_general(k_dec, v_new, _TN, preferred_element_type=F32, precision=_HI)
        o = o * lax.rsqrt(jnp.mean(o * o, axis=-1, keepdims=True) + RMS_EPS) * gn_ref[...]
        o_ref[0, :, h * GDN_DV:(h + 1) * GDN_DV] = o * _silu(z_ref[0, :, h * GDN_DV:(h + 1) * GDN_DV])

    @pl.when(c == n_c - 1)
    def _():
        sfin_ref[0] = s_ref[...]


def gdn_heads(qkv_pre, z, ab, conv_buf, s0, conv_w, a_log, dt_bias, gnorm, t_valid, name="gdn"):
    b, t, cd = qkv_pre.shape
    assert t % CHUNK == 0
    n_c = t // CHUNK
    pad_l = lambda a: jnp.pad(a.astype(F32), (0, LANES - a.shape[0])).reshape(1, LANES)
    const = lambda *shape: pl.BlockSpec(shape, lambda i, j: (0,) * len(shape))
    return pl.pallas_call(
        functools.partial(_gdn_body, t_valid=t_valid),
        grid=(b, n_c),
        in_specs=[
            pl.BlockSpec((1, CHUNK, cd), lambda i, j: (i, j, 0)),
            pl.BlockSpec((1, CHUNK, GDN_V), lambda i, j: (i, j, 0)),
            pl.BlockSpec((1, CHUNK, LANES), lambda i, j: (i, j, 0)),
            pl.BlockSpec((1, GDN_CONV - 1, cd), lambda i, j: (i, 0, 0)),
            pl.BlockSpec((1, GDN_HEADS, GDN_DK, GDN_DV), lambda i, j: (i, 0, 0, 0)),
            const(GDN_CONV, cd), const(1, LANES), const(1, LANES), const(1, GDN_DV),
        ],
        out_specs=[
            pl.BlockSpec((1, CHUNK, GDN_V), lambda i, j: (i, j, 0)),
            pl.BlockSpec((1, GDN_HEADS, GDN_DK, GDN_DV), lambda i, j: (i, 0, 0, 0)),
        ],
        out_shape=[jax.ShapeDtypeStruct((b, t, GDN_V), F32),
                   jax.ShapeDtypeStruct((b, GDN_HEADS, GDN_DK, GDN_DV), F32)],
        scratch_shapes=[pltpu.VMEM((_CONV_PAD + CHUNK, cd), F32), pltpu.VMEM((GDN_HEADS, GDN_DK, GDN_DV), F32)],
        compiler_params=_cparams("parallel", "arbitrary"),
        name=name,
    )(qkv_pre, z, ab, conv_buf.astype(F32), s0.astype(F32), conv_w.astype(F32), pad_l(a_log), pad_l(dt_bias),
      gnorm.astype(F32).reshape(1, GDN_DV))


def even_mixer(h, g, w_in, w_out, conv_w, a_log, dt_bias, gnorm, sb_k_past, sb_v_past, gdn_s0, conv_buf, t_valid):
    b, t, d = h.shape
    p = sb_k_past.shape[1]
    rows = b * t
    o0 = 3 * SB_W
    w_ab = jnp.pad(w_in[:, o0 + GDN_CONV_DIM + GDN_V:], ((0, 0), (0, LANES - 2 * GDN_HEADS)))
    w_bf = jnp.concatenate([w_in[:, :o0 + GDN_CONV_DIM + GDN_V], w_ab], axis=1).astype(BF16)
    q, k, v, qkv_pre, z, ab = norm_proj(h.reshape(rows, d), g, w_bf,
                                        (SB_W, SB_W, SB_W, GDN_CONV_DIM, GDN_V, LANES), name="even_in_proj")
    q = q.reshape(b, t, SB_W)
    k = k.reshape(b, t, SB_W)
    v = v.reshape(b, t, SB_W)
    tk = ROW_TILE
    tq_pad = _round_up(t, LANES)
    tk_pad = _round_up(p + t, tk)
    k_all = _pad_rows(jnp.concatenate([sb_k_past.reshape(b, p, SB_W), k], axis=1), tk_pad, 1)
    v_all = _pad_rows(jnp.concatenate([sb_v_past.reshape(b, p, SB_W), v], axis=1), tk_pad, 1)
    k_bf, vt_bf = _kv_layouts(k_all, v_all, tk)
    q_bf = _pad_rows(q, tq_pad, 1).astype(BF16)
    o_sb = sb_attention(q_bf, k_bf, vt_bf, p)[:, :t]

    t_c = _round_up(t, CHUNK)
    qkv_pre = qkv_pre.reshape(b, t, GDN_CONV_DIM)
    o_gdn, s_fin = gdn_heads(_pad_rows(qkv_pre, t_c, 1), _pad_rows(z.reshape(b, t, GDN_V), t_c, 1),
                             _pad_rows(ab.reshape(b, t, LANES), t_c, 1), conv_buf, gdn_s0,
                             conv_w, a_log, dt_bias, gnorm, t_valid)
    o_gdn = o_gdn[:, :t]
    hist = GDN_CONV - 1
    assert t_valid >= hist
    xp_tail = qkv_pre[:, t_valid - hist:t_valid]
    h_new = out_proj_residual([o_sb.reshape(rows, SB_W), o_gdn.reshape(rows, GDN_V)], h.reshape(rows, d),
                              w_out.astype(BF16), name="even_out_proj")
    return (h_new.reshape(b, t, d), k.reshape(b, t, SB_HEADS, SB_DIM), v.reshape(b, t, SB_HEADS, SB_DIM),
            s_fin, xp_tail)


PEER_HALF = PEER_QDIM // 2
_NSEL = PEER_TOPK + 1
_SUB = 256
_CAND = tuple((a, b) for a in range(_NSEL) for b in range(_NSEL) if (a + 1) * (b + 1) <= _NSEL)
_NCAND = _round_up(len(_CAND), 8)


def _top_values(x, n, out_ref):
    rows = x.shape[0]
    rid = lax.broadcasted_iota(jnp.int32, (rows, 1), 0)

    def it(i, x):
        m = jnp.max(x, axis=0, keepdims=True)
        first = jnp.min(jnp.where(x == m, rid, rows), axis=0, keepdims=True)
        out_ref[pl.ds(i, 1), :] = m
        return jnp.where(rid == first, -jnp.inf, x)

    lax.fori_loop(0, n, it, x)


def _gelu_tanh(x):
    return 0.5 * x * (1.0 + jnp.tanh(0.7978845608028654 * (x + 0.044715 * (x * x * x))))


def _peer_body(h_ref, g_ref, wq_ref, k1_ref, k2_ref, u_ref, vt_ref, gf_ref, o_ref,
               xn_ref, q_ref, ns1_ref, s2m_ref, e1_ref, e2_ref, t1_ref, t2_ref, cand_ref, csort_ref, acc_ref,
               *, te, final_norm):
    e = pl.program_id(1)
    n_e = pl.num_programs(1)
    tn = h_ref.shape[0]

    @pl.when(e == 0)
    def _prologue():
        x = h_ref[...]
        xn = x * lax.rsqrt(jnp.mean(x * x, axis=-1, keepdims=True) + RMS_EPS) * g_ref[...]
        xb = xn.astype(BF16)
        xn_ref[...] = xb
        q = jnp.dot(xb, wq_ref[...], preferred_element_type=F32)
        for j in range(2 * PEER_HEADS):
            q_ref[j] = q[:, j * PEER_HALF:(j + 1) * PEER_HALF]
        acc_ref[...] = jnp.zeros_like(acc_ref)
        cand_ref[...] = jnp.full(cand_ref.shape, -jnp.inf, F32)

        def per_head(h, _):
            s1 = lax.dot_general(k1_ref[h], q_ref[2 * h], _NT, preferred_element_type=F32, precision=_HI)
            s2 = lax.dot_general(k2_ref[h], q_ref[2 * h + 1], _NT, preferred_element_type=F32, precision=_HI)
            _top_values(s1, _NSEL, t1_ref)
            _top_values(s2, _NSEL, t2_ref)
            for r, (a, b) in enumerate(_CAND):
                cand_ref[r:r + 1, :] = t1_ref[a:a + 1, :] + t2_ref[b:b + 1, :]
            _top_values(cand_ref[...], _NSEL, csort_ref)
            thr = 0.5 * (csort_ref[PEER_TOPK - 1:PEER_TOPK, :] + csort_ref[PEER_TOPK:PEER_TOPK + 1, :])
            s_max = t1_ref[0:1, :] + t2_ref[0:1, :]
            cand = cand_ref[...]
            zsum = jnp.sum(jnp.where(cand >= thr, jnp.exp(cand - s_max), 0.0), axis=0, keepdims=True)
            ns1_ref[h] = -s1
            s2m_ref[h] = s2 - thr
            e1_ref[h] = jnp.exp(s1 - t1_ref[0:1, :]) / zsum
            e2_ref[h] = jnp.exp(s2 - t2_ref[0:1, :])
            return 0

        lax.fori_loop(0, PEER_HEADS, per_head, 0)

    xb = xn_ref[...]

    def sub_tile(j, _):
        r0 = pl.multiple_of(j * _SUB, _SUB)
        a_t = lax.dot_general(u_ref[pl.ds(r0, _SUB), :], xb, _NT, preferred_element_type=F32)
        gates = []
        for r in range(_SUB // PEER_NKEYS):
            i1 = e * (te // PEER_NKEYS) + j * (_SUB // PEER_NKEYS) + r
            gsum = jnp.zeros((PEER_NKEYS, tn), F32)
            for h in range(PEER_HEADS):
                sel = s2m_ref[h] >= ns1_ref[h, pl.ds(i1, 1), :]
                gsum = gsum + jnp.where(sel, e2_ref[h] * e1_ref[h, pl.ds(i1, 1), :], 0.0)
            gates.append(gsum)
        w_t = (_gelu_tanh(a_t) * jnp.concatenate(gates, axis=0)).astype(BF16)
        acc_ref[...] += jnp.dot(vt_ref[:, pl.ds(r0, _SUB)], w_t, preferred_element_type=F32)
        return 0

    lax.fori_loop(0, te // _SUB, sub_tile, 0)

    @pl.when(e == n_e - 1)
    def _epilogue():
        y = h_ref[...] + acc_ref[...].T
        if final_norm:
            y = y * lax.rsqrt(jnp.mean(y * y, axis=-1, keepdims=True) + RMS_EPS) * gf_ref[...]
        o_ref[...] = y


def peer_residual(h, g, wq, k1, k2, u_bf, vt_bf, final_g=None, tn=640, te=1024, name="peer"):
    m, d = h.shape
    tn = min(tn, m)
    assert m % tn == 0 and N_EXPERTS % te == 0 and te % _SUB == 0
    final_norm = final_g is not None
    gf = (final_g if final_norm else jnp.ones((d,), F32)).astype(F32).reshape(1, d)
    const = lambda *shape: pl.BlockSpec(shape, lambda i, j: (0,) * len(shape))
    big = lambda: pltpu.VMEM((PEER_HEADS, PEER_NKEYS, tn), F32)
    return pl.pallas_call(
        functools.partial(_peer_body, te=te, final_norm=final_norm),
        grid=(m // tn, N_EXPERTS // te),
        in_specs=[
            pl.BlockSpec((tn, d), lambda i, j: (i, 0)),
            const(1, d),
            const(d, PEER_HEADS * PEER_QDIM),
            const(PEER_HEADS, PEER_NKEYS, PEER_HALF),
            const(PEER_HEADS, PEER_NKEYS, PEER_HALF),
            pl.BlockSpec((te, d), lambda i, j: (j, 0)),
            pl.BlockSpec((d, te), lambda i, j: (0, j)),
            const(1, d),
        ],
        out_specs=pl.BlockSpec((tn, d), lambda i, j: (i, 0)),
        out_shape=jax.ShapeDtypeStruct((m, d), F32),
        scratch_shapes=[
            pltpu.VMEM((tn, d), BF16),
            pltpu.VMEM((2 * PEER_HEADS, tn, PEER_HALF), F32),
            big(), big(), big(), big(),
            pltpu.VMEM((_round_up(_NSEL, 8), tn), F32),
            pltpu.VMEM((_round_up(_NSEL, 8), tn), F32),
            pltpu.VMEM((_NCAND, tn), F32),
            pltpu.VMEM((_round_up(_NSEL, 8), tn), F32),
            pltpu.VMEM((d, tn), F32),
        ],
        compiler_params=_cparams("parallel", "arbitrary"),
        name=name,
    )(h, g.astype(F32).reshape(1, d), wq.astype(BF16), k1.astype(F32), k2.astype(F32), u_bf, vt_bf, gf)


def _kv_layouts(k_all, v_all, tk):
    b, t, w = k_all.shape
    vt = v_all.astype(BF16).reshape(b, t // tk, tk, w).transpose(0, 1, 3, 2)
    return k_all.astype(BF16), vt


def odd_mixer(h, g, w_in, b_f, w_out, k_past, v_past, logf_past):
    b, t, d = h.shape
    p = k_past.shape[1]
    rows = b * t
    w_f = jnp.pad(w_in[:, 3 * FOX_W:], ((0, 0), (0, LANES - FOX_HEADS)))
    w_bf = jnp.concatenate([w_in[:, :3 * FOX_W], w_f], axis=1).astype(BF16)
    bias = jnp.pad(b_f.astype(F32), (0, LANES - FOX_HEADS)).reshape(1, LANES)
    q, k, v, logf = norm_proj(h.reshape(rows, d), g, w_bf, (FOX_W, FOX_W, FOX_W, LANES), bias=bias,
                              name="odd_in_proj")
    q = q.reshape(b, t, FOX_W)
    k = k.reshape(b, t, FOX_W)
    v = v.reshape(b, t, FOX_W)
    logf = logf.reshape(b, t, LANES)
    tk = ROW_TILE
    tq_pad = _round_up(t, LANES)
    tk_pad = _round_up(p + t, tk)
    k_all = _pad_rows(jnp.concatenate([k_past.reshape(b, p, FOX_W), k], axis=1), tk_pad, 1)
    v_all = _pad_rows(jnp.concatenate([v_past.reshape(b, p, FOX_W), v], axis=1), tk_pad, 1)
    logf_past = jnp.pad(logf_past.astype(F32), ((0, 0), (0, 0), (0, LANES - FOX_HEADS)))
    logf_all = _pad_rows(jnp.concatenate([logf_past, logf], axis=1), tk_pad, 1)
    f_cum = cumsum_rows(logf_all, name="fox_cumsum")
    f_q_t = _pad_rows(f_cum[:, p:p + t, :FOX_HEADS], tq_pad, 1).transpose(0, 2, 1)
    k_bf, vt_bf = _kv_layouts(k_all, v_all, tk)
    q_bf = _pad_rows(q, tq_pad, 1).astype(BF16)
    o = fox_attention(q_bf, k_bf, vt_bf, f_cum, f_q_t, p)[:, :t]
    h_new = out_proj_residual([o.reshape(rows, FOX_W)], h.reshape(rows, d), w_out.astype(BF16), name="odd_out_proj")
    return (h_new.reshape(b, t, d), k.reshape(b, t, FOX_HEADS, FOX_DIM), v.reshape(b, t, FOX_HEADS, FOX_DIM),
            logf[..., :FOX_HEADS])


def kernel(x_prompt, x_sample, cache_sb_k, cache_sb_v, state_gdn, state_gdn_conv, cache_fox_k, cache_fox_v, cache_fox_logf, meta_tokens, norm_mix, norm_ffn, norm_final, w_in_even, w_out_even, gdn_conv_w, gdn_a_log, gdn_dt_bias, gdn_norm, w_in_odd, b_forget, w_out_odd, peer_wq, peer_k1, peer_k2, peer_u, peer_v):
    bsz, seq, d = x_prompt.shape
    dec_b, dec_t, _ = x_sample.shape
    depth = norm_mix.shape[0]
    dt = x_prompt.dtype
    t_p = N_META + seq
    t_pad = _round_up(t_p, 5 * ROW_TILE)

    meta = jnp.broadcast_to(meta_tokens.astype(dt)[None], (bsz, N_META, d))
    hp = _pad_rows(jnp.concatenate([meta, x_prompt], axis=1), t_pad, 1)
    hs = x_sample

    empty_sb = jnp.zeros((bsz, 0, SB_HEADS, SB_DIM), dt)
    zero_s = jnp.zeros((bsz, GDN_HEADS, GDN_DK, GDN_DV), dt)
    zero_buf = jnp.zeros((bsz, GDN_CONV - 1, GDN_CONV_DIM), dt)
    empty_fox = jnp.zeros((bsz, 0, FOX_HEADS, FOX_DIM), dt)
    empty_logf = jnp.zeros((bsz, 0, FOX_HEADS), dt)

    sbk_p, sbv_p, sbk_s, sbv_s = [], [], [], []
    gs_p, gs_s, gc_p, gc_s = [], [], [], []
    fk_p, fv_p, ff_p, fk_s, fv_s, ff_s = [], [], [], [], [], []

    for layer in range(depth):
        if layer % 2 == 0:
            e = layer // 2
            w = (norm_mix[layer], w_in_even[e], w_out_even[e], gdn_conv_w[e], gdn_a_log[e], gdn_dt_bias[e], gdn_norm[e])
            hp, kp, vp, sp, bp = even_mixer(hp, *w, empty_sb, empty_sb, zero_s, zero_buf, t_p)
            hs, ks_, vs_, ss, bs = even_mixer(hs, *w, cache_sb_k[e], cache_sb_v[e], state_gdn[e], state_gdn_conv[e], dec_t)
            sbk_p.append(kp[:, :t_p]); sbv_p.append(vp[:, :t_p]); sbk_s.append(ks_); sbv_s.append(vs_)
            gs_p.append(sp); gs_s.append(ss); gc_p.append(bp); gc_s.append(bs)
        else:
            o = layer // 2
            w = (norm_mix[layer], w_in_odd[o], b_forget[o], w_out_odd[o])
            hp, kp, vp, fp = odd_mixer(hp, *w, empty_fox, empty_fox, empty_logf)
            hs, ks_, vs_, fs = odd_mixer(hs, *w, cache_fox_k[o], cache_fox_v[o], cache_fox_logf[o])
            fk_p.append(kp[:, :t_p]); fv_p.append(vp[:, :t_p]); ff_p.append(fp[:, :t_p])
            fk_s.append(ks_); fv_s.append(vs_); ff_s.append(fs)
        last = layer == depth - 1
        u_bf = peer_u[layer].astype(BF16)
        vt_bf = peer_v[layer].T.astype(BF16)
        pw = (norm_ffn[layer], peer_wq[layer], peer_k1[layer], peer_k2[layer], u_bf, vt_bf, norm_final if last else None)
        hp = peer_residual(hp.reshape(bsz * t_pad, d), *pw, name="peer_prompt").reshape(bsz, t_pad, d)
        hs = peer_residual(hs.reshape(dec_b * dec_t, d), *pw, name="peer_sample").reshape(dec_b, dec_t, d)

    y_prompt = hp[:, N_META:t_p]
    y_sample = hs
    return (y_prompt, y_sample,
            jnp.stack(sbk_p), jnp.stack(sbv_p), jnp.stack(sbk_s), jnp.stack(sbv_s),
            jnp.stack(gs_p), jnp.stack(gs_s), jnp.stack(gc_p), jnp.stack(gc_s),
            jnp.stack(fk_p), jnp.stack(fv_p), jnp.stack(ff_p),
            jnp.stack(fk_s), jnp.stack(fv_s), jnp.stack(ff_s))
```

```python
import functools

import jax
import jax.numpy as jnp
from jax import lax
from jax.experimental import pallas as pl
from jax.experimental.pallas import tpu as pltpu

F32 = jnp.float32
BF16 = jnp.bfloat16

D_MODEL = 1024
N_META = 16
CHUNK = 64
SB_DIM = 64
SB_HEADS = 8
SB_W = SB_HEADS * SB_DIM
GDN_DK = 128
GDN_DV = 128
GDN_HEADS = 4
GDN_QK = GDN_HEADS * GDN_DK
GDN_V = GDN_HEADS * GDN_DV
GDN_CONV = 4
GDN_CONV_DIM = 2 * GDN_QK + GDN_V
FOX_DIM = 64
FOX_HEADS = 16
FOX_W = FOX_HEADS * FOX_DIM
PEER_HEADS = 8
PEER_NKEYS = 128
PEER_TOPK = 16
PEER_QDIM = 256
N_EXPERTS = PEER_NKEYS ** 2
RMS_EPS = 1e-6

LANES = 128
ROW_TILE = 256
VMEM_LIMIT = 56 * 1024 * 1024
NEG_BIG = -1e30

_NT = (((1,), (1,)), ((), ()))


def _cparams(*sem):
    return pltpu.CompilerParams(dimension_semantics=sem, vmem_limit_bytes=VMEM_LIMIT)


def _round_up(n, m):
    return -(-n // m) * m


def _pad_rows(a, rows, axis=0):
    pad = [(0, 0)] * a.ndim
    pad[axis] = (0, rows - a.shape[axis])
    return jnp.pad(a, pad)


def _norm_proj_body(x_ref, g_ref, w_ref, b_ref, *out_refs, offs, logsig_last):
    x = x_ref[...]
    xn = x * lax.rsqrt(jnp.mean(x * x, axis=-1, keepdims=True) + RMS_EPS) * g_ref[...]
    xb = xn.astype(BF16)
    n_out = len(out_refs)
    for i, o_ref in enumerate(out_refs):
        y = jnp.dot(xb, w_ref[:, offs[i]:offs[i + 1]], preferred_element_type=F32)
        if logsig_last and i == n_out - 1:
            y = jax.nn.log_sigmoid(y + b_ref[...])
        o_ref[...] = y.astype(o_ref.dtype)


def norm_proj(x, g, w_bf, splits, bias=None, name="norm_proj"):
    m, d = x.shape
    n = w_bf.shape[1]
    offs = [0]
    for s in splits:
        offs.append(offs[-1] + s)
    tm = min(ROW_TILE, m)
    assert offs[-1] == n and m % tm == 0
    logsig_last = bias is not None
    if bias is None:
        bias = jnp.zeros((1, splits[-1]), F32)
    return pl.pallas_call(
        functools.partial(_norm_proj_body, offs=tuple(offs), logsig_last=logsig_last),
        grid=(m // tm,),
        in_specs=[
            pl.BlockSpec((tm, d), lambda i: (i, 0)),
            pl.BlockSpec((1, d), lambda i: (0, 0)),
            pl.BlockSpec((d, n), lambda i: (0, 0)),
            pl.BlockSpec((1, splits[-1]), lambda i: (0, 0)),
        ],
        out_specs=[pl.BlockSpec((tm, s), lambda i: (i, 0)) for s in splits],
        out_shape=[jax.ShapeDtypeStruct((m, s), F32) for s in splits],
        compiler_params=_cparams("parallel"),
        name=name,
    )(x, g.reshape(1, d), w_bf, bias)


def _out_proj_body(*refs, n_in, offs):
    a_refs = refs[:n_in]
    h_ref, w_ref, o_ref = refs[n_in:]
    acc = h_ref[...]
    for i, a_ref in enumerate(a_refs):
        acc = acc + jnp.dot(a_ref[...].astype(BF16), w_ref[offs[i]:offs[i + 1], :], preferred_element_type=F32)
    o_ref[...] = acc


def out_proj_residual(parts, h, w_bf, name="out_proj"):
    m, d = h.shape
    offs = [0]
    for a in parts:
        offs.append(offs[-1] + a.shape[1])
    tm = min(ROW_TILE, m)
    assert offs[-1] == w_bf.shape[0] and m % tm == 0
    return pl.pallas_call(
        functools.partial(_out_proj_body, n_in=len(parts), offs=tuple(offs)),
        grid=(m // tm,),
        in_specs=[pl.BlockSpec((tm, a.shape[1]), lambda i: (i, 0)) for a in parts] + [
            pl.BlockSpec((tm, d), lambda i: (i, 0)),
            pl.BlockSpec(w_bf.shape, lambda i: (0, 0)),
        ],
        out_specs=pl.BlockSpec((tm, d), lambda i: (i, 0)),
        out_shape=jax.ShapeDtypeStruct((m, d), F32),
        compiler_params=_cparams("parallel"),
        name=name,
    )(*parts, h, w_bf)


def _cumsum_body(x_ref, o_ref, carry_ref):
    @pl.when(pl.program_id(1) == 0)
    def _():
        carry_ref[...] = jnp.zeros_like(carry_ref)

    x = x_ref[0]
    t = x.shape[0]
    tri = (lax.broadcasted_iota(jnp.int32, (t, t), 0) >= lax.broadcasted_iota(jnp.int32, (t, t), 1)).astype(F32)
    c = jnp.dot(tri, x, preferred_element_type=F32, precision=lax.Precision.HIGHEST) + carry_ref[...]
    o_ref[0] = c
    carry_ref[...] = c[t - 1:t, :]


def cumsum_rows(x, name="cumsum_rows"):
    b, l, c = x.shape
    tm = ROW_TILE
    assert l % tm == 0
    return pl.pallas_call(
        _cumsum_body,
        grid=(b, l // tm),
        in_specs=[pl.BlockSpec((1, tm, c), lambda i, j: (i, j, 0))],
        out_specs=pl.BlockSpec((1, tm, c), lambda i, j: (i, j, 0)),
        out_shape=jax.ShapeDtypeStruct((b, l, c), F32),
        scratch_shapes=[pltpu.VMEM((1, c), F32)],
        compiler_params=_cparams("parallel", "arbitrary"),
        name=name,
    )(x)


def _fox_body(nsteps_ref, q_ref, k_ref, vt_ref, fk_ref, fqt_ref, o_ref, acc_ref, *, q_offset, tq, tk):
    bi = pl.program_id(0)
    hp = pl.program_id(1)
    qi = pl.program_id(2)
    q = q_ref[0] * jnp.asarray(FOX_DIM ** -0.5, BF16)
    lane = lax.broadcasted_iota(jnp.int32, (1, LANES), 1)
    q_heads = (jnp.where(lane < FOX_DIM, q, jnp.zeros_like(q)), jnp.where(lane >= FOX_DIM, q, jnp.zeros_like(q)))
    q0 = q_offset + qi * tq
    q_pos = q0 + lax.broadcasted_iota(jnp.int32, (1, tq), 1)
    kb_diag = (q0 + tq - 1) // tk
    fq = [fqt_ref[0, pl.ds(2 * hp + h, 1), :] for h in range(2)]
    acc_ref[...] = jnp.zeros_like(acc_ref)

    def step(j, carry):
        kb = kb_diag - j
        k0 = pl.multiple_of(kb * tk, tk)
        k = k_ref[0, pl.ds(k0, tk), :]
        vt = vt_ref[0, kb]
        k_pos = k0 + lax.broadcasted_iota(jnp.int32, (tk, 1), 0)
        mask = k_pos <= q_pos
        fk_blk = fk_ref[0, pl.ds(k0, tk), :]
        out = []
        for h in range(2):
            m_prev, l_prev = carry[2 * h], carry[2 * h + 1]
            lane_h = lax.broadcasted_iota(jnp.int32, (1, LANES), 1) == (2 * hp + h)
            fk = jnp.sum(jnp.where(lane_h, fk_blk, 0.0), axis=1, keepdims=True)
            s = lax.dot_general(k, q_heads[h], _NT, preferred_element_type=F32)
            s = jnp.where(mask, s + fq[h] - fk, NEG_BIG)
            m_new = jnp.maximum(m_prev, jnp.max(s, axis=0, keepdims=True))
            p = jnp.exp(s - m_new)
            alpha = jnp.exp(m_prev - m_new)
            l_new = alpha * l_prev + jnp.sum(p, axis=0, keepdims=True)
            pv = jnp.dot(vt, p.astype(BF16), preferred_element_type=F32)
            acc_ref[h] = alpha * acc_ref[h] + pv
            out += [m_new, l_new]
        return tuple(out)

    init = tuple(jnp.full((1, tq), NEG_BIG, F32) if i % 2 == 0 else jnp.zeros((1, tq), F32) for i in range(4))
    fin = lax.fori_loop(0, nsteps_ref[bi, hp, qi], step, init)
    row = lax.broadcasted_iota(jnp.int32, (LANES, 1), 0)
    ot = jnp.where(row < FOX_DIM, acc_ref[0] / fin[1], acc_ref[1] / fin[3])
    o_ref[0] = ot.T


_UNDERFLOW = 110.0


def _fox_block_counts(q_bf, k_bf, f_keys, f_q, q_offset, t_valid, tq, tk):
    b, tq_all, w = q_bf.shape
    tk_all = k_bf.shape[1]
    nq, nk = tq_all // tq, tk_all // tk
    heads = w // FOX_DIM
    qn = jnp.sqrt(jnp.sum(jnp.square(q_bf.astype(F32)).reshape(b, tq_all, heads, FOX_DIM), axis=-1)) * (FOX_DIM ** -0.5)
    kmax = jnp.max(jnp.sqrt(jnp.sum(jnp.square(k_bf.astype(F32)).reshape(b, tk_all, heads, FOX_DIM), axis=-1)), axis=1)
    bound = 2.0 * qn * kmax[:, None, :] * (1.0 + 1e-3) + f_q
    valid = (jnp.arange(tq_all) < t_valid)[None, :, None]
    cq = jnp.max(jnp.where(valid, bound, -jnp.inf).reshape(b, nq, tq, heads), axis=2)
    f_end = f_keys[:, tk - 1::tk, :heads]
    kb_diag = (q_offset + jnp.arange(nq) * tq + tq - 1) // tk
    need = (cq[:, :, None, :] - f_end[:, None, :, :]) > -_UNDERFLOW
    need = need & (jnp.arange(nk)[None, None, :, None] <= kb_diag[None, :, None, None])
    first = jnp.min(jnp.where(need, jnp.arange(nk)[None, None, :, None], nk), axis=2)
    first = jnp.min(first.reshape(b, nq, heads // 2, 2), axis=-1)
    steps = jnp.clip(kb_diag[None, :, None] - first + 1, 1, kb_diag[None, :, None] + 1)
    return steps.transpose(0, 2, 1).astype(jnp.int32)


def fox_attention(q_bf, k_bf, vt_bf, f_keys, f_q, q_offset, t_valid, name="fox_attention"):
    b, tq_all, w = q_bf.shape
    tk_all = k_bf.shape[1]
    tq = min(ROW_TILE, tq_all)
    tk = ROW_TILE
    assert tq_all % tq == 0 and tk_all % tk == 0
    nsteps = _fox_block_counts(q_bf, k_bf, f_keys, f_q, q_offset, t_valid, tq, tk)
    f_q_t = f_q.transpose(0, 2, 1)
    return pl.pallas_call(
        functools.partial(_fox_body, q_offset=q_offset, tq=tq, tk=tk),
        grid_spec=pltpu.PrefetchScalarGridSpec(
            num_scalar_prefetch=1,
            grid=(b, w // LANES, tq_all // tq),
            in_specs=[
                pl.BlockSpec((1, tq, LANES), lambda i, h, j, n: (i, j, h)),
                pl.BlockSpec((1, tk_all, LANES), lambda i, h, j, n: (i, 0, h)),
                pl.BlockSpec((1, tk_all // tk, LANES, tk), lambda i, h, j, n: (i, 0, h, 0)),
                pl.BlockSpec((1, tk_all, LANES), lambda i, h, j, n: (i, 0, 0)),
                pl.BlockSpec((1, FOX_HEADS, tq), lambda i, h, j, n: (i, 0, j)),
            ],
            out_specs=pl.BlockSpec((1, tq, LANES), lambda i, h, j, n: (i, j, h)),
            scratch_shapes=[pltpu.VMEM((2, LANES, tq), F32)],
        ),
        out_shape=jax.ShapeDtypeStruct((b, tq_all, w), F32),
        compiler_params=_cparams("parallel", "parallel", "arbitrary"),
        name=name,
    )(nsteps, q_bf, k_bf, vt_bf, f_keys, f_q_t)


def _sb_body(q_ref, k_ref, vt_ref, o_ref, acc_ref, *, q_offset, tq, tk):
    qi = pl.program_id(2)
    q = q_ref[0] * jnp.asarray(SB_DIM ** -0.5, BF16)
    lane = lax.broadcasted_iota(jnp.int32, (1, LANES), 1)
    q_heads = (jnp.where(lane < SB_DIM, q, jnp.zeros_like(q)), jnp.where(lane >= SB_DIM, q, jnp.zeros_like(q)))
    q0 = q_offset + qi * tq
    q_pos = q0 + lax.broadcasted_iota(jnp.int32, (1, tq), 1)
    n_kb = jnp.maximum(q0 + tq - 2, 0) // tk + 1
    upper = (lax.broadcasted_iota(jnp.int32, (tk, tk), 1) > lax.broadcasted_iota(jnp.int32, (tk, tk), 0)).astype(BF16)
    acc_ref[...] = jnp.zeros_like(acc_ref)

    def step(carry):
        j = carry[0]
        kb = n_kb - 1 - j
        k0 = pl.multiple_of(kb * tk, tk)
        k = k_ref[0, pl.ds(k0, tk), :]
        vt = vt_ref[0, kb]
        k_pos = k0 + lax.broadcasted_iota(jnp.int32, (tk, 1), 0)
        mask = k_pos < q_pos
        out = []
        for h in range(2):
            r_prev = carry[2 + h]
            z = lax.dot_general(k, q_heads[h], _NT, preferred_element_type=F32)
            sp = jnp.maximum(z, 0.0) + jnp.log1p(jnp.exp(-jnp.abs(z)))
            l = jnp.where(mask, -sp, 0.0)
            l_hi = l.astype(BF16)
            l_lo = (l - l_hi.astype(F32)).astype(BF16)
            later = (jnp.dot(upper, l_hi, preferred_element_type=F32)
                     + jnp.dot(upper, l_lo, preferred_element_type=F32))
            w = jnp.where(mask, jnp.exp((z - sp) + later + r_prev), 0.0)
            acc_ref[h] += jnp.dot(vt, w.astype(BF16), preferred_element_type=F32)
            out.append(r_prev + later[0:1, :] + l[0:1, :])
        return (j + 1, jnp.max(jnp.maximum(out[0], out[1])), out[0], out[1])

    def more(carry):
        return (carry[0] < n_kb) & (carry[1] > -_UNDERFLOW)

    lax.while_loop(more, step, (jnp.int32(0), jnp.float32(0.0), jnp.zeros((1, tq), F32), jnp.zeros((1, tq), F32)))
    row = lax.broadcasted_iota(jnp.int32, (LANES, 1), 0)
    o_ref[0] = jnp.where(row < SB_DIM, acc_ref[0], acc_ref[1]).T


def sb_attention(q_bf, k_bf, vt_bf, q_offset, name="sb_attention"):
    b, tq_all, w = q_bf.shape
    tk_all = k_bf.shape[1]
    tq = min(ROW_TILE, tq_all)
    tk = ROW_TILE
    assert tq_all % tq == 0 and tk_all % tk == 0
    return pl.pallas_call(
        functools.partial(_sb_body, q_offset=q_offset, tq=tq, tk=tk),
        grid=(b, w // LANES, tq_all // tq),
        in_specs=[
            pl.BlockSpec((1, tq, LANES), lambda i, h, j: (i, j, h)),
            pl.BlockSpec((1, tk_all, LANES), lambda i, h, j: (i, 0, h)),
            pl.BlockSpec((1, tk_all // tk, LANES, tk), lambda i, h, j: (i, 0, h, 0)),
        ],
        out_specs=pl.BlockSpec((1, tq, LANES), lambda i, h, j: (i, j, h)),
        out_shape=jax.ShapeDtypeStruct((b, tq_all, w), F32),
        scratch_shapes=[pltpu.VMEM((2, LANES, tq), F32)],
        compiler_params=_cparams("parallel", "parallel", "arbitrary"),
        name=name,
    )(q_bf, k_bf, vt_bf)


_HI = lax.Precision.HIGHEST
_TN = (((0,), (0,)), ((), ()))
_CONV_PAD = 8


def _dot_hi(a, b):
    return jnp.dot(a, b, preferred_element_type=F32, precision=_HI)


def _softplus(x):
    return jnp.maximum(x, 0.0) + jnp.log1p(jnp.exp(-jnp.abs(x)))


def _silu(x):
    return x / (1.0 + jnp.exp(-x))


def _unit_lower_inverse(m):
    c_len = m.shape[0]
    ri = lax.broadcasted_iota(jnp.int32, (c_len, c_len), 0)
    ci = lax.broadcasted_iota(jnp.int32, (c_len, c_len), 1)
    d = (ri == ci).astype(F32)
    s = 1
    while s < c_len:
        join = (ri // (2 * s) == ci // (2 * s)) & (ri % (2 * s) >= s) & (ci % (2 * s) < s)
        c = jnp.where(join, m, 0.0)
        d = d - (c if s == 1 else _dot_hi(_dot_hi(d, c), d))
        s *= 2
    return d


def _gdn_body(x_ref, z_ref, ab_ref, buf_ref, s0_ref, cw_ref, alog_ref, dt_ref, gn_ref,
              o_ref, sfin_ref, xwin_ref, s_ref, *, t_valid):
    c = pl.program_id(1)
    n_c = pl.num_programs(1)
    hist = GDN_CONV - 1

    @pl.when(c == 0)
    def _():
        xwin_ref[_CONV_PAD - hist:_CONV_PAD, :] = buf_ref[0]
        s_ref[...] = s0_ref[0]

    xwin_ref[_CONV_PAD:_CONV_PAD + CHUNK, :] = x_ref[0]
    conv = xwin_ref[_CONV_PAD - hist:_CONV_PAD - hist + CHUNK, :] * cw_ref[0:1, :]
    for i in range(1, GDN_CONV):
        conv = conv + xwin_ref[_CONV_PAD - hist + i:_CONV_PAD - hist + i + CHUNK, :] * cw_ref[i:i + 1, :]
    tail = xwin_ref[_CONV_PAD + CHUNK - hist:_CONV_PAD + CHUNK, :]
    xwin_ref[_CONV_PAD - hist:_CONV_PAD, :] = tail
    act = _silu(conv)

    ab = ab_ref[0]
    row_ok = (c * CHUNK + lax.broadcasted_iota(jnp.int32, (CHUNK, 1), 0)) < t_valid
    g_all = jnp.where(row_ok, -jnp.exp(alog_ref[...]) * _softplus(ab + dt_ref[...]), 0.0)
    beta_all = jnp.where(row_ok, 1.0 / (1.0 + jnp.exp(-ab)), 0.0)
    ri = lax.broadcasted_iota(jnp.int32, (CHUNK, CHUNK), 0)
    ci = lax.broadcasted_iota(jnp.int32, (CHUNK, CHUNK), 1)
    tri = ri >= ci
    strict = ri > ci
    gcum_all = _dot_hi(tri.astype(F32), g_all)
    sel = (lax.broadcasted_iota(jnp.int32, (8, LANES), 0) == lax.broadcasted_iota(jnp.int32, (8, LANES), 1)).astype(F32)
    gcum_rows = lax.dot_general(sel, gcum_all, _NT, preferred_element_type=F32, precision=_HI)

    for h in range(GDN_HEADS):
        lo = h * GDN_DK
        qh = act[:, lo:lo + GDN_DK]
        kh = act[:, GDN_QK + lo:GDN_QK + lo + GDN_DK]
        vh = act[:, 2 * GDN_QK + h * GDN_DV:2 * GDN_QK + (h + 1) * GDN_DV]
        qh = qh * lax.rsqrt(jnp.sum(qh * qh, axis=-1, keepdims=True) + RMS_EPS) * (GDN_DK ** -0.5)
        kh = kh * lax.rsqrt(jnp.sum(kh * kh, axis=-1, keepdims=True) + RMS_EPS)
        beta = beta_all[:, GDN_HEADS + h:GDN_HEADS + h + 1]
        gc = gcum_all[:, h:h + 1]
        gr = gcum_rows[h:h + 1, :]
        decay = jnp.exp(jnp.where(tri, gc - gr, NEG_BIG))
        kb = kh * beta
        m = jnp.where(strict, lax.dot_general(kb, kh, _NT, preferred_element_type=F32, precision=_HI) * decay, 0.0)
        tinv = _unit_lower_inverse(m)
        eg = jnp.exp(gc)
        s_h = s_ref[h]
        u = _dot_hi(tinv, vh * beta)
        w = _dot_hi(tinv, kb * eg)
        v_new = u - _dot_hi(w, s_h)
        attn = lax.dot_general(qh, kh, _NT, preferred_element_type=F32, precision=_HI) * decay
        o = _dot_hi(qh * eg, s_h) + _dot_hi(attn, v_new)
        g_last = gc[CHUNK - 1:CHUNK, :]
        k_dec = kh * jnp.exp(g_last - gc)
        s_ref[h] = s_h * jnp.exp(g_last) + lax.dot_general(k_dec, v_new, _TN, preferred_element_type=F32, precision=_HI)
        o = o * lax.rsqrt(jnp.mean(o * o, axis=-1, keepdims=True) + RMS_EPS) * gn_ref[...]
        o_ref[0, :, h * GDN_DV:(h + 1) * GDN_DV] = o * _silu(z_ref[0, :, h * GDN_DV:(h + 1) * GDN_DV])

    @pl.when(c == n_c - 1)
    def _():
        sfin_ref[0] = s_ref[...]


def gdn_heads(qkv_pre, z, ab, conv_buf, s0, conv_w, a_log, dt_bias, gnorm, t_valid, name="gdn"):
    b, t, cd = qkv_pre.shape
    assert t % CHUNK == 0
    n_c = t // CHUNK
    pad_l = lambda a: jnp.pad(a.astype(F32), (0, LANES - a.shape[0])).reshape(1, LANES)
    const = lambda *shape: pl.BlockSpec(shape, lambda i, j: (0,) * len(shape))
    return pl.pallas_call(
        functools.partial(_gdn_body, t_valid=t_valid),
        grid=(b, n_c),
        in_specs=[
            pl.BlockSpec((1, CHUNK, cd), lambda i, j: (i, j, 0)),
            pl.BlockSpec((1, CHUNK, GDN_V), lambda i, j: (i, j, 0)),
            pl.BlockSpec((1, CHUNK, LANES), lambda i, j: (i, j, 0)),
            pl.BlockSpec((1, GDN_CONV - 1, cd), lambda i, j: (i, 0, 0)),
            pl.BlockSpec((1, GDN_HEADS, GDN_DK, GDN_DV), lambda i, j: (i, 0, 0, 0)),
            const(GDN_CONV, cd), const(1, LANES), const(1, LANES), const(1, GDN_DV),
        ],
        out_specs=[
            pl.BlockSpec((1, CHUNK, GDN_V), lambda i, j: (i, j, 0)),
            pl.BlockSpec((1, GDN_HEADS, GDN_DK, GDN_DV), lambda i, j: (i, 0, 0, 0)),
        ],
        out_shape=[jax.ShapeDtypeStruct((b, t, GDN_V), F32),
                   jax.ShapeDtypeStruct((b, GDN_HEADS, GDN_DK, GDN_DV), F32)],
        scratch_shapes=[pltpu.VMEM((_CONV_PAD + CHUNK, cd), F32), pltpu.VMEM((GDN_HEADS, GDN_DK, GDN_DV), F32)],
        compiler_params=_cparams("parallel", "arbitrary"),
        name=name,
    )(qkv_pre, z, ab, conv_buf.astype(F32), s0.astype(F32), conv_w.astype(F32), pad_l(a_log), pad_l(dt_bias),
      gnorm.astype(F32).reshape(1, GDN_DV))


def even_mixer(h, g, w_in, w_out, conv_w, a_log, dt_bias, gnorm, sb_k_past, sb_v_past, gdn_s0, conv_buf, t_valid):
    b, t, d = h.shape
    p = sb_k_past.shape[1]
    rows = b * t
    o0 = 3 * SB_W
    w_ab = jnp.pad(w_in[:, o0 + GDN_CONV_DIM + GDN_V:], ((0, 0), (0, LANES - 2 * GDN_HEADS)))
    w_bf = jnp.concatenate([w_in[:, :o0 + GDN_CONV_DIM + GDN_V], w_ab], axis=1).astype(BF16)
    q, k, v, qkv_pre, z, ab = norm_proj(h.reshape(rows, d), g, w_bf,
                                        (SB_W, SB_W, SB_W, GDN_CONV_DIM, GDN_V, LANES), name="even_in_proj")
    q = q.reshape(b, t, SB_W)
    k = k.reshape(b, t, SB_W)
    v = v.reshape(b, t, SB_W)
    tk = ROW_TILE
    tq_pad = _round_up(t, LANES)
    tk_pad = _round_up(p + t, tk)
    k_all = _pad_rows(jnp.concatenate([sb_k_past.reshape(b, p, SB_W), k], axis=1), tk_pad, 1)
    v_all = _pad_rows(jnp.concatenate([sb_v_past.reshape(b, p, SB_W), v], axis=1), tk_pad, 1)
    k_bf, vt_bf = _kv_layouts(k_all, v_all, tk)
    q_bf = _pad_rows(q, tq_pad, 1).astype(BF16)
    o_sb = sb_attention(q_bf, k_bf, vt_bf, p)[:, :t]

    t_c = _round_up(t, CHUNK)
    qkv_pre = qkv_pre.reshape(b, t, GDN_CONV_DIM)
    o_gdn, s_fin = gdn_heads(_pad_rows(qkv_pre, t_c, 1), _pad_rows(z.reshape(b, t, GDN_V), t_c, 1),
                             _pad_rows(ab.reshape(b, t, LANES), t_c, 1), conv_buf, gdn_s0,
                             conv_w, a_log, dt_bias, gnorm, t_valid)
    o_gdn = o_gdn[:, :t]
    hist = GDN_CONV - 1
    assert t_valid >= hist
    xp_tail = qkv_pre[:, t_valid - hist:t_valid]
    h_new = out_proj_residual([o_sb.reshape(rows, SB_W), o_gdn.reshape(rows, GDN_V)], h.reshape(rows, d),
                              w_out.astype(BF16), name="even_out_proj")
    return (h_new.reshape(b, t, d), k.reshape(b, t, SB_HEADS, SB_DIM), v.reshape(b, t, SB_HEADS, SB_DIM),
            s_fin, xp_tail)


PEER_HALF = PEER_QDIM // 2
_NSEL = PEER_TOPK + 1
_SUB = 256
_CAND = tuple((a, b) for a in range(_NSEL) for b in range(_NSEL) if (a + 1) * (b + 1) <= _NSEL)
_NCAND = _round_up(len(_CAND), 8)


def _top_values(x, n, out_ref):
    rows = x.shape[0]
    rid = lax.broadcasted_iota(jnp.int32, (rows, 1), 0)

    def it(i, x):
        m = jnp.max(x, axis=0, keepdims=True)
        first = jnp.min(jnp.where(x == m, rid, rows), axis=0, keepdims=True)
        out_ref[pl.ds(i, 1), :] = m
        return jnp.where(rid == first, -jnp.inf, x)

    lax.fori_loop(0, n, it, x)


def _gelu_tanh(x):
    return 0.5 * x * (1.0 + jnp.tanh(0.7978845608028654 * (x + 0.044715 * (x * x * x))))


def _peer_body(h_ref, g_ref, wq_ref, k1_ref, k2_ref, u_ref, vt_ref, gf_ref, o_ref,
               xn_ref, q_ref, ns1_ref, s2m_ref, e1_ref, e2_ref, t1_ref, t2_ref, cand_ref, csort_ref, acc_ref,
               *, te, final_norm):
    e = pl.program_id(1)
    n_e = pl.num_programs(1)
    tn = h_ref.shape[0]

    @pl.when(e == 0)
    def _prologue():
        x = h_ref[...]
        xn = x * lax.rsqrt(jnp.mean(x * x, axis=-1, keepdims=True) + RMS_EPS) * g_ref[...]
        xb = xn.astype(BF16)
        xn_ref[...] = xb
        q = jnp.dot(xb, wq_ref[...], preferred_element_type=F32)
        for j in range(2 * PEER_HEADS):
            q_ref[j] = q[:, j * PEER_HALF:(j + 1) * PEER_HALF]
        acc_ref[...] = jnp.zeros_like(acc_ref)
        cand_ref[...] = jnp.full(cand_ref.shape, -jnp.inf, F32)

        def per_head(h, _):
            s1 = lax.dot_general(k1_ref[h], q_ref[2 * h], _NT, preferred_element_type=F32, precision=_HI)
            s2 = lax.dot_general(k2_ref[h], q_ref[2 * h + 1], _NT, preferred_element_type=F32, precision=_HI)
            _top_values(s1, _NSEL, t1_ref)
            _top_values(s2, _NSEL, t2_ref)
            for r, (a, b) in enumerate(_CAND):
                cand_ref[r:r + 1, :] = t1_ref[a:a + 1, :] + t2_ref[b:b + 1, :]
            _top_values(cand_ref[...], _NSEL, csort_ref)
            thr = 0.5 * (csort_ref[PEER_TOPK - 1:PEER_TOPK, :] + csort_ref[PEER_TOPK:PEER_TOPK + 1, :])
            s_max = t1_ref[0:1, :] + t2_ref[0:1, :]
            cand = cand_ref[...]
            zsum = jnp.sum(jnp.where(cand >= thr, jnp.exp(cand - s_max), 0.0), axis=0, keepdims=True)
            ns1_ref[h] = -s1
            s2m_ref[h] = s2 - thr
            e1_ref[h] = jnp.exp(s1 - t1_ref[0:1, :]) / zsum
            e2_ref[h] = jnp.exp(s2 - t2_ref[0:1, :])
            return 0

        lax.fori_loop(0, PEER_HEADS, per_head, 0)

    xb = xn_ref[...]

    w_tiles = []
    for j in range(te // _SUB):
        r0 = j * _SUB
        a_t = lax.dot_general(u_ref[r0:r0 + _SUB, :], xb, _NT, preferred_element_type=F32)
        gate_rows = []
        for r in range(_SUB // PEER_NKEYS):
            i1 = e * (te // PEER_NKEYS) + j * (_SUB // PEER_NKEYS) + r
            ns1_rows = [ns1_ref[h, pl.ds(i1, 1), :] for h in range(PEER_HEADS)]
            e1_rows = [e1_ref[h, pl.ds(i1, 1), :] for h in range(PEER_HEADS)]
            tiles = []
            for c0 in range(0, tn, LANES):
                gsum = None
                for h in range(PEER_HEADS):
                    sel = s2m_ref[h, :, c0:c0 + LANES] >= ns1_rows[h][:, c0:c0 + LANES]
                    term = jnp.where(sel, e2_ref[h, :, c0:c0 + LANES] * e1_rows[h][:, c0:c0 + LANES], 0.0)
                    gsum = term if gsum is None else gsum + term
                tiles.append(gsum)
            gate_rows.append(jnp.concatenate(tiles, axis=1))
        w_tiles.append((_gelu_tanh(a_t) * jnp.concatenate(gate_rows, axis=0)).astype(BF16))
    acc_ref[...] += jnp.dot(vt_ref[...], jnp.concatenate(w_tiles, axis=0), preferred_element_type=F32)

    @pl.when(e == n_e - 1)
    def _epilogue():
        y = h_ref[...] + acc_ref[...].T
        if final_norm:
            y = y * lax.rsqrt(jnp.mean(y * y, axis=-1, keepdims=True) + RMS_EPS) * gf_ref[...]
        o_ref[...] = y


def peer_residual(h, g, wq, k1, k2, u_bf, vt_bf, final_g=None, tn=640, te=1024, name="peer"):
    m, d = h.shape
    tn = min(tn, m)
    assert m % tn == 0 and N_EXPERTS % te == 0 and te % _SUB == 0
    final_norm = final_g is not None
    gf = (final_g if final_norm else jnp.ones((d,), F32)).astype(F32).reshape(1, d)
    const = lambda *shape: pl.BlockSpec(shape, lambda i, j: (0,) * len(shape))
    big = lambda: pltpu.VMEM((PEER_HEADS, PEER_NKEYS, tn), F32)
    return pl.pallas_call(
        functools.partial(_peer_body, te=te, final_norm=final_norm),
        grid=(m // tn, N_EXPERTS // te),
        in_specs=[
            pl.BlockSpec((tn, d), lambda i, j: (i, 0)),
            const(1, d),
            const(d, PEER_HEADS * PEER_QDIM),
            const(PEER_HEADS, PEER_NKEYS, PEER_HALF),
            const(PEER_HEADS, PEER_NKEYS, PEER_HALF),
            pl.BlockSpec((te, d), lambda i, j: (j, 0)),
            pl.BlockSpec((d, te), lambda i, j: (0, j)),
            const(1, d),
        ],
        out_specs=pl.BlockSpec((tn, d), lambda i, j: (i, 0)),
        out_shape=jax.ShapeDtypeStruct((m, d), F32),
        scratch_shapes=[
            pltpu.VMEM((tn, d), BF16),
            pltpu.VMEM((2 * PEER_HEADS, tn, PEER_HALF), F32),
            big(), big(), big(), big(),
            pltpu.VMEM((_round_up(_NSEL, 8), tn), F32),
            pltpu.VMEM((_round_up(_NSEL, 8), tn), F32),
            pltpu.VMEM((_NCAND, tn), F32),
            pltpu.VMEM((_round_up(_NSEL, 8), tn), F32),
            pltpu.VMEM((d, tn), F32),
        ],
        compiler_params=_cparams("parallel", "arbitrary"),
        name=name,
    )(h, g.astype(F32).reshape(1, d), wq.astype(BF16), k1.astype(F32), k2.astype(F32), u_bf, vt_bf, gf)


def _kv_layouts(k_all, v_all, tk):
    b, t, w = k_all.shape
    vt = v_all.astype(BF16).reshape(b, t // tk, tk, w).transpose(0, 1, 3, 2)
    return k_all.astype(BF16), vt


def odd_mixer(h, g, w_in, b_f, w_out, k_past, v_past, logf_past):
    b, t, d = h.shape
    p = k_past.shape[1]
    rows = b * t
    w_f = jnp.pad(w_in[:, 3 * FOX_W:], ((0, 0), (0, LANES - FOX_HEADS)))
    w_bf = jnp.concatenate([w_in[:, :3 * FOX_W], w_f], axis=1).astype(BF16)
    bias = jnp.pad(b_f.astype(F32), (0, LANES - FOX_HEADS)).reshape(1, LANES)
    q, k, v, logf = norm_proj(h.reshape(rows, d), g, w_bf, (FOX_W, FOX_W, FOX_W, LANES), bias=bias,
                              name="odd_in_proj")
    q = q.reshape(b, t, FOX_W)
    k = k.reshape(b, t, FOX_W)
    v = v.reshape(b, t, FOX_W)
    logf = logf.reshape(b, t, LANES)
    tk = ROW_TILE
    tq_pad = _round_up(t, LANES)
    tk_pad = _round_up(p + t, tk)
    k_all = _pad_rows(jnp.concatenate([k_past.reshape(b, p, FOX_W), k], axis=1), tk_pad, 1)
    v_all = _pad_rows(jnp.concatenate([v_past.reshape(b, p, FOX_W), v], axis=1), tk_pad, 1)
    logf_past = jnp.pad(logf_past.astype(F32), ((0, 0), (0, 0), (0, LANES - FOX_HEADS)))
    logf_all = _pad_rows(jnp.concatenate([logf_past, logf], axis=1), tk_pad, 1)
    f_cum = cumsum_rows(logf_all, name="fox_cumsum")
    f_q = _pad_rows(f_cum[:, p:p + t, :FOX_HEADS], tq_pad, 1)
    k_bf, vt_bf = _kv_layouts(k_all, v_all, tk)
    q_bf = _pad_rows(q, tq_pad, 1).astype(BF16)
    o = fox_attention(q_bf, k_bf, vt_bf, f_cum, f_q, p, t)[:, :t]
    h_new = out_proj_residual([o.reshape(rows, FOX_W)], h.reshape(rows, d), w_out.astype(BF16), name="odd_out_proj")
    return (h_new.reshape(b, t, d), k.reshape(b, t, FOX_HEADS, FOX_DIM), v.reshape(b, t, FOX_HEADS, FOX_DIM),
            logf[..., :FOX_HEADS])


def kernel(x_prompt, x_sample, cache_sb_k, cache_sb_v, state_gdn, state_gdn_conv, cache_fox_k, cache_fox_v, cache_fox_logf, meta_tokens, norm_mix, norm_ffn, norm_final, w_in_even, w_out_even, gdn_conv_w, gdn_a_log, gdn_dt_bias, gdn_norm, w_in_odd, b_forget, w_out_odd, peer_wq, peer_k1, peer_k2, peer_u, peer_v):
    bsz, seq, d = x_prompt.shape
    dec_b, dec_t, _ = x_sample.shape
    depth = norm_mix.shape[0]
    dt = x_prompt.dtype
    t_p = N_META + seq
    t_pad = _round_up(t_p, 5 * ROW_TILE)

    meta = jnp.broadcast_to(meta_tokens.astype(dt)[None], (bsz, N_META, d))
    hp = _pad_rows(jnp.concatenate([meta, x_prompt], axis=1), t_pad, 1)
    hs = x_sample

    empty_sb = jnp.zeros((bsz, 0, SB_HEADS, SB_DIM), dt)
    zero_s = jnp.zeros((bsz, GDN_HEADS, GDN_DK, GDN_DV), dt)
    zero_buf = jnp.zeros((bsz, GDN_CONV - 1, GDN_CONV_DIM), dt)
    empty_fox = jnp.zeros((bsz, 0, FOX_HEADS, FOX_DIM), dt)
    empty_logf = jnp.zeros((bsz, 0, FOX_HEADS), dt)

    sbk_p, sbv_p, sbk_s, sbv_s = [], [], [], []
    gs_p, gs_s, gc_p, gc_s = [], [], [], []
    fk_p, fv_p, ff_p, fk_s, fv_s, ff_s = [], [], [], [], [], []

    for layer in range(depth):
        if layer % 2 == 0:
            e = layer // 2
            w = (norm_mix[layer], w_in_even[e], w_out_even[e], gdn_conv_w[e], gdn_a_log[e], gdn_dt_bias[e], gdn_norm[e])
            hp, kp, vp, sp, bp = even_mixer(hp, *w, empty_sb, empty_sb, zero_s, zero_buf, t_p)
            hs, ks_, vs_, ss, bs = even_mixer(hs, *w, cache_sb_k[e], cache_sb_v[e], state_gdn[e], state_gdn_conv[e], dec_t)
            sbk_p.append(kp[:, :t_p]); sbv_p.append(vp[:, :t_p]); sbk_s.append(ks_); sbv_s.append(vs_)
            gs_p.append(sp); gs_s.append(ss); gc_p.append(bp); gc_s.append(bs)
        else:
            o = layer // 2
            w = (norm_mix[layer], w_in_odd[o], b_forget[o], w_out_odd[o])
            hp, kp, vp, fp = odd_mixer(hp, *w, empty_fox, empty_fox, empty_logf)
            hs, ks_, vs_, fs = odd_mixer(hs, *w, cache_fox_k[o], cache_fox_v[o], cache_fox_logf[o])
            fk_p.append(kp[:, :t_p]); fv_p.append(vp[:, :t_p]); ff_p.append(fp[:, :t_p])
            fk_s.append(ks_); fv_s.append(vs_); ff_s.append(fs)
        last = layer == depth - 1
        u_bf = peer_u[layer].astype(BF16)
        vt_bf = peer_v[layer].T.astype(BF16)
        pw = (norm_ffn[layer], peer_wq[layer], peer_k1[layer], peer_k2[layer], u_bf, vt_bf, norm_final if last else None)
        hp = peer_residual(hp.reshape(bsz * t_pad, d), *pw, name="peer_prompt").reshape(bsz, t_pad, d)
        hs = peer_residual(hs.reshape(dec_b * dec_t, d), *pw, name="peer_sample").reshape(dec_b, dec_t, d)

    y_prompt = hp[:, N_META:t_p]
    y_sample = hs
    return (y_prompt, y_sample,
            jnp.stack(sbk_p), jnp.stack(sbv_p), jnp.stack(sbk_s), jnp.stack(sbv_s),
            jnp.stack(gs_p), jnp.stack(gs_s), jnp.stack(gc_p), jnp.stack(gc_s),
            jnp.stack(fk_p), jnp.stack(fv_p), jnp.stack(ff_p),
            jnp.stack(fk_s), jnp.stack(fv_s), jnp.stack(ff_s))
```

```python
import functools

import jax
import jax.numpy as jnp
from jax import lax
from jax.experimental import pallas as pl
from jax.experimental.pallas import tpu as pltpu

F32 = jnp.float32
BF16 = jnp.bfloat16

D_MODEL = 1024
N_META = 16
CHUNK = 64
SB_DIM = 64
SB_HEADS = 8
SB_W = SB_HEADS * SB_DIM
GDN_DK = 128
GDN_DV = 128
GDN_HEADS = 4
GDN_QK = GDN_HEADS * GDN_DK
GDN_V = GDN_HEADS * GDN_DV
GDN_CONV = 4
GDN_CONV_DIM = 2 * GDN_QK + GDN_V
FOX_DIM = 64
FOX_HEADS = 16
FOX_W = FOX_HEADS * FOX_DIM
PEER_HEADS = 8
PEER_NKEYS = 128
PEER_TOPK = 16
PEER_QDIM = 256
N_EXPERTS = PEER_NKEYS ** 2
RMS_EPS = 1e-6

LANES = 128
ROW_TILE = 256
VMEM_LIMIT = 56 * 1024 * 1024
NEG_BIG = -1e30

_NT = (((1,), (1,)), ((), ()))


def _cparams(*sem):
    return pltpu.CompilerParams(dimension_semantics=sem, vmem_limit_bytes=VMEM_LIMIT)


def _round_up(n, m):
    return -(-n // m) * m


def _pad_rows(a, rows, axis=0):
    pad = [(0, 0)] * a.ndim
    pad[axis] = (0, rows - a.shape[axis])
    return jnp.pad(a, pad)


def _norm_proj_body(x_ref, g_ref, w_ref, b_ref, *out_refs, offs, logsig_last):
    x = x_ref[...]
    xn = x * lax.rsqrt(jnp.mean(x * x, axis=-1, keepdims=True) + RMS_EPS) * g_ref[...]
    xb = xn.astype(BF16)
    n_out = len(out_refs)
    for i, o_ref in enumerate(out_refs):
        y = jnp.dot(xb, w_ref[:, offs[i]:offs[i + 1]], preferred_element_type=F32)
        if logsig_last and i == n_out - 1:
            y = jax.nn.log_sigmoid(y + b_ref[...])
        o_ref[...] = y.astype(o_ref.dtype)


def norm_proj(x, g, w_bf, splits, bias=None, name="norm_proj"):
    m, d = x.shape
    n = w_bf.shape[1]
    offs = [0]
    for s in splits:
        offs.append(offs[-1] + s)
    tm = min(ROW_TILE, m)
    assert offs[-1] == n and m % tm == 0
    logsig_last = bias is not None
    if bias is None:
        bias = jnp.zeros((1, splits[-1]), F32)
    return pl.pallas_call(
        functools.partial(_norm_proj_body, offs=tuple(offs), logsig_last=logsig_last),
        grid=(m // tm,),
        in_specs=[
            pl.BlockSpec((tm, d), lambda i: (i, 0)),
            pl.BlockSpec((1, d), lambda i: (0, 0)),
            pl.BlockSpec((d, n), lambda i: (0, 0)),
            pl.BlockSpec((1, splits[-1]), lambda i: (0, 0)),
        ],
        out_specs=[pl.BlockSpec((tm, s), lambda i: (i, 0)) for s in splits],
        out_shape=[jax.ShapeDtypeStruct((m, s), F32) for s in splits],
        compiler_params=_cparams("parallel"),
        name=name,
    )(x, g.reshape(1, d), w_bf, bias)


def _out_proj_body(*refs, n_in, offs):
    a_refs = refs[:n_in]
    h_ref, w_ref, o_ref = refs[n_in:]
    acc = h_ref[...]
    for i, a_ref in enumerate(a_refs):
        acc = acc + jnp.dot(a_ref[...].astype(BF16), w_ref[offs[i]:offs[i + 1], :], preferred_element_type=F32)
    o_ref[...] = acc


def out_proj_residual(parts, h, w_bf, name="out_proj"):
    m, d = h.shape
    offs = [0]
    for a in parts:
        offs.append(offs[-1] + a.shape[1])
    tm = min(ROW_TILE, m)
    assert offs[-1] == w_bf.shape[0] and m % tm == 0
    return pl.pallas_call(
        functools.partial(_out_proj_body, n_in=len(parts), offs=tuple(offs)),
        grid=(m // tm,),
        in_specs=[pl.BlockSpec((tm, a.shape[1]), lambda i: (i, 0)) for a in parts] + [
            pl.BlockSpec((tm, d), lambda i: (i, 0)),
            pl.BlockSpec(w_bf.shape, lambda i: (0, 0)),
        ],
        out_specs=pl.BlockSpec((tm, d), lambda i: (i, 0)),
        out_shape=jax.ShapeDtypeStruct((m, d), F32),
        compiler_params=_cparams("parallel"),
        name=name,
    )(*parts, h, w_bf)


def _cumsum_body(x_ref, o_ref, carry_ref):
    @pl.when(pl.program_id(1) == 0)
    def _():
        carry_ref[...] = jnp.zeros_like(carry_ref)

    x = x_ref[0]
    t = x.shape[0]
    tri = (lax.broadcasted_iota(jnp.int32, (t, t), 0) >= lax.broadcasted_iota(jnp.int32, (t, t), 1)).astype(F32)
    c = jnp.dot(tri, x, preferred_element_type=F32, precision=lax.Precision.HIGHEST) + carry_ref[...]
    o_ref[0] = c
    carry_ref[...] = c[t - 1:t, :]


def cumsum_rows(x, name="cumsum_rows"):
    b, l, c = x.shape
    tm = ROW_TILE
    assert l % tm == 0
    return pl.pallas_call(
        _cumsum_body,
        grid=(b, l // tm),
        in_specs=[pl.BlockSpec((1, tm, c), lambda i, j: (i, j, 0))],
        out_specs=pl.BlockSpec((1, tm, c), lambda i, j: (i, j, 0)),
        out_shape=jax.ShapeDtypeStruct((b, l, c), F32),
        scratch_shapes=[pltpu.VMEM((1, c), F32)],
        compiler_params=_cparams("parallel", "arbitrary"),
        name=name,
    )(x)


def _fox_body(nsteps_ref, q_ref, k_ref, vt_ref, fk_ref, fqt_ref, o_ref, acc_ref, s_ref, p_ref, *, q_offset, tq, tk):
    bi = pl.program_id(0)
    hp = pl.program_id(1)
    qi = pl.program_id(2)
    q = q_ref[0] * jnp.asarray(FOX_DIM ** -0.5, BF16)
    lane = lax.broadcasted_iota(jnp.int32, (1, LANES), 1)
    q_heads = (jnp.where(lane < FOX_DIM, q, jnp.zeros_like(q)), jnp.where(lane >= FOX_DIM, q, jnp.zeros_like(q)))
    q0 = q_offset + qi * tq
    q_pos = q0 + lax.broadcasted_iota(jnp.int32, (1, tq), 1)
    kb_diag = (q0 + tq - 1) // tk
    fq = [fqt_ref[0, pl.ds(2 * hp + h, 1), :] for h in range(2)]
    acc_ref[...] = jnp.zeros_like(acc_ref)
    p_ref[...] = jnp.zeros_like(p_ref)

    def scores(kb):
        k = k_ref[0, pl.ds(pl.multiple_of(kb * tk, tk), tk), :]
        for h in range(2):
            s_ref[h] = lax.dot_general(k, q_heads[h], _NT, preferred_element_type=F32)

    def accumulate(kb, alphas):
        vt = vt_ref[0, kb]
        for h in range(2):
            acc_ref[h] = alphas[h] * acc_ref[h] + jnp.dot(vt, p_ref[h], preferred_element_type=F32)

    def step(j, carry):
        kb = kb_diag - j
        accumulate(jnp.minimum(kb + 1, kb_diag), carry[4:6])
        k0 = pl.multiple_of(kb * tk, tk)
        k_pos = k0 + lax.broadcasted_iota(jnp.int32, (tk, 1), 0)
        mask = k_pos <= q_pos
        fk_blk = fk_ref[0, pl.ds(k0, tk), :]
        stats, alphas = [], []
        for h in range(2):
            m_prev, l_prev = carry[2 * h], carry[2 * h + 1]
            lane_h = lax.broadcasted_iota(jnp.int32, (1, LANES), 1) == (2 * hp + h)
            fk = jnp.sum(jnp.where(lane_h, fk_blk, 0.0), axis=1, keepdims=True)
            s = jnp.where(mask, s_ref[h] + fq[h] - fk, NEG_BIG)
            m_new = jnp.maximum(m_prev, jnp.max(s, axis=0, keepdims=True))
            p = jnp.exp(s - m_new)
            alpha = jnp.exp(m_prev - m_new)
            p_ref[h] = p.astype(BF16)
            stats += [m_new, alpha * l_prev + jnp.sum(p, axis=0, keepdims=True)]
            alphas.append(alpha)
        scores(jnp.maximum(kb - 1, 0))
        return tuple(stats + alphas)

    scores(kb_diag)
    ones = jnp.ones((1, tq), F32)
    init = (jnp.full((1, tq), NEG_BIG, F32), jnp.zeros((1, tq), F32),
            jnp.full((1, tq), NEG_BIG, F32), jnp.zeros((1, tq), F32), ones, ones)
    n_steps = nsteps_ref[bi, hp, qi]
    fin = lax.fori_loop(0, n_steps, step, init)
    accumulate(kb_diag - (n_steps - 1), fin[4:6])
    row = lax.broadcasted_iota(jnp.int32, (LANES, 1), 0)
    ot = jnp.where(row < FOX_DIM, acc_ref[0] / fin[1], acc_ref[1] / fin[3])
    o_ref[0] = ot.T


_UNDERFLOW = 110.0


def _fox_block_counts(q_bf, k_bf, f_keys, f_q, q_offset, t_valid, tq, tk):
    b, tq_all, w = q_bf.shape
    tk_all = k_bf.shape[1]
    nq, nk = tq_all // tq, tk_all // tk
    heads = w // FOX_DIM
    qn = jnp.sqrt(jnp.sum(jnp.square(q_bf.astype(F32)).reshape(b, tq_all, heads, FOX_DIM), axis=-1)) * (FOX_DIM ** -0.5)
    kmax = jnp.max(jnp.sqrt(jnp.sum(jnp.square(k_bf.astype(F32)).reshape(b, tk_all, heads, FOX_DIM), axis=-1)), axis=1)
    bound = 2.0 * qn * kmax[:, None, :] * (1.0 + 1e-3) + f_q
    valid = (jnp.arange(tq_all) < t_valid)[None, :, None]
    cq = jnp.max(jnp.where(valid, bound, -jnp.inf).reshape(b, nq, tq, heads), axis=2)
    f_end = f_keys[:, tk - 1::tk, :heads]
    kb_diag = (q_offset + jnp.arange(nq) * tq + tq - 1) // tk
    need = (cq[:, :, None, :] - f_end[:, None, :, :]) > -_UNDERFLOW
    need = need & (jnp.arange(nk)[None, None, :, None] <= kb_diag[None, :, None, None])
    first = jnp.min(jnp.where(need, jnp.arange(nk)[None, None, :, None], nk), axis=2)
    first = jnp.min(first.reshape(b, nq, heads // 2, 2), axis=-1)
    steps = jnp.clip(kb_diag[None, :, None] - first + 1, 1, kb_diag[None, :, None] + 1)
    return steps.transpose(0, 2, 1).astype(jnp.int32)


def fox_attention(q_bf, k_bf, vt_bf, f_keys, f_q, q_offset, t_valid, name="fox_attention"):
    b, tq_all, w = q_bf.shape
    tk_all = k_bf.shape[1]
    tq = min(ROW_TILE, tq_all)
    tk = ROW_TILE
    assert tq_all % tq == 0 and tk_all % tk == 0
    nsteps = _fox_block_counts(q_bf, k_bf, f_keys, f_q, q_offset, t_valid, tq, tk)
    f_q_t = f_q.transpose(0, 2, 1)
    return pl.pallas_call(
        functools.partial(_fox_body, q_offset=q_offset, tq=tq, tk=tk),
        grid_spec=pltpu.PrefetchScalarGridSpec(
            num_scalar_prefetch=1,
            grid=(b, w // LANES, tq_all // tq),
            in_specs=[
                pl.BlockSpec((1, tq, LANES), lambda i, h, j, n: (i, j, h)),
                pl.BlockSpec((1, tk_all, LANES), lambda i, h, j, n: (i, 0, h)),
                pl.BlockSpec((1, tk_all // tk, LANES, tk), lambda i, h, j, n: (i, 0, h, 0)),
                pl.BlockSpec((1, tk_all, LANES), lambda i, h, j, n: (i, 0, 0)),
                pl.BlockSpec((1, FOX_HEADS, tq), lambda i, h, j, n: (i, 0, j)),
            ],
            out_specs=pl.BlockSpec((1, tq, LANES), lambda i, h, j, n: (i, j, h)),
            scratch_shapes=[pltpu.VMEM((2, LANES, tq), F32), pltpu.VMEM((2, tk, tq), F32),
                            pltpu.VMEM((2, tk, tq), BF16)],
        ),
        out_shape=jax.ShapeDtypeStruct((b, tq_all, w), F32),
        compiler_params=_cparams("parallel", "parallel", "arbitrary"),
        name=name,
    )(nsteps, q_bf, k_bf, vt_bf, f_keys, f_q_t)


def _sb_body(q_ref, k_ref, vt_ref, o_ref, acc_ref, *, q_offset, tq, tk):
    qi = pl.program_id(2)
    q = q_ref[0] * jnp.asarray(SB_DIM ** -0.5, BF16)
    lane = lax.broadcasted_iota(jnp.int32, (1, LANES), 1)
    q_heads = (jnp.where(lane < SB_DIM, q, jnp.zeros_like(q)), jnp.where(lane >= SB_DIM, q, jnp.zeros_like(q)))
    q0 = q_offset + qi * tq
    q_pos = q0 + lax.broadcasted_iota(jnp.int32, (1, tq), 1)
    n_kb = jnp.maximum(q0 + tq - 2, 0) // tk + 1
    upper = (lax.broadcasted_iota(jnp.int32, (tk, tk), 1) > lax.broadcasted_iota(jnp.int32, (tk, tk), 0)).astype(BF16)
    acc_ref[...] = jnp.zeros_like(acc_ref)

    def step(carry):
        j = carry[0]
        kb = n_kb - 1 - j
        k0 = pl.multiple_of(kb * tk, tk)
        k = k_ref[0, pl.ds(k0, tk), :]
        vt = vt_ref[0, kb]
        k_pos = k0 + lax.broadcasted_iota(jnp.int32, (tk, 1), 0)
        mask = k_pos < q_pos
        out = []
        for h in range(2):
            r_prev = carry[2 + h]
            z = lax.dot_general(k, q_heads[h], _NT, preferred_element_type=F32)
            sp = jnp.maximum(z, 0.0) + jnp.log1p(jnp.exp(-jnp.abs(z)))
            l = jnp.where(mask, -sp, 0.0)
            l_hi = l.astype(BF16)
            l_lo = (l - l_hi.astype(F32)).astype(BF16)
            later = (jnp.dot(upper, l_hi, preferred_element_type=F32)
                     + jnp.dot(upper, l_lo, preferred_element_type=F32))
            w = jnp.where(mask, jnp.exp((z - sp) + later + r_prev), 0.0)
            acc_ref[h] += jnp.dot(vt, w.astype(BF16), preferred_element_type=F32)
            out.append(r_prev + later[0:1, :] + l[0:1, :])
        return (j + 1, jnp.max(jnp.maximum(out[0], out[1])), out[0], out[1])

    def more(carry):
        return (carry[0] < n_kb) & (carry[1] > -_UNDERFLOW)

    lax.while_loop(more, step, (jnp.int32(0), jnp.float32(0.0), jnp.zeros((1, tq), F32), jnp.zeros((1, tq), F32)))
    row = lax.broadcasted_iota(jnp.int32, (LANES, 1), 0)
    o_ref[0] = jnp.where(row < SB_DIM, acc_ref[0], acc_ref[1]).T


def sb_attention(q_bf, k_bf, vt_bf, q_offset, name="sb_attention"):
    b, tq_all, w = q_bf.shape
    tk_all = k_bf.shape[1]
    tq = min(ROW_TILE, tq_all)
    tk = ROW_TILE
    assert tq_all % tq == 0 and tk_all % tk == 0
    return pl.pallas_call(
        functools.partial(_sb_body, q_offset=q_offset, tq=tq, tk=tk),
        grid=(b, w // LANES, tq_all // tq),
        in_specs=[
            pl.BlockSpec((1, tq, LANES), lambda i, h, j: (i, j, h)),
            pl.BlockSpec((1, tk_all, LANES), lambda i, h, j: (i, 0, h)),
            pl.BlockSpec((1, tk_all // tk, LANES, tk), lambda i, h, j: (i, 0, h, 0)),
        ],
        out_specs=pl.BlockSpec((1, tq, LANES), lambda i, h, j: (i, j, h)),
        out_shape=jax.ShapeDtypeStruct((b, tq_all, w), F32),
        scratch_shapes=[pltpu.VMEM((2, LANES, tq), F32)],
        compiler_params=_cparams("parallel", "parallel", "arbitrary"),
        name=name,
    )(q_bf, k_bf, vt_bf)


_HI = lax.Precision.HIGHEST
_TN = (((0,), (0,)), ((), ()))
_CONV_PAD = 8


def _dot_hi(a, b):
    return jnp.dot(a, b, preferred_element_type=F32, precision=_HI)


def _bmm(a, b):
    return jnp.einsum('hij,hjk->hik', a, b, preferred_element_type=F32, precision=_HI)


def _bmm_nt(a, b):
    return jnp.einsum('hik,hjk->hij', a, b, preferred_element_type=F32, precision=_HI)


def _softplus(x):
    return jnp.maximum(x, 0.0) + jnp.log1p(jnp.exp(-jnp.abs(x)))


def _silu(x):
    return x / (1.0 + jnp.exp(-x))


def _unit_lower_inverse(m):
    c_len = m.shape[-1]
    ri = lax.broadcasted_iota(jnp.int32, (c_len, c_len), 0)
    ci = lax.broadcasted_iota(jnp.int32, (c_len, c_len), 1)
    d = jnp.broadcast_to((ri == ci).astype(F32), m.shape)
    s = 1
    while s < c_len:
        join = (ri // (2 * s) == ci // (2 * s)) & (ri % (2 * s) >= s) & (ci % (2 * s) < s)
        c = jnp.where(join, m, 0.0)
        d = d - (c if s == 1 else _bmm(_bmm(d, c), d))
        s *= 2
    return d


def _gdn_body(x_ref, z_ref, ab_ref, buf_ref, s0_ref, cw_ref, alog_ref, dt_ref, gn_ref,
              o_ref, sfin_ref, xwin_ref, s_ref, *, t_valid):
    c = pl.program_id(1)
    n_c = pl.num_programs(1)
    hist = GDN_CONV - 1

    @pl.when(c == 0)
    def _():
        xwin_ref[_CONV_PAD - hist:_CONV_PAD, :] = buf_ref[0]
        s_ref[...] = s0_ref[0]

    xwin_ref[_CONV_PAD:_CONV_PAD + CHUNK, :] = x_ref[0]
    conv = xwin_ref[_CONV_PAD - hist:_CONV_PAD - hist + CHUNK, :] * cw_ref[0:1, :]
    for i in range(1, GDN_CONV):
        conv = conv + xwin_ref[_CONV_PAD - hist + i:_CONV_PAD - hist + i + CHUNK, :] * cw_ref[i:i + 1, :]
    tail = xwin_ref[_CONV_PAD + CHUNK - hist:_CONV_PAD + CHUNK, :]
    xwin_ref[_CONV_PAD - hist:_CONV_PAD, :] = tail
    act = _silu(conv)

    ab = ab_ref[0]
    row_ok = (c * CHUNK + lax.broadcasted_iota(jnp.int32, (CHUNK, 1), 0)) < t_valid
    g_all = jnp.where(row_ok, -jnp.exp(alog_ref[...]) * _softplus(ab + dt_ref[...]), 0.0)
    beta_all = jnp.where(row_ok, 1.0 / (1.0 + jnp.exp(-ab)), 0.0)
    ri = lax.broadcasted_iota(jnp.int32, (CHUNK, CHUNK), 0)
    ci = lax.broadcasted_iota(jnp.int32, (CHUNK, CHUNK), 1)
    tri = ri >= ci
    strict = ri > ci
    gcum_all = _dot_hi(tri.astype(F32), g_all)
    sel = (lax.broadcasted_iota(jnp.int32, (8, LANES), 0) == lax.broadcasted_iota(jnp.int32, (8, LANES), 1)).astype(F32)
    gcum_rows = lax.dot_general(sel, gcum_all, _NT, preferred_element_type=F32, precision=_HI)

    heads = range(GDN_HEADS)
    q4 = jnp.stack([act[:, h * GDN_DK:(h + 1) * GDN_DK] for h in heads])
    k4 = jnp.stack([act[:, GDN_QK + h * GDN_DK:GDN_QK + (h + 1) * GDN_DK] for h in heads])
    v4 = jnp.stack([act[:, 2 * GDN_QK + h * GDN_DV:2 * GDN_QK + (h + 1) * GDN_DV] for h in heads])
    q4 = q4 * lax.rsqrt(jnp.sum(q4 * q4, axis=-1, keepdims=True) + RMS_EPS) * (GDN_DK ** -0.5)
    k4 = k4 * lax.rsqrt(jnp.sum(k4 * k4, axis=-1, keepdims=True) + RMS_EPS)
    beta = jnp.stack([beta_all[:, GDN_HEADS + h:GDN_HEADS + h + 1] for h in heads])
    gc = jnp.stack([gcum_all[:, h:h + 1] for h in heads])
    gr = jnp.stack([gcum_rows[h:h + 1, :] for h in heads])
    decay = jnp.exp(jnp.where(tri, gc - gr, NEG_BIG))
    kb = k4 * beta
    m = jnp.where(strict, _bmm_nt(kb, k4) * decay, 0.0)
    tinv = _unit_lower_inverse(m)
    eg = jnp.exp(gc)
    s4 = s_ref[...]
    u = _bmm(tinv, v4 * beta)
    w = _bmm(tinv, kb * eg)
    v_new = u - _bmm(w, s4)
    attn = _bmm_nt(q4, k4) * decay
    o = _bmm(q4 * eg, s4) + _bmm(attn, v_new)
    g_last = gc[:, CHUNK - 1:CHUNK, :]
    k_dec = k4 * jnp.exp(g_last - gc)
    s_ref[...] = s4 * jnp.exp(g_last) + jnp.einsum('hck,hcv->hkv', k_dec, v_new,
                                                   preferred_element_type=F32, precision=_HI)
    o = o * lax.rsqrt(jnp.mean(o * o, axis=-1, keepdims=True) + RMS_EPS) * gn_ref[...]
    for h in heads:
        o_ref[0, :, h * GDN_DV:(h + 1) * GDN_DV] = o[h] * _silu(z_ref[0, :, h * GDN_DV:(h + 1) * GDN_DV])

    @pl.when(c == n_c - 1)
    def _():
        sfin_ref[0] = s_ref[...]


def gdn_heads(qkv_pre, z, ab, conv_buf, s0, conv_w, a_log, dt_bias, gnorm, t_valid, name="gdn"):
    b, t, cd = qkv_pre.shape
    assert t % CHUNK == 0
    n_c = t // CHUNK
    pad_l = lambda a: jnp.pad(a.astype(F32), (0, LANES - a.shape[0])).reshape(1, LANES)
    const = lambda *shape: pl.BlockSpec(shape, lambda i, j: (0,) * len(shape))
    return pl.pallas_call(
        functools.partial(_gdn_body, t_valid=t_valid),
        grid=(b, n_c),
        in_specs=[
            pl.BlockSpec((1, CHUNK, cd), lambda i, j: (i, j, 0)),
            pl.BlockSpec((1, CHUNK, GDN_V), lambda i, j: (i, j, 0)),
            pl.BlockSpec((1, CHUNK, LANES), lambda i, j: (i, j, 0)),
            pl.BlockSpec((1, GDN_CONV - 1, cd), lambda i, j: (i, 0, 0)),
            pl.BlockSpec((1, GDN_HEADS, GDN_DK, GDN_DV), lambda i, j: (i, 0, 0, 0)),
            const(GDN_CONV, cd), const(1, LANES), const(1, LANES), const(1, GDN_DV),
        ],
        out_specs=[
            pl.BlockSpec((1, CHUNK, GDN_V), lambda i, j: (i, j, 0)),
            pl.BlockSpec((1, GDN_HEADS, GDN_DK, GDN_DV), lambda i, j: (i, 0, 0, 0)),
        ],
        out_shape=[jax.ShapeDtypeStruct((b, t, GDN_V), F32),
                   jax.ShapeDtypeStruct((b, GDN_HEADS, GDN_DK, GDN_DV), F32)],
        scratch_shapes=[pltpu.VMEM((_CONV_PAD + CHUNK, cd), F32), pltpu.VMEM((GDN_HEADS, GDN_DK, GDN_DV), F32)],
        compiler_params=_cparams("parallel", "arbitrary"),
        name=name,
    )(qkv_pre, z, ab, conv_buf.astype(F32), s0.astype(F32), conv_w.astype(F32), pad_l(a_log), pad_l(dt_bias),
      gnorm.astype(F32).reshape(1, GDN_DV))


def even_mixer(h, g, w_in, w_out, conv_w, a_log, dt_bias, gnorm, sb_k_past, sb_v_past, gdn_s0, conv_buf, t_valid):
    b, t, d = h.shape
    p = sb_k_past.shape[1]
    rows = b * t
    o0 = 3 * SB_W
    w_ab = jnp.pad(w_in[:, o0 + GDN_CONV_DIM + GDN_V:], ((0, 0), (0, LANES - 2 * GDN_HEADS)))
    w_bf = jnp.concatenate([w_in[:, :o0 + GDN_CONV_DIM + GDN_V], w_ab], axis=1).astype(BF16)
    q, k, v, qkv_pre, z, ab = norm_proj(h.reshape(rows, d), g, w_bf,
                                        (SB_W, SB_W, SB_W, GDN_CONV_DIM, GDN_V, LANES), name="even_in_proj")
    q = q.reshape(b, t, SB_W)
    k = k.reshape(b, t, SB_W)
    v = v.reshape(b, t, SB_W)
    tk = ROW_TILE
    tq_pad = _round_up(t, LANES)
    tk_pad = _round_up(p + t, tk)
    k_all = _pad_rows(jnp.concatenate([sb_k_past.reshape(b, p, SB_W), k], axis=1), tk_pad, 1)
    v_all = _pad_rows(jnp.concatenate([sb_v_past.reshape(b, p, SB_W), v], axis=1), tk_pad, 1)
    k_bf, vt_bf = _kv_layouts(k_all, v_all, tk)
    q_bf = _pad_rows(q, tq_pad, 1).astype(BF16)
    o_sb = sb_attention(q_bf, k_bf, vt_bf, p)[:, :t]

    t_c = _round_up(t, CHUNK)
    qkv_pre = qkv_pre.reshape(b, t, GDN_CONV_DIM)
    o_gdn, s_fin = gdn_heads(_pad_rows(qkv_pre, t_c, 1), _pad_rows(z.reshape(b, t, GDN_V), t_c, 1),
                             _pad_rows(ab.reshape(b, t, LANES), t_c, 1), conv_buf, gdn_s0,
                             conv_w, a_log, dt_bias, gnorm, t_valid)
    o_gdn = o_gdn[:, :t]
    hist = GDN_CONV - 1
    assert t_valid >= hist
    xp_tail = qkv_pre[:, t_valid - hist:t_valid]
    h_new = out_proj_residual([o_sb.reshape(rows, SB_W), o_gdn.reshape(rows, GDN_V)], h.reshape(rows, d),
                              w_out.astype(BF16), name="even_out_proj")
    return (h_new.reshape(b, t, d), k.reshape(b, t, SB_HEADS, SB_DIM), v.reshape(b, t, SB_HEADS, SB_DIM),
            s_fin, xp_tail)


PEER_HALF = PEER_QDIM // 2
_NSEL = PEER_TOPK + 1
_SUB = 256
_CAND = tuple((a, b) for a in range(_NSEL) for b in range(_NSEL) if (a + 1) * (b + 1) <= _NSEL)
_NCAND = _round_up(len(_CAND), 8)


_SUBLANES = 8


def _sorting_network(n):
    pairs = []

    def merge(lo, m, r):
        step = 2 * r
        if step < m:
            merge(lo, m, step)
            merge(lo + r, m, step)
            pairs.extend((i, i + r) for i in range(lo + r, lo + m - r, step))
        else:
            pairs.append((lo, lo + r))

    def sort(lo, m):
        if m > 1:
            sort(lo, m // 2)
            sort(lo + m // 2, m // 2)
            merge(lo, m, 1)

    sort(0, n)
    return pairs


def _top_values(x, n, out_ref):
    rows, tn = x.shape
    groups = rows // _SUBLANES
    width = 1 << (groups - 1).bit_length()
    minus_inf = jnp.full((_SUBLANES, tn), -jnp.inf, F32)
    lists = [x[r * _SUBLANES:(r + 1) * _SUBLANES, :] for r in range(groups)] + [minus_inf] * (width - groups)
    for i, j in _sorting_network(width):
        lists[i], lists[j] = jnp.maximum(lists[i], lists[j]), jnp.minimum(lists[i], lists[j])
    lists = lists[:groups]
    sub = lax.broadcasted_iota(jnp.int32, (_SUBLANES, 1), 0)
    for it in range(n):
        head = lists[0]
        m = jnp.max(head, axis=0, keepdims=True)
        out_ref[it:it + 1, :] = m
        still_needed = n - it - 1
        if still_needed == 0:
            break
        first = jnp.min(jnp.where(head == m, sub, _SUBLANES), axis=0, keepdims=True)
        won = sub == first
        for r in range(min(groups, still_needed)):
            below = lists[r + 1] if r + 1 < groups else minus_inf
            lists[r] = jnp.where(won, below, lists[r])


def _gelu_tanh(x):
    return 0.5 * x * (1.0 + jnp.tanh(0.7978845608028654 * (x + 0.044715 * (x * x * x))))


def _peer_body(h_ref, g_ref, wq_ref, k1_ref, k2_ref, u_ref, vt_ref, gf_ref, o_ref,
               xn_ref, q_ref, ns1_ref, s2m_ref, e1_ref, e2_ref, t1_ref, t2_ref, cand_ref, csort_ref, acc_ref,
               *, te, final_norm):
    e = pl.program_id(1)
    n_e = pl.num_programs(1)
    tn = h_ref.shape[0]

    @pl.when(e == 0)
    def _prologue():
        x = h_ref[...]
        xn = x * lax.rsqrt(jnp.mean(x * x, axis=-1, keepdims=True) + RMS_EPS) * g_ref[...]
        xb = xn.astype(BF16)
        xn_ref[...] = xb
        q = jnp.dot(xb, wq_ref[...], preferred_element_type=F32)
        for j in range(2 * PEER_HEADS):
            q_ref[j] = q[:, j * PEER_HALF:(j + 1) * PEER_HALF]
        acc_ref[...] = jnp.zeros_like(acc_ref)
        cand_ref[...] = jnp.full(cand_ref.shape, -jnp.inf, F32)

        def per_head(h, _):
            s1 = lax.dot_general(k1_ref[h], q_ref[2 * h], _NT, preferred_element_type=F32, precision=_HI)
            s2 = lax.dot_general(k2_ref[h], q_ref[2 * h + 1], _NT, preferred_element_type=F32, precision=_HI)
            _top_values(s1, _NSEL, t1_ref)
            _top_values(s2, _NSEL, t2_ref)
            for r, (a, b) in enumerate(_CAND):
                cand_ref[r:r + 1, :] = t1_ref[a:a + 1, :] + t2_ref[b:b + 1, :]
            _top_values(cand_ref[...], _NSEL, csort_ref)
            thr = 0.5 * (csort_ref[PEER_TOPK - 1:PEER_TOPK, :] + csort_ref[PEER_TOPK:PEER_TOPK + 1, :])
            s_max = t1_ref[0:1, :] + t2_ref[0:1, :]
            cand = cand_ref[...]
            zsum = jnp.sum(jnp.where(cand >= thr, jnp.exp(cand - s_max), 0.0), axis=0, keepdims=True)
            ns1_ref[h] = -s1
            s2m_ref[h] = s2 - thr
            e1_ref[h] = jnp.exp(s1 - t1_ref[0:1, :]) / zsum
            e2_ref[h] = jnp.exp(s2 - t2_ref[0:1, :])
            return 0

        lax.fori_loop(0, PEER_HEADS, per_head, 0)

    xb = xn_ref[...]

    w_tiles = []
    for j in range(te // _SUB):
        r0 = j * _SUB
        a_t = lax.dot_general(u_ref[r0:r0 + _SUB, :], xb, _NT, preferred_element_type=F32)
        gate_rows = []
        for r in range(_SUB // PEER_NKEYS):
            i1 = e * (te // PEER_NKEYS) + j * (_SUB // PEER_NKEYS) + r
            ns1_rows = [ns1_ref[h, pl.ds(i1, 1), :] for h in range(PEER_HEADS)]
            e1_rows = [e1_ref[h, pl.ds(i1, 1), :] for h in range(PEER_HEADS)]
            tiles = []
            for c0 in range(0, tn, LANES):
                gsum = None
                for h in range(PEER_HEADS):
                    sel = s2m_ref[h, :, c0:c0 + LANES] >= ns1_rows[h][:, c0:c0 + LANES]
                    term = jnp.where(sel, e2_ref[h, :, c0:c0 + LANES] * e1_rows[h][:, c0:c0 + LANES], 0.0)
                    gsum = term if gsum is None else gsum + term
                tiles.append(gsum)
            gate_rows.append(jnp.concatenate(tiles, axis=1))
        w_tiles.append((_gelu_tanh(a_t) * jnp.concatenate(gate_rows, axis=0)).astype(BF16))
    acc_ref[...] += jnp.dot(vt_ref[...], jnp.concatenate(w_tiles, axis=0), preferred_element_type=F32)

    @pl.when(e == n_e - 1)
    def _epilogue():
        y = h_ref[...] + acc_ref[...].T
        if final_norm:
            y = y * lax.rsqrt(jnp.mean(y * y, axis=-1, keepdims=True) + RMS_EPS) * gf_ref[...]
        o_ref[...] = y


def peer_residual(h, g, wq, k1, k2, u_bf, vt_bf, final_g=None, tn=640, te=1024, name="peer"):
    m, d = h.shape
    tn = min(tn, m)
    assert m % tn == 0 and N_EXPERTS % te == 0 and te % _SUB == 0
    final_norm = final_g is not None
    gf = (final_g if final_norm else jnp.ones((d,), F32)).astype(F32).reshape(1, d)
    const = lambda *shape: pl.BlockSpec(shape, lambda i, j: (0,) * len(shape))
    big = lambda: pltpu.VMEM((PEER_HEADS, PEER_NKEYS, tn), F32)
    return pl.pallas_call(
        functools.partial(_peer_body, te=te, final_norm=final_norm),
        grid=(m // tn, N_EXPERTS // te),
        in_specs=[
            pl.BlockSpec((tn, d), lambda i, j: (i, 0)),
            const(1, d),
            const(d, PEER_HEADS * PEER_QDIM),
            const(PEER_HEADS, PEER_NKEYS, PEER_HALF),
            const(PEER_HEADS, PEER_NKEYS, PEER_HALF),
            pl.BlockSpec((te, d), lambda i, j: (j, 0)),
            pl.BlockSpec((d, te), lambda i, j: (0, j)),
            const(1, d),
        ],
        out_specs=pl.BlockSpec((tn, d), lambda i, j: (i, 0)),
        out_shape=jax.ShapeDtypeStruct((m, d), F32),
        scratch_shapes=[
            pltpu.VMEM((tn, d), BF16),
            pltpu.VMEM((2 * PEER_HEADS, tn, PEER_HALF), F32),
            big(), big(), big(), big(),
            pltpu.VMEM((_round_up(_NSEL, 8), tn), F32),
            pltpu.VMEM((_round_up(_NSEL, 8), tn), F32),
            pltpu.VMEM((_NCAND, tn), F32),
            pltpu.VMEM((_round_up(_NSEL, 8), tn), F32),
            pltpu.VMEM((d, tn), F32),
        ],
        compiler_params=_cparams("parallel", "arbitrary"),
        name=name,
    )(h, g.astype(F32).reshape(1, d), wq.astype(BF16), k1.astype(F32), k2.astype(F32), u_bf, vt_bf, gf)


def _kv_layouts(k_all, v_all, tk):
    b, t, w = k_all.shape
    vt = v_all.astype(BF16).reshape(b, t // tk, tk, w).transpose(0, 1, 3, 2)
    return k_all.astype(BF16), vt


def odd_mixer(h, g, w_in, b_f, w_out, k_past, v_past, logf_past):
    b, t, d = h.shape
    p = k_past.shape[1]
    rows = b * t
    w_f = jnp.pad(w_in[:, 3 * FOX_W:], ((0, 0), (0, LANES - FOX_HEADS)))
    w_bf = jnp.concatenate([w_in[:, :3 * FOX_W], w_f], axis=1).astype(BF16)
    bias = jnp.pad(b_f.astype(F32), (0, LANES - FOX_HEADS)).reshape(1, LANES)
    q, k, v, logf = norm_proj(h.reshape(rows, d), g, w_bf, (FOX_W, FOX_W, FOX_W, LANES), bias=bias,
                              name="odd_in_proj")
    q = q.reshape(b, t, FOX_W)
    k = k.reshape(b, t, FOX_W)
    v = v.reshape(b, t, FOX_W)
    logf = logf.reshape(b, t, LANES)
    tk = ROW_TILE
    tq_pad = _round_up(t, LANES)
    tk_pad = _round_up(p + t, tk)
    k_all = _pad_rows(jnp.concatenate([k_past.reshape(b, p, FOX_W), k], axis=1), tk_pad, 1)
    v_all = _pad_rows(jnp.concatenate([v_past.reshape(b, p, FOX_W), v], axis=1), tk_pad, 1)
    logf_past = jnp.pad(logf_past.astype(F32), ((0, 0), (0, 0), (0, LANES - FOX_HEADS)))
    logf_all = _pad_rows(jnp.concatenate([logf_past, logf], axis=1), tk_pad, 1)
    f_cum = cumsum_rows(logf_all, name="fox_cumsum")
    f_q = _pad_rows(f_cum[:, p:p + t, :FOX_HEADS], tq_pad, 1)
    k_bf, vt_bf = _kv_layouts(k_all, v_all, tk)
    q_bf = _pad_rows(q, tq_pad, 1).astype(BF16)
    o = fox_attention(q_bf, k_bf, vt_bf, f_cum, f_q, p, t)[:, :t]
    h_new = out_proj_residual([o.reshape(rows, FOX_W)], h.reshape(rows, d), w_out.astype(BF16), name="odd_out_proj")
    return (h_new.reshape(b, t, d), k.reshape(b, t, FOX_HEADS, FOX_DIM), v.reshape(b, t, FOX_HEADS, FOX_DIM),
            logf[..., :FOX_HEADS])


def kernel(x_prompt, x_sample, cache_sb_k, cache_sb_v, state_gdn, state_gdn_conv, cache_fox_k, cache_fox_v, cache_fox_logf, meta_tokens, norm_mix, norm_ffn, norm_final, w_in_even, w_out_even, gdn_conv_w, gdn_a_log, gdn_dt_bias, gdn_norm, w_in_odd, b_forget, w_out_odd, peer_wq, peer_k1, peer_k2, peer_u, peer_v):
    bsz, seq, d = x_prompt.shape
    dec_b, dec_t, _ = x_sample.shape
    depth = norm_mix.shape[0]
    dt = x_prompt.dtype
    t_p = N_META + seq
    t_pad = _round_up(t_p, 5 * ROW_TILE)

    meta = jnp.broadcast_to(meta_tokens.astype(dt)[None], (bsz, N_META, d))
    hp = _pad_rows(jnp.concatenate([meta, x_prompt], axis=1), t_pad, 1)
    hs = x_sample

    empty_sb = jnp.zeros((bsz, 0, SB_HEADS, SB_DIM), dt)
    zero_s = jnp.zeros((bsz, GDN_HEADS, GDN_DK, GDN_DV), dt)
    zero_buf = jnp.zeros((bsz, GDN_CONV - 1, GDN_CONV_DIM), dt)
    empty_fox = jnp.zeros((bsz, 0, FOX_HEADS, FOX_DIM), dt)
    empty_logf = jnp.zeros((bsz, 0, FOX_HEADS), dt)

    sbk_p, sbv_p, sbk_s, sbv_s = [], [], [], []
    gs_p, gs_s, gc_p, gc_s = [], [], [], []
    fk_p, fv_p, ff_p, fk_s, fv_s, ff_s = [], [], [], [], [], []

    for layer in range(depth):
        if layer % 2 == 0:
            e = layer // 2
            w = (norm_mix[layer], w_in_even[e], w_out_even[e], gdn_conv_w[e], gdn_a_log[e], gdn_dt_bias[e], gdn_norm[e])
            hp, kp, vp, sp, bp = even_mixer(hp, *w, empty_sb, empty_sb, zero_s, zero_buf, t_p)
            hs, ks_, vs_, ss, bs = even_mixer(hs, *w, cache_sb_k[e], cache_sb_v[e], state_gdn[e], state_gdn_conv[e], dec_t)
            sbk_p.append(kp[:, :t_p]); sbv_p.append(vp[:, :t_p]); sbk_s.append(ks_); sbv_s.append(vs_)
            gs_p.append(sp); gs_s.append(ss); gc_p.append(bp); gc_s.append(bs)
        else:
            o = layer // 2
            w = (norm_mix[layer], w_in_odd[o], b_forget[o], w_out_odd[o])
            hp, kp, vp, fp = odd_mixer(hp, *w, empty_fox, empty_fox, empty_logf)
            hs, ks_, vs_, fs = odd_mixer(hs, *w, cache_fox_k[o], cache_fox_v[o], cache_fox_logf[o])
            fk_p.append(kp[:, :t_p]); fv_p.append(vp[:, :t_p]); ff_p.append(fp[:, :t_p])
            fk_s.append(ks_); fv_s.append(vs_); ff_s.append(fs)
        last = layer == depth - 1
        u_bf = peer_u[layer].astype(BF16)
        vt_bf = peer_v[layer].T.astype(BF16)
        pw = (norm_ffn[layer], peer_wq[layer], peer_k1[layer], peer_k2[layer], u_bf, vt_bf, norm_final if last else None)
        hp = peer_residual(hp.reshape(bsz * t_pad, d), *pw, name="peer_prompt").reshape(bsz, t_pad, d)
        hs = peer_residual(hs.reshape(dec_b * dec_t, d), *pw, name="peer_sample").reshape(dec_b, dec_t, d)

    y_prompt = hp[:, N_META:t_p]
    y_sample = hs
    return (y_prompt, y_sample,
            jnp.stack(sbk_p), jnp.stack(sbv_p), jnp.stack(sbk_s), jnp.stack(sbv_s),
            jnp.stack(gs_p), jnp.stack(gs_s), jnp.stack(gc_p), jnp.stack(gc_s),
            jnp.stack(fk_p), jnp.stack(fv_p), jnp.stack(ff_p),
            jnp.stack(fk_s), jnp.stack(fv_s), jnp.stack(ff_s))
```

```python
import functools

import jax
import jax.numpy as jnp
from jax import lax
from jax.experimental import pallas as pl
from jax.experimental.pallas import tpu as pltpu

F32 = jnp.float32
BF16 = jnp.bfloat16

D_MODEL = 1024
N_META = 16
CHUNK = 64
SB_DIM = 64
SB_HEADS = 8
SB_W = SB_HEADS * SB_DIM
GDN_DK = 128
GDN_DV = 128
GDN_HEADS = 4
GDN_QK = GDN_HEADS * GDN_DK
GDN_V = GDN_HEADS * GDN_DV
GDN_CONV = 4
GDN_CONV_DIM = 2 * GDN_QK + GDN_V
FOX_DIM = 64
FOX_HEADS = 16
FOX_W = FOX_HEADS * FOX_DIM
PEER_HEADS = 8
PEER_NKEYS = 128
PEER_TOPK = 16
PEER_QDIM = 256
N_EXPERTS = PEER_NKEYS ** 2
RMS_EPS = 1e-6

LANES = 128
ROW_TILE = 256
VMEM_LIMIT = 56 * 1024 * 1024
NEG_BIG = -1e30

_NT = (((1,), (1,)), ((), ()))


def _cparams(*sem):
    return pltpu.CompilerParams(dimension_semantics=sem, vmem_limit_bytes=VMEM_LIMIT)


def _round_up(n, m):
    return -(-n // m) * m


def _pad_rows(a, rows, axis=0):
    pad = [(0, 0)] * a.ndim
    pad[axis] = (0, rows - a.shape[axis])
    return jnp.pad(a, pad)


def _norm_proj_body(x_ref, g_ref, w_ref, b_ref, *out_refs, offs, outs, logsig_split):
    x = x_ref[...]
    xn = x * lax.rsqrt(jnp.mean(x * x, axis=-1, keepdims=True) + RMS_EPS) * g_ref[...]
    xb = xn.astype(BF16)
    cols = {}
    for (i, kind), o_ref in zip(outs, out_refs):
        if i not in cols:
            y = jnp.dot(xb, w_ref[:, offs[i]:offs[i + 1]], preferred_element_type=F32)
            cols[i] = jax.nn.log_sigmoid(y + b_ref[...]) if i == logsig_split else y
        y = cols[i]
        if kind == "bf16_t":
            o_ref[0] = y.T.astype(BF16)
        else:
            o_ref[...] = y.astype(o_ref.dtype)


def norm_proj(x, g, w_bf, splits, outs=None, bias=None, rows_out=None, name="norm_proj"):
    m, d = x.shape
    n = w_bf.shape[1]
    offs = [0]
    for s in splits:
        offs.append(offs[-1] + s)
    tm = min(ROW_TILE, m)
    assert offs[-1] == n and m % tm == 0
    outs = tuple((i, "f32") for i in range(len(splits))) if outs is None else tuple(outs)
    logsig_split = len(splits) - 1 if bias is not None else -1
    if bias is None:
        bias = jnp.zeros((1, splits[-1]), F32)
    out_specs, out_shape = [], []
    for i, kind in outs:
        s = splits[i]
        if kind == "bf16_t":
            out_specs.append(pl.BlockSpec((1, s, tm), lambda r: (r, 0, 0)))
            out_shape.append(jax.ShapeDtypeStruct((m // tm, s, tm), BF16))
        else:
            rows = rows_out if kind == "f32_rows" else m
            out_specs.append(pl.BlockSpec((tm, s), lambda r: (r, 0)))
            out_shape.append(jax.ShapeDtypeStruct((rows, s), BF16 if kind == "bf16" else F32))
    return pl.pallas_call(
        functools.partial(_norm_proj_body, offs=tuple(offs), outs=outs, logsig_split=logsig_split),
        grid=(m // tm,),
        in_specs=[
            pl.BlockSpec((tm, d), lambda i: (i, 0)),
            pl.BlockSpec((1, d), lambda i: (0, 0)),
            pl.BlockSpec((d, n), lambda i: (0, 0)),
            pl.BlockSpec((1, splits[-1]), lambda i: (0, 0)),
        ],
        out_specs=out_specs,
        out_shape=out_shape,
        compiler_params=_cparams("parallel"),
        name=name,
    )(x, g.reshape(1, d), w_bf, bias)


def _out_proj_body(*refs, n_in, offs):
    a_refs = refs[:n_in]
    h_ref, w_ref, o_ref = refs[n_in:]
    acc = h_ref[...]
    for i, a_ref in enumerate(a_refs):
        acc = acc + jnp.dot(a_ref[...].astype(BF16), w_ref[offs[i]:offs[i + 1], :], preferred_element_type=F32)
    o_ref[...] = acc


def out_proj_residual(parts, h, w_bf, name="out_proj"):
    m, d = h.shape
    offs = [0]
    for a in parts:
        offs.append(offs[-1] + a.shape[1])
    tm = min(ROW_TILE, m)
    assert offs[-1] == w_bf.shape[0] and m % tm == 0
    return pl.pallas_call(
        functools.partial(_out_proj_body, n_in=len(parts), offs=tuple(offs)),
        grid=(m // tm,),
        in_specs=[pl.BlockSpec((tm, a.shape[1]), lambda i: (i, 0)) for a in parts] + [
            pl.BlockSpec((tm, d), lambda i: (i, 0)),
            pl.BlockSpec(w_bf.shape, lambda i: (0, 0)),
        ],
        out_specs=pl.BlockSpec((tm, d), lambda i: (i, 0)),
        out_shape=jax.ShapeDtypeStruct((m, d), F32),
        compiler_params=_cparams("parallel"),
        name=name,
    )(*parts, h, w_bf)


def _cumsum_body(x_ref, o_ref, carry_ref):
    @pl.when(pl.program_id(1) == 0)
    def _():
        carry_ref[...] = jnp.zeros_like(carry_ref)

    x = x_ref[0]
    t = x.shape[0]
    tri = (lax.broadcasted_iota(jnp.int32, (t, t), 0) >= lax.broadcasted_iota(jnp.int32, (t, t), 1)).astype(F32)
    c = jnp.dot(tri, x, preferred_element_type=F32, precision=lax.Precision.HIGHEST) + carry_ref[...]
    o_ref[0] = c
    carry_ref[...] = c[t - 1:t, :]


def cumsum_rows(x, name="cumsum_rows"):
    b, l, c = x.shape
    tm = ROW_TILE
    assert l % tm == 0
    return pl.pallas_call(
        _cumsum_body,
        grid=(b, l // tm),
        in_specs=[pl.BlockSpec((1, tm, c), lambda i, j: (i, j, 0))],
        out_specs=pl.BlockSpec((1, tm, c), lambda i, j: (i, j, 0)),
        out_shape=jax.ShapeDtypeStruct((b, l, c), F32),
        scratch_shapes=[pltpu.VMEM((1, c), F32)],
        compiler_params=_cparams("parallel", "arbitrary"),
        name=name,
    )(x)


def _fox_body(nsteps_ref, q_ref, k_ref, vt_ref, fk_ref, fqt_ref, o_ref, acc_ref, s_ref, p_ref, *, q_offset, tq, tk):
    bi = pl.program_id(0)
    hp = pl.program_id(1)
    qi = pl.program_id(2)
    q = q_ref[0] * jnp.asarray(FOX_DIM ** -0.5, BF16)
    lane = lax.broadcasted_iota(jnp.int32, (1, LANES), 1)
    q_heads = (jnp.where(lane < FOX_DIM, q, jnp.zeros_like(q)), jnp.where(lane >= FOX_DIM, q, jnp.zeros_like(q)))
    q0 = q_offset + qi * tq
    q_pos = q0 + lax.broadcasted_iota(jnp.int32, (1, tq), 1)
    kb_diag = (q0 + tq - 1) // tk
    fq = [fqt_ref[0, pl.ds(2 * hp + h, 1), :] for h in range(2)]
    acc_ref[...] = jnp.zeros_like(acc_ref)

    def scores(kb):
        k = k_ref[0, pl.ds(pl.multiple_of(kb * tk, tk), tk), :]
        for h in range(2):
            s_ref[h] = lax.dot_general(k, q_heads[h], _NT, preferred_element_type=F32)

    def accumulate(kb, alphas):
        vt = vt_ref[0, kb]
        for h in range(2):
            acc_ref[h] = alphas[h] * acc_ref[h] + jnp.dot(vt, p_ref[h], preferred_element_type=F32)

    def softmax(kb, carry, causal):
        k0 = pl.multiple_of(kb * tk, tk)
        fk_blk = fk_ref[0, pl.ds(k0, tk), :]
        if causal:
            mask = (k0 + lax.broadcasted_iota(jnp.int32, (tk, 1), 0)) <= q_pos
        stats, alphas = [], []
        for h in range(2):
            m_prev, l_prev = carry[2 * h], carry[2 * h + 1]
            lane_h = lax.broadcasted_iota(jnp.int32, (1, LANES), 1) == (2 * hp + h)
            fk = jnp.sum(jnp.where(lane_h, fk_blk, 0.0), axis=1, keepdims=True)
            s = s_ref[h] + fq[h] - fk
            if causal:
                s = jnp.where(mask, s, NEG_BIG)
            m_new = jnp.maximum(m_prev, jnp.max(s, axis=0, keepdims=True))
            p = jnp.exp(s - m_new)
            alpha = jnp.exp(m_prev - m_new)
            p_ref[h] = p.astype(BF16)
            stats += [m_new, alpha * l_prev + jnp.sum(p, axis=0, keepdims=True)]
            alphas.append(alpha)
        return tuple(stats + alphas)

    def step(j, carry):
        kb = kb_diag - j
        accumulate(kb + 1, carry[4:6])
        carry = softmax(kb, carry, causal=False)
        scores(jnp.maximum(kb - 1, 0))
        return carry

    ones = jnp.ones((1, tq), F32)
    init = (jnp.full((1, tq), NEG_BIG, F32), jnp.zeros((1, tq), F32),
            jnp.full((1, tq), NEG_BIG, F32), jnp.zeros((1, tq), F32), ones, ones)
    n_steps = nsteps_ref[bi, hp, qi]
    scores(kb_diag)
    first = softmax(kb_diag, init, causal=True)
    scores(jnp.maximum(kb_diag - 1, 0))
    fin = lax.fori_loop(1, n_steps, step, first)
    accumulate(kb_diag - (n_steps - 1), fin[4:6])
    row = lax.broadcasted_iota(jnp.int32, (LANES, 1), 0)
    ot = jnp.where(row < FOX_DIM, acc_ref[0] / fin[1], acc_ref[1] / fin[3])
    o_ref[0] = ot.T


_UNDERFLOW = 110.0


def _fox_block_counts(q_bf, k_bf, f_keys, f_q, q_offset, t_valid, tq, tk):
    b, tq_all, w = q_bf.shape
    tk_all = k_bf.shape[1]
    nq, nk = tq_all // tq, tk_all // tk
    heads = w // FOX_DIM
    qn = jnp.sqrt(jnp.sum(jnp.square(q_bf.astype(F32)).reshape(b, tq_all, heads, FOX_DIM), axis=-1)) * (FOX_DIM ** -0.5)
    kmax = jnp.max(jnp.sqrt(jnp.sum(jnp.square(k_bf.astype(F32)).reshape(b, tk_all, heads, FOX_DIM), axis=-1)), axis=1)
    bound = 2.0 * qn * kmax[:, None, :] * (1.0 + 1e-3) + f_q
    valid = (jnp.arange(tq_all) < t_valid)[None, :, None]
    cq = jnp.max(jnp.where(valid, bound, -jnp.inf).reshape(b, nq, tq, heads), axis=2)
    f_end = f_keys[:, tk - 1::tk, :heads]
    kb_diag = (q_offset + jnp.arange(nq) * tq + tq - 1) // tk
    need = (cq[:, :, None, :] - f_end[:, None, :, :]) > -_UNDERFLOW
    need = need & (jnp.arange(nk)[None, None, :, None] <= kb_diag[None, :, None, None])
    first = jnp.min(jnp.where(need, jnp.arange(nk)[None, None, :, None], nk), axis=2)
    first = jnp.min(first.reshape(b, nq, heads // 2, 2), axis=-1)
    steps = jnp.clip(kb_diag[None, :, None] - first + 1, 1, kb_diag[None, :, None] + 1)
    return steps.transpose(0, 2, 1).astype(jnp.int32)


def fox_attention(q_bf, k_bf, vt_bf, f_keys, f_q, q_offset, t_valid, name="fox_attention"):
    b, tq_all, w = q_bf.shape
    tk_all = k_bf.shape[1]
    tq = min(ROW_TILE, tq_all)
    tk = ROW_TILE
    assert tq_all % tq == 0 and tk_all % tk == 0
    assert tk % tq == 0 and q_offset % tq == 0
    nsteps = _fox_block_counts(q_bf, k_bf, f_keys, f_q, q_offset, t_valid, tq, tk)
    f_q_t = f_q.transpose(0, 2, 1)
    return pl.pallas_call(
        functools.partial(_fox_body, q_offset=q_offset, tq=tq, tk=tk),
        grid_spec=pltpu.PrefetchScalarGridSpec(
            num_scalar_prefetch=1,
            grid=(b, w // LANES, tq_all // tq),
            in_specs=[
                pl.BlockSpec((1, tq, LANES), lambda i, h, j, n: (i, j, h)),
                pl.BlockSpec((1, tk_all, LANES), lambda i, h, j, n: (i, 0, h)),
                pl.BlockSpec((1, tk_all // tk, LANES, tk), lambda i, h, j, n: (i, 0, h, 0)),
                pl.BlockSpec((1, tk_all, LANES), lambda i, h, j, n: (i, 0, 0)),
                pl.BlockSpec((1, FOX_HEADS, tq), lambda i, h, j, n: (i, 0, j)),
            ],
            out_specs=pl.BlockSpec((1, tq, LANES), lambda i, h, j, n: (i, j, h)),
            scratch_shapes=[pltpu.VMEM((2, LANES, tq), F32), pltpu.VMEM((2, tk, tq), F32),
                            pltpu.VMEM((2, tk, tq), BF16)],
        ),
        out_shape=jax.ShapeDtypeStruct((b, tq_all, w), F32),
        compiler_params=_cparams("parallel", "parallel", "arbitrary"),
        name=name,
    )(nsteps, q_bf, k_bf, vt_bf, f_keys, f_q_t)


def _sb_body(q_ref, k_ref, vt_ref, o_ref, acc_ref, *, q_offset, tq, tk):
    qi = pl.program_id(2)
    q = q_ref[0] * jnp.asarray(SB_DIM ** -0.5, BF16)
    lane = lax.broadcasted_iota(jnp.int32, (1, LANES), 1)
    q_heads = (jnp.where(lane < SB_DIM, q, jnp.zeros_like(q)), jnp.where(lane >= SB_DIM, q, jnp.zeros_like(q)))
    q0 = q_offset + qi * tq
    q_pos = q0 + lax.broadcasted_iota(jnp.int32, (1, tq), 1)
    n_kb = jnp.maximum(q0 + tq - 2, 0) // tk + 1
    upper = (lax.broadcasted_iota(jnp.int32, (tk, tk), 1) > lax.broadcasted_iota(jnp.int32, (tk, tk), 0)).astype(BF16)
    acc_ref[...] = jnp.zeros_like(acc_ref)

    def step(carry):
        j = carry[0]
        kb = n_kb - 1 - j
        k0 = pl.multiple_of(kb * tk, tk)
        k = k_ref[0, pl.ds(k0, tk), :]
        vt = vt_ref[0, kb]
        k_pos = k0 + lax.broadcasted_iota(jnp.int32, (tk, 1), 0)
        mask = k_pos < q_pos
        out = []
        for h in range(2):
            r_prev = carry[2 + h]
            z = lax.dot_general(k, q_heads[h], _NT, preferred_element_type=F32)
            sp = jnp.maximum(z, 0.0) + jnp.log1p(jnp.exp(-jnp.abs(z)))
            l = jnp.where(mask, -sp, 0.0)
            l_hi = l.astype(BF16)
            l_lo = (l - l_hi.astype(F32)).astype(BF16)
            later = (jnp.dot(upper, l_hi, preferred_element_type=F32)
                     + jnp.dot(upper, l_lo, preferred_element_type=F32))
            w = jnp.where(mask, jnp.exp((z - sp) + later + r_prev), 0.0)
            acc_ref[h] += jnp.dot(vt, w.astype(BF16), preferred_element_type=F32)
            out.append(r_prev + later[0:1, :] + l[0:1, :])
        return (j + 1, jnp.max(jnp.maximum(out[0], out[1])), out[0], out[1])

    def more(carry):
        return (carry[0] < n_kb) & (carry[1] > -_UNDERFLOW)

    lax.while_loop(more, step, (jnp.int32(0), jnp.float32(0.0), jnp.zeros((1, tq), F32), jnp.zeros((1, tq), F32)))
    row = lax.broadcasted_iota(jnp.int32, (LANES, 1), 0)
    o_ref[0] = jnp.where(row < SB_DIM, acc_ref[0], acc_ref[1]).T


def sb_attention(q_bf, k_bf, vt_bf, q_offset, name="sb_attention"):
    b, tq_all, w = q_bf.shape
    tk_all = k_bf.shape[1]
    tq = min(ROW_TILE, tq_all)
    tk = ROW_TILE
    assert tq_all % tq == 0 and tk_all % tk == 0
    return pl.pallas_call(
        functools.partial(_sb_body, q_offset=q_offset, tq=tq, tk=tk),
        grid=(b, w // LANES, tq_all // tq),
        in_specs=[
            pl.BlockSpec((1, tq, LANES), lambda i, h, j: (i, j, h)),
            pl.BlockSpec((1, tk_all, LANES), lambda i, h, j: (i, 0, h)),
            pl.BlockSpec((1, tk_all // tk, LANES, tk), lambda i, h, j: (i, 0, h, 0)),
        ],
        out_specs=pl.BlockSpec((1, tq, LANES), lambda i, h, j: (i, j, h)),
        out_shape=jax.ShapeDtypeStruct((b, tq_all, w), F32),
        scratch_shapes=[pltpu.VMEM((2, LANES, tq), F32)],
        compiler_params=_cparams("parallel", "parallel", "arbitrary"),
        name=name,
    )(q_bf, k_bf, vt_bf)


_HI = lax.Precision.HIGHEST
_TN = (((0,), (0,)), ((), ()))
_CONV_PAD = 8


def _dot_hi(a, b):
    return jnp.dot(a, b, preferred_element_type=F32, precision=_HI)


def _bmm(a, b):
    return jnp.einsum('hij,hjk->hik', a, b, preferred_element_type=F32, precision=_HI)


def _bmm_nt(a, b):
    return jnp.einsum('hik,hjk->hij', a, b, preferred_element_type=F32, precision=_HI)


def _softplus(x):
    return jnp.maximum(x, 0.0) + jnp.log1p(jnp.exp(-jnp.abs(x)))


def _silu(x):
    return x / (1.0 + jnp.exp(-x))


def _unit_lower_inverse(m):
    c_len = m.shape[-1]
    ri = lax.broadcasted_iota(jnp.int32, (c_len, c_len), 0)
    ci = lax.broadcasted_iota(jnp.int32, (c_len, c_len), 1)
    d = jnp.broadcast_to((ri == ci).astype(F32), m.shape)
    s = 1
    while s < c_len:
        join = (ri // (2 * s) == ci // (2 * s)) & (ri % (2 * s) >= s) & (ci % (2 * s) < s)
        c = jnp.where(join, m, 0.0)
        d = d - (c if s == 1 else _bmm(_bmm(d, c), d))
        s *= 2
    return d


def _gdn_body(x_ref, z_ref, ab_ref, buf_ref, s0_ref, cw_ref, alog_ref, dt_ref, gn_ref,
              o_ref, sfin_ref, xwin_ref, s_ref, *, t_valid):
    c = pl.program_id(1)
    n_c = pl.num_programs(1)
    hist = GDN_CONV - 1

    @pl.when(c == 0)
    def _():
        xwin_ref[_CONV_PAD - hist:_CONV_PAD, :] = buf_ref[0]
        s_ref[...] = s0_ref[0]

    xwin_ref[_CONV_PAD:_CONV_PAD + CHUNK, :] = x_ref[0]
    conv = xwin_ref[_CONV_PAD - hist:_CONV_PAD - hist + CHUNK, :] * cw_ref[0:1, :]
    for i in range(1, GDN_CONV):
        conv = conv + xwin_ref[_CONV_PAD - hist + i:_CONV_PAD - hist + i + CHUNK, :] * cw_ref[i:i + 1, :]
    tail = xwin_ref[_CONV_PAD + CHUNK - hist:_CONV_PAD + CHUNK, :]
    xwin_ref[_CONV_PAD - hist:_CONV_PAD, :] = tail
    act = _silu(conv)

    ab = ab_ref[0]
    row_ok = (c * CHUNK + lax.broadcasted_iota(jnp.int32, (CHUNK, 1), 0)) < t_valid
    g_all = jnp.where(row_ok, -jnp.exp(alog_ref[...]) * _softplus(ab + dt_ref[...]), 0.0)
    beta_all = jnp.where(row_ok, 1.0 / (1.0 + jnp.exp(-ab)), 0.0)
    ri = lax.broadcasted_iota(jnp.int32, (CHUNK, CHUNK), 0)
    ci = lax.broadcasted_iota(jnp.int32, (CHUNK, CHUNK), 1)
    tri = ri >= ci
    strict = ri > ci
    gcum_all = _dot_hi(tri.astype(F32), g_all)
    sel = (lax.broadcasted_iota(jnp.int32, (8, LANES), 0) == lax.broadcasted_iota(jnp.int32, (8, LANES), 1)).astype(F32)
    gcum_rows = lax.dot_general(sel, gcum_all, _NT, preferred_element_type=F32, precision=_HI)

    heads = range(GDN_HEADS)
    q4 = jnp.stack([act[:, h * GDN_DK:(h + 1) * GDN_DK] for h in heads])
    k4 = jnp.stack([act[:, GDN_QK + h * GDN_DK:GDN_QK + (h + 1) * GDN_DK] for h in heads])
    v4 = jnp.stack([act[:, 2 * GDN_QK + h * GDN_DV:2 * GDN_QK + (h + 1) * GDN_DV] for h in heads])
    q4 = q4 * lax.rsqrt(jnp.sum(q4 * q4, axis=-1, keepdims=True) + RMS_EPS) * (GDN_DK ** -0.5)
    k4 = k4 * lax.rsqrt(jnp.sum(k4 * k4, axis=-1, keepdims=True) + RMS_EPS)
    beta = jnp.stack([beta_all[:, GDN_HEADS + h:GDN_HEADS + h + 1] for h in heads])
    gc = jnp.stack([gcum_all[:, h:h + 1] for h in heads])
    gr = jnp.stack([gcum_rows[h:h + 1, :] for h in heads])
    decay = jnp.exp(jnp.where(tri, gc - gr, NEG_BIG))
    kb = k4 * beta
    m = jnp.where(strict, _bmm_nt(kb, k4) * decay, 0.0)
    tinv = _unit_lower_inverse(m)
    eg = jnp.exp(gc)
    s4 = s_ref[...]
    u = _bmm(tinv, v4 * beta)
    w = _bmm(tinv, kb * eg)
    v_new = u - _bmm(w, s4)
    attn = _bmm_nt(q4, k4) * decay
    o = _bmm(q4 * eg, s4) + _bmm(attn, v_new)
    g_last = gc[:, CHUNK - 1:CHUNK, :]
    k_dec = k4 * jnp.exp(g_last - gc)
    s_ref[...] = s4 * jnp.exp(g_last) + jnp.einsum('hck,hcv->hkv', k_dec, v_new,
                                                   preferred_element_type=F32, precision=_HI)
    o = o * lax.rsqrt(jnp.mean(o * o, axis=-1, keepdims=True) + RMS_EPS) * gn_ref[...]
    for h in heads:
        o_ref[0, :, h * GDN_DV:(h + 1) * GDN_DV] = o[h] * _silu(z_ref[0, :, h * GDN_DV:(h + 1) * GDN_DV])

    @pl.when(c == n_c - 1)
    def _():
        sfin_ref[0] = s_ref[...]


def gdn_heads(qkv_pre, z, ab, conv_buf, s0, conv_w, a_log, dt_bias, gnorm, t_valid, name="gdn"):
    b, t, cd = qkv_pre.shape
    assert t % CHUNK == 0
    n_c = t // CHUNK
    pad_l = lambda a: jnp.pad(a.astype(F32), (0, LANES - a.shape[0])).reshape(1, LANES)
    const = lambda *shape: pl.BlockSpec(shape, lambda i, j: (0,) * len(shape))
    return pl.pallas_call(
        functools.partial(_gdn_body, t_valid=t_valid),
        grid=(b, n_c),
        in_specs=[
            pl.BlockSpec((1, CHUNK, cd), lambda i, j: (i, j, 0)),
            pl.BlockSpec((1, CHUNK, GDN_V), lambda i, j: (i, j, 0)),
            pl.BlockSpec((1, CHUNK, LANES), lambda i, j: (i, j, 0)),
            pl.BlockSpec((1, GDN_CONV - 1, cd), lambda i, j: (i, 0, 0)),
            pl.BlockSpec((1, GDN_HEADS, GDN_DK, GDN_DV), lambda i, j: (i, 0, 0, 0)),
            const(GDN_CONV, cd), const(1, LANES), const(1, LANES), const(1, GDN_DV),
        ],
        out_specs=[
            pl.BlockSpec((1, CHUNK, GDN_V), lambda i, j: (i, j, 0)),
            pl.BlockSpec((1, GDN_HEADS, GDN_DK, GDN_DV), lambda i, j: (i, 0, 0, 0)),
        ],
        out_shape=[jax.ShapeDtypeStruct((b, t, GDN_V), F32),
                   jax.ShapeDtypeStruct((b, GDN_HEADS, GDN_DK, GDN_DV), F32)],
        scratch_shapes=[pltpu.VMEM((_CONV_PAD + CHUNK, cd), F32), pltpu.VMEM((GDN_HEADS, GDN_DK, GDN_DV), F32)],
        compiler_params=_cparams("parallel", "arbitrary"),
        name=name,
    )(qkv_pre, z, ab, conv_buf.astype(F32), s0.astype(F32), conv_w.astype(F32), pad_l(a_log), pad_l(dt_bias),
      gnorm.astype(F32).reshape(1, GDN_DV))


def even_mixer(h, g, w_in, w_out, conv_w, a_log, dt_bias, gnorm, sb_k_past, sb_v_past, gdn_s0, conv_buf, t_valid):
    b, t, d = h.shape
    p = sb_k_past.shape[1]
    rows = b * t
    o0 = 3 * SB_W
    w_ab = jnp.pad(w_in[:, o0 + GDN_CONV_DIM + GDN_V:], ((0, 0), (0, LANES - 2 * GDN_HEADS)))
    w_bf = jnp.concatenate([w_in[:, :o0 + GDN_CONV_DIM + GDN_V], w_ab], axis=1).astype(BF16)
    splits = (SB_W, SB_W, SB_W, GDN_CONV_DIM, GDN_V, LANES)
    tk = ROW_TILE
    if p == 0 and b == 1 and t % tk == 0:
        q_bf, k, v, k_bf, vt_bf, qkv_pre, z, ab = norm_proj(
            h.reshape(rows, d), g, w_bf, splits, rows_out=t_valid, name="even_in_proj",
            outs=((0, "bf16"), (1, "f32_rows"), (2, "f32_rows"), (1, "bf16"), (2, "bf16_t"), (3, "f32"), (4, "f32"), (5, "f32")))
        o_sb = sb_attention(q_bf[None], k_bf[None], vt_bf[None], 0)
        k = k[None]
        v = v[None]
    else:
        q, k, v, qkv_pre, z, ab = norm_proj(h.reshape(rows, d), g, w_bf, splits, name="even_in_proj")
        q = q.reshape(b, t, SB_W)
        k = k.reshape(b, t, SB_W)
        v = v.reshape(b, t, SB_W)
        tq_pad = _round_up(t, LANES)
        tk_pad = _round_up(p + t, tk)
        k_all = _pad_rows(jnp.concatenate([sb_k_past.reshape(b, p, SB_W), k], axis=1), tk_pad, 1)
        v_all = _pad_rows(jnp.concatenate([sb_v_past.reshape(b, p, SB_W), v], axis=1), tk_pad, 1)
        k_bf, vt_bf = _kv_layouts(k_all, v_all, tk)
        q_bf = _pad_rows(q, tq_pad, 1).astype(BF16)
        o_sb = sb_attention(q_bf, k_bf, vt_bf, p)[:, :t]
        k = k[:, :t_valid]
        v = v[:, :t_valid]

    t_c = _round_up(t, CHUNK)
    qkv_pre = qkv_pre.reshape(b, t, GDN_CONV_DIM)
    o_gdn, s_fin = gdn_heads(_pad_rows(qkv_pre, t_c, 1), _pad_rows(z.reshape(b, t, GDN_V), t_c, 1),
                             _pad_rows(ab.reshape(b, t, LANES), t_c, 1), conv_buf, gdn_s0,
                             conv_w, a_log, dt_bias, gnorm, t_valid)
    o_gdn = o_gdn[:, :t]
    hist = GDN_CONV - 1
    assert t_valid >= hist
    xp_tail = qkv_pre[:, t_valid - hist:t_valid]
    h_new = out_proj_residual([o_sb.reshape(rows, SB_W), o_gdn.reshape(rows, GDN_V)], h.reshape(rows, d),
                              w_out.astype(BF16), name="even_out_proj")
    return (h_new.reshape(b, t, d), k.reshape(b, t_valid, SB_HEADS, SB_DIM), v.reshape(b, t_valid, SB_HEADS, SB_DIM),
            s_fin, xp_tail)


PEER_HALF = PEER_QDIM // 2
_NSEL = PEER_TOPK + 1
_SUB = 256
_CAND = tuple((a, b) for a in range(_NSEL) for b in range(_NSEL) if (a + 1) * (b + 1) <= _NSEL)
_NCAND = _round_up(len(_CAND), 8)


_SUBLANES = 8


def _sorting_network(n):
    pairs = []

    def merge(lo, m, r):
        step = 2 * r
        if step < m:
            merge(lo, m, step)
            merge(lo + r, m, step)
            pairs.extend((i, i + r) for i in range(lo + r, lo + m - r, step))
        else:
            pairs.append((lo, lo + r))

    def sort(lo, m):
        if m > 1:
            sort(lo, m // 2)
            sort(lo + m // 2, m // 2)
            merge(lo, m, 1)

    sort(0, n)
    return pairs


def _top_values(x, n, out_ref):
    rows, tn = x.shape
    groups = rows // _SUBLANES
    width = 1 << (groups - 1).bit_length()
    minus_inf = jnp.full((_SUBLANES, tn), -jnp.inf, F32)
    lists = [x[r * _SUBLANES:(r + 1) * _SUBLANES, :] for r in range(groups)] + [minus_inf] * (width - groups)
    for i, j in _sorting_network(width):
        lists[i], lists[j] = jnp.maximum(lists[i], lists[j]), jnp.minimum(lists[i], lists[j])
    lists = lists[:groups]
    sub = lax.broadcasted_iota(jnp.int32, (_SUBLANES, 1), 0)
    for it in range(n):
        head = lists[0]
        m = jnp.max(head, axis=0, keepdims=True)
        out_ref[it:it + 1, :] = m
        still_needed = n - it - 1
        if still_needed == 0:
            break
        first = jnp.min(jnp.where(head == m, sub, _SUBLANES), axis=0, keepdims=True)
        won = sub == first
        for r in range(min(groups, still_needed)):
            below = lists[r + 1] if r + 1 < groups else minus_inf
            lists[r] = jnp.where(won, below, lists[r])


def _gelu_tanh(x):
    return 0.5 * x * (1.0 + jnp.tanh(0.7978845608028654 * (x + 0.044715 * (x * x * x))))


def _peer_body(h_ref, g_ref, wq_ref, k1_ref, k2_ref, u_ref, vt_ref, gf_ref, o_ref,
               xn_ref, q_ref, ns1_ref, s2m_ref, e1_ref, e2_ref, t1_ref, t2_ref, cand_ref, csort_ref, acc_ref,
               *, te, final_norm):
    e = pl.program_id(1)
    n_e = pl.num_programs(1)
    tn = h_ref.shape[0]

    @pl.when(e == 0)
    def _prologue():
        x = h_ref[...]
        xn = x * lax.rsqrt(jnp.mean(x * x, axis=-1, keepdims=True) + RMS_EPS) * g_ref[...]
        xb = xn.astype(BF16)
        xn_ref[...] = xb
        q = jnp.dot(xb, wq_ref[...], preferred_element_type=F32)
        for j in range(2 * PEER_HEADS):
            q_ref[j] = q[:, j * PEER_HALF:(j + 1) * PEER_HALF]
        acc_ref[...] = jnp.zeros_like(acc_ref)
        cand_ref[...] = jnp.full(cand_ref.shape, -jnp.inf, F32)

        def per_head(h, _):
            s1 = lax.dot_general(k1_ref[h], q_ref[2 * h], _NT, preferred_element_type=F32, precision=_HI)
            s2 = lax.dot_general(k2_ref[h], q_ref[2 * h + 1], _NT, preferred_element_type=F32, precision=_HI)
            _top_values(s1, _NSEL, t1_ref)
            _top_values(s2, _NSEL, t2_ref)
            for r, (a, b) in enumerate(_CAND):
                cand_ref[r:r + 1, :] = t1_ref[a:a + 1, :] + t2_ref[b:b + 1, :]
            _top_values(cand_ref[...], _NSEL, csort_ref)
            thr = 0.5 * (csort_ref[PEER_TOPK - 1:PEER_TOPK, :] + csort_ref[PEER_TOPK:PEER_TOPK + 1, :])
            s_max = t1_ref[0:1, :] + t2_ref[0:1, :]
            cand = cand_ref[...]
            zsum = jnp.sum(jnp.where(cand >= thr, jnp.exp(cand - s_max), 0.0), axis=0, keepdims=True)
            ns1_ref[h] = -s1
            s2m_ref[h] = s2 - thr
            e1_ref[h] = jnp.exp(s1 - t1_ref[0:1, :]) / zsum
            e2_ref[h] = jnp.exp(s2 - t2_ref[0:1, :]).astype(BF16)
            return 0

        lax.fori_loop(0, PEER_HEADS, per_head, 0)

    xb = xn_ref[...]

    w_tiles = []
    for j in range(te // _SUB):
        r0 = j * _SUB
        a_t = lax.dot_general(u_ref[r0:r0 + _SUB, :], xb, _NT, preferred_element_type=F32)
        gate_rows = []
        for r in range(_SUB // PEER_NKEYS):
            i1 = e * (te // PEER_NKEYS) + j * (_SUB // PEER_NKEYS) + r
            ns1_rows = [ns1_ref[h, pl.ds(i1, 1), :] for h in range(PEER_HEADS)]
            e1_rows = [e1_ref[h, pl.ds(i1, 1), :].astype(BF16) for h in range(PEER_HEADS)]
            tiles = []
            for c0 in range(0, tn, LANES):
                gsum = None
                for h in range(PEER_HEADS):
                    sel = s2m_ref[h, :, c0:c0 + LANES] >= ns1_rows[h][:, c0:c0 + LANES]
                    term = jnp.where(sel, e2_ref[h, :, c0:c0 + LANES] * e1_rows[h][:, c0:c0 + LANES],
                                     jnp.zeros((), BF16))
                    gsum = term if gsum is None else gsum + term
                tiles.append(gsum)
            gate_rows.append(jnp.concatenate(tiles, axis=1))
        w_tiles.append(_gelu_tanh(a_t).astype(BF16) * jnp.concatenate(gate_rows, axis=0))
    acc_ref[...] += jnp.dot(vt_ref[...], jnp.concatenate(w_tiles, axis=0), preferred_element_type=F32)

    @pl.when(e == n_e - 1)
    def _epilogue():
        y = h_ref[...] + acc_ref[...].T
        if final_norm:
            y = y * lax.rsqrt(jnp.mean(y * y, axis=-1, keepdims=True) + RMS_EPS) * gf_ref[...]
        o_ref[...] = y


def peer_residual(h, g, wq, k1, k2, u_bf, vt_bf, final_g=None, tn=640, te=1024, name="peer"):
    m, d = h.shape
    tn = min(tn, m)
    assert m % tn == 0 and N_EXPERTS % te == 0 and te % _SUB == 0
    final_norm = final_g is not None
    gf = (final_g if final_norm else jnp.ones((d,), F32)).astype(F32).reshape(1, d)
    const = lambda *shape: pl.BlockSpec(shape, lambda i, j: (0,) * len(shape))
    big = lambda: pltpu.VMEM((PEER_HEADS, PEER_NKEYS, tn), F32)
    return pl.pallas_call(
        functools.partial(_peer_body, te=te, final_norm=final_norm),
        grid=(m // tn, N_EXPERTS // te),
        in_specs=[
            pl.BlockSpec((tn, d), lambda i, j: (i, 0)),
            const(1, d),
            const(d, PEER_HEADS * PEER_QDIM),
            const(PEER_HEADS, PEER_NKEYS, PEER_HALF),
            const(PEER_HEADS, PEER_NKEYS, PEER_HALF),
            pl.BlockSpec((te, d), lambda i, j: (j, 0)),
            pl.BlockSpec((d, te), lambda i, j: (0, j)),
            const(1, d),
        ],
        out_specs=pl.BlockSpec((tn, d), lambda i, j: (i, 0)),
        out_shape=jax.ShapeDtypeStruct((m, d), F32),
        scratch_shapes=[
            pltpu.VMEM((tn, d), BF16),
            pltpu.VMEM((2 * PEER_HEADS, tn, PEER_HALF), F32),
            big(), big(), big(),
            pltpu.VMEM((PEER_HEADS, PEER_NKEYS, tn), BF16),
            pltpu.VMEM((_round_up(_NSEL, 8), tn), F32),
            pltpu.VMEM((_round_up(_NSEL, 8), tn), F32),
            pltpu.VMEM((_NCAND, tn), F32),
            pltpu.VMEM((_round_up(_NSEL, 8), tn), F32),
            pltpu.VMEM((d, tn), F32),
        ],
        compiler_params=_cparams("parallel", "arbitrary"),
        name=name,
    )(h, g.astype(F32).reshape(1, d), wq.astype(BF16), k1.astype(F32), k2.astype(F32), u_bf, vt_bf, gf)


def _kv_layouts(k_all, v_all, tk):
    b, t, w = k_all.shape
    vt = v_all.astype(BF16).reshape(b, t // tk, tk, w).transpose(0, 1, 3, 2)
    return k_all.astype(BF16), vt


def odd_mixer(h, g, w_in, b_f, w_out, k_past, v_past, logf_past, t_valid):
    b, t, d = h.shape
    p = k_past.shape[1]
    rows = b * t
    w_f = jnp.pad(w_in[:, 3 * FOX_W:], ((0, 0), (0, LANES - FOX_HEADS)))
    w_bf = jnp.concatenate([w_in[:, :3 * FOX_W], w_f], axis=1).astype(BF16)
    bias = jnp.pad(b_f.astype(F32), (0, LANES - FOX_HEADS)).reshape(1, LANES)
    splits = (FOX_W, FOX_W, FOX_W, LANES)
    tk = ROW_TILE
    if p == 0 and b == 1 and t % tk == 0:
        q_bf, k, v, k_bf, vt_bf, logf = norm_proj(
            h.reshape(rows, d), g, w_bf, splits, bias=bias, rows_out=t_valid, name="odd_in_proj",
            outs=((0, "bf16"), (1, "f32_rows"), (2, "f32_rows"), (1, "bf16"), (2, "bf16_t"), (3, "f32")))
        logf = logf[None]
        f_cum = cumsum_rows(logf, name="fox_cumsum")
        o = fox_attention(q_bf[None], k_bf[None], vt_bf[None], f_cum, f_cum[:, :, :FOX_HEADS], 0, t_valid)
        k = k[None]
        v = v[None]
    else:
        q, k, v, logf = norm_proj(h.reshape(rows, d), g, w_bf, splits, bias=bias, name="odd_in_proj")
        q = q.reshape(b, t, FOX_W)
        k = k.reshape(b, t, FOX_W)
        v = v.reshape(b, t, FOX_W)
        logf = logf.reshape(b, t, LANES)
        tq_pad = _round_up(t, LANES)
        tk_pad = _round_up(p + t, tk)
        k_all = _pad_rows(jnp.concatenate([k_past.reshape(b, p, FOX_W), k], axis=1), tk_pad, 1)
        v_all = _pad_rows(jnp.concatenate([v_past.reshape(b, p, FOX_W), v], axis=1), tk_pad, 1)
        logf_past = jnp.pad(logf_past.astype(F32), ((0, 0), (0, 0), (0, LANES - FOX_HEADS)))
        logf_all = _pad_rows(jnp.concatenate([logf_past, logf], axis=1), tk_pad, 1)
        f_cum = cumsum_rows(logf_all, name="fox_cumsum")
        f_q = _pad_rows(f_cum[:, p:p + t, :FOX_HEADS], tq_pad, 1)
        k_bf, vt_bf = _kv_layouts(k_all, v_all, tk)
        q_bf = _pad_rows(q, tq_pad, 1).astype(BF16)
        o = fox_attention(q_bf, k_bf, vt_bf, f_cum, f_q, p, t_valid)[:, :t]
        k = k[:, :t_valid]
        v = v[:, :t_valid]
    h_new = out_proj_residual([o.reshape(rows, FOX_W)], h.reshape(rows, d), w_out.astype(BF16), name="odd_out_proj")
    return (h_new.reshape(b, t, d), k.reshape(b, t_valid, FOX_HEADS, FOX_DIM), v.reshape(b, t_valid, FOX_HEADS, FOX_DIM),
            logf[:, :t_valid, :FOX_HEADS])


def kernel(x_prompt, x_sample, cache_sb_k, cache_sb_v, state_gdn, state_gdn_conv, cache_fox_k, cache_fox_v, cache_fox_logf, meta_tokens, norm_mix, norm_ffn, norm_final, w_in_even, w_out_even, gdn_conv_w, gdn_a_log, gdn_dt_bias, gdn_norm, w_in_odd, b_forget, w_out_odd, peer_wq, peer_k1, peer_k2, peer_u, peer_v):
    bsz, seq, d = x_prompt.shape
    dec_b, dec_t, _ = x_sample.shape
    depth = norm_mix.shape[0]
    dt = x_prompt.dtype
    t_p = N_META + seq
    t_pad = _round_up(t_p, 5 * ROW_TILE)

    meta = jnp.broadcast_to(meta_tokens.astype(dt)[None], (bsz, N_META, d))
    hp = _pad_rows(jnp.concatenate([meta, x_prompt], axis=1), t_pad, 1)
    hs = x_sample

    empty_sb = jnp.zeros((bsz, 0, SB_HEADS, SB_DIM), dt)
    zero_s = jnp.zeros((bsz, GDN_HEADS, GDN_DK, GDN_DV), dt)
    zero_buf = jnp.zeros((bsz, GDN_CONV - 1, GDN_CONV_DIM), dt)
    empty_fox = jnp.zeros((bsz, 0, FOX_HEADS, FOX_DIM), dt)
    empty_logf = jnp.zeros((bsz, 0, FOX_HEADS), dt)

    sbk_p, sbv_p, sbk_s, sbv_s = [], [], [], []
    gs_p, gs_s, gc_p, gc_s = [], [], [], []
    fk_p, fv_p, ff_p, fk_s, fv_s, ff_s = [], [], [], [], [], []

    for layer in range(depth):
        if layer % 2 == 0:
            e = layer // 2
            w = (norm_mix[layer], w_in_even[e], w_out_even[e], gdn_conv_w[e], gdn_a_log[e], gdn_dt_bias[e], gdn_norm[e])
            hp, kp, vp, sp, bp = even_mixer(hp, *w, empty_sb, empty_sb, zero_s, zero_buf, t_p)
            hs, ks_, vs_, ss, bs = even_mixer(hs, *w, cache_sb_k[e], cache_sb_v[e], state_gdn[e], state_gdn_conv[e], dec_t)
            sbk_p.append(kp); sbv_p.append(vp); sbk_s.append(ks_); sbv_s.append(vs_)
            gs_p.append(sp); gs_s.append(ss); gc_p.append(bp); gc_s.append(bs)
        else:
            o = layer // 2
            w = (norm_mix[layer], w_in_odd[o], b_forget[o], w_out_odd[o])
            hp, kp, vp, fp = odd_mixer(hp, *w, empty_fox, empty_fox, empty_logf, t_p)
            hs, ks_, vs_, fs = odd_mixer(hs, *w, cache_fox_k[o], cache_fox_v[o], cache_fox_logf[o], dec_t)
            fk_p.append(kp); fv_p.append(vp); ff_p.append(fp)
            fk_s.append(ks_); fv_s.append(vs_); ff_s.append(fs)
        last = layer == depth - 1
        u_bf = peer_u[layer].astype(BF16)
        vt_bf = peer_v[layer].T.astype(BF16)
        pw = (norm_ffn[layer], peer_wq[layer], peer_k1[layer], peer_k2[layer], u_bf, vt_bf, norm_final if last else None)
        hp = peer_residual(hp.reshape(bsz * t_pad, d), *pw, name="peer_prompt").reshape(bsz, t_pad, d)
        hs = peer_residual(hs.reshape(dec_b * dec_t, d), *pw, name="peer_sample").reshape(dec_b, dec_t, d)

    y_prompt = hp[:, N_META:t_p]
    y_sample = hs
    return (y_prompt, y_sample,
            jnp.stack(sbk_p), jnp.stack(sbv_p), jnp.stack(sbk_s), jnp.stack(sbv_s),
            jnp.stack(gs_p), jnp.stack(gs_s), jnp.stack(gc_p), jnp.stack(gc_s),
            jnp.stack(fk_p), jnp.stack(fv_p), jnp.stack(ff_p),
            jnp.stack(fk_s), jnp.stack(fv_s), jnp.stack(ff_s))
```

```python
import functools

import jax
import jax.numpy as jnp
from jax import lax
from jax.experimental import pallas as pl
from jax.experimental.pallas import tpu as pltpu

F32 = jnp.float32
BF16 = jnp.bfloat16

D_MODEL = 1024
N_META = 16
CHUNK = 64
SB_DIM = 64
SB_HEADS = 8
SB_W = SB_HEADS * SB_DIM
GDN_DK = 128
GDN_DV = 128
GDN_HEADS = 4
GDN_QK = GDN_HEADS * GDN_DK
GDN_V = GDN_HEADS * GDN_DV
GDN_CONV = 4
GDN_CONV_DIM = 2 * GDN_QK + GDN_V
FOX_DIM = 64
FOX_HEADS = 16
FOX_W = FOX_HEADS * FOX_DIM
PEER_HEADS = 8
PEER_NKEYS = 128
PEER_TOPK = 16
PEER_QDIM = 256
N_EXPERTS = PEER_NKEYS ** 2
RMS_EPS = 1e-6

LANES = 128
ROW_TILE = 256
ATT_TQ = 512
VMEM_LIMIT = 56 * 1024 * 1024
NEG_BIG = -1e30

_NT = (((1,), (1,)), ((), ()))


def _cparams(*sem):
    return pltpu.CompilerParams(dimension_semantics=sem, vmem_limit_bytes=VMEM_LIMIT)


def _round_up(n, m):
    return -(-n // m) * m


def _pad_rows(a, rows, axis=0):
    pad = [(0, 0)] * a.ndim
    pad[axis] = (0, rows - a.shape[axis])
    return jnp.pad(a, pad)


def _norm_proj_body(x_ref, g_ref, w_ref, b_ref, *out_refs, offs, outs, logsig_split):
    x = x_ref[...]
    xn = x * lax.rsqrt(jnp.mean(x * x, axis=-1, keepdims=True) + RMS_EPS) * g_ref[...]
    xb = xn.astype(BF16)
    cols = {}
    for (i, kind), o_ref in zip(outs, out_refs):
        if i not in cols:
            y = jnp.dot(xb, w_ref[:, offs[i]:offs[i + 1]], preferred_element_type=F32)
            cols[i] = jax.nn.log_sigmoid(y + b_ref[...]) if i == logsig_split else y
        y = cols[i]
        if kind == "bf16_t":
            o_ref[0] = y.T.astype(BF16)
        else:
            o_ref[...] = y.astype(o_ref.dtype)


def norm_proj(x, g, w_bf, splits, outs=None, bias=None, rows_out=None, name="norm_proj"):
    m, d = x.shape
    n = w_bf.shape[1]
    offs = [0]
    for s in splits:
        offs.append(offs[-1] + s)
    tm = min(ROW_TILE, m)
    assert offs[-1] == n and m % tm == 0
    outs = tuple((i, "f32") for i in range(len(splits))) if outs is None else tuple(outs)
    logsig_split = len(splits) - 1 if bias is not None else -1
    if bias is None:
        bias = jnp.zeros((1, splits[-1]), F32)
    out_specs, out_shape = [], []
    for i, kind in outs:
        s = splits[i]
        if kind == "bf16_t":
            out_specs.append(pl.BlockSpec((1, s, tm), lambda r: (r, 0, 0)))
            out_shape.append(jax.ShapeDtypeStruct((m // tm, s, tm), BF16))
        else:
            rows = rows_out if kind == "f32_rows" else m
            out_specs.append(pl.BlockSpec((tm, s), lambda r: (r, 0)))
            out_shape.append(jax.ShapeDtypeStruct((rows, s), BF16 if kind == "bf16" else F32))
    return pl.pallas_call(
        functools.partial(_norm_proj_body, offs=tuple(offs), outs=outs, logsig_split=logsig_split),
        grid=(m // tm,),
        in_specs=[
            pl.BlockSpec((tm, d), lambda i: (i, 0)),
            pl.BlockSpec((1, d), lambda i: (0, 0)),
            pl.BlockSpec((d, n), lambda i: (0, 0)),
            pl.BlockSpec((1, splits[-1]), lambda i: (0, 0)),
        ],
        out_specs=out_specs,
        out_shape=out_shape,
        compiler_params=_cparams("parallel"),
        name=name,
    )(x, g.reshape(1, d), w_bf, bias)


def _out_proj_body(*refs, n_in, offs):
    a_refs = refs[:n_in]
    h_ref, w_ref, o_ref = refs[n_in:]
    acc = h_ref[...]
    for i, a_ref in enumerate(a_refs):
        acc = acc + jnp.dot(a_ref[...].astype(BF16), w_ref[offs[i]:offs[i + 1], :], preferred_element_type=F32)
    o_ref[...] = acc


def out_proj_residual(parts, h, w_bf, name="out_proj"):
    m, d = h.shape
    offs = [0]
    for a in parts:
        offs.append(offs[-1] + a.shape[1])
    tm = min(ROW_TILE, m)
    assert offs[-1] == w_bf.shape[0] and m % tm == 0
    return pl.pallas_call(
        functools.partial(_out_proj_body, n_in=len(parts), offs=tuple(offs)),
        grid=(m // tm,),
        in_specs=[pl.BlockSpec((tm, a.shape[1]), lambda i: (i, 0)) for a in parts] + [
            pl.BlockSpec((tm, d), lambda i: (i, 0)),
            pl.BlockSpec(w_bf.shape, lambda i: (0, 0)),
        ],
        out_specs=pl.BlockSpec((tm, d), lambda i: (i, 0)),
        out_shape=jax.ShapeDtypeStruct((m, d), F32),
        compiler_params=_cparams("parallel"),
        name=name,
    )(*parts, h, w_bf)


def _cumsum_body(x_ref, o_ref, carry_ref):
    @pl.when(pl.program_id(1) == 0)
    def _():
        carry_ref[...] = jnp.zeros_like(carry_ref)

    x = x_ref[0]
    t = x.shape[0]
    tri = (lax.broadcasted_iota(jnp.int32, (t, t), 0) >= lax.broadcasted_iota(jnp.int32, (t, t), 1)).astype(F32)
    c = jnp.dot(tri, x, preferred_element_type=F32, precision=lax.Precision.HIGHEST) + carry_ref[...]
    o_ref[0] = c
    carry_ref[...] = c[t - 1:t, :]


def cumsum_rows(x, name="cumsum_rows"):
    b, l, c = x.shape
    tm = ROW_TILE
    assert l % tm == 0
    return pl.pallas_call(
        _cumsum_body,
        grid=(b, l // tm),
        in_specs=[pl.BlockSpec((1, tm, c), lambda i, j: (i, j, 0))],
        out_specs=pl.BlockSpec((1, tm, c), lambda i, j: (i, j, 0)),
        out_shape=jax.ShapeDtypeStruct((b, l, c), F32),
        scratch_shapes=[pltpu.VMEM((1, c), F32)],
        compiler_params=_cparams("parallel", "arbitrary"),
        name=name,
    )(x)


def _fox_body(nsteps_ref, q_ref, k_ref, vt_ref, fk_ref, fqt_ref, o_ref, acc_ref, s_ref, p_ref, *, q_offset, tq, tk):
    bi = pl.program_id(0)
    hp = pl.program_id(1)
    qi = pl.program_id(2)
    q = q_ref[0] * jnp.asarray(FOX_DIM ** -0.5, BF16)
    lane = lax.broadcasted_iota(jnp.int32, (1, LANES), 1)
    q_heads = (jnp.where(lane < FOX_DIM, q, jnp.zeros_like(q)), jnp.where(lane >= FOX_DIM, q, jnp.zeros_like(q)))
    q0 = q_offset + qi * tq
    q_pos = q0 + lax.broadcasted_iota(jnp.int32, (1, tq), 1)
    kb_diag = jnp.minimum((q0 + tq - 1) // tk, k_ref.shape[1] // tk - 1)
    n_causal = max(tq // tk, 1)
    fq = [fqt_ref[0, pl.ds(2 * hp + h, 1), :] for h in range(2)]
    acc_ref[...] = jnp.zeros_like(acc_ref)

    def scores(kb):
        k = k_ref[0, pl.ds(pl.multiple_of(kb * tk, tk), tk), :]
        for h in range(2):
            s_ref[h] = lax.dot_general(k, q_heads[h], _NT, preferred_element_type=F32)

    def accumulate(kb, alphas):
        vt = vt_ref[0, kb]
        for h in range(2):
            acc_ref[h] = alphas[h] * acc_ref[h] + jnp.dot(vt, p_ref[h], preferred_element_type=F32)

    def softmax(kb, carry, causal):
        k0 = pl.multiple_of(kb * tk, tk)
        fk_blk = fk_ref[0, pl.ds(k0, tk), :]
        if causal:
            mask = (k0 + lax.broadcasted_iota(jnp.int32, (tk, 1), 0)) <= q_pos
        stats, alphas = [], []
        for h in range(2):
            m_prev, l_prev = carry[2 * h], carry[2 * h + 1]
            lane_h = lax.broadcasted_iota(jnp.int32, (1, LANES), 1) == (2 * hp + h)
            fk = jnp.sum(jnp.where(lane_h, fk_blk, 0.0), axis=1, keepdims=True)
            s = s_ref[h] + fq[h] - fk
            if causal:
                s = jnp.where(mask, s, NEG_BIG)
            m_new = jnp.maximum(m_prev, jnp.max(s, axis=0, keepdims=True))
            p = jnp.exp(s - m_new)
            alpha = jnp.exp(m_prev - m_new)
            p_ref[h] = p.astype(BF16)
            stats += [m_new, alpha * l_prev + jnp.sum(p, axis=0, keepdims=True)]
            alphas.append(alpha)
        return tuple(stats + alphas)

    def step(j, carry):
        kb = kb_diag - j
        accumulate(kb + 1, carry[4:6])
        carry = softmax(kb, carry, causal=False)
        scores(jnp.maximum(kb - 1, 0))
        return carry

    ones = jnp.ones((1, tq), F32)
    init = (jnp.full((1, tq), NEG_BIG, F32), jnp.zeros((1, tq), F32),
            jnp.full((1, tq), NEG_BIG, F32), jnp.zeros((1, tq), F32), ones, ones)
    n_steps = nsteps_ref[bi, hp, qi]
    scores(kb_diag)
    carry = init
    for c in range(n_causal):
        if c > 0:
            accumulate(kb_diag - c + 1, carry[4:6])
        carry = softmax(kb_diag - c, carry, causal=True)
        scores(jnp.maximum(kb_diag - c - 1, 0))
    fin = lax.fori_loop(n_causal, n_steps, step, carry)
    accumulate(kb_diag - (n_steps - 1), fin[4:6])
    row = lax.broadcasted_iota(jnp.int32, (LANES, 1), 0)
    ot = jnp.where(row < FOX_DIM, acc_ref[0] / fin[1], acc_ref[1] / fin[3])
    o_ref[0] = ot.T


_UNDERFLOW = 110.0


def _fox_block_counts(q_bf, k_bf, f_keys, f_q, q_offset, t_valid, tq, tk):
    b, tq_all, w = q_bf.shape
    tk_all = k_bf.shape[1]
    nq, nk = pl.cdiv(tq_all, tq), tk_all // tk
    heads = w // FOX_DIM
    qn = jnp.sqrt(jnp.sum(jnp.square(q_bf.astype(F32)).reshape(b, tq_all, heads, FOX_DIM), axis=-1)) * (FOX_DIM ** -0.5)
    kmax = jnp.max(jnp.sqrt(jnp.sum(jnp.square(k_bf.astype(F32)).reshape(b, tk_all, heads, FOX_DIM), axis=-1)), axis=1)
    bound = 2.0 * qn * kmax[:, None, :] * (1.0 + 1e-3) + f_q
    valid = (jnp.arange(tq_all) < t_valid)[None, :, None]
    bound = jnp.pad(jnp.where(valid, bound, -jnp.inf), ((0, 0), (0, nq * tq - tq_all), (0, 0)),
                    constant_values=-jnp.inf)
    cq = jnp.max(bound.reshape(b, nq, tq, heads), axis=2)
    f_end = f_keys[:, tk - 1::tk, :heads]
    kb_diag = jnp.minimum((q_offset + jnp.arange(nq) * tq + tq - 1) // tk, nk - 1)
    need = (cq[:, :, None, :] - f_end[:, None, :, :]) > -_UNDERFLOW
    need = need & (jnp.arange(nk)[None, None, :, None] <= kb_diag[None, :, None, None])
    first = jnp.min(jnp.where(need, jnp.arange(nk)[None, None, :, None], nk), axis=2)
    first = jnp.min(first.reshape(b, nq, heads // 2, 2), axis=-1)
    most = kb_diag[None, :, None] + 1
    steps = jnp.clip(kb_diag[None, :, None] - first + 1, jnp.minimum(max(tq // tk, 1), most), most)
    return steps.transpose(0, 2, 1).astype(jnp.int32)


def fox_attention(q_bf, k_bf, vt_bf, f_keys, f_q, q_offset, t_valid, name="fox_attention"):
    b, tq_all, w = q_bf.shape
    tk_all = k_bf.shape[1]
    tq = min(ATT_TQ, tq_all)
    tk = ROW_TILE
    assert tk_all % tk == 0
    assert (tq % tk == 0 and q_offset % tk == 0) or (tk % tq == 0 and q_offset % tq == 0)
    nsteps = _fox_block_counts(q_bf, k_bf, f_keys, f_q, q_offset, t_valid, tq, tk)
    f_q_t = f_q.transpose(0, 2, 1)
    return pl.pallas_call(
        functools.partial(_fox_body, q_offset=q_offset, tq=tq, tk=tk),
        grid_spec=pltpu.PrefetchScalarGridSpec(
            num_scalar_prefetch=1,
            grid=(b, w // LANES, pl.cdiv(tq_all, tq)),
            in_specs=[
                pl.BlockSpec((1, tq, LANES), lambda i, h, j, n: (i, j, h)),
                pl.BlockSpec((1, tk_all, LANES), lambda i, h, j, n: (i, 0, h)),
                pl.BlockSpec((1, tk_all // tk, LANES, tk), lambda i, h, j, n: (i, 0, h, 0)),
                pl.BlockSpec((1, tk_all, LANES), lambda i, h, j, n: (i, 0, 0)),
                pl.BlockSpec((1, FOX_HEADS, tq), lambda i, h, j, n: (i, 0, j)),
            ],
            out_specs=pl.BlockSpec((1, tq, LANES), lambda i, h, j, n: (i, j, h)),
            scratch_shapes=[pltpu.VMEM((2, LANES, tq), F32), pltpu.VMEM((2, tk, tq), F32),
                            pltpu.VMEM((2, tk, tq), BF16)],
        ),
        out_shape=jax.ShapeDtypeStruct((b, tq_all, w), F32),
        compiler_params=_cparams("parallel", "parallel", "arbitrary"),
        name=name,
    )(nsteps, q_bf, k_bf, vt_bf, f_keys, f_q_t)


def _sb_body(q_ref, k_ref, vt_ref, o_ref, acc_ref, *, q_offset, tq, tk, tq_all):
    qi = pl.program_id(2)
    q = q_ref[0] * jnp.asarray(SB_DIM ** -0.5, BF16)
    lane = lax.broadcasted_iota(jnp.int32, (1, LANES), 1)
    q_heads = (jnp.where(lane < SB_DIM, q, jnp.zeros_like(q)), jnp.where(lane >= SB_DIM, q, jnp.zeros_like(q)))
    q0 = q_offset + qi * tq
    q_pos = q0 + lax.broadcasted_iota(jnp.int32, (1, tq), 1)
    n_kb = jnp.minimum(jnp.maximum(q0 + tq - 2, 0) // tk + 1, k_ref.shape[1] // tk)
    real_query = q_pos < q_offset + tq_all
    upper = (lax.broadcasted_iota(jnp.int32, (tk, tk), 1) > lax.broadcasted_iota(jnp.int32, (tk, tk), 0)).astype(BF16)
    acc_ref[...] = jnp.zeros_like(acc_ref)

    def step(carry):
        j = carry[0]
        kb = n_kb - 1 - j
        k0 = pl.multiple_of(kb * tk, tk)
        k = k_ref[0, pl.ds(k0, tk), :]
        vt = vt_ref[0, kb]
        k_pos = k0 + lax.broadcasted_iota(jnp.int32, (tk, 1), 0)
        mask = k_pos < q_pos
        out = []
        for h in range(2):
            r_prev = carry[2 + h]
            z = lax.dot_general(k, q_heads[h], _NT, preferred_element_type=F32)
            sp = jnp.maximum(z, 0.0) + jnp.log(1.0 + jnp.exp(-jnp.abs(z)))
            l = jnp.where(mask, -sp, 0.0)
            l_hi = l.astype(BF16)
            l_lo = (l - l_hi.astype(F32)).astype(BF16)
            later = (jnp.dot(upper, l_hi, preferred_element_type=F32)
                     + jnp.dot(upper, l_lo, preferred_element_type=F32))
            w = jnp.where(mask, jnp.exp((z - sp) + later + r_prev), 0.0)
            acc_ref[h] += jnp.dot(vt, w.astype(BF16), preferred_element_type=F32)
            out.append(r_prev + later[0:1, :] + l[0:1, :])
        live = jnp.where(real_query, jnp.maximum(out[0], out[1]), -jnp.inf)
        return (j + 1, jnp.max(live), out[0], out[1])

    def more(carry):
        return (carry[0] < n_kb) & (carry[1] > -_UNDERFLOW)

    lax.while_loop(more, step, (jnp.int32(0), jnp.float32(0.0), jnp.zeros((1, tq), F32), jnp.zeros((1, tq), F32)))
    row = lax.broadcasted_iota(jnp.int32, (LANES, 1), 0)
    o_ref[0] = jnp.where(row < SB_DIM, acc_ref[0], acc_ref[1]).T


def sb_attention(q_bf, k_bf, vt_bf, q_offset, name="sb_attention"):
    b, tq_all, w = q_bf.shape
    tk_all = k_bf.shape[1]
    tq = min(ROW_TILE, tq_all)
    tk = ROW_TILE
    assert tk_all % tk == 0
    return pl.pallas_call(
        functools.partial(_sb_body, q_offset=q_offset, tq=tq, tk=tk, tq_all=tq_all),
        grid=(b, w // LANES, pl.cdiv(tq_all, tq)),
        in_specs=[
            pl.BlockSpec((1, tq, LANES), lambda i, h, j: (i, j, h)),
            pl.BlockSpec((1, tk_all, LANES), lambda i, h, j: (i, 0, h)),
            pl.BlockSpec((1, tk_all // tk, LANES, tk), lambda i, h, j: (i, 0, h, 0)),
        ],
        out_specs=pl.BlockSpec((1, tq, LANES), lambda i, h, j: (i, j, h)),
        out_shape=jax.ShapeDtypeStruct((b, tq_all, w), F32),
        scratch_shapes=[pltpu.VMEM((2, LANES, tq), F32)],
        compiler_params=_cparams("parallel", "parallel", "arbitrary"),
        name=name,
    )(q_bf, k_bf, vt_bf)


_HI = lax.Precision.HIGHEST
_TN = (((0,), (0,)), ((), ()))
_CONV_PAD = 8


def _dot_hi(a, b):
    return jnp.dot(a, b, preferred_element_type=F32, precision=_HI)


def _split_bf16(x):
    hi = x.astype(BF16)
    return hi, (x - hi.astype(F32)).astype(BF16)


def _einsum3(spec, a, b):
    ah, al = a if isinstance(a, tuple) else _split_bf16(a)
    bh, bl = b if isinstance(b, tuple) else _split_bf16(b)
    prod = functools.partial(jnp.einsum, spec, preferred_element_type=F32)
    return prod(ah, bh) + (prod(ah, bl) + prod(al, bh))


def _bmm(a, b):
    return _einsum3('hij,hjk->hik', a, b)


def _bmm_nt(a, b):
    return _einsum3('hik,hjk->hij', a, b)


def _softplus(x):
    return jnp.maximum(x, 0.0) + jnp.log1p(jnp.exp(-jnp.abs(x)))


def _silu(x):
    return x / (1.0 + jnp.exp(-x))


def _unit_lower_inverse(m):
    c_len = m.shape[-1]
    ri = lax.broadcasted_iota(jnp.int32, (c_len, c_len), 0)
    ci = lax.broadcasted_iota(jnp.int32, (c_len, c_len), 1)
    d = jnp.broadcast_to((ri == ci).astype(F32), m.shape)
    s = 1
    while s < c_len:
        join = (ri // (2 * s) == ci // (2 * s)) & (ri % (2 * s) >= s) & (ci % (2 * s) < s)
        c = jnp.where(join, m, 0.0)
        if s == 1:
            d = d - c
        else:
            d_s = _split_bf16(d)
            d = d - _bmm(_bmm(d_s, c), d_s)
        s *= 2
    return d


def _gdn_body(x_ref, z_ref, ab_ref, buf_ref, s0_ref, cw_ref, alog_ref, dt_ref, gn_ref,
              o_ref, sfin_ref, xwin_ref, s_ref, *, t_valid):
    c = pl.program_id(1)
    n_c = pl.num_programs(1)
    hist = GDN_CONV - 1

    @pl.when(c == 0)
    def _():
        xwin_ref[_CONV_PAD - hist:_CONV_PAD, :] = buf_ref[0]
        s_ref[...] = s0_ref[0]

    xwin_ref[_CONV_PAD:_CONV_PAD + CHUNK, :] = x_ref[0]
    conv = xwin_ref[_CONV_PAD - hist:_CONV_PAD - hist + CHUNK, :] * cw_ref[0:1, :]
    for i in range(1, GDN_CONV):
        conv = conv + xwin_ref[_CONV_PAD - hist + i:_CONV_PAD - hist + i + CHUNK, :] * cw_ref[i:i + 1, :]
    tail = xwin_ref[_CONV_PAD + CHUNK - hist:_CONV_PAD + CHUNK, :]
    xwin_ref[_CONV_PAD - hist:_CONV_PAD, :] = tail
    act = _silu(conv)

    ab = ab_ref[0]
    row_ok = (c * CHUNK + lax.broadcasted_iota(jnp.int32, (CHUNK, 1), 0)) < t_valid
    g_all = jnp.where(row_ok, -jnp.exp(alog_ref[...]) * _softplus(ab + dt_ref[...]), 0.0)
    beta_all = jnp.where(row_ok, 1.0 / (1.0 + jnp.exp(-ab)), 0.0)
    ri = lax.broadcasted_iota(jnp.int32, (CHUNK, CHUNK), 0)
    ci = lax.broadcasted_iota(jnp.int32, (CHUNK, CHUNK), 1)
    tri = ri >= ci
    strict = ri > ci
    gcum_all = _dot_hi(tri.astype(F32), g_all)
    sel = (lax.broadcasted_iota(jnp.int32, (8, LANES), 0) == lax.broadcasted_iota(jnp.int32, (8, LANES), 1)).astype(F32)
    gcum_rows = lax.dot_general(sel, gcum_all, _NT, preferred_element_type=F32, precision=_HI)

    heads = range(GDN_HEADS)
    q4 = jnp.stack([act[:, h * GDN_DK:(h + 1) * GDN_DK] for h in heads])
    k4 = jnp.stack([act[:, GDN_QK + h * GDN_DK:GDN_QK + (h + 1) * GDN_DK] for h in heads])
    v4 = jnp.stack([act[:, 2 * GDN_QK + h * GDN_DV:2 * GDN_QK + (h + 1) * GDN_DV] for h in heads])
    q4 = q4 * lax.rsqrt(jnp.sum(q4 * q4, axis=-1, keepdims=True) + RMS_EPS) * (GDN_DK ** -0.5)
    k4 = k4 * lax.rsqrt(jnp.sum(k4 * k4, axis=-1, keepdims=True) + RMS_EPS)
    beta = jnp.stack([beta_all[:, GDN_HEADS + h:GDN_HEADS + h + 1] for h in heads])
    gc = jnp.stack([gcum_all[:, h:h + 1] for h in heads])
    gr = jnp.stack([gcum_rows[h:h + 1, :] for h in heads])
    decay = jnp.exp(jnp.where(tri, gc - gr, NEG_BIG))
    kb = k4 * beta
    k4_s = _split_bf16(k4)
    m = jnp.where(strict, _bmm_nt(kb, k4_s) * decay, 0.0)
    tinv = _split_bf16(_unit_lower_inverse(m))
    eg = jnp.exp(gc)
    s4 = s_ref[...]
    s4_s = _split_bf16(s4)
    u = _bmm(tinv, v4 * beta)
    w = _bmm(tinv, kb * eg)
    v_new = u - _bmm(w, s4_s)
    v_new_s = _split_bf16(v_new)
    attn = _bmm_nt(q4, k4_s) * decay
    o = _bmm(q4 * eg, s4_s) + _bmm(attn, v_new_s)
    g_last = gc[:, CHUNK - 1:CHUNK, :]
    k_dec = k4 * jnp.exp(g_last - gc)
    s_ref[...] = s4 * jnp.exp(g_last) + _einsum3('hck,hcv->hkv', k_dec, v_new_s)
    o = o * lax.rsqrt(jnp.mean(o * o, axis=-1, keepdims=True) + RMS_EPS) * gn_ref[...]
    for h in heads:
        o_ref[0, :, h * GDN_DV:(h + 1) * GDN_DV] = o[h] * _silu(z_ref[0, :, h * GDN_DV:(h + 1) * GDN_DV])

    @pl.when(c == n_c - 1)
    def _():
        sfin_ref[0] = s_ref[...]


def gdn_heads(qkv_pre, z, ab, conv_buf, s0, conv_w, a_log, dt_bias, gnorm, t_valid, name="gdn"):
    b, t, cd = qkv_pre.shape
    assert t % CHUNK == 0
    n_c = t // CHUNK
    pad_l = lambda a: jnp.pad(a.astype(F32), (0, LANES - a.shape[0])).reshape(1, LANES)
    const = lambda *shape: pl.BlockSpec(shape, lambda i, j: (0,) * len(shape))
    return pl.pallas_call(
        functools.partial(_gdn_body, t_valid=t_valid),
        grid=(b, n_c),
        in_specs=[
            pl.BlockSpec((1, CHUNK, cd), lambda i, j: (i, j, 0)),
            pl.BlockSpec((1, CHUNK, GDN_V), lambda i, j: (i, j, 0)),
            pl.BlockSpec((1, CHUNK, LANES), lambda i, j: (i, j, 0)),
            pl.BlockSpec((1, GDN_CONV - 1, cd), lambda i, j: (i, 0, 0)),
            pl.BlockSpec((1, GDN_HEADS, GDN_DK, GDN_DV), lambda i, j: (i, 0, 0, 0)),
            const(GDN_CONV, cd), const(1, LANES), const(1, LANES), const(1, GDN_DV),
        ],
        out_specs=[
            pl.BlockSpec((1, CHUNK, GDN_V), lambda i, j: (i, j, 0)),
            pl.BlockSpec((1, GDN_HEADS, GDN_DK, GDN_DV), lambda i, j: (i, 0, 0, 0)),
        ],
        out_shape=[jax.ShapeDtypeStruct((b, t, GDN_V), F32),
                   jax.ShapeDtypeStruct((b, GDN_HEADS, GDN_DK, GDN_DV), F32)],
        scratch_shapes=[pltpu.VMEM((_CONV_PAD + CHUNK, cd), F32), pltpu.VMEM((GDN_HEADS, GDN_DK, GDN_DV), F32)],
        compiler_params=_cparams("parallel", "arbitrary"),
        name=name,
    )(qkv_pre, z, ab, conv_buf.astype(F32), s0.astype(F32), conv_w.astype(F32), pad_l(a_log), pad_l(dt_bias),
      gnorm.astype(F32).reshape(1, GDN_DV))


def even_mixer(h, g, w_in, w_out, conv_w, a_log, dt_bias, gnorm, sb_k_past, sb_v_past, gdn_s0, conv_buf, t_valid):
    b, t, d = h.shape
    p = sb_k_past.shape[1]
    rows = b * t
    o0 = 3 * SB_W
    w_ab = jnp.pad(w_in[:, o0 + GDN_CONV_DIM + GDN_V:], ((0, 0), (0, LANES - 2 * GDN_HEADS)))
    w_bf = jnp.concatenate([w_in[:, :o0 + GDN_CONV_DIM + GDN_V], w_ab], axis=1).astype(BF16)
    splits = (SB_W, SB_W, SB_W, GDN_CONV_DIM, GDN_V, LANES)
    tk = ROW_TILE
    if p == 0 and b == 1 and t % tk == 0:
        q_bf, k, v, k_bf, vt_bf, qkv_pre, z, ab = norm_proj(
            h.reshape(rows, d), g, w_bf, splits, rows_out=t_valid, name="even_in_proj",
            outs=((0, "bf16"), (1, "f32_rows"), (2, "f32_rows"), (1, "bf16"), (2, "bf16_t"), (3, "f32"), (4, "f32"), (5, "f32")))
        o_sb = sb_attention(q_bf[None], k_bf[None], vt_bf[None], 0)
        k = k[None]
        v = v[None]
    else:
        q, k, v, qkv_pre, z, ab = norm_proj(h.reshape(rows, d), g, w_bf, splits, name="even_in_proj")
        q = q.reshape(b, t, SB_W)
        k = k.reshape(b, t, SB_W)
        v = v.reshape(b, t, SB_W)
        tq_pad = _round_up(t, LANES)
        tk_pad = _round_up(p + t, tk)
        k_all = _pad_rows(jnp.concatenate([sb_k_past.reshape(b, p, SB_W), k], axis=1), tk_pad, 1)
        v_all = _pad_rows(jnp.concatenate([sb_v_past.reshape(b, p, SB_W), v], axis=1), tk_pad, 1)
        k_bf, vt_bf = _kv_layouts(k_all, v_all, tk)
        q_bf = _pad_rows(q, tq_pad, 1).astype(BF16)
        o_sb = sb_attention(q_bf, k_bf, vt_bf, p)[:, :t]
        k = k[:, :t_valid]
        v = v[:, :t_valid]

    t_c = _round_up(t, CHUNK)
    qkv_pre = qkv_pre.reshape(b, t, GDN_CONV_DIM)
    o_gdn, s_fin = gdn_heads(_pad_rows(qkv_pre, t_c, 1), _pad_rows(z.reshape(b, t, GDN_V), t_c, 1),
                             _pad_rows(ab.reshape(b, t, LANES), t_c, 1), conv_buf, gdn_s0,
                             conv_w, a_log, dt_bias, gnorm, t_valid)
    o_gdn = o_gdn[:, :t]
    hist = GDN_CONV - 1
    assert t_valid >= hist
    xp_tail = qkv_pre[:, t_valid - hist:t_valid]
    h_new = out_proj_residual([o_sb.reshape(rows, SB_W), o_gdn.reshape(rows, GDN_V)], h.reshape(rows, d),
                              w_out.astype(BF16), name="even_out_proj")
    return (h_new.reshape(b, t, d), k.reshape(b, t_valid, SB_HEADS, SB_DIM), v.reshape(b, t_valid, SB_HEADS, SB_DIM),
            s_fin, xp_tail)


PEER_HALF = PEER_QDIM // 2
_NSEL = PEER_TOPK + 1
_SUB = 256
_CAND = tuple((a, b) for a in range(_NSEL) for b in range(_NSEL) if (a + 1) * (b + 1) <= _NSEL)
_NCAND = _round_up(len(_CAND), 8)


_SUBLANES = 8


def _sorting_network(n):
    pairs = []

    def merge(lo, m, r):
        step = 2 * r
        if step < m:
            merge(lo, m, step)
            merge(lo + r, m, step)
            pairs.extend((i, i + r) for i in range(lo + r, lo + m - r, step))
        else:
            pairs.append((lo, lo + r))

    def sort(lo, m):
        if m > 1:
            sort(lo, m // 2)
            sort(lo + m // 2, m // 2)
            merge(lo, m, 1)

    sort(0, n)
    return pairs


def _top_values(x, n, out_ref):
    rows, tn = x.shape
    groups = rows // _SUBLANES
    width = 1 << (groups - 1).bit_length()
    minus_inf = jnp.full((_SUBLANES, tn), -jnp.inf, F32)
    lists = [x[r * _SUBLANES:(r + 1) * _SUBLANES, :] for r in range(groups)] + [minus_inf] * (width - groups)
    for i, j in _sorting_network(width):
        lists[i], lists[j] = jnp.maximum(lists[i], lists[j]), jnp.minimum(lists[i], lists[j])
    lists = lists[:groups]
    sub = lax.broadcasted_iota(jnp.int32, (_SUBLANES, 1), 0)
    for it in range(n):
        head = lists[0]
        m = jnp.max(head, axis=0, keepdims=True)
        out_ref[it:it + 1, :] = m
        still_needed = n - it - 1
        if still_needed == 0:
            break
        first = jnp.min(jnp.where(head == m, sub, _SUBLANES), axis=0, keepdims=True)
        won = sub == first
        for r in range(min(groups, still_needed)):
            below = lists[r + 1] if r + 1 < groups else minus_inf
            lists[r] = jnp.where(won, below, lists[r])


def _gelu_tanh(x):
    return 0.5 * x * (1.0 + jnp.tanh(0.7978845608028654 * (x + 0.044715 * (x * x * x))))


def _peer_body(h_ref, g_ref, wq_ref, k1_ref, k2_ref, u_ref, vt_ref, gf_ref, o_ref,
               xn_ref, q_ref, ns1_ref, s2m_ref, e1_ref, e2_ref, t1_ref, t2_ref, cand_ref, csort_ref, acc_ref,
               *, te, final_norm):
    e = pl.program_id(1)
    n_e = pl.num_programs(1)
    tn = h_ref.shape[0]

    @pl.when(e == 0)
    def _prologue():
        x = h_ref[...]
        xn = x * lax.rsqrt(jnp.mean(x * x, axis=-1, keepdims=True) + RMS_EPS) * g_ref[...]
        xb = xn.astype(BF16)
        xn_ref[...] = xb
        q = jnp.dot(xb, wq_ref[...], preferred_element_type=F32)
        for j in range(2 * PEER_HEADS):
            q_ref[j] = q[:, j * PEER_HALF:(j + 1) * PEER_HALF]
        acc_ref[...] = jnp.zeros_like(acc_ref)
        cand_ref[...] = jnp.full(cand_ref.shape, -jnp.inf, F32)

        def per_head(h, _):
            s1 = lax.dot_general(k1_ref[h], q_ref[2 * h], _NT, preferred_element_type=F32)
            s2 = lax.dot_general(k2_ref[h], q_ref[2 * h + 1], _NT, preferred_element_type=F32)
            _top_values(s1, _NSEL, t1_ref)
            _top_values(s2, _NSEL, t2_ref)
            for r, (a, b) in enumerate(_CAND):
                cand_ref[r:r + 1, :] = t1_ref[a:a + 1, :] + t2_ref[b:b + 1, :]
            _top_values(cand_ref[...], _NSEL, csort_ref)
            thr = 0.5 * (csort_ref[PEER_TOPK - 1:PEER_TOPK, :] + csort_ref[PEER_TOPK:PEER_TOPK + 1, :])
            s_max = t1_ref[0:1, :] + t2_ref[0:1, :]
            cand = cand_ref[...]
            zsum = jnp.sum(jnp.where(cand >= thr, jnp.exp(cand - s_max), 0.0), axis=0, keepdims=True)
            ns1_ref[h] = -s1
            s2m_ref[h] = s2 - thr
            e1_ref[h] = jnp.exp(s1 - t1_ref[0:1, :]) / zsum
            e2_ref[h] = jnp.exp(s2 - t2_ref[0:1, :])
            return 0

        lax.fori_loop(0, PEER_HEADS, per_head, 0)

    xb = xn_ref[...]

    w_tiles = []
    for j in range(te // _SUB):
        r0 = j * _SUB
        a_t = lax.dot_general(u_ref[r0:r0 + _SUB, :], xb, _NT, preferred_element_type=F32)
        n_i1 = _SUB // PEER_NKEYS
        i1s = [e * (te // PEER_NKEYS) + j * n_i1 + r for r in range(n_i1)]
        ns1_rows = [[ns1_ref[h, pl.ds(i1, 1), :] for h in range(PEER_HEADS)] for i1 in i1s]
        e1_rows = [[e1_ref[h, pl.ds(i1, 1), :] for h in range(PEER_HEADS)] for i1 in i1s]
        tiles = [[] for _ in i1s]
        for c0 in range(0, tn, LANES):
            gsums = [None] * n_i1
            for h in range(PEER_HEADS):
                s2m_t = s2m_ref[h, :, c0:c0 + LANES]
                e2_t = e2_ref[h, :, c0:c0 + LANES]
                for r in range(n_i1):
                    term = jnp.where(s2m_t >= ns1_rows[r][h][:, c0:c0 + LANES],
                                     e2_t * e1_rows[r][h][:, c0:c0 + LANES], 0.0)
                    gsums[r] = term if gsums[r] is None else gsums[r] + term
            for r in range(n_i1):
                tiles[r].append(gsums[r])
        gates = jnp.concatenate([jnp.concatenate(t, axis=1) for t in tiles], axis=0)
        w_tiles.append((_gelu_tanh(a_t) * gates).astype(BF16))
    acc_ref[...] += jnp.dot(vt_ref[...], jnp.concatenate(w_tiles, axis=0), preferred_element_type=F32)

    @pl.when(e == n_e - 1)
    def _epilogue():
        y = h_ref[...] + acc_ref[...].T
        if final_norm:
            y = y * lax.rsqrt(jnp.mean(y * y, axis=-1, keepdims=True) + RMS_EPS) * gf_ref[...]
        o_ref[...] = y


def peer_residual(h, g, wq, k1, k2, u_bf, vt_bf, final_g=None, tn=640, te=1024, name="peer"):
    m, d = h.shape
    tn = min(tn, m)
    assert m % tn == 0 and N_EXPERTS % te == 0 and te % _SUB == 0
    final_norm = final_g is not None
    gf = (final_g if final_norm else jnp.ones((d,), F32)).astype(F32).reshape(1, d)
    const = lambda *shape: pl.BlockSpec(shape, lambda i, j: (0,) * len(shape))
    big = lambda: pltpu.VMEM((PEER_HEADS, PEER_NKEYS, tn), F32)
    return pl.pallas_call(
        functools.partial(_peer_body, te=te, final_norm=final_norm),
        grid=(m // tn, N_EXPERTS // te),
        in_specs=[
            pl.BlockSpec((tn, d), lambda i, j: (i, 0)),
            const(1, d),
            const(d, PEER_HEADS * PEER_QDIM),
            const(PEER_HEADS, PEER_NKEYS, PEER_HALF),
            const(PEER_HEADS, PEER_NKEYS, PEER_HALF),
            pl.BlockSpec((te, d), lambda i, j: (j, 0)),
            pl.BlockSpec((d, te), lambda i, j: (0, j)),
            const(1, d),
        ],
        out_specs=pl.BlockSpec((tn, d), lambda i, j: (i, 0)),
        out_shape=jax.ShapeDtypeStruct((m, d), F32),
        scratch_shapes=[
            pltpu.VMEM((tn, d), BF16),
            pltpu.VMEM((2 * PEER_HEADS, tn, PEER_HALF), F32),
            big(), big(), big(), big(),
            pltpu.VMEM((_round_up(_NSEL, 8), tn), F32),
            pltpu.VMEM((_round_up(_NSEL, 8), tn), F32),
            pltpu.VMEM((_NCAND, tn), F32),
            pltpu.VMEM((_round_up(_NSEL, 8), tn), F32),
            pltpu.VMEM((d, tn), F32),
        ],
        compiler_params=_cparams("parallel", "arbitrary"),
        name=name,
    )(h, g.astype(F32).reshape(1, d), wq.astype(BF16), k1.astype(F32), k2.astype(F32), u_bf, vt_bf, gf)


def _kv_layouts(k_all, v_all, tk):
    b, t, w = k_all.shape
    vt = v_all.astype(BF16).reshape(b, t // tk, tk, w).transpose(0, 1, 3, 2)
    return k_all.astype(BF16), vt


def odd_mixer(h, g, w_in, b_f, w_out, k_past, v_past, logf_past, t_valid):
    b, t, d = h.shape
    p = k_past.shape[1]
    rows = b * t
    w_f = jnp.pad(w_in[:, 3 * FOX_W:], ((0, 0), (0, LANES - FOX_HEADS)))
    w_bf = jnp.concatenate([w_in[:, :3 * FOX_W], w_f], axis=1).astype(BF16)
    bias = jnp.pad(b_f.astype(F32), (0, LANES - FOX_HEADS)).reshape(1, LANES)
    splits = (FOX_W, FOX_W, FOX_W, LANES)
    tk = ROW_TILE
    if p == 0 and b == 1 and t % tk == 0:
        q_bf, k, v, k_bf, vt_bf, logf = norm_proj(
            h.reshape(rows, d), g, w_bf, splits, bias=bias, rows_out=t_valid, name="odd_in_proj",
            outs=((0, "bf16"), (1, "f32_rows"), (2, "f32_rows"), (1, "bf16"), (2, "bf16_t"), (3, "f32")))
        logf = logf[None]
        f_cum = cumsum_rows(logf, name="fox_cumsum")
        o = fox_attention(q_bf[None], k_bf[None], vt_bf[None], f_cum, f_cum[:, :, :FOX_HEADS], 0, t_valid)
        k = k[None]
        v = v[None]
    else:
        q, k, v, logf = norm_proj(h.reshape(rows, d), g, w_bf, splits, bias=bias, name="odd_in_proj")
        q = q.reshape(b, t, FOX_W)
        k = k.reshape(b, t, FOX_W)
        v = v.reshape(b, t, FOX_W)
        logf = logf.reshape(b, t, LANES)
        tq_pad = _round_up(t, LANES)
        tk_pad = _round_up(p + t, tk)
        k_all = _pad_rows(jnp.concatenate([k_past.reshape(b, p, FOX_W), k], axis=1), tk_pad, 1)
        v_all = _pad_rows(jnp.concatenate([v_past.reshape(b, p, FOX_W), v], axis=1), tk_pad, 1)
        logf_past = jnp.pad(logf_past.astype(F32), ((0, 0), (0, 0), (0, LANES - FOX_HEADS)))
        logf_all = _pad_rows(jnp.concatenate([logf_past, logf], axis=1), tk_pad, 1)
        f_cum = cumsum_rows(logf_all, name="fox_cumsum")
        f_q = _pad_rows(f_cum[:, p:p + t, :FOX_HEADS], tq_pad, 1)
        k_bf, vt_bf = _kv_layouts(k_all, v_all, tk)
        q_bf = _pad_rows(q, tq_pad, 1).astype(BF16)
        o = fox_attention(q_bf, k_bf, vt_bf, f_cum, f_q, p, t_valid)[:, :t]
        k = k[:, :t_valid]
        v = v[:, :t_valid]
    h_new = out_proj_residual([o.reshape(rows, FOX_W)], h.reshape(rows, d), w_out.astype(BF16), name="odd_out_proj")
    return (h_new.reshape(b, t, d), k.reshape(b, t_valid, FOX_HEADS, FOX_DIM), v.reshape(b, t_valid, FOX_HEADS, FOX_DIM),
            logf[:, :t_valid, :FOX_HEADS])


def kernel(x_prompt, x_sample, cache_sb_k, cache_sb_v, state_gdn, state_gdn_conv, cache_fox_k, cache_fox_v, cache_fox_logf, meta_tokens, norm_mix, norm_ffn, norm_final, w_in_even, w_out_even, gdn_conv_w, gdn_a_log, gdn_dt_bias, gdn_norm, w_in_odd, b_forget, w_out_odd, peer_wq, peer_k1, peer_k2, peer_u, peer_v):
    bsz, seq, d = x_prompt.shape
    dec_b, dec_t, _ = x_sample.shape
    depth = norm_mix.shape[0]
    dt = x_prompt.dtype
    t_p = N_META + seq
    t_pad = _round_up(t_p, 5 * ROW_TILE)

    meta = jnp.broadcast_to(meta_tokens.astype(dt)[None], (bsz, N_META, d))
    hp = _pad_rows(jnp.concatenate([meta, x_prompt], axis=1), t_pad, 1)
    hs = x_sample

    empty_sb = jnp.zeros((bsz, 0, SB_HEADS, SB_DIM), dt)
    zero_s = jnp.zeros((bsz, GDN_HEADS, GDN_DK, GDN_DV), dt)
    zero_buf = jnp.zeros((bsz, GDN_CONV - 1, GDN_CONV_DIM), dt)
    empty_fox = jnp.zeros((bsz, 0, FOX_HEADS, FOX_DIM), dt)
    empty_logf = jnp.zeros((bsz, 0, FOX_HEADS), dt)

    sbk_p, sbv_p, sbk_s, sbv_s = [], [], [], []
    gs_p, gs_s, gc_p, gc_s = [], [], [], []
    fk_p, fv_p, ff_p, fk_s, fv_s, ff_s = [], [], [], [], [], []

    for layer in range(depth):
        if layer % 2 == 0:
            e = layer // 2
            w = (norm_mix[layer], w_in_even[e], w_out_even[e], gdn_conv_w[e], gdn_a_log[e], gdn_dt_bias[e], gdn_norm[e])
            hp, kp, vp, sp, bp = even_mixer(hp, *w, empty_sb, empty_sb, zero_s, zero_buf, t_p)
            hs, ks_, vs_, ss, bs = even_mixer(hs, *w, cache_sb_k[e], cache_sb_v[e], state_gdn[e], state_gdn_conv[e], dec_t)
            sbk_p.append(kp); sbv_p.append(vp); sbk_s.append(ks_); sbv_s.append(vs_)
            gs_p.append(sp); gs_s.append(ss); gc_p.append(bp); gc_s.append(bs)
        else:
            o = layer // 2
            w = (norm_mix[layer], w_in_odd[o], b_forget[o], w_out_odd[o])
            hp, kp, vp, fp = odd_mixer(hp, *w, empty_fox, empty_fox, empty_logf, t_p)
            hs, ks_, vs_, fs = odd_mixer(hs, *w, cache_fox_k[o], cache_fox_v[o], cache_fox_logf[o], dec_t)
            fk_p.append(kp); fv_p.append(vp); ff_p.append(fp)
            fk_s.append(ks_); fv_s.append(vs_); ff_s.append(fs)
        last = layer == depth - 1
        u_bf = peer_u[layer].astype(BF16)
        vt_bf = peer_v[layer].T.astype(BF16)
        pw = (norm_ffn[layer], peer_wq[layer], peer_k1[layer], peer_k2[layer], u_bf, vt_bf, norm_final if last else None)
        hp = peer_residual(hp.reshape(bsz * t_pad, d), *pw, name="peer_prompt").reshape(bsz, t_pad, d)
        hs = peer_residual(hs.reshape(dec_b * dec_t, d), *pw, name="peer_sample").reshape(dec_b, dec_t, d)

    y_prompt = hp[:, N_META:t_p]
    y_sample = hs
    return (y_prompt, y_sample,
            jnp.stack(sbk_p), jnp.stack(sbv_p), jnp.stack(sbk_s), jnp.stack(sbv_s),
            jnp.stack(gs_p), jnp.stack(gs_s), jnp.stack(gc_p), jnp.stack(gc_s),
            jnp.stack(fk_p), jnp.stack(fv_p), jnp.stack(ff_p),
            jnp.stack(fk_s), jnp.stack(fv_s), jnp.stack(ff_s))
```

```python
import functools

import jax
import jax.numpy as jnp
from jax import lax
from jax.experimental import pallas as pl
from jax.experimental.pallas import tpu as pltpu

F32 = jnp.float32
BF16 = jnp.bfloat16

D_MODEL = 1024
N_META = 16
CHUNK = 64
SB_DIM = 64
SB_HEADS = 8
SB_W = SB_HEADS * SB_DIM
GDN_DK = 128
GDN_DV = 128
GDN_HEADS = 4
GDN_QK = GDN_HEADS * GDN_DK
GDN_V = GDN_HEADS * GDN_DV
GDN_CONV = 4
GDN_CONV_DIM = 2 * GDN_QK + GDN_V
FOX_DIM = 64
FOX_HEADS = 16
FOX_W = FOX_HEADS * FOX_DIM
PEER_HEADS = 8
PEER_NKEYS = 128
PEER_TOPK = 16
PEER_QDIM = 256
N_EXPERTS = PEER_NKEYS ** 2
RMS_EPS = 1e-6

LANES = 128
ROW_TILE = 256
ATT_TQ = 512
VMEM_LIMIT = 56 * 1024 * 1024
NEG_BIG = -1e30

_NT = (((1,), (1,)), ((), ()))


def _cparams(*sem):
    return pltpu.CompilerParams(dimension_semantics=sem, vmem_limit_bytes=VMEM_LIMIT)


def _round_up(n, m):
    return -(-n // m) * m


def _pad_rows(a, rows, axis=0):
    pad = [(0, 0)] * a.ndim
    pad[axis] = (0, rows - a.shape[axis])
    return jnp.pad(a, pad)


def _norm_proj_body(x_ref, g_ref, w_ref, b_ref, *out_refs, offs, outs, logsig_split):
    x = x_ref[...]
    xn = x * lax.rsqrt(jnp.mean(x * x, axis=-1, keepdims=True) + RMS_EPS) * g_ref[...]
    xb = xn.astype(BF16)
    cols = {}
    for (i, kind), o_ref in zip(outs, out_refs):
        if i not in cols:
            y = jnp.dot(xb, w_ref[:, offs[i]:offs[i + 1]], preferred_element_type=F32)
            cols[i] = jax.nn.log_sigmoid(y + b_ref[...]) if i == logsig_split else y
        y = cols[i]
        if kind == "bf16_t":
            o_ref[0] = y.T.astype(BF16)
        else:
            o_ref[...] = y.astype(o_ref.dtype)


def norm_proj(x, g, w_bf, splits, outs=None, bias=None, rows_out=None, name="norm_proj"):
    m, d = x.shape
    n = w_bf.shape[1]
    offs = [0]
    for s in splits:
        offs.append(offs[-1] + s)
    tm = min(ROW_TILE, m)
    assert offs[-1] == n and m % tm == 0
    outs = tuple((i, "f32") for i in range(len(splits))) if outs is None else tuple(outs)
    logsig_split = len(splits) - 1 if bias is not None else -1
    if bias is None:
        bias = jnp.zeros((1, splits[-1]), F32)
    out_specs, out_shape = [], []
    for i, kind in outs:
        s = splits[i]
        if kind == "bf16_t":
            out_specs.append(pl.BlockSpec((1, s, tm), lambda r: (r, 0, 0)))
            out_shape.append(jax.ShapeDtypeStruct((m // tm, s, tm), BF16))
        else:
            rows = rows_out if kind == "f32_rows" else m
            out_specs.append(pl.BlockSpec((tm, s), lambda r: (r, 0)))
            out_shape.append(jax.ShapeDtypeStruct((rows, s), BF16 if kind == "bf16" else F32))
    return pl.pallas_call(
        functools.partial(_norm_proj_body, offs=tuple(offs), outs=outs, logsig_split=logsig_split),
        grid=(m // tm,),
        in_specs=[
            pl.BlockSpec((tm, d), lambda i: (i, 0)),
            pl.BlockSpec((1, d), lambda i: (0, 0)),
            pl.BlockSpec((d, n), lambda i: (0, 0)),
            pl.BlockSpec((1, splits[-1]), lambda i: (0, 0)),
        ],
        out_specs=out_specs,
        out_shape=out_shape,
        compiler_params=_cparams("parallel"),
        name=name,
    )(x, g.reshape(1, d), w_bf, bias)


def _out_proj_body(*refs, n_in, offs):
    a_refs = refs[:n_in]
    h_ref, w_ref, o_ref = refs[n_in:]
    acc = h_ref[...]
    for i, a_ref in enumerate(a_refs):
        acc = acc + jnp.dot(a_ref[...].astype(BF16), w_ref[offs[i]:offs[i + 1], :], preferred_element_type=F32)
    o_ref[...] = acc


def out_proj_residual(parts, h, w_bf, name="out_proj"):
    m, d = h.shape
    offs = [0]
    for a in parts:
        offs.append(offs[-1] + a.shape[1])
    tm = min(ROW_TILE, m)
    assert offs[-1] == w_bf.shape[0] and m % tm == 0
    return pl.pallas_call(
        functools.partial(_out_proj_body, n_in=len(parts), offs=tuple(offs)),
        grid=(m // tm,),
        in_specs=[pl.BlockSpec((tm, a.shape[1]), lambda i: (i, 0)) for a in parts] + [
            pl.BlockSpec((tm, d), lambda i: (i, 0)),
            pl.BlockSpec(w_bf.shape, lambda i: (0, 0)),
        ],
        out_specs=pl.BlockSpec((tm, d), lambda i: (i, 0)),
        out_shape=jax.ShapeDtypeStruct((m, d), F32),
        compiler_params=_cparams("parallel"),
        name=name,
    )(*parts, h, w_bf)


def _cumsum_body(x_ref, o_ref, carry_ref):
    @pl.when(pl.program_id(1) == 0)
    def _():
        carry_ref[...] = jnp.zeros_like(carry_ref)

    x = x_ref[0]
    t = x.shape[0]
    tri = (lax.broadcasted_iota(jnp.int32, (t, t), 0) >= lax.broadcasted_iota(jnp.int32, (t, t), 1)).astype(F32)
    c = jnp.dot(tri, x, preferred_element_type=F32, precision=lax.Precision.HIGHEST) + carry_ref[...]
    o_ref[0] = c
    carry_ref[...] = c[t - 1:t, :]


def cumsum_rows(x, name="cumsum_rows"):
    b, l, c = x.shape
    tm = ROW_TILE
    assert l % tm == 0
    return pl.pallas_call(
        _cumsum_body,
        grid=(b, l // tm),
        in_specs=[pl.BlockSpec((1, tm, c), lambda i, j: (i, j, 0))],
        out_specs=pl.BlockSpec((1, tm, c), lambda i, j: (i, j, 0)),
        out_shape=jax.ShapeDtypeStruct((b, l, c), F32),
        scratch_shapes=[pltpu.VMEM((1, c), F32)],
        compiler_params=_cparams("parallel", "arbitrary"),
        name=name,
    )(x)


def _fox_body(nsteps_ref, q_ref, k_ref, vt_ref, fk_ref, fqt_ref, o_ref, acc_ref, s_ref, p_ref, *, q_offset, tq, tk):
    bi = pl.program_id(0)
    hp = pl.program_id(1)
    qi = pl.program_id(2)
    q = q_ref[0] * jnp.asarray(FOX_DIM ** -0.5, BF16)
    lane = lax.broadcasted_iota(jnp.int32, (1, LANES), 1)
    q_heads = (jnp.where(lane < FOX_DIM, q, jnp.zeros_like(q)), jnp.where(lane >= FOX_DIM, q, jnp.zeros_like(q)))
    q0 = q_offset + qi * tq
    q_pos = q0 + lax.broadcasted_iota(jnp.int32, (1, tq), 1)
    kb_diag = jnp.minimum((q0 + tq - 1) // tk, k_ref.shape[1] // tk - 1)
    n_causal = max(tq // tk, 1)
    fq = [fqt_ref[0, pl.ds(2 * hp + h, 1), :] for h in range(2)]
    acc_ref[...] = jnp.zeros_like(acc_ref)

    def scores(kb):
        k = k_ref[0, pl.ds(pl.multiple_of(kb * tk, tk), tk), :]
        for h in range(2):
            s_ref[h] = lax.dot_general(k, q_heads[h], _NT, preferred_element_type=F32)

    def accumulate(kb, alphas):
        vt = vt_ref[0, kb]
        for h in range(2):
            acc_ref[h] = alphas[h] * acc_ref[h] + jnp.dot(vt, p_ref[h], preferred_element_type=F32)

    def softmax(kb, carry, causal):
        k0 = pl.multiple_of(kb * tk, tk)
        fk_blk = fk_ref[0, pl.ds(k0, tk), :]
        if causal:
            mask = (k0 + lax.broadcasted_iota(jnp.int32, (tk, 1), 0)) <= q_pos
        stats, alphas = [], []
        for h in range(2):
            m_prev, l_prev = carry[2 * h], carry[2 * h + 1]
            lane_h = lax.broadcasted_iota(jnp.int32, (1, LANES), 1) == (2 * hp + h)
            fk = jnp.sum(jnp.where(lane_h, fk_blk, 0.0), axis=1, keepdims=True)
            s = s_ref[h] + fq[h] - fk
            if causal:
                s = jnp.where(mask, s, NEG_BIG)
            m_new = jnp.maximum(m_prev, jnp.max(s, axis=0, keepdims=True))
            p = jnp.exp(s - m_new)
            alpha = jnp.exp(m_prev - m_new)
            p_ref[h] = p.astype(BF16)
            stats += [m_new, alpha * l_prev + jnp.sum(p, axis=0, keepdims=True)]
            alphas.append(alpha)
        return tuple(stats + alphas)

    def step(j, carry):
        kb = kb_diag - j
        accumulate(kb + 1, carry[4:6])
        carry = softmax(kb, carry, causal=False)
        scores(jnp.maximum(kb - 1, 0))
        return carry

    ones = jnp.ones((1, tq), F32)
    init = (jnp.full((1, tq), NEG_BIG, F32), jnp.zeros((1, tq), F32),
            jnp.full((1, tq), NEG_BIG, F32), jnp.zeros((1, tq), F32), ones, ones)
    n_steps = nsteps_ref[bi, hp, qi]
    scores(kb_diag)
    carry = init
    for c in range(n_causal):
        if c > 0:
            accumulate(kb_diag - c + 1, carry[4:6])
        carry = softmax(kb_diag - c, carry, causal=True)
        scores(jnp.maximum(kb_diag - c - 1, 0))
    fin = lax.fori_loop(n_causal, n_steps, step, carry)
    accumulate(kb_diag - (n_steps - 1), fin[4:6])
    row = lax.broadcasted_iota(jnp.int32, (LANES, 1), 0)
    ot = jnp.where(row < FOX_DIM, acc_ref[0] / fin[1], acc_ref[1] / fin[3])
    o_ref[0] = ot.T


_UNDERFLOW = 110.0


def _fox_block_counts(q_bf, k_bf, f_keys, f_q, q_offset, t_valid, tq, tk):
    b, tq_all, w = q_bf.shape
    tk_all = k_bf.shape[1]
    nq, nk = pl.cdiv(tq_all, tq), tk_all // tk
    heads = w // FOX_DIM
    qn = jnp.sqrt(jnp.sum(jnp.square(q_bf.astype(F32)).reshape(b, tq_all, heads, FOX_DIM), axis=-1)) * (FOX_DIM ** -0.5)
    kmax = jnp.max(jnp.sqrt(jnp.sum(jnp.square(k_bf.astype(F32)).reshape(b, tk_all, heads, FOX_DIM), axis=-1)), axis=1)
    bound = 2.0 * qn * kmax[:, None, :] * (1.0 + 1e-3) + f_q
    valid = (jnp.arange(tq_all) < t_valid)[None, :, None]
    bound = jnp.pad(jnp.where(valid, bound, -jnp.inf), ((0, 0), (0, nq * tq - tq_all), (0, 0)),
                    constant_values=-jnp.inf)
    cq = jnp.max(bound.reshape(b, nq, tq, heads), axis=2)
    f_end = f_keys[:, tk - 1::tk, :heads]
    kb_diag = jnp.minimum((q_offset + jnp.arange(nq) * tq + tq - 1) // tk, nk - 1)
    need = (cq[:, :, None, :] - f_end[:, None, :, :]) > -_UNDERFLOW
    need = need & (jnp.arange(nk)[None, None, :, None] <= kb_diag[None, :, None, None])
    first = jnp.min(jnp.where(need, jnp.arange(nk)[None, None, :, None], nk), axis=2)
    first = jnp.min(first.reshape(b, nq, heads // 2, 2), axis=-1)
    most = kb_diag[None, :, None] + 1
    steps = jnp.clip(kb_diag[None, :, None] - first + 1, jnp.minimum(max(tq // tk, 1), most), most)
    return steps.transpose(0, 2, 1).astype(jnp.int32)


def fox_attention(q_bf, k_bf, vt_bf, f_keys, f_q, q_offset, t_valid, name="fox_attention"):
    b, tq_all, w = q_bf.shape
    tk_all = k_bf.shape[1]
    tq = min(ATT_TQ, tq_all)
    tk = ROW_TILE
    assert tk_all % tk == 0
    assert (tq % tk == 0 and q_offset % tk == 0) or (tk % tq == 0 and q_offset % tq == 0)
    nsteps = _fox_block_counts(q_bf, k_bf, f_keys, f_q, q_offset, t_valid, tq, tk)
    f_q_t = f_q.transpose(0, 2, 1)
    return pl.pallas_call(
        functools.partial(_fox_body, q_offset=q_offset, tq=tq, tk=tk),
        grid_spec=pltpu.PrefetchScalarGridSpec(
            num_scalar_prefetch=1,
            grid=(b, w // LANES, pl.cdiv(tq_all, tq)),
            in_specs=[
                pl.BlockSpec((1, tq, LANES), lambda i, h, j, n: (i, j, h)),
                pl.BlockSpec((1, tk_all, LANES), lambda i, h, j, n: (i, 0, h)),
                pl.BlockSpec((1, tk_all // tk, LANES, tk), lambda i, h, j, n: (i, 0, h, 0)),
                pl.BlockSpec((1, tk_all, LANES), lambda i, h, j, n: (i, 0, 0)),
                pl.BlockSpec((1, FOX_HEADS, tq), lambda i, h, j, n: (i, 0, j)),
            ],
            out_specs=pl.BlockSpec((1, tq, LANES), lambda i, h, j, n: (i, j, h)),
            scratch_shapes=[pltpu.VMEM((2, LANES, tq), F32), pltpu.VMEM((2, tk, tq), F32),
                            pltpu.VMEM((2, tk, tq), BF16)],
        ),
        out_shape=jax.ShapeDtypeStruct((b, tq_all, w), F32),
        compiler_params=_cparams("parallel", "parallel", "arbitrary"),
        name=name,
    )(nsteps, q_bf, k_bf, vt_bf, f_keys, f_q_t)


def _sb_body(q_ref, k_ref, vt_ref, o_ref, acc_ref, *, q_offset, tq, tk, tq_all):
    qi = pl.program_id(2)
    q = q_ref[0] * jnp.asarray(SB_DIM ** -0.5, BF16)
    lane = lax.broadcasted_iota(jnp.int32, (1, LANES), 1)
    q_heads = (jnp.where(lane < SB_DIM, q, jnp.zeros_like(q)), jnp.where(lane >= SB_DIM, q, jnp.zeros_like(q)))
    q0 = q_offset + qi * tq
    q_pos = q0 + lax.broadcasted_iota(jnp.int32, (1, tq), 1)
    n_kb = jnp.minimum(jnp.maximum(q0 + tq - 2, 0) // tk + 1, k_ref.shape[1] // tk)
    real_query = q_pos < q_offset + tq_all
    upper = (lax.broadcasted_iota(jnp.int32, (tk, tk), 1) > lax.broadcasted_iota(jnp.int32, (tk, tk), 0)).astype(BF16)
    acc_ref[...] = jnp.zeros_like(acc_ref)

    def step(carry):
        j = carry[0]
        kb = n_kb - 1 - j
        k0 = pl.multiple_of(kb * tk, tk)
        k = k_ref[0, pl.ds(k0, tk), :]
        vt = vt_ref[0, kb]
        k_pos = k0 + lax.broadcasted_iota(jnp.int32, (tk, 1), 0)
        mask = k_pos < q_pos
        out = []
        for h in range(2):
            r_prev = carry[2 + h]
            z = lax.dot_general(k, q_heads[h], _NT, preferred_element_type=F32)
            sp = jnp.maximum(z, 0.0) + jnp.log(1.0 + jnp.exp(-jnp.abs(z)))
            l = jnp.where(mask, -sp, 0.0)
            l_hi = l.astype(BF16)
            l_lo = (l - l_hi.astype(F32)).astype(BF16)
            later = (jnp.dot(upper, l_hi, preferred_element_type=F32)
                     + jnp.dot(upper, l_lo, preferred_element_type=F32))
            w = jnp.where(mask, jnp.exp((z - sp) + later + r_prev), 0.0)
            acc_ref[h] += jnp.dot(vt, w.astype(BF16), preferred_element_type=F32)
            out.append(r_prev + later[0:1, :] + l[0:1, :])
        live = jnp.where(real_query, jnp.maximum(out[0], out[1]), -jnp.inf)
        return (j + 1, jnp.max(live), out[0], out[1])

    def more(carry):
        return (carry[0] < n_kb) & (carry[1] > -_UNDERFLOW)

    lax.while_loop(more, step, (jnp.int32(0), jnp.float32(0.0), jnp.zeros((1, tq), F32), jnp.zeros((1, tq), F32)))
    row = lax.broadcasted_iota(jnp.int32, (LANES, 1), 0)
    o_ref[0] = jnp.where(row < SB_DIM, acc_ref[0], acc_ref[1]).T


def sb_attention(q_bf, k_bf, vt_bf, q_offset, name="sb_attention"):
    b, tq_all, w = q_bf.shape
    tk_all = k_bf.shape[1]
    tq = min(ROW_TILE, tq_all)
    tk = ROW_TILE
    assert tk_all % tk == 0
    return pl.pallas_call(
        functools.partial(_sb_body, q_offset=q_offset, tq=tq, tk=tk, tq_all=tq_all),
        grid=(b, w // LANES, pl.cdiv(tq_all, tq)),
        in_specs=[
            pl.BlockSpec((1, tq, LANES), lambda i, h, j: (i, j, h)),
            pl.BlockSpec((1, tk_all, LANES), lambda i, h, j: (i, 0, h)),
            pl.BlockSpec((1, tk_all // tk, LANES, tk), lambda i, h, j: (i, 0, h, 0)),
        ],
        out_specs=pl.BlockSpec((1, tq, LANES), lambda i, h, j: (i, j, h)),
        out_shape=jax.ShapeDtypeStruct((b, tq_all, w), F32),
        scratch_shapes=[pltpu.VMEM((2, LANES, tq), F32)],
        compiler_params=_cparams("parallel", "parallel", "arbitrary"),
        name=name,
    )(q_bf, k_bf, vt_bf)


_HI = lax.Precision.HIGHEST
_TN = (((0,), (0,)), ((), ()))
_CONV_PAD = 8
_GDN_CPS = 4
_GDN_ROWS = _GDN_CPS * CHUNK


def _dot_hi(a, b):
    return jnp.dot(a, b, preferred_element_type=F32, precision=_HI)


def _split_bf16(x):
    hi = x.astype(BF16)
    return hi, (x - hi.astype(F32)).astype(BF16)


def _einsum3(spec, a, b):
    ah, al = a if isinstance(a, tuple) else _split_bf16(a)
    bh, bl = b if isinstance(b, tuple) else _split_bf16(b)
    prod = functools.partial(jnp.einsum, spec, preferred_element_type=F32)
    return prod(ah, bh) + (prod(ah, bl) + prod(al, bh))


def _bmm(a, b):
    return _einsum3('hij,hjk->hik', a, b)


def _bmm_nt(a, b):
    return _einsum3('hik,hjk->hij', a, b)


def _softplus(x):
    return jnp.maximum(x, 0.0) + jnp.log1p(jnp.exp(-jnp.abs(x)))


def _silu(x):
    return x / (1.0 + jnp.exp(-x))


def _unit_lower_inverse(m):
    c_len = m.shape[-1]
    ri = lax.broadcasted_iota(jnp.int32, (c_len, c_len), 0)
    ci = lax.broadcasted_iota(jnp.int32, (c_len, c_len), 1)
    d = jnp.broadcast_to((ri == ci).astype(F32), m.shape)
    s = 1
    while s < c_len:
        join = (ri // (2 * s) == ci // (2 * s)) & (ri % (2 * s) >= s) & (ci % (2 * s) < s)
        c = jnp.where(join, m, 0.0)
        if s == 1:
            d = d - c
        else:
            d_s = _split_bf16(d)
            d = d - _bmm(_bmm(d_s, c), d_s)
        s *= 2
    return d


def _gdn_body(x_ref, z_ref, ab_ref, buf_ref, s0_ref, cw_ref, alog_ref, dt_ref, gn_ref,
              o_ref, sfin_ref, xwin_ref, s_ref, *, t_valid):
    c = pl.program_id(1)
    n_c = pl.num_programs(1)
    hist = GDN_CONV - 1

    @pl.when(c == 0)
    def _():
        xwin_ref[_CONV_PAD - hist:_CONV_PAD, :] = buf_ref[0]
        s_ref[...] = s0_ref[0]

    rows = _GDN_ROWS
    xwin_ref[_CONV_PAD:_CONV_PAD + rows, :] = x_ref[0]
    conv = xwin_ref[_CONV_PAD - hist:_CONV_PAD - hist + rows, :] * cw_ref[0:1, :]
    for i in range(1, GDN_CONV):
        conv = conv + xwin_ref[_CONV_PAD - hist + i:_CONV_PAD - hist + i + rows, :] * cw_ref[i:i + 1, :]
    tail = xwin_ref[_CONV_PAD + rows - hist:_CONV_PAD + rows, :]
    xwin_ref[_CONV_PAD - hist:_CONV_PAD, :] = tail
    act = _silu(conv)

    ab = ab_ref[0]
    row_ok = (c * rows + lax.broadcasted_iota(jnp.int32, (rows, 1), 0)) < t_valid
    g_all = jnp.where(row_ok, -jnp.exp(alog_ref[...]) * _softplus(ab + dt_ref[...]), 0.0)
    beta_all = jnp.where(row_ok, 1.0 / (1.0 + jnp.exp(-ab)), 0.0)
    rr = lax.broadcasted_iota(jnp.int32, (rows, rows), 0)
    rc = lax.broadcasted_iota(jnp.int32, (rows, rows), 1)
    tri_chunks = ((rr >= rc) & (rr // CHUNK == rc // CHUNK)).astype(F32)
    gcum_all = _dot_hi(tri_chunks, g_all)
    sel = (lax.broadcasted_iota(jnp.int32, (8, LANES), 0) == lax.broadcasted_iota(jnp.int32, (8, LANES), 1)).astype(F32)
    gcum_rows = lax.dot_general(sel, gcum_all, _NT, preferred_element_type=F32, precision=_HI)
    ri = lax.broadcasted_iota(jnp.int32, (CHUNK, CHUNK), 0)
    ci = lax.broadcasted_iota(jnp.int32, (CHUNK, CHUNK), 1)
    tri = ri >= ci
    strict = ri > ci

    pairs = [(ck, h) for ck in range(_GDN_CPS) for h in range(GDN_HEADS)]
    rows_of = lambda ck: slice(ck * CHUNK, (ck + 1) * CHUNK)
    q4 = jnp.stack([act[rows_of(ck), h * GDN_DK:(h + 1) * GDN_DK] for ck, h in pairs])
    k4 = jnp.stack([act[rows_of(ck), GDN_QK + h * GDN_DK:GDN_QK + (h + 1) * GDN_DK] for ck, h in pairs])
    v4 = jnp.stack([act[rows_of(ck), 2 * GDN_QK + h * GDN_DV:2 * GDN_QK + (h + 1) * GDN_DV] for ck, h in pairs])
    q4 = q4 * lax.rsqrt(jnp.sum(q4 * q4, axis=-1, keepdims=True) + RMS_EPS) * (GDN_DK ** -0.5)
    k4 = k4 * lax.rsqrt(jnp.sum(k4 * k4, axis=-1, keepdims=True) + RMS_EPS)
    beta = jnp.stack([beta_all[rows_of(ck), GDN_HEADS + h:GDN_HEADS + h + 1] for ck, h in pairs])
    gc = jnp.stack([gcum_all[rows_of(ck), h:h + 1] for ck, h in pairs])
    gr = jnp.stack([gcum_rows[h:h + 1, rows_of(ck)] for ck, h in pairs])
    decay = jnp.exp(jnp.where(tri, gc - gr, NEG_BIG))
    kb = k4 * beta
    k4_s = _split_bf16(k4)
    m = jnp.where(strict, _bmm_nt(kb, k4_s) * decay, 0.0)
    tinv = _split_bf16(_unit_lower_inverse(m))
    eg = jnp.exp(gc)
    u = _bmm(tinv, v4 * beta)
    w = _split_bf16(_bmm(tinv, kb * eg))
    attn = _split_bf16(_bmm_nt(q4, k4_s) * decay)
    qe = _split_bf16(q4 * eg)
    g_last = gc[:, CHUNK - 1:CHUNK, :]
    k_dec = _split_bf16(k4 * jnp.exp(g_last - gc))
    s_scale = jnp.exp(g_last)

    s4 = s_ref[...]
    outs = []
    for ck in range(_GDN_CPS):
        sl = slice(ck * GDN_HEADS, (ck + 1) * GDN_HEADS)
        part = lambda pair: (pair[0][sl], pair[1][sl])
        s4_s = _split_bf16(s4)
        v_new_s = _split_bf16(u[sl] - _bmm(part(w), s4_s))
        outs.append(_bmm(part(qe), s4_s) + _bmm(part(attn), v_new_s))
        s4 = s4 * s_scale[sl] + _einsum3('hck,hcv->hkv', part(k_dec), v_new_s)
    s_ref[...] = s4
    for ck in range(_GDN_CPS):
        o = outs[ck]
        o = o * lax.rsqrt(jnp.mean(o * o, axis=-1, keepdims=True) + RMS_EPS) * gn_ref[...]
        for h in range(GDN_HEADS):
            cols = slice(h * GDN_DV, (h + 1) * GDN_DV)
            o_ref[0, rows_of(ck), cols] = o[h] * _silu(z_ref[0, rows_of(ck), cols])

    @pl.when(c == n_c - 1)
    def _():
        sfin_ref[0] = s_ref[...]


def gdn_heads(qkv_pre, z, ab, conv_buf, s0, conv_w, a_log, dt_bias, gnorm, t_valid, name="gdn"):
    b, t, cd = qkv_pre.shape
    rows = _GDN_ROWS
    assert t % rows == 0
    n_c = t // rows
    pad_l = lambda a: jnp.pad(a.astype(F32), (0, LANES - a.shape[0])).reshape(1, LANES)
    const = lambda *shape: pl.BlockSpec(shape, lambda i, j: (0,) * len(shape))
    return pl.pallas_call(
        functools.partial(_gdn_body, t_valid=t_valid),
        grid=(b, n_c),
        in_specs=[
            pl.BlockSpec((1, rows, cd), lambda i, j: (i, j, 0)),
            pl.BlockSpec((1, rows, GDN_V), lambda i, j: (i, j, 0)),
            pl.BlockSpec((1, rows, LANES), lambda i, j: (i, j, 0)),
            pl.BlockSpec((1, GDN_CONV - 1, cd), lambda i, j: (i, 0, 0)),
            pl.BlockSpec((1, GDN_HEADS, GDN_DK, GDN_DV), lambda i, j: (i, 0, 0, 0)),
            const(GDN_CONV, cd), const(1, LANES), const(1, LANES), const(1, GDN_DV),
        ],
        out_specs=[
            pl.BlockSpec((1, rows, GDN_V), lambda i, j: (i, j, 0)),
            pl.BlockSpec((1, GDN_HEADS, GDN_DK, GDN_DV), lambda i, j: (i, 0, 0, 0)),
        ],
        out_shape=[jax.ShapeDtypeStruct((b, t, GDN_V), F32),
                   jax.ShapeDtypeStruct((b, GDN_HEADS, GDN_DK, GDN_DV), F32)],
        scratch_shapes=[pltpu.VMEM((_CONV_PAD + rows, cd), F32), pltpu.VMEM((GDN_HEADS, GDN_DK, GDN_DV), F32)],
        compiler_params=_cparams("parallel", "arbitrary"),
        name=name,
    )(qkv_pre, z, ab, conv_buf.astype(F32), s0.astype(F32), conv_w.astype(F32), pad_l(a_log), pad_l(dt_bias),
      gnorm.astype(F32).reshape(1, GDN_DV))


def even_mixer(h, g, w_in, w_out, conv_w, a_log, dt_bias, gnorm, sb_k_past, sb_v_past, gdn_s0, conv_buf, t_valid):
    b, t, d = h.shape
    p = sb_k_past.shape[1]
    rows = b * t
    o0 = 3 * SB_W
    w_ab = jnp.pad(w_in[:, o0 + GDN_CONV_DIM + GDN_V:], ((0, 0), (0, LANES - 2 * GDN_HEADS)))
    w_bf = jnp.concatenate([w_in[:, :o0 + GDN_CONV_DIM + GDN_V], w_ab], axis=1).astype(BF16)
    splits = (SB_W, SB_W, SB_W, GDN_CONV_DIM, GDN_V, LANES)
    tk = ROW_TILE
    if p == 0 and b == 1 and t % tk == 0:
        q_bf, k, v, k_bf, vt_bf, qkv_pre, z, ab = norm_proj(
            h.reshape(rows, d), g, w_bf, splits, rows_out=t_valid, name="even_in_proj",
            outs=((0, "bf16"), (1, "f32_rows"), (2, "f32_rows"), (1, "bf16"), (2, "bf16_t"), (3, "f32"), (4, "f32"), (5, "f32")))
        o_sb = sb_attention(q_bf[None], k_bf[None], vt_bf[None], 0)
        k = k[None]
        v = v[None]
    else:
        q, k, v, qkv_pre, z, ab = norm_proj(h.reshape(rows, d), g, w_bf, splits, name="even_in_proj")
        q = q.reshape(b, t, SB_W)
        k = k.reshape(b, t, SB_W)
        v = v.reshape(b, t, SB_W)
        tq_pad = _round_up(t, LANES)
        tk_pad = _round_up(p + t, tk)
        k_all = _pad_rows(jnp.concatenate([sb_k_past.reshape(b, p, SB_W), k], axis=1), tk_pad, 1)
        v_all = _pad_rows(jnp.concatenate([sb_v_past.reshape(b, p, SB_W), v], axis=1), tk_pad, 1)
        k_bf, vt_bf = _kv_layouts(k_all, v_all, tk)
        q_bf = _pad_rows(q, tq_pad, 1).astype(BF16)
        o_sb = sb_attention(q_bf, k_bf, vt_bf, p)[:, :t]
        k = k[:, :t_valid]
        v = v[:, :t_valid]

    t_c = _round_up(t, _GDN_ROWS)
    qkv_pre = qkv_pre.reshape(b, t, GDN_CONV_DIM)
    o_gdn, s_fin = gdn_heads(_pad_rows(qkv_pre, t_c, 1), _pad_rows(z.reshape(b, t, GDN_V), t_c, 1),
                             _pad_rows(ab.reshape(b, t, LANES), t_c, 1), conv_buf, gdn_s0,
                             conv_w, a_log, dt_bias, gnorm, t_valid)
    o_gdn = o_gdn[:, :t]
    hist = GDN_CONV - 1
    assert t_valid >= hist
    xp_tail = qkv_pre[:, t_valid - hist:t_valid]
    h_new = out_proj_residual([o_sb.reshape(rows, SB_W), o_gdn.reshape(rows, GDN_V)], h.reshape(rows, d),
                              w_out.astype(BF16), name="even_out_proj")
    return (h_new.reshape(b, t, d), k.reshape(b, t_valid, SB_HEADS, SB_DIM), v.reshape(b, t_valid, SB_HEADS, SB_DIM),
            s_fin, xp_tail)


PEER_HALF = PEER_QDIM // 2
_NSEL = PEER_TOPK + 1
_SUB = 256
_CAND = tuple((a, b) for a in range(_NSEL) for b in range(_NSEL) if (a + 1) * (b + 1) <= _NSEL)
_NCAND = _round_up(len(_CAND), 8)


_SUBLANES = 8


def _sorting_network(n):
    pairs = []

    def merge(lo, m, r):
        step = 2 * r
        if step < m:
            merge(lo, m, step)
            merge(lo + r, m, step)
            pairs.extend((i, i + r) for i in range(lo + r, lo + m - r, step))
        else:
            pairs.append((lo, lo + r))

    def sort(lo, m):
        if m > 1:
            sort(lo, m // 2)
            sort(lo + m // 2, m // 2)
            merge(lo, m, 1)

    sort(0, n)
    return pairs


def _top_values(x, n, out_ref):
    rows, tn = x.shape
    groups = rows // _SUBLANES
    width = 1 << (groups - 1).bit_length()
    minus_inf = jnp.full((_SUBLANES, tn), -jnp.inf, F32)
    lists = [x[r * _SUBLANES:(r + 1) * _SUBLANES, :] for r in range(groups)] + [minus_inf] * (width - groups)
    for i, j in _sorting_network(width):
        lists[i], lists[j] = jnp.maximum(lists[i], lists[j]), jnp.minimum(lists[i], lists[j])
    lists = lists[:groups]
    sub = lax.broadcasted_iota(jnp.int32, (_SUBLANES, 1), 0)
    for it in range(n):
        head = lists[0]
        m = jnp.max(head, axis=0, keepdims=True)
        out_ref[it:it + 1, :] = m
        still_needed = n - it - 1
        if still_needed == 0:
            break
        first = jnp.min(jnp.where(head == m, sub, _SUBLANES), axis=0, keepdims=True)
        won = sub == first
        for r in range(min(groups, still_needed)):
            below = lists[r + 1] if r + 1 < groups else minus_inf
            lists[r] = jnp.where(won, below, lists[r])


def _gelu_tanh(x):
    return 0.5 * x * (1.0 + jnp.tanh(0.7978845608028654 * (x + 0.044715 * (x * x * x))))


def _peer_body(h_ref, g_ref, wq_ref, k1_ref, k2_ref, u_ref, vt_ref, gf_ref, o_ref,
               xn_ref, q_ref, ns1_ref, s2m_ref, e1_ref, e2_ref, t1_ref, t2_ref, cand_ref, csort_ref, acc_ref,
               *, te, final_norm):
    e = pl.program_id(1)
    n_e = pl.num_programs(1)
    tn = h_ref.shape[0]

    @pl.when(e == 0)
    def _prologue():
        x = h_ref[...]
        xn = x * lax.rsqrt(jnp.mean(x * x, axis=-1, keepdims=True) + RMS_EPS) * g_ref[...]
        xb = xn.astype(BF16)
        xn_ref[...] = xb
        q = jnp.dot(xb, wq_ref[...], preferred_element_type=F32)
        for j in range(2 * PEER_HEADS):
            q_ref[j] = q[:, j * PEER_HALF:(j + 1) * PEER_HALF]
        acc_ref[...] = jnp.zeros_like(acc_ref)
        cand_ref[...] = jnp.full(cand_ref.shape, -jnp.inf, F32)

        def per_head(h, _):
            s1 = lax.dot_general(k1_ref[h], q_ref[2 * h], _NT, preferred_element_type=F32)
            s2 = lax.dot_general(k2_ref[h], q_ref[2 * h + 1], _NT, preferred_element_type=F32)
            _top_values(s1, _NSEL, t1_ref)
            _top_values(s2, _NSEL, t2_ref)
            for r, (a, b) in enumerate(_CAND):
                cand_ref[r:r + 1, :] = t1_ref[a:a + 1, :] + t2_ref[b:b + 1, :]
            _top_values(cand_ref[...], _NSEL, csort_ref)
            thr = 0.5 * (csort_ref[PEER_TOPK - 1:PEER_TOPK, :] + csort_ref[PEER_TOPK:PEER_TOPK + 1, :])
            s_max = t1_ref[0:1, :] + t2_ref[0:1, :]
            cand = cand_ref[...]
            zsum = jnp.sum(jnp.where(cand >= thr, jnp.exp(cand - s_max), 0.0), axis=0, keepdims=True)
            ns1_ref[h] = -s1
            s2m_ref[h] = s2 - thr
            e1_ref[h] = jnp.exp(s1 - t1_ref[0:1, :]) / zsum
            e2_ref[h] = jnp.exp(s2 - t2_ref[0:1, :])
            return 0

        lax.fori_loop(0, PEER_HEADS, per_head, 0)

    xb = xn_ref[...]

    w_tiles = []
    for j in range(te // _SUB):
        r0 = j * _SUB
        a_t = lax.dot_general(u_ref[r0:r0 + _SUB, :], xb, _NT, preferred_element_type=F32)
        n_i1 = _SUB // PEER_NKEYS
        i1s = [e * (te // PEER_NKEYS) + j * n_i1 + r for r in range(n_i1)]
        ns1_rows = [[ns1_ref[h, pl.ds(i1, 1), :] for h in range(PEER_HEADS)] for i1 in i1s]
        e1_rows = [[e1_ref[h, pl.ds(i1, 1), :] for h in range(PEER_HEADS)] for i1 in i1s]
        tiles = [[] for _ in i1s]
        for c0 in range(0, tn, LANES):
            gsums = [None] * n_i1
            for h in range(PEER_HEADS):
                s2m_t = s2m_ref[h, :, c0:c0 + LANES]
                e2_t = e2_ref[h, :, c0:c0 + LANES]
                for r in range(n_i1):
                    term = jnp.where(s2m_t >= ns1_rows[r][h][:, c0:c0 + LANES],
                                     e2_t * e1_rows[r][h][:, c0:c0 + LANES], 0.0)
                    gsums[r] = term if gsums[r] is None else gsums[r] + term
            for r in range(n_i1):
                tiles[r].append(gsums[r])
        gates = jnp.concatenate([jnp.concatenate(t, axis=1) for t in tiles], axis=0)
        w_tiles.append((_gelu_tanh(a_t) * gates).astype(BF16))
    acc_ref[...] += jnp.dot(vt_ref[...], jnp.concatenate(w_tiles, axis=0), preferred_element_type=F32)

    @pl.when(e == n_e - 1)
    def _epilogue():
        y = h_ref[...] + acc_ref[...].T
        if final_norm:
            y = y * lax.rsqrt(jnp.mean(y * y, axis=-1, keepdims=True) + RMS_EPS) * gf_ref[...]
        o_ref[...] = y


def peer_residual(h, g, wq, k1, k2, u_bf, vt_bf, final_g=None, tn=640, te=1024, name="peer"):
    m, d = h.shape
    tn = min(tn, m)
    assert m % tn == 0 and N_EXPERTS % te == 0 and te % _SUB == 0
    final_norm = final_g is not None
    gf = (final_g if final_norm else jnp.ones((d,), F32)).astype(F32).reshape(1, d)
    const = lambda *shape: pl.BlockSpec(shape, lambda i, j: (0,) * len(shape))
    big = lambda: pltpu.VMEM((PEER_HEADS, PEER_NKEYS, tn), F32)
    return pl.pallas_call(
        functools.partial(_peer_body, te=te, final_norm=final_norm),
        grid=(m // tn, N_EXPERTS // te),
        in_specs=[
            pl.BlockSpec((tn, d), lambda i, j: (i, 0)),
            const(1, d),
            const(d, PEER_HEADS * PEER_QDIM),
            const(PEER_HEADS, PEER_NKEYS, PEER_HALF),
            const(PEER_HEADS, PEER_NKEYS, PEER_HALF),
            pl.BlockSpec((te, d), lambda i, j: (j, 0)),
            pl.BlockSpec((d, te), lambda i, j: (0, j)),
            const(1, d),
        ],
        out_specs=pl.BlockSpec((tn, d), lambda i, j: (i, 0)),
        out_shape=jax.ShapeDtypeStruct((m, d), F32),
        scratch_shapes=[
            pltpu.VMEM((tn, d), BF16),
            pltpu.VMEM((2 * PEER_HEADS, tn, PEER_HALF), F32),
            big(), big(), big(), big(),
            pltpu.VMEM((_round_up(_NSEL, 8), tn), F32),
            pltpu.VMEM((_round_up(_NSEL, 8), tn), F32),
            pltpu.VMEM((_NCAND, tn), F32),
            pltpu.VMEM((_round_up(_NSEL, 8), tn), F32),
            pltpu.VMEM((d, tn), F32),
        ],
        compiler_params=_cparams("parallel", "arbitrary"),
        name=name,
    )(h, g.astype(F32).reshape(1, d), wq.astype(BF16), k1.astype(F32), k2.astype(F32), u_bf, vt_bf, gf)


def _kv_layouts(k_all, v_all, tk):
    b, t, w = k_all.shape
    vt = v_all.astype(BF16).reshape(b, t // tk, tk, w).transpose(0, 1, 3, 2)
    return k_all.astype(BF16), vt


def odd_mixer(h, g, w_in, b_f, w_out, k_past, v_past, logf_past, t_valid):
    b, t, d = h.shape
    p = k_past.shape[1]
    rows = b * t
    w_f = jnp.pad(w_in[:, 3 * FOX_W:], ((0, 0), (0, LANES - FOX_HEADS)))
    w_bf = jnp.concatenate([w_in[:, :3 * FOX_W], w_f], axis=1).astype(BF16)
    bias = jnp.pad(b_f.astype(F32), (0, LANES - FOX_HEADS)).reshape(1, LANES)
    splits = (FOX_W, FOX_W, FOX_W, LANES)
    tk = ROW_TILE
    if p == 0 and b == 1 and t % tk == 0:
        q_bf, k, v, k_bf, vt_bf, logf = norm_proj(
            h.reshape(rows, d), g, w_bf, splits, bias=bias, rows_out=t_valid, name="odd_in_proj",
            outs=((0, "bf16"), (1, "f32_rows"), (2, "f32_rows"), (1, "bf16"), (2, "bf16_t"), (3, "f32")))
        logf = logf[None]
        f_cum = cumsum_rows(logf, name="fox_cumsum")
        o = fox_attention(q_bf[None], k_bf[None], vt_bf[None], f_cum, f_cum[:, :, :FOX_HEADS], 0, t_valid)
        k = k[None]
        v = v[None]
    else:
        q, k, v, logf = norm_proj(h.reshape(rows, d), g, w_bf, splits, bias=bias, name="odd_in_proj")
        q = q.reshape(b, t, FOX_W)
        k = k.reshape(b, t, FOX_W)
        v = v.reshape(b, t, FOX_W)
        logf = logf.reshape(b, t, LANES)
        tq_pad = _round_up(t, LANES)
        tk_pad = _round_up(p + t, tk)
        k_all = _pad_rows(jnp.concatenate([k_past.reshape(b, p, FOX_W), k], axis=1), tk_pad, 1)
        v_all = _pad_rows(jnp.concatenate([v_past.reshape(b, p, FOX_W), v], axis=1), tk_pad, 1)
        logf_past = jnp.pad(logf_past.astype(F32), ((0, 0), (0, 0), (0, LANES - FOX_HEADS)))
        logf_all = _pad_rows(jnp.concatenate([logf_past, logf], axis=1), tk_pad, 1)
        f_cum = cumsum_rows(logf_all, name="fox_cumsum")
        f_q = _pad_rows(f_cum[:, p:p + t, :FOX_HEADS], tq_pad, 1)
        k_bf, vt_bf = _kv_layouts(k_all, v_all, tk)
        q_bf = _pad_rows(q, tq_pad, 1).astype(BF16)
        o = fox_attention(q_bf, k_bf, vt_bf, f_cum, f_q, p, t_valid)[:, :t]
        k = k[:, :t_valid]
        v = v[:, :t_valid]
    h_new = out_proj_residual([o.reshape(rows, FOX_W)], h.reshape(rows, d), w_out.astype(BF16), name="odd_out_proj")
    return (h_new.reshape(b, t, d), k.reshape(b, t_valid, FOX_HEADS, FOX_DIM), v.reshape(b, t_valid, FOX_HEADS, FOX_DIM),
            logf[:, :t_valid, :FOX_HEADS])


def kernel(x_prompt, x_sample, cache_sb_k, cache_sb_v, state_gdn, state_gdn_conv, cache_fox_k, cache_fox_v, cache_fox_logf, meta_tokens, norm_mix, norm_ffn, norm_final, w_in_even, w_out_even, gdn_conv_w, gdn_a_log, gdn_dt_bias, gdn_norm, w_in_odd, b_forget, w_out_odd, peer_wq, peer_k1, peer_k2, peer_u, peer_v):
    bsz, seq, d = x_prompt.shape
    dec_b, dec_t, _ = x_sample.shape
    depth = norm_mix.shape[0]
    dt = x_prompt.dtype
    t_p = N_META + seq
    t_pad = _round_up(t_p, 5 * ROW_TILE)

    meta = jnp.broadcast_to(meta_tokens.astype(dt)[None], (bsz, N_META, d))
    hp = _pad_rows(jnp.concatenate([meta, x_prompt], axis=1), t_pad, 1)
    hs = x_sample

    empty_sb = jnp.zeros((bsz, 0, SB_HEADS, SB_DIM), dt)
    zero_s = jnp.zeros((bsz, GDN_HEADS, GDN_DK, GDN_DV), dt)
    zero_buf = jnp.zeros((bsz, GDN_CONV - 1, GDN_CONV_DIM), dt)
    empty_fox = jnp.zeros((bsz, 0, FOX_HEADS, FOX_DIM), dt)
    empty_logf = jnp.zeros((bsz, 0, FOX_HEADS), dt)

    sbk_p, sbv_p, sbk_s, sbv_s = [], [], [], []
    gs_p, gs_s, gc_p, gc_s = [], [], [], []
    fk_p, fv_p, ff_p, fk_s, fv_s, ff_s = [], [], [], [], [], []

    for layer in range(depth):
        if layer % 2 == 0:
            e = layer // 2
            w = (norm_mix[layer], w_in_even[e], w_out_even[e], gdn_conv_w[e], gdn_a_log[e], gdn_dt_bias[e], gdn_norm[e])
            hp, kp, vp, sp, bp = even_mixer(hp, *w, empty_sb, empty_sb, zero_s, zero_buf, t_p)
            hs, ks_, vs_, ss, bs = even_mixer(hs, *w, cache_sb_k[e], cache_sb_v[e], state_gdn[e], state_gdn_conv[e], dec_t)
            sbk_p.append(kp); sbv_p.append(vp); sbk_s.append(ks_); sbv_s.append(vs_)
            gs_p.append(sp); gs_s.append(ss); gc_p.append(bp); gc_s.append(bs)
        else:
            o = layer // 2
            w = (norm_mix[layer], w_in_odd[o], b_forget[o], w_out_odd[o])
            hp, kp, vp, fp = odd_mixer(hp, *w, empty_fox, empty_fox, empty_logf, t_p)
            hs, ks_, vs_, fs = odd_mixer(hs, *w, cache_fox_k[o], cache_fox_v[o], cache_fox_logf[o], dec_t)
            fk_p.append(kp); fv_p.append(vp); ff_p.append(fp)
            fk_s.append(ks_); fv_s.append(vs_); ff_s.append(fs)
        last = layer == depth - 1
        u_bf = peer_u[layer].astype(BF16)
        vt_bf = peer_v[layer].T.astype(BF16)
        pw = (norm_ffn[layer], peer_wq[layer], peer_k1[layer], peer_k2[layer], u_bf, vt_bf, norm_final if last else None)
        hp = peer_residual(hp.reshape(bsz * t_pad, d), *pw, name="peer_prompt").reshape(bsz, t_pad, d)
        hs = peer_residual(hs.reshape(dec_b * dec_t, d), *pw, name="peer_sample").reshape(dec_b, dec_t, d)

    y_prompt = hp[:, N_META:t_p]
    y_sample = hs
    return (y_prompt, y_sample,
            jnp.stack(sbk_p), jnp.stack(sbv_p), jnp.stack(sbk_s), jnp.stack(sbv_s),
            jnp.stack(gs_p), jnp.stack(gs_s), jnp.stack(gc_p), jnp.stack(gc_s),
            jnp.stack(fk_p), jnp.stack(fv_p), jnp.stack(ff_p),
            jnp.stack(fk_s), jnp.stack(fv_s), jnp.stack(ff_s))
```

```python
import functools

import jax
import jax.numpy as jnp
from jax import lax
from jax.experimental import pallas as pl
from jax.experimental.pallas import tpu as pltpu

F32 = jnp.float32
BF16 = jnp.bfloat16

D_MODEL = 1024
N_META = 16
CHUNK = 64
SB_DIM = 64
SB_HEADS = 8
SB_W = SB_HEADS * SB_DIM
GDN_DK = 128
GDN_DV = 128
GDN_HEADS = 4
GDN_QK = GDN_HEADS * GDN_DK
GDN_V = GDN_HEADS * GDN_DV
GDN_CONV = 4
GDN_CONV_DIM = 2 * GDN_QK + GDN_V
FOX_DIM = 64
FOX_HEADS = 16
FOX_W = FOX_HEADS * FOX_DIM
PEER_HEADS = 8
PEER_NKEYS = 128
PEER_TOPK = 16
PEER_QDIM = 256
N_EXPERTS = PEER_NKEYS ** 2
RMS_EPS = 1e-6

LANES = 128
ROW_TILE = 256
ATT_TQ = 512
VMEM_LIMIT = 56 * 1024 * 1024
NEG_BIG = -1e30

_NT = (((1,), (1,)), ((), ()))


def _cparams(*sem):
    return pltpu.CompilerParams(dimension_semantics=sem, vmem_limit_bytes=VMEM_LIMIT)


def _round_up(n, m):
    return -(-n // m) * m


def _pad_rows(a, rows, axis=0):
    pad = [(0, 0)] * a.ndim
    pad[axis] = (0, rows - a.shape[axis])
    return jnp.pad(a, pad)


def _norm_proj_body(x_ref, g_ref, w_ref, b_ref, *out_refs, offs, outs, logsig_split):
    x = x_ref[...]
    xn = x * lax.rsqrt(jnp.mean(x * x, axis=-1, keepdims=True) + RMS_EPS) * g_ref[...]
    xb = xn.astype(BF16)
    cols = {}
    for (i, kind), o_ref in zip(outs, out_refs):
        if i not in cols:
            y = jnp.dot(xb, w_ref[:, offs[i]:offs[i + 1]], preferred_element_type=F32)
            cols[i] = jax.nn.log_sigmoid(y + b_ref[...]) if i == logsig_split else y
        y = cols[i]
        if kind == "bf16_t":
            o_ref[0] = y.T.astype(BF16)
        else:
            o_ref[...] = y.astype(o_ref.dtype)


def norm_proj(x, g, w_bf, splits, outs=None, bias=None, rows_out=None, name="norm_proj"):
    m, d = x.shape
    n = w_bf.shape[1]
    offs = [0]
    for s in splits:
        offs.append(offs[-1] + s)
    tm = min(ROW_TILE, m)
    assert offs[-1] == n and m % tm == 0
    outs = tuple((i, "f32") for i in range(len(splits))) if outs is None else tuple(outs)
    logsig_split = len(splits) - 1 if bias is not None else -1
    if bias is None:
        bias = jnp.zeros((1, splits[-1]), F32)
    out_specs, out_shape = [], []
    for i, kind in outs:
        s = splits[i]
        if kind == "bf16_t":
            out_specs.append(pl.BlockSpec((1, s, tm), lambda r: (r, 0, 0)))
            out_shape.append(jax.ShapeDtypeStruct((m // tm, s, tm), BF16))
        else:
            rows = rows_out if kind == "f32_rows" else m
            out_specs.append(pl.BlockSpec((tm, s), lambda r: (r, 0)))
            out_shape.append(jax.ShapeDtypeStruct((rows, s), BF16 if kind == "bf16" else F32))
    return pl.pallas_call(
        functools.partial(_norm_proj_body, offs=tuple(offs), outs=outs, logsig_split=logsig_split),
        grid=(m // tm,),
        in_specs=[
            pl.BlockSpec((tm, d), lambda i: (i, 0)),
            pl.BlockSpec((1, d), lambda i: (0, 0)),
            pl.BlockSpec((d, n), lambda i: (0, 0)),
            pl.BlockSpec((1, splits[-1]), lambda i: (0, 0)),
        ],
        out_specs=out_specs,
        out_shape=out_shape,
        compiler_params=_cparams("parallel"),
        name=name,
    )(x, g.reshape(1, d), w_bf, bias)


def _out_proj_body(*refs, n_in, offs):
    a_refs = refs[:n_in]
    h_ref, w_ref, o_ref = refs[n_in:]
    acc = h_ref[...]
    for i, a_ref in enumerate(a_refs):
        acc = acc + jnp.dot(a_ref[...].astype(BF16), w_ref[offs[i]:offs[i + 1], :], preferred_element_type=F32)
    o_ref[...] = acc


def out_proj_residual(parts, h, w_bf, name="out_proj"):
    m, d = h.shape
    offs = [0]
    for a in parts:
        offs.append(offs[-1] + a.shape[1])
    tm = min(ROW_TILE, m)
    assert offs[-1] == w_bf.shape[0] and m % tm == 0
    return pl.pallas_call(
        functools.partial(_out_proj_body, n_in=len(parts), offs=tuple(offs)),
        grid=(m // tm,),
        in_specs=[pl.BlockSpec((tm, a.shape[1]), lambda i: (i, 0)) for a in parts] + [
            pl.BlockSpec((tm, d), lambda i: (i, 0)),
            pl.BlockSpec(w_bf.shape, lambda i: (0, 0)),
        ],
        out_specs=pl.BlockSpec((tm, d), lambda i: (i, 0)),
        out_shape=jax.ShapeDtypeStruct((m, d), F32),
        compiler_params=_cparams("parallel"),
        name=name,
    )(*parts, h, w_bf)


def _cumsum_body(x_ref, o_ref, carry_ref):
    @pl.when(pl.program_id(1) == 0)
    def _():
        carry_ref[...] = jnp.zeros_like(carry_ref)

    x = x_ref[0]
    t = x.shape[0]
    tri = (lax.broadcasted_iota(jnp.int32, (t, t), 0) >= lax.broadcasted_iota(jnp.int32, (t, t), 1)).astype(F32)
    c = jnp.dot(tri, x, preferred_element_type=F32, precision=lax.Precision.HIGHEST) + carry_ref[...]
    o_ref[0] = c
    carry_ref[...] = c[t - 1:t, :]


def cumsum_rows(x, name="cumsum_rows"):
    b, l, c = x.shape
    tm = ROW_TILE
    assert l % tm == 0
    return pl.pallas_call(
        _cumsum_body,
        grid=(b, l // tm),
        in_specs=[pl.BlockSpec((1, tm, c), lambda i, j: (i, j, 0))],
        out_specs=pl.BlockSpec((1, tm, c), lambda i, j: (i, j, 0)),
        out_shape=jax.ShapeDtypeStruct((b, l, c), F32),
        scratch_shapes=[pltpu.VMEM((1, c), F32)],
        compiler_params=_cparams("parallel", "arbitrary"),
        name=name,
    )(x)


def _fox_body(nsteps_ref, q_ref, k_ref, vt_ref, fk_ref, fq_ref, o_ref, acc_ref, s_ref, p_ref, *, q_offset, tq, tk):
    bi = pl.program_id(0)
    hp = pl.program_id(1)
    qi = pl.program_id(2)
    q = q_ref[0] * jnp.asarray(FOX_DIM ** -0.5, BF16)
    lane = lax.broadcasted_iota(jnp.int32, (1, LANES), 1)
    q_heads = (jnp.where(lane < FOX_DIM, q, jnp.zeros_like(q)), jnp.where(lane >= FOX_DIM, q, jnp.zeros_like(q)))
    q0 = q_offset + qi * tq
    q_pos = q0 + lax.broadcasted_iota(jnp.int32, (1, tq), 1)
    kb_diag = jnp.minimum((q0 + tq - 1) // tk, k_ref.shape[1] // tk - 1)
    n_causal = max(tq // tk, 1)
    fq_all = fq_ref[0]
    qf_heads = []
    for h in range(2):
        lo = _F_PARTS * (2 * hp + h)
        mine = ((lane >= lo) & (lane < lo + _F_PARTS)) | ((lane >= _F_ONES + lo) & (lane < _F_ONES + lo + _F_PARTS))
        qf_heads.append(jnp.concatenate([q_heads[h], jnp.where(mine, fq_all, jnp.zeros_like(fq_all))], axis=1))
    acc_ref[...] = jnp.zeros_like(acc_ref)

    def scores(kb):
        k0 = pl.multiple_of(kb * tk, tk)
        kf = jnp.concatenate([k_ref[0, pl.ds(k0, tk), :], fk_ref[0, pl.ds(k0, tk), :]], axis=1)
        for h in range(2):
            s_ref[h] = lax.dot_general(kf, qf_heads[h], _NT, preferred_element_type=F32)

    def accumulate(kb, alphas):
        vt = vt_ref[0, kb]
        for h in range(2):
            acc_ref[h] = alphas[h] * acc_ref[h] + jnp.dot(vt, p_ref[h], preferred_element_type=F32)

    def softmax(kb, carry, causal):
        if causal:
            mask = (kb * tk + lax.broadcasted_iota(jnp.int32, (tk, 1), 0)) <= q_pos
        stats, alphas = [], []
        for h in range(2):
            m_prev, l_prev = carry[2 * h], carry[2 * h + 1]
            s = s_ref[h]
            if causal:
                s = jnp.where(mask, s, NEG_BIG)
            m_new = jnp.maximum(m_prev, jnp.max(s, axis=0, keepdims=True))
            p = jnp.exp(s - m_new)
            alpha = jnp.exp(m_prev - m_new)
            p_ref[h] = p.astype(BF16)
            stats += [m_new, alpha * l_prev + jnp.sum(p, axis=0, keepdims=True)]
            alphas.append(alpha)
        return tuple(stats + alphas)

    def step(j, carry):
        kb = kb_diag - j
        accumulate(kb + 1, carry[4:6])
        carry = softmax(kb, carry, causal=False)
        scores(jnp.maximum(kb - 1, 0))
        return carry

    ones = jnp.ones((1, tq), F32)
    init = (jnp.full((1, tq), NEG_BIG, F32), jnp.zeros((1, tq), F32),
            jnp.full((1, tq), NEG_BIG, F32), jnp.zeros((1, tq), F32), ones, ones)
    n_steps = nsteps_ref[bi, hp, qi]
    scores(kb_diag)
    carry = init
    for c in range(n_causal):
        if c > 0:
            accumulate(kb_diag - c + 1, carry[4:6])
        carry = softmax(kb_diag - c, carry, causal=True)
        scores(jnp.maximum(kb_diag - c - 1, 0))
    fin = lax.fori_loop(n_causal, n_steps, step, carry)
    accumulate(kb_diag - (n_steps - 1), fin[4:6])
    row = lax.broadcasted_iota(jnp.int32, (LANES, 1), 0)
    ot = jnp.where(row < FOX_DIM, acc_ref[0] / fin[1], acc_ref[1] / fin[3])
    o_ref[0] = ot.T


_UNDERFLOW = 110.0


def _fox_block_counts(q_bf, k_bf, f_keys, f_q, q_offset, t_valid, tq, tk):
    b, tq_all, w = q_bf.shape
    tk_all = k_bf.shape[1]
    nq, nk = pl.cdiv(tq_all, tq), tk_all // tk
    heads = w // FOX_DIM
    qn = jnp.sqrt(jnp.sum(jnp.square(q_bf.astype(F32)).reshape(b, tq_all, heads, FOX_DIM), axis=-1)) * (FOX_DIM ** -0.5)
    kmax = jnp.max(jnp.sqrt(jnp.sum(jnp.square(k_bf.astype(F32)).reshape(b, tk_all, heads, FOX_DIM), axis=-1)), axis=1)
    bound = 2.0 * qn * kmax[:, None, :] * (1.0 + 1e-3) + f_q
    valid = (jnp.arange(tq_all) < t_valid)[None, :, None]
    bound = jnp.pad(jnp.where(valid, bound, -jnp.inf), ((0, 0), (0, nq * tq - tq_all), (0, 0)),
                    constant_values=-jnp.inf)
    cq = jnp.max(bound.reshape(b, nq, tq, heads), axis=2)
    f_end = f_keys[:, tk - 1::tk, :heads]
    kb_diag = jnp.minimum((q_offset + jnp.arange(nq) * tq + tq - 1) // tk, nk - 1)
    need = (cq[:, :, None, :] - f_end[:, None, :, :]) > -_UNDERFLOW
    need = need & (jnp.arange(nk)[None, None, :, None] <= kb_diag[None, :, None, None])
    first = jnp.min(jnp.where(need, jnp.arange(nk)[None, None, :, None], nk), axis=2)
    first = jnp.min(first.reshape(b, nq, heads // 2, 2), axis=-1)
    most = kb_diag[None, :, None] + 1
    steps = jnp.clip(kb_diag[None, :, None] - first + 1, jnp.minimum(max(tq // tk, 1), most), most)
    return steps.transpose(0, 2, 1).astype(jnp.int32)


_F_PARTS = 3
_F_ONES = _F_PARTS * FOX_HEADS


def _bf16_head(x):
    bits = lax.bitcast_convert_type(x, jnp.uint32) & jnp.uint32(0xFFFF0000)
    return lax.bitcast_convert_type(bits, F32)


def _split_f_operand(f, key_side):
    b, t, heads = f.shape
    f = f.astype(F32)
    hi = _bf16_head(f)
    mid = _bf16_head(f - hi)
    lo = f - hi - mid
    pieces = jnp.stack([hi, mid, lo], axis=-1).reshape(b, t, heads * _F_PARTS).astype(BF16)
    const = jnp.full((b, t, heads * _F_PARTS), 1.0 if key_side else -1.0, BF16)
    both = [pieces, const] if key_side else [const, pieces]
    return jnp.pad(jnp.concatenate(both, axis=-1), ((0, 0), (0, 0), (0, LANES - 2 * heads * _F_PARTS)))


def fox_attention(q_bf, k_bf, vt_bf, f_keys, f_q, q_offset, t_valid, name="fox_attention"):
    b, tq_all, w = q_bf.shape
    tk_all = k_bf.shape[1]
    tq = min(ATT_TQ, tq_all)
    tk = ROW_TILE
    assert tk_all % tk == 0
    assert (tq % tk == 0 and q_offset % tk == 0) or (tk % tq == 0 and q_offset % tq == 0)
    nsteps = _fox_block_counts(q_bf, k_bf, f_keys, f_q, q_offset, t_valid, tq, tk)
    fk_aug = _split_f_operand(f_keys[:, :, :FOX_HEADS], key_side=True)
    fq_aug = _split_f_operand(f_q, key_side=False)
    return pl.pallas_call(
        functools.partial(_fox_body, q_offset=q_offset, tq=tq, tk=tk),
        grid_spec=pltpu.PrefetchScalarGridSpec(
            num_scalar_prefetch=1,
            grid=(b, w // LANES, pl.cdiv(tq_all, tq)),
            in_specs=[
                pl.BlockSpec((1, tq, LANES), lambda i, h, j, n: (i, j, h)),
                pl.BlockSpec((1, tk_all, LANES), lambda i, h, j, n: (i, 0, h)),
                pl.BlockSpec((1, tk_all // tk, LANES, tk), lambda i, h, j, n: (i, 0, h, 0)),
                pl.BlockSpec((1, tk_all, LANES), lambda i, h, j, n: (i, 0, 0)),
                pl.BlockSpec((1, tq, LANES), lambda i, h, j, n: (i, j, 0)),
            ],
            out_specs=pl.BlockSpec((1, tq, LANES), lambda i, h, j, n: (i, j, h)),
            scratch_shapes=[pltpu.VMEM((2, LANES, tq), F32), pltpu.VMEM((2, tk, tq), F32),
                            pltpu.VMEM((2, tk, tq), BF16)],
        ),
        out_shape=jax.ShapeDtypeStruct((b, tq_all, w), F32),
        compiler_params=_cparams("parallel", "parallel", "arbitrary"),
        name=name,
    )(nsteps, q_bf, k_bf, vt_bf, fk_aug, fq_aug)


def _sb_body(q_ref, k_ref, vt_ref, o_ref, acc_ref, *, q_offset, tq, tk, tq_all):
    qi = pl.program_id(2)
    q = q_ref[0] * jnp.asarray(SB_DIM ** -0.5, BF16)
    lane = lax.broadcasted_iota(jnp.int32, (1, LANES), 1)
    q_heads = (jnp.where(lane < SB_DIM, q, jnp.zeros_like(q)), jnp.where(lane >= SB_DIM, q, jnp.zeros_like(q)))
    q0 = q_offset + qi * tq
    q_pos = q0 + lax.broadcasted_iota(jnp.int32, (1, tq), 1)
    n_kb = jnp.minimum(jnp.maximum(q0 + tq - 2, 0) // tk + 1, k_ref.shape[1] // tk)
    real_query = q_pos < q_offset + tq_all
    upper = (lax.broadcasted_iota(jnp.int32, (tk, tk), 1) > lax.broadcasted_iota(jnp.int32, (tk, tk), 0)).astype(BF16)
    acc_ref[...] = jnp.zeros_like(acc_ref)

    def step(carry):
        j = carry[0]
        kb = n_kb - 1 - j
        k0 = pl.multiple_of(kb * tk, tk)
        k = k_ref[0, pl.ds(k0, tk), :]
        vt = vt_ref[0, kb]
        k_pos = k0 + lax.broadcasted_iota(jnp.int32, (tk, 1), 0)
        mask = k_pos < q_pos
        out = []
        for h in range(2):
            r_prev = carry[2 + h]
            z = lax.dot_general(k, q_heads[h], _NT, preferred_element_type=F32)
            sp = jnp.maximum(z, 0.0) + jnp.log(1.0 + jnp.exp(-jnp.abs(z)))
            l = jnp.where(mask, -sp, 0.0)
            l_hi = l.astype(BF16)
            l_lo = (l - l_hi.astype(F32)).astype(BF16)
            later = (jnp.dot(upper, l_hi, preferred_element_type=F32)
                     + jnp.dot(upper, l_lo, preferred_element_type=F32))
            w = jnp.where(mask, jnp.exp((z - sp) + later + r_prev), 0.0)
            acc_ref[h] += jnp.dot(vt, w.astype(BF16), preferred_element_type=F32)
            out.append(r_prev + later[0:1, :] + l[0:1, :])
        live = jnp.where(real_query, jnp.maximum(out[0], out[1]), -jnp.inf)
        return (j + 1, jnp.max(live), out[0], out[1])

    def more(carry):
        return (carry[0] < n_kb) & (carry[1] > -_UNDERFLOW)

    lax.while_loop(more, step, (jnp.int32(0), jnp.float32(0.0), jnp.zeros((1, tq), F32), jnp.zeros((1, tq), F32)))
    row = lax.broadcasted_iota(jnp.int32, (LANES, 1), 0)
    o_ref[0] = jnp.where(row < SB_DIM, acc_ref[0], acc_ref[1]).T


def sb_attention(q_bf, k_bf, vt_bf, q_offset, name="sb_attention"):
    b, tq_all, w = q_bf.shape
    tk_all = k_bf.shape[1]
    tq = min(ROW_TILE, tq_all)
    tk = ROW_TILE
    assert tk_all % tk == 0
    return pl.pallas_call(
        functools.partial(_sb_body, q_offset=q_offset, tq=tq, tk=tk, tq_all=tq_all),
        grid=(b, w // LANES, pl.cdiv(tq_all, tq)),
        in_specs=[
            pl.BlockSpec((1, tq, LANES), lambda i, h, j: (i, j, h)),
            pl.BlockSpec((1, tk_all, LANES), lambda i, h, j: (i, 0, h)),
            pl.BlockSpec((1, tk_all // tk, LANES, tk), lambda i, h, j: (i, 0, h, 0)),
        ],
        out_specs=pl.BlockSpec((1, tq, LANES), lambda i, h, j: (i, j, h)),
        out_shape=jax.ShapeDtypeStruct((b, tq_all, w), F32),
        scratch_shapes=[pltpu.VMEM((2, LANES, tq), F32)],
        compiler_params=_cparams("parallel", "parallel", "arbitrary"),
        name=name,
    )(q_bf, k_bf, vt_bf)


_HI = lax.Precision.HIGHEST
_TN = (((0,), (0,)), ((), ()))
_CONV_PAD = 8
_GDN_CPS = 4
_GDN_ROWS = _GDN_CPS * CHUNK


def _dot_hi(a, b):
    return jnp.dot(a, b, preferred_element_type=F32, precision=_HI)


def _split_bf16(x):
    hi = x.astype(BF16)
    return hi, (x - hi.astype(F32)).astype(BF16)


def _einsum3(spec, a, b):
    ah, al = a if isinstance(a, tuple) else _split_bf16(a)
    bh, bl = b if isinstance(b, tuple) else _split_bf16(b)
    prod = functools.partial(jnp.einsum, spec, preferred_element_type=F32)
    return prod(ah, bh) + (prod(ah, bl) + prod(al, bh))


def _bmm(a, b):
    return _einsum3('hij,hjk->hik', a, b)


def _bmm_nt(a, b):
    return _einsum3('hik,hjk->hij', a, b)


def _softplus(x):
    return jnp.maximum(x, 0.0) + jnp.log1p(jnp.exp(-jnp.abs(x)))


def _silu(x):
    return x / (1.0 + jnp.exp(-x))


def _unit_lower_inverse(m):
    c_len = m.shape[-1]
    ri = lax.broadcasted_iota(jnp.int32, (c_len, c_len), 0)
    ci = lax.broadcasted_iota(jnp.int32, (c_len, c_len), 1)
    d = jnp.broadcast_to((ri == ci).astype(F32), m.shape)
    s = 1
    while s < c_len:
        join = (ri // (2 * s) == ci // (2 * s)) & (ri % (2 * s) >= s) & (ci % (2 * s) < s)
        c = jnp.where(join, m, 0.0)
        if s == 1:
            d = d - c
        else:
            d_s = _split_bf16(d)
            d = d - _bmm(_bmm(d_s, c), d_s)
        s *= 2
    return d


def _gdn_body(x_ref, z_ref, ab_ref, buf_ref, s0_ref, cw_ref, alog_ref, dt_ref, gn_ref,
              o_ref, sfin_ref, xwin_ref, s_ref, *, t_valid):
    c = pl.program_id(1)
    n_c = pl.num_programs(1)
    hist = GDN_CONV - 1

    @pl.when(c == 0)
    def _():
        xwin_ref[_CONV_PAD - hist:_CONV_PAD, :] = buf_ref[0]
        s_ref[...] = s0_ref[0]

    rows = _GDN_ROWS
    xwin_ref[_CONV_PAD:_CONV_PAD + rows, :] = x_ref[0]
    conv = xwin_ref[_CONV_PAD - hist:_CONV_PAD - hist + rows, :] * cw_ref[0:1, :]
    for i in range(1, GDN_CONV):
        conv = conv + xwin_ref[_CONV_PAD - hist + i:_CONV_PAD - hist + i + rows, :] * cw_ref[i:i + 1, :]
    tail = xwin_ref[_CONV_PAD + rows - hist:_CONV_PAD + rows, :]
    xwin_ref[_CONV_PAD - hist:_CONV_PAD, :] = tail
    act = _silu(conv)

    ab = ab_ref[0]
    row_ok = (c * rows + lax.broadcasted_iota(jnp.int32, (rows, 1), 0)) < t_valid
    g_all = jnp.where(row_ok, -jnp.exp(alog_ref[...]) * _softplus(ab + dt_ref[...]), 0.0)
    beta_all = jnp.where(row_ok, 1.0 / (1.0 + jnp.exp(-ab)), 0.0)
    rr = lax.broadcasted_iota(jnp.int32, (rows, rows), 0)
    rc = lax.broadcasted_iota(jnp.int32, (rows, rows), 1)
    tri_chunks = ((rr >= rc) & (rr // CHUNK == rc // CHUNK)).astype(F32)
    gcum_all = _dot_hi(tri_chunks, g_all)
    sel = (lax.broadcasted_iota(jnp.int32, (8, LANES), 0) == lax.broadcasted_iota(jnp.int32, (8, LANES), 1)).astype(F32)
    gcum_rows = lax.dot_general(sel, gcum_all, _NT, preferred_element_type=F32, precision=_HI)
    ri = lax.broadcasted_iota(jnp.int32, (CHUNK, CHUNK), 0)
    ci = lax.broadcasted_iota(jnp.int32, (CHUNK, CHUNK), 1)
    tri = ri >= ci
    strict = ri > ci

    pairs = [(ck, h) for ck in range(_GDN_CPS) for h in range(GDN_HEADS)]
    rows_of = lambda ck: slice(ck * CHUNK, (ck + 1) * CHUNK)
    q4 = jnp.stack([act[rows_of(ck), h * GDN_DK:(h + 1) * GDN_DK] for ck, h in pairs])
    k4 = jnp.stack([act[rows_of(ck), GDN_QK + h * GDN_DK:GDN_QK + (h + 1) * GDN_DK] for ck, h in pairs])
    v4 = jnp.stack([act[rows_of(ck), 2 * GDN_QK + h * GDN_DV:2 * GDN_QK + (h + 1) * GDN_DV] for ck, h in pairs])
    q4 = q4 * lax.rsqrt(jnp.sum(q4 * q4, axis=-1, keepdims=True) + RMS_EPS) * (GDN_DK ** -0.5)
    k4 = k4 * lax.rsqrt(jnp.sum(k4 * k4, axis=-1, keepdims=True) + RMS_EPS)
    beta = jnp.stack([beta_all[rows_of(ck), GDN_HEADS + h:GDN_HEADS + h + 1] for ck, h in pairs])
    gc = jnp.stack([gcum_all[rows_of(ck), h:h + 1] for ck, h in pairs])
    gr = jnp.stack([gcum_rows[h:h + 1, rows_of(ck)] for ck, h in pairs])
    decay = jnp.exp(jnp.where(tri, gc - gr, NEG_BIG))
    kb = k4 * beta
    k4_s = _split_bf16(k4)
    m = jnp.where(strict, _bmm_nt(kb, k4_s) * decay, 0.0)
    tinv = _split_bf16(_unit_lower_inverse(m))
    eg = jnp.exp(gc)
    u = _bmm(tinv, v4 * beta)
    w = _split_bf16(_bmm(tinv, kb * eg))
    attn = _split_bf16(_bmm_nt(q4, k4_s) * decay)
    qe = _split_bf16(q4 * eg)
    g_last = gc[:, CHUNK - 1:CHUNK, :]
    k_dec = _split_bf16(k4 * jnp.exp(g_last - gc))
    s_scale = jnp.exp(g_last)

    s4 = s_ref[...]
    outs = []
    for ck in range(_GDN_CPS):
        sl = slice(ck * GDN_HEADS, (ck + 1) * GDN_HEADS)
        part = lambda pair: (pair[0][sl], pair[1][sl])
        s4_s = _split_bf16(s4)
        v_new_s = _split_bf16(u[sl] - _bmm(part(w), s4_s))
        outs.append(_bmm(part(qe), s4_s) + _bmm(part(attn), v_new_s))
        s4 = s4 * s_scale[sl] + _einsum3('hck,hcv->hkv', part(k_dec), v_new_s)
    s_ref[...] = s4
    for ck in range(_GDN_CPS):
        o = outs[ck]
        o = o * lax.rsqrt(jnp.mean(o * o, axis=-1, keepdims=True) + RMS_EPS) * gn_ref[...]
        for h in range(GDN_HEADS):
            cols = slice(h * GDN_DV, (h + 1) * GDN_DV)
            o_ref[0, rows_of(ck), cols] = o[h] * _silu(z_ref[0, rows_of(ck), cols])

    @pl.when(c == n_c - 1)
    def _():
        sfin_ref[0] = s_ref[...]


def gdn_heads(qkv_pre, z, ab, conv_buf, s0, conv_w, a_log, dt_bias, gnorm, t_valid, name="gdn"):
    b, t, cd = qkv_pre.shape
    rows = _GDN_ROWS
    assert t % rows == 0
    n_c = t // rows
    pad_l = lambda a: jnp.pad(a.astype(F32), (0, LANES - a.shape[0])).reshape(1, LANES)
    const = lambda *shape: pl.BlockSpec(shape, lambda i, j: (0,) * len(shape))
    return pl.pallas_call(
        functools.partial(_gdn_body, t_valid=t_valid),
        grid=(b, n_c),
        in_specs=[
            pl.BlockSpec((1, rows, cd), lambda i, j: (i, j, 0)),
            pl.BlockSpec((1, rows, GDN_V), lambda i, j: (i, j, 0)),
            pl.BlockSpec((1, rows, LANES), lambda i, j: (i, j, 0)),
            pl.BlockSpec((1, GDN_CONV - 1, cd), lambda i, j: (i, 0, 0)),
            pl.BlockSpec((1, GDN_HEADS, GDN_DK, GDN_DV), lambda i, j: (i, 0, 0, 0)),
            const(GDN_CONV, cd), const(1, LANES), const(1, LANES), const(1, GDN_DV),
        ],
        out_specs=[
            pl.BlockSpec((1, rows, GDN_V), lambda i, j: (i, j, 0)),
            pl.BlockSpec((1, GDN_HEADS, GDN_DK, GDN_DV), lambda i, j: (i, 0, 0, 0)),
        ],
        out_shape=[jax.ShapeDtypeStruct((b, t, GDN_V), F32),
                   jax.ShapeDtypeStruct((b, GDN_HEADS, GDN_DK, GDN_DV), F32)],
        scratch_shapes=[pltpu.VMEM((_CONV_PAD + rows, cd), F32), pltpu.VMEM((GDN_HEADS, GDN_DK, GDN_DV), F32)],
        compiler_params=_cparams("parallel", "arbitrary"),
        name=name,
    )(qkv_pre, z, ab, conv_buf.astype(F32), s0.astype(F32), conv_w.astype(F32), pad_l(a_log), pad_l(dt_bias),
      gnorm.astype(F32).reshape(1, GDN_DV))


def even_mixer(h, g, w_in, w_out, conv_w, a_log, dt_bias, gnorm, sb_k_past, sb_v_past, gdn_s0, conv_buf, t_valid):
    b, t, d = h.shape
    p = sb_k_past.shape[1]
    rows = b * t
    o0 = 3 * SB_W
    w_ab = jnp.pad(w_in[:, o0 + GDN_CONV_DIM + GDN_V:], ((0, 0), (0, LANES - 2 * GDN_HEADS)))
    w_bf = jnp.concatenate([w_in[:, :o0 + GDN_CONV_DIM + GDN_V], w_ab], axis=1).astype(BF16)
    splits = (SB_W, SB_W, SB_W, GDN_CONV_DIM, GDN_V, LANES)
    tk = ROW_TILE
    if p == 0 and b == 1 and t % tk == 0:
        q_bf, k, v, k_bf, vt_bf, qkv_pre, z, ab = norm_proj(
            h.reshape(rows, d), g, w_bf, splits, rows_out=t_valid, name="even_in_proj",
            outs=((0, "bf16"), (1, "f32_rows"), (2, "f32_rows"), (1, "bf16"), (2, "bf16_t"), (3, "f32"), (4, "f32"), (5, "f32")))
        o_sb = sb_attention(q_bf[None], k_bf[None], vt_bf[None], 0)
        k = k[None]
        v = v[None]
    else:
        q, k, v, qkv_pre, z, ab = norm_proj(h.reshape(rows, d), g, w_bf, splits, name="even_in_proj")
        q = q.reshape(b, t, SB_W)
        k = k.reshape(b, t, SB_W)
        v = v.reshape(b, t, SB_W)
        tq_pad = _round_up(t, LANES)
        tk_pad = _round_up(p + t, tk)
        k_all = _pad_rows(jnp.concatenate([sb_k_past.reshape(b, p, SB_W), k], axis=1), tk_pad, 1)
        v_all = _pad_rows(jnp.concatenate([sb_v_past.reshape(b, p, SB_W), v], axis=1), tk_pad, 1)
        k_bf, vt_bf = _kv_layouts(k_all, v_all, tk)
        q_bf = _pad_rows(q, tq_pad, 1).astype(BF16)
        o_sb = sb_attention(q_bf, k_bf, vt_bf, p)[:, :t]
        k = k[:, :t_valid]
        v = v[:, :t_valid]

    t_c = _round_up(t, _GDN_ROWS)
    qkv_pre = qkv_pre.reshape(b, t, GDN_CONV_DIM)
    o_gdn, s_fin = gdn_heads(_pad_rows(qkv_pre, t_c, 1), _pad_rows(z.reshape(b, t, GDN_V), t_c, 1),
                             _pad_rows(ab.reshape(b, t, LANES), t_c, 1), conv_buf, gdn_s0,
                             conv_w, a_log, dt_bias, gnorm, t_valid)
    o_gdn = o_gdn[:, :t]
    hist = GDN_CONV - 1
    assert t_valid >= hist
    xp_tail = qkv_pre[:, t_valid - hist:t_valid]
    h_new = out_proj_residual([o_sb.reshape(rows, SB_W), o_gdn.reshape(rows, GDN_V)], h.reshape(rows, d),
                              w_out.astype(BF16), name="even_out_proj")
    return (h_new.reshape(b, t, d), k.reshape(b, t_valid, SB_HEADS, SB_DIM), v.reshape(b, t_valid, SB_HEADS, SB_DIM),
            s_fin, xp_tail)


PEER_HALF = PEER_QDIM // 2
_NSEL = PEER_TOPK + 1
_SUB = 256
_CAND = tuple((a, b) for a in range(_NSEL) for b in range(_NSEL) if (a + 1) * (b + 1) <= _NSEL)
_NCAND = _round_up(len(_CAND), 8)


_SUBLANES = 8


def _sorting_network(n):
    pairs = []

    def merge(lo, m, r):
        step = 2 * r
        if step < m:
            merge(lo, m, step)
            merge(lo + r, m, step)
            pairs.extend((i, i + r) for i in range(lo + r, lo + m - r, step))
        else:
            pairs.append((lo, lo + r))

    def sort(lo, m):
        if m > 1:
            sort(lo, m // 2)
            sort(lo + m // 2, m // 2)
            merge(lo, m, 1)

    sort(0, n)
    return pairs


def _top_values(x, n, out_ref):
    rows, tn = x.shape
    groups = rows // _SUBLANES
    width = 1 << (groups - 1).bit_length()
    minus_inf = jnp.full((_SUBLANES, tn), -jnp.inf, F32)
    lists = [x[r * _SUBLANES:(r + 1) * _SUBLANES, :] for r in range(groups)] + [minus_inf] * (width - groups)
    for i, j in _sorting_network(width):
        lists[i], lists[j] = jnp.maximum(lists[i], lists[j]), jnp.minimum(lists[i], lists[j])
    lists = lists[:groups]
    sub = lax.broadcasted_iota(jnp.int32, (_SUBLANES, 1), 0)
    for it in range(n):
        head = lists[0]
        m = jnp.max(head, axis=0, keepdims=True)
        out_ref[it:it + 1, :] = m
        still_needed = n - it - 1
        if still_needed == 0:
            break
        first = jnp.min(jnp.where(head == m, sub, _SUBLANES), axis=0, keepdims=True)
        won = sub == first
        for r in range(min(groups, still_needed)):
            below = lists[r + 1] if r + 1 < groups else minus_inf
            lists[r] = jnp.where(won, below, lists[r])


def _gelu_tanh(x):
    return 0.5 * x * (1.0 + jnp.tanh(0.7978845608028654 * (x + 0.044715 * (x * x * x))))


def _peer_body(h_ref, g_ref, wq_ref, k1_ref, k2_ref, u_ref, vt_ref, gf_ref, o_ref,
               xn_ref, q_ref, ns1_ref, s2m_ref, e1_ref, e2_ref, t1_ref, t2_ref, cand_ref, csort_ref, acc_ref,
               *, te, final_norm):
    e = pl.program_id(1)
    n_e = pl.num_programs(1)
    tn = h_ref.shape[0]

    @pl.when(e == 0)
    def _prologue():
        x = h_ref[...]
        xn = x * lax.rsqrt(jnp.mean(x * x, axis=-1, keepdims=True) + RMS_EPS) * g_ref[...]
        xb = xn.astype(BF16)
        xn_ref[...] = xb
        q = jnp.dot(xb, wq_ref[...], preferred_element_type=F32)
        for j in range(2 * PEER_HEADS):
            q_ref[j] = q[:, j * PEER_HALF:(j + 1) * PEER_HALF]
        acc_ref[...] = jnp.zeros_like(acc_ref)
        cand_ref[...] = jnp.full(cand_ref.shape, -jnp.inf, F32)

        def per_head(h, _):
            s1 = lax.dot_general(k1_ref[h], q_ref[2 * h], _NT, preferred_element_type=F32)
            s2 = lax.dot_general(k2_ref[h], q_ref[2 * h + 1], _NT, preferred_element_type=F32)
            _top_values(s1, _NSEL, t1_ref)
            _top_values(s2, _NSEL, t2_ref)
            for r, (a, b) in enumerate(_CAND):
                cand_ref[r:r + 1, :] = t1_ref[a:a + 1, :] + t2_ref[b:b + 1, :]
            _top_values(cand_ref[...], _NSEL, csort_ref)
            thr = 0.5 * (csort_ref[PEER_TOPK - 1:PEER_TOPK, :] + csort_ref[PEER_TOPK:PEER_TOPK + 1, :])
            s_max = t1_ref[0:1, :] + t2_ref[0:1, :]
            cand = cand_ref[...]
            zsum = jnp.sum(jnp.where(cand >= thr, jnp.exp(cand - s_max), 0.0), axis=0, keepdims=True)
            ns1_ref[h] = -s1
            s2m_ref[h] = s2 - thr
            e1_ref[h] = jnp.exp(s1 - t1_ref[0:1, :]) / zsum
            e2_ref[h] = jnp.exp(s2 - t2_ref[0:1, :])
            return 0

        lax.fori_loop(0, PEER_HEADS, per_head, 0)

    xb = xn_ref[...]

    w_tiles = []
    for j in range(te // _SUB):
        r0 = j * _SUB
        a_t = lax.dot_general(u_ref[r0:r0 + _SUB, :], xb, _NT, preferred_element_type=F32)
        n_i1 = _SUB // PEER_NKEYS
        i1s = [e * (te // PEER_NKEYS) + j * n_i1 + r for r in range(n_i1)]
        ns1_rows = [[ns1_ref[h, pl.ds(i1, 1), :] for h in range(PEER_HEADS)] for i1 in i1s]
        e1_rows = [[e1_ref[h, pl.ds(i1, 1), :] for h in range(PEER_HEADS)] for i1 in i1s]
        tiles = [[] for _ in i1s]
        for c0 in range(0, tn, LANES):
            gsums = [None] * n_i1
            for h in range(PEER_HEADS):
                s2m_t = s2m_ref[h, :, c0:c0 + LANES]
                e2_t = e2_ref[h, :, c0:c0 + LANES]
                for r in range(n_i1):
                    term = jnp.where(s2m_t >= ns1_rows[r][h][:, c0:c0 + LANES],
                                     e2_t * e1_rows[r][h][:, c0:c0 + LANES], 0.0)
                    gsums[r] = term if gsums[r] is None else gsums[r] + term
            for r in range(n_i1):
                tiles[r].append(gsums[r])
        gates = jnp.concatenate([jnp.concatenate(t, axis=1) for t in tiles], axis=0)
        w_tiles.append((_gelu_tanh(a_t) * gates).astype(BF16))
    acc_ref[...] += jnp.dot(vt_ref[...], jnp.concatenate(w_tiles, axis=0), preferred_element_type=F32)

    @pl.when(e == n_e - 1)
    def _epilogue():
        y = h_ref[...] + acc_ref[...].T
        if final_norm:
            y = y * lax.rsqrt(jnp.mean(y * y, axis=-1, keepdims=True) + RMS_EPS) * gf_ref[...]
        o_ref[...] = y


def peer_residual(h, g, wq, k1, k2, u_bf, vt_bf, final_g=None, tn=640, te=1024, name="peer"):
    m, d = h.shape
    tn = min(tn, m)
    assert m % tn == 0 and N_EXPERTS % te == 0 and te % _SUB == 0
    final_norm = final_g is not None
    gf = (final_g if final_norm else jnp.ones((d,), F32)).astype(F32).reshape(1, d)
    const = lambda *shape: pl.BlockSpec(shape, lambda i, j: (0,) * len(shape))
    big = lambda: pltpu.VMEM((PEER_HEADS, PEER_NKEYS, tn), F32)
    return pl.pallas_call(
        functools.partial(_peer_body, te=te, final_norm=final_norm),
        grid=(m // tn, N_EXPERTS // te),
        in_specs=[
            pl.BlockSpec((tn, d), lambda i, j: (i, 0)),
            const(1, d),
            const(d, PEER_HEADS * PEER_QDIM),
            const(PEER_HEADS, PEER_NKEYS, PEER_HALF),
            const(PEER_HEADS, PEER_NKEYS, PEER_HALF),
            pl.BlockSpec((te, d), lambda i, j: (j, 0)),
            pl.BlockSpec((d, te), lambda i, j: (0, j)),
            const(1, d),
        ],
        out_specs=pl.BlockSpec((tn, d), lambda i, j: (i, 0)),
        out_shape=jax.ShapeDtypeStruct((m, d), F32),
        scratch_shapes=[
            pltpu.VMEM((tn, d), BF16),
            pltpu.VMEM((2 * PEER_HEADS, tn, PEER_HALF), F32),
            big(), big(), big(), big(),
            pltpu.VMEM((_round_up(_NSEL, 8), tn), F32),
            pltpu.VMEM((_round_up(_NSEL, 8), tn), F32),
            pltpu.VMEM((_NCAND, tn), F32),
            pltpu.VMEM((_round_up(_NSEL, 8), tn), F32),
            pltpu.VMEM((d, tn), F32),
        ],
        compiler_params=_cparams("parallel", "arbitrary"),
        name=name,
    )(h, g.astype(F32).reshape(1, d), wq.astype(BF16), k1.astype(F32), k2.astype(F32), u_bf, vt_bf, gf)


def _kv_layouts(k_all, v_all, tk):
    b, t, w = k_all.shape
    vt = v_all.astype(BF16).reshape(b, t // tk, tk, w).transpose(0, 1, 3, 2)
    return k_all.astype(BF16), vt


def odd_mixer(h, g, w_in, b_f, w_out, k_past, v_past, logf_past, t_valid):
    b, t, d = h.shape
    p = k_past.shape[1]
    rows = b * t
    w_f = jnp.pad(w_in[:, 3 * FOX_W:], ((0, 0), (0, LANES - FOX_HEADS)))
    w_bf = jnp.concatenate([w_in[:, :3 * FOX_W], w_f], axis=1).astype(BF16)
    bias = jnp.pad(b_f.astype(F32), (0, LANES - FOX_HEADS)).reshape(1, LANES)
    splits = (FOX_W, FOX_W, FOX_W, LANES)
    tk = ROW_TILE
    if p == 0 and b == 1 and t % tk == 0:
        q_bf, k, v, k_bf, vt_bf, logf = norm_proj(
            h.reshape(rows, d), g, w_bf, splits, bias=bias, rows_out=t_valid, name="odd_in_proj",
            outs=((0, "bf16"), (1, "f32_rows"), (2, "f32_rows"), (1, "bf16"), (2, "bf16_t"), (3, "f32")))
        logf = logf[None]
        f_cum = cumsum_rows(logf, name="fox_cumsum")
        o = fox_attention(q_bf[None], k_bf[None], vt_bf[None], f_cum, f_cum[:, :, :FOX_HEADS], 0, t_valid)
        k = k[None]
        v = v[None]
    else:
        q, k, v, logf = norm_proj(h.reshape(rows, d), g, w_bf, splits, bias=bias, name="odd_in_proj")
        q = q.reshape(b, t, FOX_W)
        k = k.reshape(b, t, FOX_W)
        v = v.reshape(b, t, FOX_W)
        logf = logf.reshape(b, t, LANES)
        tq_pad = _round_up(t, LANES)
        tk_pad = _round_up(p + t, tk)
        k_all = _pad_rows(jnp.concatenate([k_past.reshape(b, p, FOX_W), k], axis=1), tk_pad, 1)
        v_all = _pad_rows(jnp.concatenate([v_past.reshape(b, p, FOX_W), v], axis=1), tk_pad, 1)
        logf_past = jnp.pad(logf_past.astype(F32), ((0, 0), (0, 0), (0, LANES - FOX_HEADS)))
        logf_all = _pad_rows(jnp.concatenate([logf_past, logf], axis=1), tk_pad, 1)
        f_cum = cumsum_rows(logf_all, name="fox_cumsum")
        f_q = _pad_rows(f_cum[:, p:p + t, :FOX_HEADS], tq_pad, 1)
        k_bf, vt_bf = _kv_layouts(k_all, v_all, tk)
        q_bf = _pad_rows(q, tq_pad, 1).astype(BF16)
        o = fox_attention(q_bf, k_bf, vt_bf, f_cum, f_q, p, t_valid)[:, :t]
        k = k[:, :t_valid]
        v = v[:, :t_valid]
    h_new = out_proj_residual([o.reshape(rows, FOX_W)], h.reshape(rows, d), w_out.astype(BF16), name="odd_out_proj")
    return (h_new.reshape(b, t, d), k.reshape(b, t_valid, FOX_HEADS, FOX_DIM), v.reshape(b, t_valid, FOX_HEADS, FOX_DIM),
            logf[:, :t_valid, :FOX_HEADS])


def kernel(x_prompt, x_sample, cache_sb_k, cache_sb_v, state_gdn, state_gdn_conv, cache_fox_k, cache_fox_v, cache_fox_logf, meta_tokens, norm_mix, norm_ffn, norm_final, w_in_even, w_out_even, gdn_conv_w, gdn_a_log, gdn_dt_bias, gdn_norm, w_in_odd, b_forget, w_out_odd, peer_wq, peer_k1, peer_k2, peer_u, peer_v):
    bsz, seq, d = x_prompt.shape
    dec_b, dec_t, _ = x_sample.shape
    depth = norm_mix.shape[0]
    dt = x_prompt.dtype
    t_p = N_META + seq
    t_pad = _round_up(t_p, 5 * ROW_TILE)

    meta = jnp.broadcast_to(meta_tokens.astype(dt)[None], (bsz, N_META, d))
    hp = _pad_rows(jnp.concatenate([meta, x_prompt], axis=1), t_pad, 1)
    hs = x_sample

    empty_sb = jnp.zeros((bsz, 0, SB_HEADS, SB_DIM), dt)
    zero_s = jnp.zeros((bsz, GDN_HEADS, GDN_DK, GDN_DV), dt)
    zero_buf = jnp.zeros((bsz, GDN_CONV - 1, GDN_CONV_DIM), dt)
    empty_fox = jnp.zeros((bsz, 0, FOX_HEADS, FOX_DIM), dt)
    empty_logf = jnp.zeros((bsz, 0, FOX_HEADS), dt)

    sbk_p, sbv_p, sbk_s, sbv_s = [], [], [], []
    gs_p, gs_s, gc_p, gc_s = [], [], [], []
    fk_p, fv_p, ff_p, fk_s, fv_s, ff_s = [], [], [], [], [], []

    for layer in range(depth):
        if layer % 2 == 0:
            e = layer // 2
            w = (norm_mix[layer], w_in_even[e], w_out_even[e], gdn_conv_w[e], gdn_a_log[e], gdn_dt_bias[e], gdn_norm[e])
            hp, kp, vp, sp, bp = even_mixer(hp, *w, empty_sb, empty_sb, zero_s, zero_buf, t_p)
            hs, ks_, vs_, ss, bs = even_mixer(hs, *w, cache_sb_k[e], cache_sb_v[e], state_gdn[e], state_gdn_conv[e], dec_t)
            sbk_p.append(kp); sbv_p.append(vp); sbk_s.append(ks_); sbv_s.append(vs_)
            gs_p.append(sp); gs_s.append(ss); gc_p.append(bp); gc_s.append(bs)
        else:
            o = layer // 2
            w = (norm_mix[layer], w_in_odd[o], b_forget[o], w_out_odd[o])
            hp, kp, vp, fp = odd_mixer(hp, *w, empty_fox, empty_fox, empty_logf, t_p)
            hs, ks_, vs_, fs = odd_mixer(hs, *w, cache_fox_k[o], cache_fox_v[o], cache_fox_logf[o], dec_t)
            fk_p.append(kp); fv_p.append(vp); ff_p.append(fp)
            fk_s.append(ks_); fv_s.append(vs_); ff_s.append(fs)
        last = layer == depth - 1
        u_bf = peer_u[layer].astype(BF16)
        vt_bf = peer_v[layer].T.astype(BF16)
        pw = (norm_ffn[layer], peer_wq[layer], peer_k1[layer], peer_k2[layer], u_bf, vt_bf, norm_final if last else None)
        hp = peer_residual(hp.reshape(bsz * t_pad, d), *pw, name="peer_prompt").reshape(bsz, t_pad, d)
        hs = peer_residual(hs.reshape(dec_b * dec_t, d), *pw, name="peer_sample").reshape(dec_b, dec_t, d)

    y_prompt = hp[:, N_META:t_p]
    y_sample = hs
    return (y_prompt, y_sample,
            jnp.stack(sbk_p), jnp.stack(sbv_p), jnp.stack(sbk_s), jnp.stack(sbv_s),
            jnp.stack(gs_p), jnp.stack(gs_s), jnp.stack(gc_p), jnp.stack(gc_s),
            jnp.stack(fk_p), jnp.stack(fv_p), jnp.stack(ff_p),
            jnp.stack(fk_s), jnp.stack(fv_s), jnp.stack(ff_s))
```

```python
import functools

import jax
import jax.numpy as jnp
from jax import lax
from jax.experimental import pallas as pl
from jax.experimental.pallas import tpu as pltpu

F32 = jnp.float32
BF16 = jnp.bfloat16

D_MODEL = 1024
N_META = 16
CHUNK = 64
SB_DIM = 64
SB_HEADS = 8
SB_W = SB_HEADS * SB_DIM
GDN_DK = 128
GDN_DV = 128
GDN_HEADS = 4
GDN_QK = GDN_HEADS * GDN_DK
GDN_V = GDN_HEADS * GDN_DV
GDN_CONV = 4
GDN_CONV_DIM = 2 * GDN_QK + GDN_V
FOX_DIM = 64
FOX_HEADS = 16
FOX_W = FOX_HEADS * FOX_DIM
PEER_HEADS = 8
PEER_NKEYS = 128
PEER_TOPK = 16
PEER_QDIM = 256
N_EXPERTS = PEER_NKEYS ** 2
RMS_EPS = 1e-6

LANES = 128
ROW_TILE = 256
ATT_TQ = 512
VMEM_LIMIT = 60 * 1024 * 1024
NEG_BIG = -1e30

_NT = (((1,), (1,)), ((), ()))


def _cparams(*sem):
    return pltpu.CompilerParams(dimension_semantics=sem, vmem_limit_bytes=VMEM_LIMIT)


def _round_up(n, m):
    return -(-n // m) * m


def _pad_rows(a, rows, axis=0):
    pad = [(0, 0)] * a.ndim
    pad[axis] = (0, rows - a.shape[axis])
    return jnp.pad(a, pad)


def _norm_proj_body(x_ref, g_ref, w_ref, b_ref, *out_refs, offs, outs, logsig_split, head_stats):
    x = x_ref[...]
    xn = x * lax.rsqrt(jnp.mean(x * x, axis=-1, keepdims=True) + RMS_EPS) * g_ref[...]
    xb = xn.astype(BF16)
    cols = {}
    for (i, kind), o_ref in zip(outs, out_refs):
        if i not in cols:
            y = jnp.dot(xb, w_ref[:, offs[i]:offs[i + 1]], preferred_element_type=F32)
            cols[i] = jax.nn.log_sigmoid(y + b_ref[...]) if i == logsig_split else y
        y = cols[i]
        if kind == "bf16_t":
            o_ref[0] = y.T.astype(BF16)
        else:
            o_ref[...] = y.astype(o_ref.dtype)
    if head_stats is not None:
        iq, ik, dim = head_stats
        n = offs[iq + 1] - offs[iq]
        group = (lax.broadcasted_iota(jnp.int32, (n, LANES), 0) // dim
                 == lax.broadcasted_iota(jnp.int32, (n, LANES), 1)).astype(BF16)
        yq, yk = cols[iq], cols[ik]
        for o_ref, prod in zip(out_refs[len(outs):], (yq * yq, yk * yk, yq * yk)):
            o_ref[...] = jnp.dot(prod.astype(BF16), group, preferred_element_type=F32)


def norm_proj(x, g, w_bf, splits, outs=None, bias=None, rows_out=None, head_stats=None, name="norm_proj"):
    m, d = x.shape
    n = w_bf.shape[1]
    offs = [0]
    for s in splits:
        offs.append(offs[-1] + s)
    tm = min(ROW_TILE, m)
    assert offs[-1] == n and m % tm == 0
    outs = tuple((i, "f32") for i in range(len(splits))) if outs is None else tuple(outs)
    logsig_split = len(splits) - 1 if bias is not None else -1
    if bias is None:
        bias = jnp.zeros((1, splits[-1]), F32)
    out_specs, out_shape = [], []
    for i, kind in outs:
        s = splits[i]
        if kind == "bf16_t":
            out_specs.append(pl.BlockSpec((1, s, tm), lambda r: (r, 0, 0)))
            out_shape.append(jax.ShapeDtypeStruct((m // tm, s, tm), BF16))
        else:
            rows = rows_out if kind == "f32_rows" else m
            assert m - rows < tm
            out_specs.append(pl.BlockSpec((tm, s), lambda r: (r, 0)))
            out_shape.append(jax.ShapeDtypeStruct((rows, s), BF16 if kind == "bf16" else F32))
    if head_stats is not None:
        assert {head_stats[0], head_stats[1]} <= {i for i, _ in outs}
        out_specs += [pl.BlockSpec((tm, LANES), lambda r: (r, 0))] * 3
        out_shape += [jax.ShapeDtypeStruct((m, LANES), F32)] * 3
    return pl.pallas_call(
        functools.partial(_norm_proj_body, offs=tuple(offs), outs=outs, logsig_split=logsig_split,
                          head_stats=head_stats),
        grid=(m // tm,),
        in_specs=[
            pl.BlockSpec((tm, d), lambda i: (i, 0)),
            pl.BlockSpec((1, d), lambda i: (0, 0)),
            pl.BlockSpec((d, n), lambda i: (0, 0)),
            pl.BlockSpec((1, splits[-1]), lambda i: (0, 0)),
        ],
        out_specs=out_specs,
        out_shape=out_shape,
        compiler_params=_cparams("parallel"),
        name=name,
    )(x, g.reshape(1, d), w_bf, bias)


def _out_proj_body(*refs, n_in, offs):
    a_refs = refs[:n_in]
    h_ref, w_ref, o_ref = refs[n_in:]
    acc = h_ref[...]
    for i, a_ref in enumerate(a_refs):
        acc = acc + jnp.dot(a_ref[...].astype(BF16), w_ref[offs[i]:offs[i + 1], :], preferred_element_type=F32)
    o_ref[...] = acc


def out_proj_residual(parts, h, w_bf, name="out_proj"):
    m, d = h.shape
    offs = [0]
    for a in parts:
        offs.append(offs[-1] + a.shape[1])
    tm = min(ROW_TILE, m)
    assert offs[-1] == w_bf.shape[0] and m % tm == 0
    return pl.pallas_call(
        functools.partial(_out_proj_body, n_in=len(parts), offs=tuple(offs)),
        grid=(m // tm,),
        in_specs=[pl.BlockSpec((tm, a.shape[1]), lambda i: (i, 0)) for a in parts] + [
            pl.BlockSpec((tm, d), lambda i: (i, 0)),
            pl.BlockSpec(w_bf.shape, lambda i: (0, 0)),
        ],
        out_specs=pl.BlockSpec((tm, d), lambda i: (i, 0)),
        out_shape=jax.ShapeDtypeStruct((m, d), F32),
        compiler_params=_cparams("parallel"),
        name=name,
    )(*parts, h, w_bf)


def _cumsum_body(x_ref, o_ref, carry_ref):
    @pl.when(pl.program_id(1) == 0)
    def _():
        carry_ref[...] = jnp.zeros_like(carry_ref)

    x = x_ref[0]
    t = x.shape[0]
    tri = (lax.broadcasted_iota(jnp.int32, (t, t), 0) >= lax.broadcasted_iota(jnp.int32, (t, t), 1)).astype(F32)
    c = jnp.dot(tri, x, preferred_element_type=F32, precision=lax.Precision.HIGHEST) + carry_ref[...]
    o_ref[0] = c
    carry_ref[...] = c[t - 1:t, :]


def cumsum_rows(x, name="cumsum_rows"):
    b, l, c = x.shape
    tm = ROW_TILE
    assert l % tm == 0
    return pl.pallas_call(
        _cumsum_body,
        grid=(b, l // tm),
        in_specs=[pl.BlockSpec((1, tm, c), lambda i, j: (i, j, 0))],
        out_specs=pl.BlockSpec((1, tm, c), lambda i, j: (i, j, 0)),
        out_shape=jax.ShapeDtypeStruct((b, l, c), F32),
        scratch_shapes=[pltpu.VMEM((1, c), F32)],
        compiler_params=_cparams("parallel", "arbitrary"),
        name=name,
    )(x)


def _fox_body(nsteps_ref, q_ref, k_ref, vt_ref, fk_ref, fq_ref, o_ref, acc_ref, s_ref, p_ref, *, q_offset, tq, tk):
    bi = pl.program_id(0)
    hp = pl.program_id(1)
    qi = pl.program_id(2)
    q = q_ref[0] * jnp.asarray(FOX_DIM ** -0.5, BF16)
    lane = lax.broadcasted_iota(jnp.int32, (1, LANES), 1)
    q_heads = (jnp.where(lane < FOX_DIM, q, jnp.zeros_like(q)), jnp.where(lane >= FOX_DIM, q, jnp.zeros_like(q)))
    q0 = q_offset + qi * tq
    q_pos = q0 + lax.broadcasted_iota(jnp.int32, (1, tq), 1)
    kb_diag = jnp.minimum((q0 + tq - 1) // tk, k_ref.shape[1] // tk - 1)
    n_causal = max(tq // tk, 1)
    fq_all = fq_ref[0]
    qf_heads = []
    for h in range(2):
        lo = _F_PARTS * (2 * hp + h)
        mine = ((lane >= lo) & (lane < lo + _F_PARTS)) | ((lane >= _F_ONES + lo) & (lane < _F_ONES + lo + _F_PARTS))
        qf_heads.append(jnp.concatenate([q_heads[h], jnp.where(mine, fq_all, jnp.zeros_like(fq_all))], axis=1))
    acc_ref[...] = jnp.zeros_like(acc_ref)

    def scores(kb):
        k0 = pl.multiple_of(kb * tk, tk)
        kf = jnp.concatenate([k_ref[0, pl.ds(k0, tk), :], fk_ref[0, pl.ds(k0, tk), :]], axis=1)
        for h in range(2):
            s_ref[h] = lax.dot_general(kf, qf_heads[h], _NT, preferred_element_type=F32)

    def accumulate(kb, alphas):
        vt = vt_ref[0, kb]
        for h in range(2):
            acc_ref[h] = alphas[h] * acc_ref[h] + jnp.dot(vt, p_ref[h], preferred_element_type=F32)

    def softmax(kb, carry, causal):
        if causal:
            mask = (kb * tk + lax.broadcasted_iota(jnp.int32, (tk, 1), 0)) <= q_pos
        stats, alphas = [], []
        for h in range(2):
            m_prev, l_prev = carry[2 * h], carry[2 * h + 1]
            s = s_ref[h]
            if causal:
                s = jnp.where(mask, s, NEG_BIG)
            m_new = jnp.maximum(m_prev, jnp.max(s, axis=0, keepdims=True))
            p = jnp.exp(s - m_new)
            alpha = jnp.exp(m_prev - m_new)
            p_ref[h] = p.astype(BF16)
            stats += [m_new, alpha * l_prev + jnp.sum(p, axis=0, keepdims=True)]
            alphas.append(alpha)
        return tuple(stats + alphas)

    def step(j, carry):
        kb = kb_diag - j
        accumulate(kb + 1, carry[4:6])
        carry = softmax(kb, carry, causal=False)
        scores(jnp.maximum(kb - 1, 0))
        return carry

    ones = jnp.ones((1, tq), F32)
    init = (jnp.full((1, tq), NEG_BIG, F32), jnp.zeros((1, tq), F32),
            jnp.full((1, tq), NEG_BIG, F32), jnp.zeros((1, tq), F32), ones, ones)
    n_steps = nsteps_ref[bi, hp, qi]
    scores(kb_diag)
    carry = init
    for c in range(n_causal):
        if c > 0:
            accumulate(kb_diag - c + 1, carry[4:6])
        carry = softmax(kb_diag - c, carry, causal=True)
        scores(jnp.maximum(kb_diag - c - 1, 0))
    fin = lax.fori_loop(n_causal, n_steps, step, carry)
    accumulate(kb_diag - (n_steps - 1), fin[4:6])
    row = lax.broadcasted_iota(jnp.int32, (LANES, 1), 0)
    ot = jnp.where(row < FOX_DIM, acc_ref[0] / fin[1], acc_ref[1] / fin[3])
    o_ref[0] = ot.T


_UNDERFLOW = 110.0


def _fox_block_counts(stats, f_keys, f_q, q_offset, t_valid, tq_all, tq, tk):
    qq, kk_max, qk = stats
    b, _, heads = qq.shape
    tk_all = f_keys.shape[1]
    nq, nk = pl.cdiv(tq_all, tq), tk_all // tk
    scale = FOX_DIM ** -0.5
    reach = jnp.sqrt(qq * kk_max[:, None, :]) * scale
    bound = _pad_rows(1.02 * reach - qk * scale, tq_all, 1) + f_q
    valid = (jnp.arange(tq_all) < t_valid)[None, :, None]
    bound = jnp.pad(jnp.where(valid, bound, -jnp.inf), ((0, 0), (0, nq * tq - tq_all), (0, 0)),
                    constant_values=-jnp.inf)
    cq = jnp.max(bound.reshape(b, nq, tq, heads), axis=2)
    f_end = f_keys[:, tk - 1::tk, :heads]
    kb_diag = jnp.minimum((q_offset + jnp.arange(nq) * tq + tq - 1) // tk, nk - 1)
    need = (cq[:, :, None, :] - f_end[:, None, :, :]) > -_UNDERFLOW
    need = need & (jnp.arange(nk)[None, None, :, None] <= kb_diag[None, :, None, None])
    first = jnp.min(jnp.where(need, jnp.arange(nk)[None, None, :, None], nk), axis=2)
    first = jnp.min(first.reshape(b, nq, heads // 2, 2), axis=-1)
    most = kb_diag[None, :, None] + 1
    steps = jnp.clip(kb_diag[None, :, None] - first + 1, jnp.minimum(max(tq // tk, 1), most), most)
    return steps.transpose(0, 2, 1).astype(jnp.int32)


_F_PARTS = 3
_F_ONES = _F_PARTS * FOX_HEADS


def _bf16_head(x):
    bits = lax.bitcast_convert_type(x, jnp.uint32) & jnp.uint32(0xFFFF0000)
    return lax.bitcast_convert_type(bits, F32)


def _split_f_operand(f, key_side):
    b, t, heads = f.shape
    f = f.astype(F32)
    hi = _bf16_head(f)
    mid = _bf16_head(f - hi)
    lo = f - hi - mid
    pieces = jnp.stack([hi, mid, lo], axis=-1).reshape(b, t, heads * _F_PARTS).astype(BF16)
    const = jnp.full((b, t, heads * _F_PARTS), 1.0 if key_side else -1.0, BF16)
    both = [pieces, const] if key_side else [const, pieces]
    return jnp.pad(jnp.concatenate(both, axis=-1), ((0, 0), (0, 0), (0, LANES - 2 * heads * _F_PARTS)))


def fox_attention(q_bf, k_bf, vt_bf, f_keys, f_q, stats, q_offset, t_valid, name="fox_attention"):
    b, tq_all, w = q_bf.shape
    tk_all = k_bf.shape[1]
    tq = min(ATT_TQ, tq_all)
    tk = ROW_TILE
    assert tk_all % tk == 0
    assert (tq % tk == 0 and q_offset % tk == 0) or (tk % tq == 0 and q_offset % tq == 0)
    nsteps = _fox_block_counts(stats, f_keys, f_q, q_offset, t_valid, tq_all, tq, tk)
    fk_aug = _split_f_operand(f_keys[:, :, :FOX_HEADS], key_side=True)
    fq_aug = _split_f_operand(f_q, key_side=False)
    return pl.pallas_call(
        functools.partial(_fox_body, q_offset=q_offset, tq=tq, tk=tk),
        grid_spec=pltpu.PrefetchScalarGridSpec(
            num_scalar_prefetch=1,
            grid=(b, w // LANES, pl.cdiv(tq_all, tq)),
            in_specs=[
                pl.BlockSpec((1, tq, LANES), lambda i, h, j, n: (i, j, h)),
                pl.BlockSpec((1, tk_all, LANES), lambda i, h, j, n: (i, 0, h)),
                pl.BlockSpec((1, tk_all // tk, LANES, tk), lambda i, h, j, n: (i, 0, h, 0)),
                pl.BlockSpec((1, tk_all, LANES), lambda i, h, j, n: (i, 0, 0)),
                pl.BlockSpec((1, tq, LANES), lambda i, h, j, n: (i, j, 0)),
            ],
            out_specs=pl.BlockSpec((1, tq, LANES), lambda i, h, j, n: (i, j, h)),
            scratch_shapes=[pltpu.VMEM((2, LANES, tq), F32), pltpu.VMEM((2, tk, tq), F32),
                            pltpu.VMEM((2, tk, tq), BF16)],
        ),
        out_shape=jax.ShapeDtypeStruct((b, tq_all, w), F32),
        compiler_params=_cparams("parallel", "parallel", "arbitrary"),
        name=name,
    )(nsteps, q_bf, k_bf, vt_bf, fk_aug, fq_aug)


def _sb_body(q_ref, k_ref, vt_ref, o_ref, acc_ref, *, q_offset, tq, tk, tq_all):
    qi = pl.program_id(2)
    q = q_ref[0] * jnp.asarray(SB_DIM ** -0.5, BF16)
    lane = lax.broadcasted_iota(jnp.int32, (1, LANES), 1)
    q_heads = (jnp.where(lane < SB_DIM, q, jnp.zeros_like(q)), jnp.where(lane >= SB_DIM, q, jnp.zeros_like(q)))
    q0 = q_offset + qi * tq
    q_pos = q0 + lax.broadcasted_iota(jnp.int32, (1, tq), 1)
    n_kb = jnp.minimum(jnp.maximum(q0 + tq - 2, 0) // tk + 1, k_ref.shape[1] // tk)
    real_query = q_pos < q_offset + tq_all
    upper = (lax.broadcasted_iota(jnp.int32, (tk, tk), 1) > lax.broadcasted_iota(jnp.int32, (tk, tk), 0)).astype(BF16)
    acc_ref[...] = jnp.zeros_like(acc_ref)

    def step(carry):
        j = carry[0]
        kb = n_kb - 1 - j
        k0 = pl.multiple_of(kb * tk, tk)
        k = k_ref[0, pl.ds(k0, tk), :]
        vt = vt_ref[0, kb]
        k_pos = k0 + lax.broadcasted_iota(jnp.int32, (tk, 1), 0)
        mask = k_pos < q_pos
        out = []
        for h in range(2):
            r_prev = carry[2 + h]
            z = lax.dot_general(k, q_heads[h], _NT, preferred_element_type=F32)
            sp = jnp.maximum(z, 0.0) + jnp.log(1.0 + jnp.exp(-jnp.abs(z)))
            l = jnp.where(mask, -sp, 0.0)
            l_hi = l.astype(BF16)
            l_lo = (l - l_hi.astype(F32)).astype(BF16)
            later = (jnp.dot(upper, l_hi, preferred_element_type=F32)
                     + jnp.dot(upper, l_lo, preferred_element_type=F32))
            w = jnp.where(mask, jnp.exp((z - sp) + later + r_prev), 0.0)
            acc_ref[h] += jnp.dot(vt, w.astype(BF16), preferred_element_type=F32)
            out.append(r_prev + later[0:1, :] + l[0:1, :])
        live = jnp.where(real_query, jnp.maximum(out[0], out[1]), -jnp.inf)
        return (j + 1, jnp.max(live), out[0], out[1])

    def more(carry):
        return (carry[0] < n_kb) & (carry[1] > -_UNDERFLOW)

    lax.while_loop(more, step, (jnp.int32(0), jnp.float32(0.0), jnp.zeros((1, tq), F32), jnp.zeros((1, tq), F32)))
    row = lax.broadcasted_iota(jnp.int32, (LANES, 1), 0)
    o_ref[0] = jnp.where(row < SB_DIM, acc_ref[0], acc_ref[1]).T


def sb_attention(q_bf, k_bf, vt_bf, q_offset, name="sb_attention"):
    b, tq_all, w = q_bf.shape
    tk_all = k_bf.shape[1]
    tq = min(ROW_TILE, tq_all)
    tk = ROW_TILE
    assert tk_all % tk == 0
    return pl.pallas_call(
        functools.partial(_sb_body, q_offset=q_offset, tq=tq, tk=tk, tq_all=tq_all),
        grid=(b, w // LANES, pl.cdiv(tq_all, tq)),
        in_specs=[
            pl.BlockSpec((1, tq, LANES), lambda i, h, j: (i, j, h)),
            pl.BlockSpec((1, tk_all, LANES), lambda i, h, j: (i, 0, h)),
            pl.BlockSpec((1, tk_all // tk, LANES, tk), lambda i, h, j: (i, 0, h, 0)),
        ],
        out_specs=pl.BlockSpec((1, tq, LANES), lambda i, h, j: (i, j, h)),
        out_shape=jax.ShapeDtypeStruct((b, tq_all, w), F32),
        scratch_shapes=[pltpu.VMEM((2, LANES, tq), F32)],
        compiler_params=_cparams("parallel", "parallel", "arbitrary"),
        name=name,
    )(q_bf, k_bf, vt_bf)


_HI = lax.Precision.HIGHEST
_TN = (((0,), (0,)), ((), ()))
_CONV_PAD = 8
_GDN_CPS = 4
_GDN_ROWS = _GDN_CPS * CHUNK


def _dot_hi(a, b):
    return jnp.dot(a, b, preferred_element_type=F32, precision=_HI)


def _split_bf16(x):
    hi = x.astype(BF16)
    return hi, (x - hi.astype(F32)).astype(BF16)


def _einsum3(spec, a, b):
    ah, al = a if isinstance(a, tuple) else _split_bf16(a)
    bh, bl = b if isinstance(b, tuple) else _split_bf16(b)
    prod = functools.partial(jnp.einsum, spec, preferred_element_type=F32)
    return prod(ah, bh) + (prod(ah, bl) + prod(al, bh))


def _bmm(a, b):
    return _einsum3('hij,hjk->hik', a, b)


def _bmm_nt(a, b):
    return _einsum3('hik,hjk->hij', a, b)


def _softplus(x):
    return jnp.maximum(x, 0.0) + jnp.log1p(jnp.exp(-jnp.abs(x)))


def _silu(x):
    return x / (1.0 + jnp.exp(-x))


def _unit_lower_inverse(m):
    c_len = m.shape[-1]
    ri = lax.broadcasted_iota(jnp.int32, (c_len, c_len), 0)
    ci = lax.broadcasted_iota(jnp.int32, (c_len, c_len), 1)
    d = jnp.broadcast_to((ri == ci).astype(F32), m.shape)
    s = 1
    while s < c_len:
        join = (ri // (2 * s) == ci // (2 * s)) & (ri % (2 * s) >= s) & (ci % (2 * s) < s)
        c = jnp.where(join, m, 0.0)
        if s == 1:
            d = d - c
        else:
            d_s = _split_bf16(d)
            d = d - _bmm(_bmm(d_s, c), d_s)
        s *= 2
    return d


def _gdn_body(x_ref, z_ref, ab_ref, buf_ref, s0_ref, cw_ref, alog_ref, dt_ref, gn_ref,
              o_ref, sfin_ref, xwin_ref, s_ref, *, t_valid):
    c = pl.program_id(1)
    n_c = pl.num_programs(1)
    hist = GDN_CONV - 1

    @pl.when(c == 0)
    def _():
        xwin_ref[_CONV_PAD - hist:_CONV_PAD, :] = buf_ref[0]
        s_ref[...] = s0_ref[0]

    rows = _GDN_ROWS
    xwin_ref[_CONV_PAD:_CONV_PAD + rows, :] = x_ref[0]
    conv = xwin_ref[_CONV_PAD - hist:_CONV_PAD - hist + rows, :] * cw_ref[0:1, :]
    for i in range(1, GDN_CONV):
        conv = conv + xwin_ref[_CONV_PAD - hist + i:_CONV_PAD - hist + i + rows, :] * cw_ref[i:i + 1, :]
    tail = xwin_ref[_CONV_PAD + rows - hist:_CONV_PAD + rows, :]
    xwin_ref[_CONV_PAD - hist:_CONV_PAD, :] = tail
    act = _silu(conv)

    ab = ab_ref[0]
    row_ok = (c * rows + lax.broadcasted_iota(jnp.int32, (rows, 1), 0)) < t_valid
    g_all = jnp.where(row_ok, -jnp.exp(alog_ref[...]) * _softplus(ab + dt_ref[...]), 0.0)
    beta_all = jnp.where(row_ok, 1.0 / (1.0 + jnp.exp(-ab)), 0.0)
    rr = lax.broadcasted_iota(jnp.int32, (rows, rows), 0)
    rc = lax.broadcasted_iota(jnp.int32, (rows, rows), 1)
    tri_chunks = ((rr >= rc) & (rr // CHUNK == rc // CHUNK)).astype(F32)
    gcum_all = _dot_hi(tri_chunks, g_all)
    sel = (lax.broadcasted_iota(jnp.int32, (8, LANES), 0) == lax.broadcasted_iota(jnp.int32, (8, LANES), 1)).astype(F32)
    gcum_rows = lax.dot_general(sel, gcum_all, _NT, preferred_element_type=F32, precision=_HI)
    ri = lax.broadcasted_iota(jnp.int32, (CHUNK, CHUNK), 0)
    ci = lax.broadcasted_iota(jnp.int32, (CHUNK, CHUNK), 1)
    tri = ri >= ci
    strict = ri > ci

    pairs = [(ck, h) for ck in range(_GDN_CPS) for h in range(GDN_HEADS)]
    rows_of = lambda ck: slice(ck * CHUNK, (ck + 1) * CHUNK)
    q4 = jnp.stack([act[rows_of(ck), h * GDN_DK:(h + 1) * GDN_DK] for ck, h in pairs])
    k4 = jnp.stack([act[rows_of(ck), GDN_QK + h * GDN_DK:GDN_QK + (h + 1) * GDN_DK] for ck, h in pairs])
    v4 = jnp.stack([act[rows_of(ck), 2 * GDN_QK + h * GDN_DV:2 * GDN_QK + (h + 1) * GDN_DV] for ck, h in pairs])
    q4 = q4 * lax.rsqrt(jnp.sum(q4 * q4, axis=-1, keepdims=True) + RMS_EPS) * (GDN_DK ** -0.5)
    k4 = k4 * lax.rsqrt(jnp.sum(k4 * k4, axis=-1, keepdims=True) + RMS_EPS)
    beta = jnp.stack([beta_all[rows_of(ck), GDN_HEADS + h:GDN_HEADS + h + 1] for ck, h in pairs])
    gc = jnp.stack([gcum_all[rows_of(ck), h:h + 1] for ck, h in pairs])
    gr = jnp.stack([gcum_rows[h:h + 1, rows_of(ck)] for ck, h in pairs])
    decay = jnp.exp(jnp.where(tri, gc - gr, NEG_BIG))
    kb = k4 * beta
    k4_s = _split_bf16(k4)
    m = jnp.where(strict, _bmm_nt(kb, k4_s) * decay, 0.0)
    tinv = _split_bf16(_unit_lower_inverse(m))
    eg = jnp.exp(gc)
    u = _bmm(tinv, v4 * beta)
    w = _split_bf16(_bmm(tinv, kb * eg))
    attn = _split_bf16(_bmm_nt(q4, k4_s) * decay)
    qe = _split_bf16(q4 * eg)
    g_last = gc[:, CHUNK - 1:CHUNK, :]
    k_dec = _split_bf16(k4 * jnp.exp(g_last - gc))
    s_scale = jnp.exp(g_last)

    s4 = s_ref[...]
    outs = []
    for ck in range(_GDN_CPS):
        sl = slice(ck * GDN_HEADS, (ck + 1) * GDN_HEADS)
        part = lambda pair: (pair[0][sl], pair[1][sl])
        s4_s = _split_bf16(s4)
        v_new_s = _split_bf16(u[sl] - _bmm(part(w), s4_s))
        outs.append(_bmm(part(qe), s4_s) + _bmm(part(attn), v_new_s))
        s4 = s4 * s_scale[sl] + _einsum3('hck,hcv->hkv', part(k_dec), v_new_s)
    s_ref[...] = s4
    for ck in range(_GDN_CPS):
        o = outs[ck]
        o = o * lax.rsqrt(jnp.mean(o * o, axis=-1, keepdims=True) + RMS_EPS) * gn_ref[...]
        for h in range(GDN_HEADS):
            cols = slice(h * GDN_DV, (h + 1) * GDN_DV)
            o_ref[0, rows_of(ck), cols] = o[h] * _silu(z_ref[0, rows_of(ck), cols])

    @pl.when(c == n_c - 1)
    def _():
        sfin_ref[0] = s_ref[...]


def gdn_heads(qkv_pre, z, ab, conv_buf, s0, conv_w, a_log, dt_bias, gnorm, t_valid, name="gdn"):
    b, t, cd = qkv_pre.shape
    rows = _GDN_ROWS
    assert t % rows == 0
    n_c = t // rows
    pad_l = lambda a: jnp.pad(a.astype(F32), (0, LANES - a.shape[0])).reshape(1, LANES)
    const = lambda *shape: pl.BlockSpec(shape, lambda i, j: (0,) * len(shape))
    return pl.pallas_call(
        functools.partial(_gdn_body, t_valid=t_valid),
        grid=(b, n_c),
        in_specs=[
            pl.BlockSpec((1, rows, cd), lambda i, j: (i, j, 0)),
            pl.BlockSpec((1, rows, GDN_V), lambda i, j: (i, j, 0)),
            pl.BlockSpec((1, rows, LANES), lambda i, j: (i, j, 0)),
            pl.BlockSpec((1, GDN_CONV - 1, cd), lambda i, j: (i, 0, 0)),
            pl.BlockSpec((1, GDN_HEADS, GDN_DK, GDN_DV), lambda i, j: (i, 0, 0, 0)),
            const(GDN_CONV, cd), const(1, LANES), const(1, LANES), const(1, GDN_DV),
        ],
        out_specs=[
            pl.BlockSpec((1, rows, GDN_V), lambda i, j: (i, j, 0)),
            pl.BlockSpec((1, GDN_HEADS, GDN_DK, GDN_DV), lambda i, j: (i, 0, 0, 0)),
        ],
        out_shape=[jax.ShapeDtypeStruct((b, t, GDN_V), F32),
                   jax.ShapeDtypeStruct((b, GDN_HEADS, GDN_DK, GDN_DV), F32)],
        scratch_shapes=[pltpu.VMEM((_CONV_PAD + rows, cd), F32), pltpu.VMEM((GDN_HEADS, GDN_DK, GDN_DV), F32)],
        compiler_params=_cparams("parallel", "arbitrary"),
        name=name,
    )(qkv_pre, z, ab, conv_buf.astype(F32), s0.astype(F32), conv_w.astype(F32), pad_l(a_log), pad_l(dt_bias),
      gnorm.astype(F32).reshape(1, GDN_DV))


def even_mixer(h, g, w_in, w_out, conv_w, a_log, dt_bias, gnorm, sb_k_past, sb_v_past, gdn_s0, conv_buf, t_valid):
    b, t, d = h.shape
    p = sb_k_past.shape[1]
    rows = b * t
    o0 = 3 * SB_W
    w_ab = jnp.pad(w_in[:, o0 + GDN_CONV_DIM + GDN_V:], ((0, 0), (0, LANES - 2 * GDN_HEADS)))
    w_bf = jnp.concatenate([w_in[:, :o0 + GDN_CONV_DIM + GDN_V], w_ab], axis=1).astype(BF16)
    splits = (SB_W, SB_W, SB_W, GDN_CONV_DIM, GDN_V, LANES)
    tk = ROW_TILE
    if p == 0 and b == 1 and t % tk == 0:
        q_bf, k, v, k_bf, vt_bf, qkv_pre, z, ab = norm_proj(
            h.reshape(rows, d), g, w_bf, splits, rows_out=t_valid, name="even_in_proj",
            outs=((0, "bf16"), (1, "f32_rows"), (2, "f32_rows"), (1, "bf16"), (2, "bf16_t"), (3, "f32"), (4, "f32"), (5, "f32")))
        o_sb = sb_attention(q_bf[None], k_bf[None], vt_bf[None], 0)
        k = k[None]
        v = v[None]
    else:
        q, k, v, qkv_pre, z, ab = norm_proj(h.reshape(rows, d), g, w_bf, splits, name="even_in_proj")
        q = q.reshape(b, t, SB_W)
        k = k.reshape(b, t, SB_W)
        v = v.reshape(b, t, SB_W)
        tq_pad = _round_up(t, LANES)
        tk_pad = _round_up(p + t, tk)
        k_all = _pad_rows(jnp.concatenate([sb_k_past.reshape(b, p, SB_W), k], axis=1), tk_pad, 1)
        v_all = _pad_rows(jnp.concatenate([sb_v_past.reshape(b, p, SB_W), v], axis=1), tk_pad, 1)
        k_bf, vt_bf = _kv_layouts(k_all, v_all, tk)
        q_bf = _pad_rows(q, tq_pad, 1).astype(BF16)
        o_sb = sb_attention(q_bf, k_bf, vt_bf, p)[:, :t]
        k = k[:, :t_valid]
        v = v[:, :t_valid]

    t_c = _round_up(t, _GDN_ROWS)
    qkv_pre = qkv_pre.reshape(b, t, GDN_CONV_DIM)
    o_gdn, s_fin = gdn_heads(_pad_rows(qkv_pre, t_c, 1), _pad_rows(z.reshape(b, t, GDN_V), t_c, 1),
                             _pad_rows(ab.reshape(b, t, LANES), t_c, 1), conv_buf, gdn_s0,
                             conv_w, a_log, dt_bias, gnorm, t_valid)
    o_gdn = o_gdn[:, :t]
    hist = GDN_CONV - 1
    assert t_valid >= hist
    xp_tail = qkv_pre[:, t_valid - hist:t_valid]
    h_new = out_proj_residual([o_sb.reshape(rows, SB_W), o_gdn.reshape(rows, GDN_V)], h.reshape(rows, d),
                              w_out.astype(BF16), name="even_out_proj")
    return (h_new.reshape(b, t, d), k.reshape(b, t_valid, SB_HEADS, SB_DIM), v.reshape(b, t_valid, SB_HEADS, SB_DIM),
            s_fin, xp_tail)


PEER_HALF = PEER_QDIM // 2
_NSEL = PEER_TOPK + 1
_SUB = 256
_CAND = tuple((a, b) for a in range(_NSEL) for b in range(_NSEL) if (a + 1) * (b + 1) <= _NSEL)
_NCAND = _round_up(len(_CAND), 8)


_SUBLANES = 8


def _sorting_network(n):
    pairs = []

    def merge(lo, m, r):
        step = 2 * r
        if step < m:
            merge(lo, m, step)
            merge(lo + r, m, step)
            pairs.extend((i, i + r) for i in range(lo + r, lo + m - r, step))
        else:
            pairs.append((lo, lo + r))

    def sort(lo, m):
        if m > 1:
            sort(lo, m // 2)
            sort(lo + m // 2, m // 2)
            merge(lo, m, 1)

    sort(0, n)
    return pairs


def _top_values(x, n, out_ref):
    rows, tn = x.shape
    groups = rows // _SUBLANES
    width = 1 << (groups - 1).bit_length()
    minus_inf = jnp.full((_SUBLANES, tn), -jnp.inf, F32)
    lists = [x[r * _SUBLANES:(r + 1) * _SUBLANES, :] for r in range(groups)] + [minus_inf] * (width - groups)
    for i, j in _sorting_network(width):
        lists[i], lists[j] = jnp.maximum(lists[i], lists[j]), jnp.minimum(lists[i], lists[j])
    lists = lists[:groups]
    sub = lax.broadcasted_iota(jnp.int32, (_SUBLANES, 1), 0)
    for it in range(n):
        head = lists[0]
        m = jnp.max(head, axis=0, keepdims=True)
        out_ref[it:it + 1, :] = m
        still_needed = n - it - 1
        if still_needed == 0:
            break
        first = jnp.min(jnp.where(head == m, sub, _SUBLANES), axis=0, keepdims=True)
        won = sub == first
        for r in range(min(groups, still_needed)):
            below = lists[r + 1] if r + 1 < groups else minus_inf
            lists[r] = jnp.where(won, below, lists[r])


def _gelu_tanh(x):
    return 0.5 * x * (1.0 + jnp.tanh(0.7978845608028654 * (x + 0.044715 * (x * x * x))))


def _peer_body(h_ref, g_ref, wq_ref, k1_ref, k2_ref, u_ref, vt_ref, gf_ref, o_ref,
               xn_ref, q_ref, ns1_ref, s2m_ref, e1_ref, e2_ref, t1_ref, t2_ref, cand_ref, csort_ref, acc_ref,
               *, te, final_norm):
    e = pl.program_id(1)
    n_e = pl.num_programs(1)
    tn = h_ref.shape[0]

    @pl.when(e == 0)
    def _prologue():
        x = h_ref[...]
        xn = x * lax.rsqrt(jnp.mean(x * x, axis=-1, keepdims=True) + RMS_EPS) * g_ref[...]
        xb = xn.astype(BF16)
        xn_ref[...] = xb
        q = jnp.dot(xb, wq_ref[...], preferred_element_type=F32)
        for j in range(2 * PEER_HEADS):
            q_ref[j] = q[:, j * PEER_HALF:(j + 1) * PEER_HALF]
        acc_ref[...] = jnp.zeros_like(acc_ref)
        cand_ref[...] = jnp.full(cand_ref.shape, -jnp.inf, F32)

        def per_head(h, _):
            s1 = lax.dot_general(k1_ref[h], q_ref[2 * h], _NT, preferred_element_type=F32)
            s2 = lax.dot_general(k2_ref[h], q_ref[2 * h + 1], _NT, preferred_element_type=F32)
            _top_values(s1, _NSEL, t1_ref)
            _top_values(s2, _NSEL, t2_ref)
            for r, (a, b) in enumerate(_CAND):
                cand_ref[r:r + 1, :] = t1_ref[a:a + 1, :] + t2_ref[b:b + 1, :]
            _top_values(cand_ref[...], _NSEL, csort_ref)
            thr = 0.5 * (csort_ref[PEER_TOPK - 1:PEER_TOPK, :] + csort_ref[PEER_TOPK:PEER_TOPK + 1, :])
            s_max = t1_ref[0:1, :] + t2_ref[0:1, :]
            cand = cand_ref[...]
            zsum = jnp.sum(jnp.where(cand >= thr, jnp.exp(cand - s_max), 0.0), axis=0, keepdims=True)
            ns1_ref[h] = -s1
            s2m_ref[h] = s2 - thr
            e1_ref[h] = jnp.exp(s1 - t1_ref[0:1, :]) / zsum
            e2_ref[h] = jnp.exp(s2 - t2_ref[0:1, :])
            return 0

        lax.fori_loop(0, PEER_HEADS, per_head, 0)

    xb = xn_ref[...]

    w_tiles = []
    for j in range(te // _SUB):
        r0 = j * _SUB
        a_t = lax.dot_general(u_ref[r0:r0 + _SUB, :], xb, _NT, preferred_element_type=F32)
        n_i1 = _SUB // PEER_NKEYS
        i1s = [e * (te // PEER_NKEYS) + j * n_i1 + r for r in range(n_i1)]
        ns1_rows = [[ns1_ref[h, pl.ds(i1, 1), :] for h in range(PEER_HEADS)] for i1 in i1s]
        e1_rows = [[e1_ref[h, pl.ds(i1, 1), :] for h in range(PEER_HEADS)] for i1 in i1s]
        tiles = [[] for _ in i1s]
        for c0 in range(0, tn, LANES):
            gsums = [None] * n_i1
            for h in range(PEER_HEADS):
                s2m_t = s2m_ref[h, :, c0:c0 + LANES]
                e2_t = e2_ref[h, :, c0:c0 + LANES]
                for r in range(n_i1):
                    term = jnp.where(s2m_t >= ns1_rows[r][h][:, c0:c0 + LANES],
                                     e2_t * e1_rows[r][h][:, c0:c0 + LANES], 0.0)
                    gsums[r] = term if gsums[r] is None else gsums[r] + term
            for r in range(n_i1):
                tiles[r].append(gsums[r])
        gates = jnp.concatenate([jnp.concatenate(t, axis=1) for t in tiles], axis=0)
        w_tiles.append((_gelu_tanh(a_t) * gates).astype(BF16))
    acc_ref[...] += jnp.dot(vt_ref[...], jnp.concatenate(w_tiles, axis=0), preferred_element_type=F32)

    @pl.when(e == n_e - 1)
    def _epilogue():
        y = h_ref[...] + acc_ref[...].T
        if final_norm:
            y = y * lax.rsqrt(jnp.mean(y * y, axis=-1, keepdims=True) + RMS_EPS) * gf_ref[...]
        o_ref[...] = y


PEER_TN = 640


def peer_residual(h, g, wq, k1, k2, u_bf, vt_bf, final_g=None, tn=PEER_TN, te=2048, name="peer"):
    m, d = h.shape
    tn = min(tn, m)
    assert m % tn == 0 and N_EXPERTS % te == 0 and te % _SUB == 0
    final_norm = final_g is not None
    gf = (final_g if final_norm else jnp.ones((d,), F32)).astype(F32).reshape(1, d)
    const = lambda *shape: pl.BlockSpec(shape, lambda i, j: (0,) * len(shape))
    big = lambda: pltpu.VMEM((PEER_HEADS, PEER_NKEYS, tn), F32)
    return pl.pallas_call(
        functools.partial(_peer_body, te=te, final_norm=final_norm),
        grid=(m // tn, N_EXPERTS // te),
        in_specs=[
            pl.BlockSpec((tn, d), lambda i, j: (i, 0)),
            const(1, d),
            const(d, PEER_HEADS * PEER_QDIM),
            const(PEER_HEADS, PEER_NKEYS, PEER_HALF),
            const(PEER_HEADS, PEER_NKEYS, PEER_HALF),
            pl.BlockSpec((te, d), lambda i, j: (j, 0)),
            pl.BlockSpec((d, te), lambda i, j: (0, j)),
            const(1, d),
        ],
        out_specs=pl.BlockSpec((tn, d), lambda i, j: (i, 0)),
        out_shape=jax.ShapeDtypeStruct((m, d), F32),
        scratch_shapes=[
            pltpu.VMEM((tn, d), BF16),
            pltpu.VMEM((2 * PEER_HEADS, tn, PEER_HALF), F32),
            big(), big(), big(), big(),
            pltpu.VMEM((_round_up(_NSEL, 8), tn), F32),
            pltpu.VMEM((_round_up(_NSEL, 8), tn), F32),
            pltpu.VMEM((_NCAND, tn), F32),
            pltpu.VMEM((_round_up(_NSEL, 8), tn), F32),
            pltpu.VMEM((d, tn), F32),
        ],
        compiler_params=_cparams("parallel", "arbitrary"),
        name=name,
    )(h, g.astype(F32).reshape(1, d), wq.astype(BF16), k1.astype(F32), k2.astype(F32), u_bf, vt_bf, gf)


def _kv_layouts(k_all, v_all, tk):
    b, t, w = k_all.shape
    vt = v_all.astype(BF16).reshape(b, t // tk, tk, w).transpose(0, 1, 3, 2)
    return k_all.astype(BF16), vt


def odd_mixer(h, g, w_in, b_f, w_out, k_past, v_past, logf_past, t_valid):
    b, t, d = h.shape
    p = k_past.shape[1]
    rows = b * t
    w_f = jnp.pad(w_in[:, 3 * FOX_W:], ((0, 0), (0, LANES - FOX_HEADS)))
    w_bf = jnp.concatenate([w_in[:, :3 * FOX_W], w_f], axis=1).astype(BF16)
    bias = jnp.pad(b_f.astype(F32), (0, LANES - FOX_HEADS)).reshape(1, LANES)
    splits = (FOX_W, FOX_W, FOX_W, LANES)
    tk = ROW_TILE
    if p == 0 and b == 1 and t % tk == 0:
        q_bf, k, v, k_bf, vt_bf, logf, qq, kk, qk = norm_proj(
            h.reshape(rows, d), g, w_bf, splits, bias=bias, rows_out=t_valid, name="odd_in_proj",
            outs=((0, "bf16"), (1, "f32_rows"), (2, "f32_rows"), (1, "bf16"), (2, "bf16_t"), (3, "f32")),
            head_stats=(0, 1, FOX_DIM))
        logf = logf[None]
        f_cum = cumsum_rows(logf, name="fox_cumsum")
        stats = (qq[None, :, :FOX_HEADS], jnp.max(kk[:, :FOX_HEADS], axis=0)[None], qk[None, :, :FOX_HEADS])
        o = fox_attention(q_bf[None], k_bf[None], vt_bf[None], f_cum, f_cum[:, :, :FOX_HEADS], stats, 0, t_valid)
        k = k[None]
        v = v[None]
    else:
        q, k, v, logf = norm_proj(h.reshape(rows, d), g, w_bf, splits, bias=bias, name="odd_in_proj")
        q = q.reshape(b, t, FOX_W)
        k = k.reshape(b, t, FOX_W)
        v = v.reshape(b, t, FOX_W)
        logf = logf.reshape(b, t, LANES)
        tq_pad = _round_up(t, LANES)
        tk_pad = _round_up(p + t, tk)
        k_all = _pad_rows(jnp.concatenate([k_past.reshape(b, p, FOX_W), k], axis=1), tk_pad, 1)
        v_all = _pad_rows(jnp.concatenate([v_past.reshape(b, p, FOX_W), v], axis=1), tk_pad, 1)
        logf_past = jnp.pad(logf_past.astype(F32), ((0, 0), (0, 0), (0, LANES - FOX_HEADS)))
        logf_all = _pad_rows(jnp.concatenate([logf_past, logf], axis=1), tk_pad, 1)
        f_cum = cumsum_rows(logf_all, name="fox_cumsum")
        f_q = _pad_rows(f_cum[:, p:p + t, :FOX_HEADS], tq_pad, 1)
        k_bf, vt_bf = _kv_layouts(k_all, v_all, tk)
        q_bf = _pad_rows(q, tq_pad, 1).astype(BF16)
        per_head = lambda a: a.reshape(*a.shape[:2], FOX_HEADS, FOX_DIM)
        stats = (jnp.sum(jnp.square(per_head(q)), axis=-1), jnp.max(jnp.sum(jnp.square(per_head(k_all)), axis=-1), axis=1),
                 jnp.sum(per_head(q) * per_head(k), axis=-1))
        o = fox_attention(q_bf, k_bf, vt_bf, f_cum, f_q, stats, p, t_valid)[:, :t]
        k = k[:, :t_valid]
        v = v[:, :t_valid]
    h_new = out_proj_residual([o.reshape(rows, FOX_W)], h.reshape(rows, d), w_out.astype(BF16), name="odd_out_proj")
    return (h_new.reshape(b, t, d), k.reshape(b, t_valid, FOX_HEADS, FOX_DIM), v.reshape(b, t_valid, FOX_HEADS, FOX_DIM),
            logf[:, :t_valid, :FOX_HEADS])


def kernel(x_prompt, x_sample, cache_sb_k, cache_sb_v, state_gdn, state_gdn_conv, cache_fox_k, cache_fox_v, cache_fox_logf, meta_tokens, norm_mix, norm_ffn, norm_final, w_in_even, w_out_even, gdn_conv_w, gdn_a_log, gdn_dt_bias, gdn_norm, w_in_odd, b_forget, w_out_odd, peer_wq, peer_k1, peer_k2, peer_u, peer_v):
    bsz, seq, d = x_prompt.shape
    dec_b, dec_t, _ = x_sample.shape
    depth = norm_mix.shape[0]
    dt = x_prompt.dtype
    t_p = N_META + seq
    t_pad = _round_up(t_p, PEER_TN)

    meta = jnp.broadcast_to(meta_tokens.astype(dt)[None], (bsz, N_META, d))
    hp = _pad_rows(jnp.concatenate([meta, x_prompt], axis=1), t_pad, 1)
    hs = x_sample

    empty_sb = jnp.zeros((bsz, 0, SB_HEADS, SB_DIM), dt)
    zero_s = jnp.zeros((bsz, GDN_HEADS, GDN_DK, GDN_DV), dt)
    zero_buf = jnp.zeros((bsz, GDN_CONV - 1, GDN_CONV_DIM), dt)
    empty_fox = jnp.zeros((bsz, 0, FOX_HEADS, FOX_DIM), dt)
    empty_logf = jnp.zeros((bsz, 0, FOX_HEADS), dt)

    sbk_p, sbv_p, sbk_s, sbv_s = [], [], [], []
    gs_p, gs_s, gc_p, gc_s = [], [], [], []
    fk_p, fv_p, ff_p, fk_s, fv_s, ff_s = [], [], [], [], [], []

    for layer in range(depth):
        if layer % 2 == 0:
            e = layer // 2
            w = (norm_mix[layer], w_in_even[e], w_out_even[e], gdn_conv_w[e], gdn_a_log[e], gdn_dt_bias[e], gdn_norm[e])
            hp, kp, vp, sp, bp = even_mixer(hp, *w, empty_sb, empty_sb, zero_s, zero_buf, t_p)
            hs, ks_, vs_, ss, bs = even_mixer(hs, *w, cache_sb_k[e], cache_sb_v[e], state_gdn[e], state_gdn_conv[e], dec_t)
            sbk_p.append(kp); sbv_p.append(vp); sbk_s.append(ks_); sbv_s.append(vs_)
            gs_p.append(sp); gs_s.append(ss); gc_p.append(bp); gc_s.append(bs)
        else:
            o = layer // 2
            w = (norm_mix[layer], w_in_odd[o], b_forget[o], w_out_odd[o])
            hp, kp, vp, fp = odd_mixer(hp, *w, empty_fox, empty_fox, empty_logf, t_p)
            hs, ks_, vs_, fs = odd_mixer(hs, *w, cache_fox_k[o], cache_fox_v[o], cache_fox_logf[o], dec_t)
            fk_p.append(kp); fv_p.append(vp); ff_p.append(fp)
            fk_s.append(ks_); fv_s.append(vs_); ff_s.append(fs)
        last = layer == depth - 1
        u_bf = peer_u[layer].astype(BF16)
        vt_bf = peer_v[layer].T.astype(BF16)
        pw = (norm_ffn[layer], peer_wq[layer], peer_k1[layer], peer_k2[layer], u_bf, vt_bf, norm_final if last else None)
        hp = peer_residual(hp.reshape(bsz * t_pad, d), *pw, name="peer_prompt").reshape(bsz, t_pad, d)
        hs = peer_residual(hs.reshape(dec_b * dec_t, d), *pw, name="peer_sample").reshape(dec_b, dec_t, d)

    y_prompt = hp[:, N_META:t_p]
    y_sample = hs
    return (y_prompt, y_sample,
            jnp.stack(sbk_p), jnp.stack(sbv_p), jnp.stack(sbk_s), jnp.stack(sbv_s),
            jnp.stack(gs_p), jnp.stack(gs_s), jnp.stack(gc_p), jnp.stack(gc_s),
            jnp.stack(fk_p), jnp.stack(fv_p), jnp.stack(ff_p),
            jnp.stack(fk_s), jnp.stack(fv_s), jnp.stack(ff_s))
```

```python
import functools

import jax
import jax.numpy as jnp
from jax import lax
from jax.experimental import pallas as pl
from jax.experimental.pallas import tpu as pltpu

F32 = jnp.float32
BF16 = jnp.bfloat16

D_MODEL = 1024
N_META = 16
CHUNK = 64
SB_DIM = 64
SB_HEADS = 8
SB_W = SB_HEADS * SB_DIM
GDN_DK = 128
GDN_DV = 128
GDN_HEADS = 4
GDN_QK = GDN_HEADS * GDN_DK
GDN_V = GDN_HEADS * GDN_DV
GDN_CONV = 4
GDN_CONV_DIM = 2 * GDN_QK + GDN_V
FOX_DIM = 64
FOX_HEADS = 16
FOX_W = FOX_HEADS * FOX_DIM
PEER_HEADS = 8
PEER_NKEYS = 128
PEER_TOPK = 16
PEER_QDIM = 256
N_EXPERTS = PEER_NKEYS ** 2
RMS_EPS = 1e-6

LANES = 128
ROW_TILE = 256
ATT_TQ = 512
VMEM_LIMIT = 60 * 1024 * 1024
NEG_BIG = -1e30

_NT = (((1,), (1,)), ((), ()))


def _cparams(*sem):
    return pltpu.CompilerParams(dimension_semantics=sem, vmem_limit_bytes=VMEM_LIMIT)


def _round_up(n, m):
    return -(-n // m) * m


def _pad_rows(a, rows, axis=0):
    pad = [(0, 0)] * a.ndim
    pad[axis] = (0, rows - a.shape[axis])
    return jnp.pad(a, pad)


def _norm_proj_body(x_ref, g_ref, w_ref, b_ref, *out_refs, offs, outs, logsig_split, head_stats):
    x = x_ref[...]
    xn = x * lax.rsqrt(jnp.mean(x * x, axis=-1, keepdims=True) + RMS_EPS) * g_ref[...]
    xb = xn.astype(BF16)
    cols = {}
    for (i, kind), o_ref in zip(outs, out_refs):
        if i not in cols:
            y = jnp.dot(xb, w_ref[:, offs[i]:offs[i + 1]], preferred_element_type=F32)
            cols[i] = jax.nn.log_sigmoid(y + b_ref[...]) if i == logsig_split else y
        y = cols[i]
        if kind == "bf16_t":
            o_ref[0] = y.T.astype(BF16)
        else:
            o_ref[...] = y.astype(o_ref.dtype)
    if head_stats is not None:
        iq, ik, dim = head_stats
        n = offs[iq + 1] - offs[iq]
        group = (lax.broadcasted_iota(jnp.int32, (n, LANES), 0) // dim
                 == lax.broadcasted_iota(jnp.int32, (n, LANES), 1)).astype(BF16)
        yq, yk = cols[iq], cols[ik]
        for o_ref, prod in zip(out_refs[len(outs):], (yq * yq, yk * yk, yq * yk)):
            o_ref[...] = jnp.dot(prod.astype(BF16), group, preferred_element_type=F32)


def norm_proj(x, g, w_bf, splits, outs=None, bias=None, rows_out=None, head_stats=None, name="norm_proj"):
    m, d = x.shape
    n = w_bf.shape[1]
    offs = [0]
    for s in splits:
        offs.append(offs[-1] + s)
    tm = min(ROW_TILE, m)
    assert offs[-1] == n and m % tm == 0
    outs = tuple((i, "f32") for i in range(len(splits))) if outs is None else tuple(outs)
    logsig_split = len(splits) - 1 if bias is not None else -1
    if bias is None:
        bias = jnp.zeros((1, splits[-1]), F32)
    out_specs, out_shape = [], []
    for i, kind in outs:
        s = splits[i]
        if kind == "bf16_t":
            out_specs.append(pl.BlockSpec((1, s, tm), lambda r: (r, 0, 0)))
            out_shape.append(jax.ShapeDtypeStruct((m // tm, s, tm), BF16))
        else:
            rows = rows_out if kind == "f32_rows" else m
            assert m - rows < tm
            out_specs.append(pl.BlockSpec((tm, s), lambda r: (r, 0)))
            out_shape.append(jax.ShapeDtypeStruct((rows, s), BF16 if kind == "bf16" else F32))
    if head_stats is not None:
        assert {head_stats[0], head_stats[1]} <= {i for i, _ in outs}
        out_specs += [pl.BlockSpec((tm, LANES), lambda r: (r, 0))] * 3
        out_shape += [jax.ShapeDtypeStruct((m, LANES), F32)] * 3
    return pl.pallas_call(
        functools.partial(_norm_proj_body, offs=tuple(offs), outs=outs, logsig_split=logsig_split,
                          head_stats=head_stats),
        grid=(m // tm,),
        in_specs=[
            pl.BlockSpec((tm, d), lambda i: (i, 0)),
            pl.BlockSpec((1, d), lambda i: (0, 0)),
            pl.BlockSpec((d, n), lambda i: (0, 0)),
            pl.BlockSpec((1, splits[-1]), lambda i: (0, 0)),
        ],
        out_specs=out_specs,
        out_shape=out_shape,
        compiler_params=_cparams("parallel"),
        name=name,
    )(x, g.reshape(1, d), w_bf, bias)


def _out_proj_body(*refs, n_in, offs):
    a_refs = refs[:n_in]
    h_ref, w_ref, o_ref = refs[n_in:]
    acc = h_ref[...]
    for i, a_ref in enumerate(a_refs):
        acc = acc + jnp.dot(a_ref[...].astype(BF16), w_ref[offs[i]:offs[i + 1], :], preferred_element_type=F32)
    o_ref[...] = acc


def out_proj_residual(parts, h, w_bf, name="out_proj"):
    m, d = h.shape
    offs = [0]
    for a in parts:
        offs.append(offs[-1] + a.shape[1])
    tm = min(ROW_TILE, m)
    assert offs[-1] == w_bf.shape[0] and m % tm == 0
    return pl.pallas_call(
        functools.partial(_out_proj_body, n_in=len(parts), offs=tuple(offs)),
        grid=(m // tm,),
        in_specs=[pl.BlockSpec((tm, a.shape[1]), lambda i: (i, 0)) for a in parts] + [
            pl.BlockSpec((tm, d), lambda i: (i, 0)),
            pl.BlockSpec(w_bf.shape, lambda i: (0, 0)),
        ],
        out_specs=pl.BlockSpec((tm, d), lambda i: (i, 0)),
        out_shape=jax.ShapeDtypeStruct((m, d), F32),
        compiler_params=_cparams("parallel"),
        name=name,
    )(*parts, h, w_bf)


def _cumsum_body(x_ref, o_ref, carry_ref):
    @pl.when(pl.program_id(1) == 0)
    def _():
        carry_ref[...] = jnp.zeros_like(carry_ref)

    x = x_ref[0]
    t = x.shape[0]
    tri = (lax.broadcasted_iota(jnp.int32, (t, t), 0) >= lax.broadcasted_iota(jnp.int32, (t, t), 1)).astype(F32)
    c = jnp.dot(tri, x, preferred_element_type=F32, precision=lax.Precision.HIGHEST) + carry_ref[...]
    o_ref[0] = c
    carry_ref[...] = c[t - 1:t, :]


def cumsum_rows(x, name="cumsum_rows"):
    b, l, c = x.shape
    tm = ROW_TILE
    assert l % tm == 0
    return pl.pallas_call(
        _cumsum_body,
        grid=(b, l // tm),
        in_specs=[pl.BlockSpec((1, tm, c), lambda i, j: (i, j, 0))],
        out_specs=pl.BlockSpec((1, tm, c), lambda i, j: (i, j, 0)),
        out_shape=jax.ShapeDtypeStruct((b, l, c), F32),
        scratch_shapes=[pltpu.VMEM((1, c), F32)],
        compiler_params=_cparams("parallel", "arbitrary"),
        name=name,
    )(x)


def _fox_body(nsteps_ref, q_ref, k_ref, vt_ref, fk_ref, fq_ref, o_ref, acc_ref, s_ref, p_ref, *, q_offset, tq, tk):
    bi = pl.program_id(0)
    hp = pl.program_id(1)
    qi = pl.program_id(2)
    q = q_ref[0] * jnp.asarray(FOX_DIM ** -0.5, BF16)
    lane = lax.broadcasted_iota(jnp.int32, (1, LANES), 1)
    q_heads = (jnp.where(lane < FOX_DIM, q, jnp.zeros_like(q)), jnp.where(lane >= FOX_DIM, q, jnp.zeros_like(q)))
    q0 = q_offset + qi * tq
    q_pos = q0 + lax.broadcasted_iota(jnp.int32, (1, tq), 1)
    kb_diag = jnp.minimum((q0 + tq - 1) // tk, k_ref.shape[1] // tk - 1)
    n_causal = max(tq // tk, 1)
    fq_all = fq_ref[0]
    qf_heads = []
    for h in range(2):
        lo = _F_PARTS * (2 * hp + h)
        mine = ((lane >= lo) & (lane < lo + _F_PARTS)) | ((lane >= _F_ONES + lo) & (lane < _F_ONES + lo + _F_PARTS))
        qf_heads.append(jnp.concatenate([q_heads[h], jnp.where(mine, fq_all, jnp.zeros_like(fq_all))], axis=1))
    acc_ref[...] = jnp.zeros_like(acc_ref)

    def scores(kb):
        k0 = pl.multiple_of(kb * tk, tk)
        kf = jnp.concatenate([k_ref[0, pl.ds(k0, tk), :], fk_ref[0, pl.ds(k0, tk), :]], axis=1)
        for h in range(2):
            s_ref[h] = lax.dot_general(kf, qf_heads[h], _NT, preferred_element_type=F32)

    def accumulate(kb, alphas):
        vt = vt_ref[0, kb]
        for h in range(2):
            acc_ref[h] = alphas[h] * acc_ref[h] + jnp.dot(vt, p_ref[h], preferred_element_type=F32)

    def softmax(kb, carry, causal):
        if causal:
            mask = (kb * tk + lax.broadcasted_iota(jnp.int32, (tk, 1), 0)) <= q_pos
        stats, alphas = [], []
        for h in range(2):
            m_prev, l_prev = carry[2 * h], carry[2 * h + 1]
            s = s_ref[h]
            if causal:
                s = jnp.where(mask, s, NEG_BIG)
            m_new = jnp.maximum(m_prev, jnp.max(s, axis=0, keepdims=True))
            p = jnp.exp(s - m_new)
            alpha = jnp.exp(m_prev - m_new)
            p_ref[h] = p.astype(BF16)
            stats += [m_new, alpha * l_prev + jnp.sum(p, axis=0, keepdims=True)]
            alphas.append(alpha)
        return tuple(stats + alphas)

    def step(j, carry):
        kb = kb_diag - j
        accumulate(kb + 1, carry[4:6])
        carry = softmax(kb, carry, causal=False)
        scores(jnp.maximum(kb - 1, 0))
        return carry

    ones = jnp.ones((1, tq), F32)
    init = (jnp.full((1, tq), NEG_BIG, F32), jnp.zeros((1, tq), F32),
            jnp.full((1, tq), NEG_BIG, F32), jnp.zeros((1, tq), F32), ones, ones)
    n_steps = nsteps_ref[bi, hp, qi]
    scores(kb_diag)
    carry = init
    for c in range(n_causal):
        if c > 0:
            accumulate(kb_diag - c + 1, carry[4:6])
        carry = softmax(kb_diag - c, carry, causal=True)
        scores(jnp.maximum(kb_diag - c - 1, 0))
    fin = lax.fori_loop(n_causal, n_steps, step, carry)
    accumulate(kb_diag - (n_steps - 1), fin[4:6])
    row = lax.broadcasted_iota(jnp.int32, (LANES, 1), 0)
    ot = jnp.where(row < FOX_DIM, acc_ref[0] / fin[1], acc_ref[1] / fin[3])
    o_ref[0] = ot.T


_UNDERFLOW = 110.0


def _fox_block_counts(stats, f_keys, f_q, q_offset, t_valid, tq_all, tq, tk):
    qq, kk_max, qk = stats
    b, _, heads = qq.shape
    tk_all = f_keys.shape[1]
    nq, nk = pl.cdiv(tq_all, tq), tk_all // tk
    scale = FOX_DIM ** -0.5
    reach = jnp.sqrt(qq * kk_max[:, None, :]) * scale
    bound = _pad_rows(1.02 * reach - qk * scale, tq_all, 1) + f_q
    valid = (jnp.arange(tq_all) < t_valid)[None, :, None]
    bound = jnp.pad(jnp.where(valid, bound, -jnp.inf), ((0, 0), (0, nq * tq - tq_all), (0, 0)),
                    constant_values=-jnp.inf)
    cq = jnp.max(bound.reshape(b, nq, tq, heads), axis=2)
    f_end = f_keys[:, tk - 1::tk, :heads]
    kb_diag = jnp.minimum((q_offset + jnp.arange(nq) * tq + tq - 1) // tk, nk - 1)
    need = (cq[:, :, None, :] - f_end[:, None, :, :]) > -_UNDERFLOW
    need = need & (jnp.arange(nk)[None, None, :, None] <= kb_diag[None, :, None, None])
    first = jnp.min(jnp.where(need, jnp.arange(nk)[None, None, :, None], nk), axis=2)
    first = jnp.min(first.reshape(b, nq, heads // 2, 2), axis=-1)
    most = kb_diag[None, :, None] + 1
    steps = jnp.clip(kb_diag[None, :, None] - first + 1, jnp.minimum(max(tq // tk, 1), most), most)
    return steps.transpose(0, 2, 1).astype(jnp.int32)


_F_PARTS = 3
_F_ONES = _F_PARTS * FOX_HEADS


def _bf16_head(x):
    bits = lax.bitcast_convert_type(x, jnp.uint32) & jnp.uint32(0xFFFF0000)
    return lax.bitcast_convert_type(bits, F32)


def _split_f_operand(f, key_side):
    b, t, heads = f.shape
    f = f.astype(F32)
    hi = _bf16_head(f)
    mid = _bf16_head(f - hi)
    lo = f - hi - mid
    pieces = jnp.stack([hi, mid, lo], axis=-1).reshape(b, t, heads * _F_PARTS).astype(BF16)
    const = jnp.full((b, t, heads * _F_PARTS), 1.0 if key_side else -1.0, BF16)
    both = [pieces, const] if key_side else [const, pieces]
    return jnp.pad(jnp.concatenate(both, axis=-1), ((0, 0), (0, 0), (0, LANES - 2 * heads * _F_PARTS)))


def fox_attention(q_bf, k_bf, vt_bf, f_keys, f_q, stats, q_offset, t_valid, name="fox_attention"):
    b, tq_all, w = q_bf.shape
    tk_all = k_bf.shape[1]
    tq = min(ATT_TQ, tq_all)
    tk = ROW_TILE
    assert tk_all % tk == 0
    assert (tq % tk == 0 and q_offset % tk == 0) or (tk % tq == 0 and q_offset % tq == 0)
    nsteps = _fox_block_counts(stats, f_keys, f_q, q_offset, t_valid, tq_all, tq, tk)
    fk_aug = _split_f_operand(f_keys[:, :, :FOX_HEADS], key_side=True)
    fq_aug = _split_f_operand(f_q, key_side=False)
    return pl.pallas_call(
        functools.partial(_fox_body, q_offset=q_offset, tq=tq, tk=tk),
        grid_spec=pltpu.PrefetchScalarGridSpec(
            num_scalar_prefetch=1,
            grid=(b, w // LANES, pl.cdiv(tq_all, tq)),
            in_specs=[
                pl.BlockSpec((1, tq, LANES), lambda i, h, j, n: (i, j, h)),
                pl.BlockSpec((1, tk_all, LANES), lambda i, h, j, n: (i, 0, h)),
                pl.BlockSpec((1, tk_all // tk, LANES, tk), lambda i, h, j, n: (i, 0, h, 0)),
                pl.BlockSpec((1, tk_all, LANES), lambda i, h, j, n: (i, 0, 0)),
                pl.BlockSpec((1, tq, LANES), lambda i, h, j, n: (i, j, 0)),
            ],
            out_specs=pl.BlockSpec((1, tq, LANES), lambda i, h, j, n: (i, j, h)),
            scratch_shapes=[pltpu.VMEM((2, LANES, tq), F32), pltpu.VMEM((2, tk, tq), F32),
                            pltpu.VMEM((2, tk, tq), BF16)],
        ),
        out_shape=jax.ShapeDtypeStruct((b, tq_all, w), F32),
        compiler_params=_cparams("parallel", "parallel", "arbitrary"),
        name=name,
    )(nsteps, q_bf, k_bf, vt_bf, fk_aug, fq_aug)


_SB_BLOCKS = 2


def _sb_body(q_ref, k_ref, vt_ref, o_ref, acc_ref, *, q_offset, tq, tk, tq_all):
    qi = pl.program_id(2)
    q = q_ref[0] * jnp.asarray(SB_DIM ** -0.5, BF16)
    lane = lax.broadcasted_iota(jnp.int32, (1, LANES), 1)
    q_heads = (jnp.where(lane < SB_DIM, q, jnp.zeros_like(q)), jnp.where(lane >= SB_DIM, q, jnp.zeros_like(q)))
    q0 = q_offset + qi * tq
    q_pos = q0 + lax.broadcasted_iota(jnp.int32, (1, tq), 1)
    n_kb = jnp.minimum(jnp.maximum(q0 + tq - 2, 0) // tk + 1, k_ref.shape[1] // tk)
    real_query = q_pos < q_offset + tq_all
    upper = (lax.broadcasted_iota(jnp.int32, (tk, tk), 1) > lax.broadcasted_iota(jnp.int32, (tk, tk), 0)).astype(BF16)
    acc_ref[...] = jnp.zeros_like(acc_ref)

    def block_terms(kb, h):
        kbc = jnp.maximum(kb, 0)
        k0 = pl.multiple_of(kbc * tk, tk)
        k = k_ref[0, pl.ds(k0, tk), :]
        edge = jnp.where(kb >= 0, q_pos, -1)
        mask = (k0 + lax.broadcasted_iota(jnp.int32, (tk, 1), 0)) < edge
        z = lax.dot_general(k, q_heads[h], _NT, preferred_element_type=F32)
        sp = jnp.maximum(z, 0.0) + jnp.log(1.0 + jnp.exp(-jnp.abs(z)))
        l = jnp.where(mask, -sp, 0.0)
        l_hi = l.astype(BF16)
        l_lo = (l - l_hi.astype(F32)).astype(BF16)
        later = (jnp.dot(upper, l_hi, preferred_element_type=F32)
                 + jnp.dot(upper, l_lo, preferred_element_type=F32))
        return mask, (z - sp) + later, later[0:1, :] + l[0:1, :], vt_ref[0, kbc]

    def step(carry):
        j = carry[0]
        kb = n_kb - 1 - _SB_BLOCKS * j
        out = []
        for h in range(2):
            r = carry[2 + h]
            terms = [block_terms(kb - u, h) for u in range(_SB_BLOCKS)]
            pv = None
            for mask, log_w, total, vt in terms:
                w = jnp.where(mask, jnp.exp(log_w + r), 0.0)
                part = jnp.dot(vt, w.astype(BF16), preferred_element_type=F32)
                pv = part if pv is None else pv + part
                r = r + total
            acc_ref[h] += pv
            out.append(r)
        live = jnp.where(real_query, jnp.maximum(out[0], out[1]), -jnp.inf)
        return (j + 1, jnp.max(live), out[0], out[1])

    def more(carry):
        return (_SB_BLOCKS * carry[0] < n_kb) & (carry[1] > -_UNDERFLOW)

    lax.while_loop(more, step, (jnp.int32(0), jnp.float32(0.0), jnp.zeros((1, tq), F32), jnp.zeros((1, tq), F32)))
    row = lax.broadcasted_iota(jnp.int32, (LANES, 1), 0)
    o_ref[0] = jnp.where(row < SB_DIM, acc_ref[0], acc_ref[1]).T


def sb_attention(q_bf, k_bf, vt_bf, q_offset, name="sb_attention"):
    b, tq_all, w = q_bf.shape
    tk_all = k_bf.shape[1]
    tq = min(ROW_TILE, tq_all)
    tk = ROW_TILE
    assert tk_all % tk == 0
    return pl.pallas_call(
        functools.partial(_sb_body, q_offset=q_offset, tq=tq, tk=tk, tq_all=tq_all),
        grid=(b, w // LANES, pl.cdiv(tq_all, tq)),
        in_specs=[
            pl.BlockSpec((1, tq, LANES), lambda i, h, j: (i, j, h)),
            pl.BlockSpec((1, tk_all, LANES), lambda i, h, j: (i, 0, h)),
            pl.BlockSpec((1, tk_all // tk, LANES, tk), lambda i, h, j: (i, 0, h, 0)),
        ],
        out_specs=pl.BlockSpec((1, tq, LANES), lambda i, h, j: (i, j, h)),
        out_shape=jax.ShapeDtypeStruct((b, tq_all, w), F32),
        scratch_shapes=[pltpu.VMEM((2, LANES, tq), F32)],
        compiler_params=_cparams("parallel", "parallel", "arbitrary"),
        name=name,
    )(q_bf, k_bf, vt_bf)


_HI = lax.Precision.HIGHEST
_TN = (((0,), (0,)), ((), ()))
_CONV_PAD = 8
_GDN_CPS = 4
_GDN_ROWS = _GDN_CPS * CHUNK


def _dot_hi(a, b):
    return jnp.dot(a, b, preferred_element_type=F32, precision=_HI)


def _split_bf16(x):
    hi = x.astype(BF16)
    return hi, (x - hi.astype(F32)).astype(BF16)


def _einsum3(spec, a, b):
    ah, al = a if isinstance(a, tuple) else _split_bf16(a)
    bh, bl = b if isinstance(b, tuple) else _split_bf16(b)
    prod = functools.partial(jnp.einsum, spec, preferred_element_type=F32)
    return prod(ah, bh) + (prod(ah, bl) + prod(al, bh))


def _bmm(a, b):
    return _einsum3('hij,hjk->hik', a, b)


def _bmm_nt(a, b):
    return _einsum3('hik,hjk->hij', a, b)


def _softplus(x):
    return jnp.maximum(x, 0.0) + jnp.log1p(jnp.exp(-jnp.abs(x)))


def _silu(x):
    return x / (1.0 + jnp.exp(-x))


def _unit_lower_inverse(m):
    c_len = m.shape[-1]
    ri = lax.broadcasted_iota(jnp.int32, (c_len, c_len), 0)
    ci = lax.broadcasted_iota(jnp.int32, (c_len, c_len), 1)
    d = jnp.broadcast_to((ri == ci).astype(F32), m.shape)
    s = 1
    while s < c_len:
        join = (ri // (2 * s) == ci // (2 * s)) & (ri % (2 * s) >= s) & (ci % (2 * s) < s)
        c = jnp.where(join, m, 0.0)
        if s == 1:
            d = d - c
        else:
            d_s = _split_bf16(d)
            d = d - _bmm(_bmm(d_s, c), d_s)
        s *= 2
    return d


def _gdn_body(x_ref, z_ref, ab_ref, buf_ref, s0_ref, cw_ref, alog_ref, dt_ref, gn_ref,
              o_ref, sfin_ref, xwin_ref, s_ref, *, t_valid):
    c = pl.program_id(1)
    n_c = pl.num_programs(1)
    hist = GDN_CONV - 1

    @pl.when(c == 0)
    def _():
        xwin_ref[_CONV_PAD - hist:_CONV_PAD, :] = buf_ref[0]
        s_ref[...] = s0_ref[0]

    rows = _GDN_ROWS
    xwin_ref[_CONV_PAD:_CONV_PAD + rows, :] = x_ref[0]
    conv = xwin_ref[_CONV_PAD - hist:_CONV_PAD - hist + rows, :] * cw_ref[0:1, :]
    for i in range(1, GDN_CONV):
        conv = conv + xwin_ref[_CONV_PAD - hist + i:_CONV_PAD - hist + i + rows, :] * cw_ref[i:i + 1, :]
    tail = xwin_ref[_CONV_PAD + rows - hist:_CONV_PAD + rows, :]
    xwin_ref[_CONV_PAD - hist:_CONV_PAD, :] = tail
    act = _silu(conv)

    ab = ab_ref[0]
    row_ok = (c * rows + lax.broadcasted_iota(jnp.int32, (rows, 1), 0)) < t_valid
    g_all = jnp.where(row_ok, -jnp.exp(alog_ref[...]) * _softplus(ab + dt_ref[...]), 0.0)
    beta_all = jnp.where(row_ok, 1.0 / (1.0 + jnp.exp(-ab)), 0.0)
    rr = lax.broadcasted_iota(jnp.int32, (rows, rows), 0)
    rc = lax.broadcasted_iota(jnp.int32, (rows, rows), 1)
    tri_chunks = ((rr >= rc) & (rr // CHUNK == rc // CHUNK)).astype(F32)
    gcum_all = _dot_hi(tri_chunks, g_all)
    sel = (lax.broadcasted_iota(jnp.int32, (8, LANES), 0) == lax.broadcasted_iota(jnp.int32, (8, LANES), 1)).astype(F32)
    gcum_rows = lax.dot_general(sel, gcum_all, _NT, preferred_element_type=F32, precision=_HI)
    ri = lax.broadcasted_iota(jnp.int32, (CHUNK, CHUNK), 0)
    ci = lax.broadcasted_iota(jnp.int32, (CHUNK, CHUNK), 1)
    tri = ri >= ci
    strict = ri > ci

    pairs = [(ck, h) for ck in range(_GDN_CPS) for h in range(GDN_HEADS)]
    rows_of = lambda ck: slice(ck * CHUNK, (ck + 1) * CHUNK)
    q4 = jnp.stack([act[rows_of(ck), h * GDN_DK:(h + 1) * GDN_DK] for ck, h in pairs])
    k4 = jnp.stack([act[rows_of(ck), GDN_QK + h * GDN_DK:GDN_QK + (h + 1) * GDN_DK] for ck, h in pairs])
    v4 = jnp.stack([act[rows_of(ck), 2 * GDN_QK + h * GDN_DV:2 * GDN_QK + (h + 1) * GDN_DV] for ck, h in pairs])
    q4 = q4 * lax.rsqrt(jnp.sum(q4 * q4, axis=-1, keepdims=True) + RMS_EPS) * (GDN_DK ** -0.5)
    k4 = k4 * lax.rsqrt(jnp.sum(k4 * k4, axis=-1, keepdims=True) + RMS_EPS)
    beta = jnp.stack([beta_all[rows_of(ck), GDN_HEADS + h:GDN_HEADS + h + 1] for ck, h in pairs])
    gc = jnp.stack([gcum_all[rows_of(ck), h:h + 1] for ck, h in pairs])
    gr = jnp.stack([gcum_rows[h:h + 1, rows_of(ck)] for ck, h in pairs])
    decay = jnp.exp(jnp.where(tri, gc - gr, NEG_BIG))
    kb = k4 * beta
    k4_s = _split_bf16(k4)
    m = jnp.where(strict, _bmm_nt(kb, k4_s) * decay, 0.0)
    tinv = _split_bf16(_unit_lower_inverse(m))
    eg = jnp.exp(gc)
    u = _bmm(tinv, v4 * beta)
    w = _split_bf16(_bmm(tinv, kb * eg))
    attn = _split_bf16(_bmm_nt(q4, k4_s) * decay)
    qe = _split_bf16(q4 * eg)
    g_last = gc[:, CHUNK - 1:CHUNK, :]
    k_dec = _split_bf16(k4 * jnp.exp(g_last - gc))
    s_scale = jnp.exp(g_last)

    s4 = s_ref[...]
    outs = []
    for ck in range(_GDN_CPS):
        sl = slice(ck * GDN_HEADS, (ck + 1) * GDN_HEADS)
        part = lambda pair: (pair[0][sl], pair[1][sl])
        s4_s = _split_bf16(s4)
        v_new_s = _split_bf16(u[sl] - _bmm(part(w), s4_s))
        outs.append(_bmm(part(qe), s4_s) + _bmm(part(attn), v_new_s))
        s4 = s4 * s_scale[sl] + _einsum3('hck,hcv->hkv', part(k_dec), v_new_s)
    s_ref[...] = s4
    for ck in range(_GDN_CPS):
        o = outs[ck]
        o = o * lax.rsqrt(jnp.mean(o * o, axis=-1, keepdims=True) + RMS_EPS) * gn_ref[...]
        for h in range(GDN_HEADS):
            cols = slice(h * GDN_DV, (h + 1) * GDN_DV)
            o_ref[0, rows_of(ck), cols] = o[h] * _silu(z_ref[0, rows_of(ck), cols])

    @pl.when(c == n_c - 1)
    def _():
        sfin_ref[0] = s_ref[...]


def gdn_heads(qkv_pre, z, ab, conv_buf, s0, conv_w, a_log, dt_bias, gnorm, t_valid, name="gdn"):
    b, t, cd = qkv_pre.shape
    rows = _GDN_ROWS
    assert t % rows == 0
    n_c = t // rows
    pad_l = lambda a: jnp.pad(a.astype(F32), (0, LANES - a.shape[0])).reshape(1, LANES)
    const = lambda *shape: pl.BlockSpec(shape, lambda i, j: (0,) * len(shape))
    return pl.pallas_call(
        functools.partial(_gdn_body, t_valid=t_valid),
        grid=(b, n_c),
        in_specs=[
            pl.BlockSpec((1, rows, cd), lambda i, j: (i, j, 0)),
            pl.BlockSpec((1, rows, GDN_V), lambda i, j: (i, j, 0)),
            pl.BlockSpec((1, rows, LANES), lambda i, j: (i, j, 0)),
            pl.BlockSpec((1, GDN_CONV - 1, cd), lambda i, j: (i, 0, 0)),
            pl.BlockSpec((1, GDN_HEADS, GDN_DK, GDN_DV), lambda i, j: (i, 0, 0, 0)),
            const(GDN_CONV, cd), const(1, LANES), const(1, LANES), const(1, GDN_DV),
        ],
        out_specs=[
            pl.BlockSpec((1, rows, GDN_V), lambda i, j: (i, j, 0)),
            pl.BlockSpec((1, GDN_HEADS, GDN_DK, GDN_DV), lambda i, j: (i, 0, 0, 0)),
        ],
        out_shape=[jax.ShapeDtypeStruct((b, t, GDN_V), F32),
                   jax.ShapeDtypeStruct((b, GDN_HEADS, GDN_DK, GDN_DV), F32)],
        scratch_shapes=[pltpu.VMEM((_CONV_PAD + rows, cd), F32), pltpu.VMEM((GDN_HEADS, GDN_DK, GDN_DV), F32)],
        compiler_params=_cparams("parallel", "arbitrary"),
        name=name,
    )(qkv_pre, z, ab, conv_buf.astype(F32), s0.astype(F32), conv_w.astype(F32), pad_l(a_log), pad_l(dt_bias),
      gnorm.astype(F32).reshape(1, GDN_DV))


def even_mixer(h, g, w_in, w_out, conv_w, a_log, dt_bias, gnorm, sb_k_past, sb_v_past, gdn_s0, conv_buf, t_valid):
    b, t, d = h.shape
    p = sb_k_past.shape[1]
    rows = b * t
    o0 = 3 * SB_W
    w_ab = jnp.pad(w_in[:, o0 + GDN_CONV_DIM + GDN_V:], ((0, 0), (0, LANES - 2 * GDN_HEADS)))
    w_bf = jnp.concatenate([w_in[:, :o0 + GDN_CONV_DIM + GDN_V], w_ab], axis=1).astype(BF16)
    splits = (SB_W, SB_W, SB_W, GDN_CONV_DIM, GDN_V, LANES)
    tk = ROW_TILE
    if p == 0 and b == 1 and t % tk == 0:
        q_bf, k, v, k_bf, vt_bf, qkv_pre, z, ab = norm_proj(
            h.reshape(rows, d), g, w_bf, splits, rows_out=t_valid, name="even_in_proj",
            outs=((0, "bf16"), (1, "f32_rows"), (2, "f32_rows"), (1, "bf16"), (2, "bf16_t"), (3, "f32"), (4, "f32"), (5, "f32")))
        o_sb = sb_attention(q_bf[None], k_bf[None], vt_bf[None], 0)
        k = k[None]
        v = v[None]
    else:
        q, k, v, qkv_pre, z, ab = norm_proj(h.reshape(rows, d), g, w_bf, splits, name="even_in_proj")
        q = q.reshape(b, t, SB_W)
        k = k.reshape(b, t, SB_W)
        v = v.reshape(b, t, SB_W)
        tq_pad = _round_up(t, LANES)
        tk_pad = _round_up(p + t, tk)
        k_all = _pad_rows(jnp.concatenate([sb_k_past.reshape(b, p, SB_W), k], axis=1), tk_pad, 1)
        v_all = _pad_rows(jnp.concatenate([sb_v_past.reshape(b, p, SB_W), v], axis=1), tk_pad, 1)
        k_bf, vt_bf = _kv_layouts(k_all, v_all, tk)
        q_bf = _pad_rows(q, tq_pad, 1).astype(BF16)
        o_sb = sb_attention(q_bf, k_bf, vt_bf, p)[:, :t]
        k = k[:, :t_valid]
        v = v[:, :t_valid]

    t_c = _round_up(t, _GDN_ROWS)
    qkv_pre = qkv_pre.reshape(b, t, GDN_CONV_DIM)
    o_gdn, s_fin = gdn_heads(_pad_rows(qkv_pre, t_c, 1), _pad_rows(z.reshape(b, t, GDN_V), t_c, 1),
                             _pad_rows(ab.reshape(b, t, LANES), t_c, 1), conv_buf, gdn_s0,
                             conv_w, a_log, dt_bias, gnorm, t_valid)
    o_gdn = o_gdn[:, :t]
    hist = GDN_CONV - 1
    assert t_valid >= hist
    xp_tail = qkv_pre[:, t_valid - hist:t_valid]
    h_new = out_proj_residual([o_sb.reshape(rows, SB_W), o_gdn.reshape(rows, GDN_V)], h.reshape(rows, d),
                              w_out.astype(BF16), name="even_out_proj")
    return (h_new.reshape(b, t, d), k.reshape(b, t_valid, SB_HEADS, SB_DIM), v.reshape(b, t_valid, SB_HEADS, SB_DIM),
            s_fin, xp_tail)


PEER_HALF = PEER_QDIM // 2
_NSEL = PEER_TOPK + 1
_SUB = 256
_CAND = tuple((a, b) for a in range(_NSEL) for b in range(_NSEL) if (a + 1) * (b + 1) <= _NSEL)
_NCAND = _round_up(len(_CAND), 8)


_SUBLANES = 8


def _sorting_network(n):
    pairs = []

    def merge(lo, m, r):
        step = 2 * r
        if step < m:
            merge(lo, m, step)
            merge(lo + r, m, step)
            pairs.extend((i, i + r) for i in range(lo + r, lo + m - r, step))
        else:
            pairs.append((lo, lo + r))

    def sort(lo, m):
        if m > 1:
            sort(lo, m // 2)
            sort(lo + m // 2, m // 2)
            merge(lo, m, 1)

    sort(0, n)
    return pairs


def _top_values(x, n, out_ref):
    rows, tn = x.shape
    groups = rows // _SUBLANES
    width = 1 << (groups - 1).bit_length()
    minus_inf = jnp.full((_SUBLANES, tn), -jnp.inf, F32)
    lists = [x[r * _SUBLANES:(r + 1) * _SUBLANES, :] for r in range(groups)] + [minus_inf] * (width - groups)
    for i, j in _sorting_network(width):
        lists[i], lists[j] = jnp.maximum(lists[i], lists[j]), jnp.minimum(lists[i], lists[j])
    lists = lists[:groups]
    sub = lax.broadcasted_iota(jnp.int32, (_SUBLANES, 1), 0)
    for it in range(n):
        head = lists[0]
        m = jnp.max(head, axis=0, keepdims=True)
        out_ref[it:it + 1, :] = m
        still_needed = n - it - 1
        if still_needed == 0:
            break
        first = jnp.min(jnp.where(head == m, sub, _SUBLANES), axis=0, keepdims=True)
        won = sub == first
        for r in range(min(groups, still_needed)):
            below = lists[r + 1] if r + 1 < groups else minus_inf
            lists[r] = jnp.where(won, below, lists[r])


def _gelu_tanh(x):
    return 0.5 * x * (1.0 + jnp.tanh(0.7978845608028654 * (x + 0.044715 * (x * x * x))))


def _peer_body(h_ref, g_ref, wq_ref, k1_ref, k2_ref, u_ref, vt_ref, gf_ref, o_ref,
               xn_ref, q_ref, ns1_ref, s2m_ref, e1_ref, e2_ref, t1_ref, t2_ref, cand_ref, csort_ref, acc_ref,
               *, te, final_norm):
    e = pl.program_id(1)
    n_e = pl.num_programs(1)
    tn = h_ref.shape[0]

    @pl.when(e == 0)
    def _prologue():
        x = h_ref[...]
        xn = x * lax.rsqrt(jnp.mean(x * x, axis=-1, keepdims=True) + RMS_EPS) * g_ref[...]
        xb = xn.astype(BF16)
        xn_ref[...] = xb
        q = jnp.dot(xb, wq_ref[...], preferred_element_type=F32)
        for j in range(2 * PEER_HEADS):
            q_ref[j] = q[:, j * PEER_HALF:(j + 1) * PEER_HALF]
        acc_ref[...] = jnp.zeros_like(acc_ref)
        cand_ref[...] = jnp.full(cand_ref.shape, -jnp.inf, F32)

        def per_head(h, _):
            s1 = lax.dot_general(k1_ref[h], q_ref[2 * h], _NT, preferred_element_type=F32)
            s2 = lax.dot_general(k2_ref[h], q_ref[2 * h + 1], _NT, preferred_element_type=F32)
            _top_values(s1, _NSEL, t1_ref)
            _top_values(s2, _NSEL, t2_ref)
            for r, (a, b) in enumerate(_CAND):
                cand_ref[r:r + 1, :] = t1_ref[a:a + 1, :] + t2_ref[b:b + 1, :]
            _top_values(cand_ref[...], _NSEL, csort_ref)
            thr = 0.5 * (csort_ref[PEER_TOPK - 1:PEER_TOPK, :] + csort_ref[PEER_TOPK:PEER_TOPK + 1, :])
            s_max = t1_ref[0:1, :] + t2_ref[0:1, :]
            cand = cand_ref[...]
            zsum = jnp.sum(jnp.where(cand >= thr, jnp.exp(cand - s_max), 0.0), axis=0, keepdims=True)
            ns1_ref[h] = -s1
            s2m_ref[h] = s2 - thr
            e1_ref[h] = jnp.exp(s1 - t1_ref[0:1, :]) / zsum
            e2_ref[h] = jnp.exp(s2 - t2_ref[0:1, :])
            return 0

        lax.fori_loop(0, PEER_HEADS, per_head, 0)

    xb = xn_ref[...]

    w_tiles = []
    for j in range(te // _SUB):
        r0 = j * _SUB
        a_t = lax.dot_general(u_ref[r0:r0 + _SUB, :], xb, _NT, preferred_element_type=F32)
        n_i1 = _SUB // PEER_NKEYS
        i1s = [e * (te // PEER_NKEYS) + j * n_i1 + r for r in range(n_i1)]
        ns1_rows = [[ns1_ref[h, pl.ds(i1, 1), :] for h in range(PEER_HEADS)] for i1 in i1s]
        e1_rows = [[e1_ref[h, pl.ds(i1, 1), :] for h in range(PEER_HEADS)] for i1 in i1s]
        tiles = [[] for _ in i1s]
        for c0 in range(0, tn, LANES):
            gsums = [None] * n_i1
            for h in range(PEER_HEADS):
                s2m_t = s2m_ref[h, :, c0:c0 + LANES]
                e2_t = e2_ref[h, :, c0:c0 + LANES]
                for r in range(n_i1):
                    term = jnp.where(s2m_t >= ns1_rows[r][h][:, c0:c0 + LANES],
                                     e2_t * e1_rows[r][h][:, c0:c0 + LANES], 0.0)
                    gsums[r] = term if gsums[r] is None else gsums[r] + term
            for r in range(n_i1):
                tiles[r].append(gsums[r])
        gates = jnp.concatenate([jnp.concatenate(t, axis=1) for t in tiles], axis=0)
        w_tiles.append((_gelu_tanh(a_t) * gates).astype(BF16))
    acc_ref[...] += jnp.dot(vt_ref[...], jnp.concatenate(w_tiles, axis=0), preferred_element_type=F32)

    @pl.when(e == n_e - 1)
    def _epilogue():
        y = h_ref[...] + acc_ref[...].T
        if final_norm:
            y = y * lax.rsqrt(jnp.mean(y * y, axis=-1, keepdims=True) + RMS_EPS) * gf_ref[...]
        o_ref[...] = y


PEER_TN = 640


def peer_residual(h, g, wq, k1, k2, u_bf, vt_bf, layer, final_g=None, tn=PEER_TN, te=2048, name="peer"):
    m, d = h.shape
    tn = min(tn, m)
    assert m % tn == 0 and N_EXPERTS % te == 0 and te % _SUB == 0
    final_norm = final_g is not None
    gf = (final_g if final_norm else jnp.ones((d,), F32)).astype(F32).reshape(1, d)
    const = lambda *shape: pl.BlockSpec(shape, lambda i, j: (0,) * len(shape))
    big = lambda: pltpu.VMEM((PEER_HEADS, PEER_NKEYS, tn), F32)
    return pl.pallas_call(
        functools.partial(_peer_body, te=te, final_norm=final_norm),
        grid=(m // tn, N_EXPERTS // te),
        in_specs=[
            pl.BlockSpec((tn, d), lambda i, j: (i, 0)),
            const(1, d),
            const(d, PEER_HEADS * PEER_QDIM),
            const(PEER_HEADS, PEER_NKEYS, PEER_HALF),
            const(PEER_HEADS, PEER_NKEYS, PEER_HALF),
            pl.BlockSpec((None, te, d), lambda i, j: (layer, j, 0)),
            pl.BlockSpec((None, d, te), lambda i, j: (layer, 0, j)),
            const(1, d),
        ],
        out_specs=pl.BlockSpec((tn, d), lambda i, j: (i, 0)),
        out_shape=jax.ShapeDtypeStruct((m, d), F32),
        scratch_shapes=[
            pltpu.VMEM((tn, d), BF16),
            pltpu.VMEM((2 * PEER_HEADS, tn, PEER_HALF), F32),
            big(), big(), big(), big(),
            pltpu.VMEM((_round_up(_NSEL, 8), tn), F32),
            pltpu.VMEM((_round_up(_NSEL, 8), tn), F32),
            pltpu.VMEM((_NCAND, tn), F32),
            pltpu.VMEM((_round_up(_NSEL, 8), tn), F32),
            pltpu.VMEM((d, tn), F32),
        ],
        compiler_params=_cparams("parallel", "arbitrary"),
        name=name,
    )(h, g.astype(F32).reshape(1, d), wq.astype(BF16), k1.astype(F32), k2.astype(F32), u_bf, vt_bf, gf)


def _kv_layouts(k_all, v_all, tk):
    b, t, w = k_all.shape
    vt = v_all.astype(BF16).reshape(b, t // tk, tk, w).transpose(0, 1, 3, 2)
    return k_all.astype(BF16), vt


def odd_mixer(h, g, w_in, b_f, w_out, k_past, v_past, logf_past, t_valid):
    b, t, d = h.shape
    p = k_past.shape[1]
    rows = b * t
    w_f = jnp.pad(w_in[:, 3 * FOX_W:], ((0, 0), (0, LANES - FOX_HEADS)))
    w_bf = jnp.concatenate([w_in[:, :3 * FOX_W], w_f], axis=1).astype(BF16)
    bias = jnp.pad(b_f.astype(F32), (0, LANES - FOX_HEADS)).reshape(1, LANES)
    splits = (FOX_W, FOX_W, FOX_W, LANES)
    tk = ROW_TILE
    if p == 0 and b == 1 and t % tk == 0:
        q_bf, k, v, k_bf, vt_bf, logf, qq, kk, qk = norm_proj(
            h.reshape(rows, d), g, w_bf, splits, bias=bias, rows_out=t_valid, name="odd_in_proj",
            outs=((0, "bf16"), (1, "f32_rows"), (2, "f32_rows"), (1, "bf16"), (2, "bf16_t"), (3, "f32")),
            head_stats=(0, 1, FOX_DIM))
        logf = logf[None]
        f_cum = cumsum_rows(logf, name="fox_cumsum")
        stats = (qq[None, :, :FOX_HEADS], jnp.max(kk[:, :FOX_HEADS], axis=0)[None], qk[None, :, :FOX_HEADS])
        o = fox_attention(q_bf[None], k_bf[None], vt_bf[None], f_cum, f_cum[:, :, :FOX_HEADS], stats, 0, t_valid)
        k = k[None]
        v = v[None]
    else:
        q, k, v, logf = norm_proj(h.reshape(rows, d), g, w_bf, splits, bias=bias, name="odd_in_proj")
        q = q.reshape(b, t, FOX_W)
        k = k.reshape(b, t, FOX_W)
        v = v.reshape(b, t, FOX_W)
        logf = logf.reshape(b, t, LANES)
        tq_pad = _round_up(t, LANES)
        tk_pad = _round_up(p + t, tk)
        k_all = _pad_rows(jnp.concatenate([k_past.reshape(b, p, FOX_W), k], axis=1), tk_pad, 1)
        v_all = _pad_rows(jnp.concatenate([v_past.reshape(b, p, FOX_W), v], axis=1), tk_pad, 1)
        logf_past = jnp.pad(logf_past.astype(F32), ((0, 0), (0, 0), (0, LANES - FOX_HEADS)))
        logf_all = _pad_rows(jnp.concatenate([logf_past, logf], axis=1), tk_pad, 1)
        f_cum = cumsum_rows(logf_all, name="fox_cumsum")
        f_q = _pad_rows(f_cum[:, p:p + t, :FOX_HEADS], tq_pad, 1)
        k_bf, vt_bf = _kv_layouts(k_all, v_all, tk)
        q_bf = _pad_rows(q, tq_pad, 1).astype(BF16)
        per_head = lambda a: a.reshape(*a.shape[:2], FOX_HEADS, FOX_DIM)
        stats = (jnp.sum(jnp.square(per_head(q)), axis=-1), jnp.max(jnp.sum(jnp.square(per_head(k_all)), axis=-1), axis=1),
                 jnp.sum(per_head(q) * per_head(k), axis=-1))
        o = fox_attention(q_bf, k_bf, vt_bf, f_cum, f_q, stats, p, t_valid)[:, :t]
        k = k[:, :t_valid]
        v = v[:, :t_valid]
    h_new = out_proj_residual([o.reshape(rows, FOX_W)], h.reshape(rows, d), w_out.astype(BF16), name="odd_out_proj")
    return (h_new.reshape(b, t, d), k.reshape(b, t_valid, FOX_HEADS, FOX_DIM), v.reshape(b, t_valid, FOX_HEADS, FOX_DIM),
            logf[:, :t_valid, :FOX_HEADS])


def kernel(x_prompt, x_sample, cache_sb_k, cache_sb_v, state_gdn, state_gdn_conv, cache_fox_k, cache_fox_v, cache_fox_logf, meta_tokens, norm_mix, norm_ffn, norm_final, w_in_even, w_out_even, gdn_conv_w, gdn_a_log, gdn_dt_bias, gdn_norm, w_in_odd, b_forget, w_out_odd, peer_wq, peer_k1, peer_k2, peer_u, peer_v):
    bsz, seq, d = x_prompt.shape
    dec_b, dec_t, _ = x_sample.shape
    depth = norm_mix.shape[0]
    dt = x_prompt.dtype
    t_p = N_META + seq
    t_pad = _round_up(t_p, PEER_TN)

    meta = jnp.broadcast_to(meta_tokens.astype(dt)[None], (bsz, N_META, d))
    hp = _pad_rows(jnp.concatenate([meta, x_prompt], axis=1), t_pad, 1)
    hs = x_sample

    empty_sb = jnp.zeros((bsz, 0, SB_HEADS, SB_DIM), dt)
    zero_s = jnp.zeros((bsz, GDN_HEADS, GDN_DK, GDN_DV), dt)
    zero_buf = jnp.zeros((bsz, GDN_CONV - 1, GDN_CONV_DIM), dt)
    empty_fox = jnp.zeros((bsz, 0, FOX_HEADS, FOX_DIM), dt)
    empty_logf = jnp.zeros((bsz, 0, FOX_HEADS), dt)

    sbk_p, sbv_p, sbk_s, sbv_s = [], [], [], []
    gs_p, gs_s, gc_p, gc_s = [], [], [], []
    fk_p, fv_p, ff_p, fk_s, fv_s, ff_s = [], [], [], [], [], []

    u_bf = peer_u.astype(BF16)
    vt_bf = jnp.swapaxes(peer_v, 1, 2).astype(BF16)

    for layer in range(depth):
        if layer % 2 == 0:
            e = layer // 2
            w = (norm_mix[layer], w_in_even[e], w_out_even[e], gdn_conv_w[e], gdn_a_log[e], gdn_dt_bias[e], gdn_norm[e])
            hp, kp, vp, sp, bp = even_mixer(hp, *w, empty_sb, empty_sb, zero_s, zero_buf, t_p)
            hs, ks_, vs_, ss, bs = even_mixer(hs, *w, cache_sb_k[e], cache_sb_v[e], state_gdn[e], state_gdn_conv[e], dec_t)
            sbk_p.append(kp); sbv_p.append(vp); sbk_s.append(ks_); sbv_s.append(vs_)
            gs_p.append(sp); gs_s.append(ss); gc_p.append(bp); gc_s.append(bs)
        else:
            o = layer // 2
            w = (norm_mix[layer], w_in_odd[o], b_forget[o], w_out_odd[o])
            hp, kp, vp, fp = odd_mixer(hp, *w, empty_fox, empty_fox, empty_logf, t_p)
            hs, ks_, vs_, fs = odd_mixer(hs, *w, cache_fox_k[o], cache_fox_v[o], cache_fox_logf[o], dec_t)
            fk_p.append(kp); fv_p.append(vp); ff_p.append(fp)
            fk_s.append(ks_); fv_s.append(vs_); ff_s.append(fs)
        last = layer == depth - 1
        pw = (norm_ffn[layer], peer_wq[layer], peer_k1[layer], peer_k2[layer], u_bf, vt_bf, layer,
              norm_final if last else None)
        n_s = dec_b * dec_t
        if bsz == 1 and t_pad - t_p >= n_s:
            both = lax.dynamic_update_slice(hp[0], hs.reshape(n_s, d), (t_p, 0))
            both = peer_residual(both, *pw, name="peer")
            hp = both[None]
            hs = both[t_p:t_p + n_s].reshape(dec_b, dec_t, d)
        else:
            hp = peer_residual(hp.reshape(bsz * t_pad, d), *pw, name="peer_prompt").reshape(bsz, t_pad, d)
            hs = peer_residual(hs.reshape(n_s, d), *pw, name="peer_sample").reshape(dec_b, dec_t, d)

    y_prompt = hp[:, N_META:t_p]
    y_sample = hs
    return (y_prompt, y_sample,
            jnp.stack(sbk_p), jnp.stack(sbv_p), jnp.stack(sbk_s), jnp.stack(sbv_s),
            jnp.stack(gs_p), jnp.stack(gs_s), jnp.stack(gc_p), jnp.stack(gc_s),
            jnp.stack(fk_p), jnp.stack(fv_p), jnp.stack(ff_p),
            jnp.stack(fk_s), jnp.stack(fv_s), jnp.stack(ff_s))
```

```python
import functools

import jax
import jax.numpy as jnp
from jax import lax
from jax.experimental import pallas as pl
from jax.experimental.pallas import tpu as pltpu

F32 = jnp.float32
BF16 = jnp.bfloat16

D_MODEL = 1024
N_META = 16
CHUNK = 64
SB_DIM = 64
SB_HEADS = 8
SB_W = SB_HEADS * SB_DIM
GDN_DK = 128
GDN_DV = 128
GDN_HEADS = 4
GDN_QK = GDN_HEADS * GDN_DK
GDN_V = GDN_HEADS * GDN_DV
GDN_CONV = 4
GDN_CONV_DIM = 2 * GDN_QK + GDN_V
FOX_DIM = 64
FOX_HEADS = 16
FOX_W = FOX_HEADS * FOX_DIM
PEER_HEADS = 8
PEER_NKEYS = 128
PEER_TOPK = 16
PEER_QDIM = 256
N_EXPERTS = PEER_NKEYS ** 2
RMS_EPS = 1e-6

LANES = 128
ROW_TILE = 256
ATT_TQ = 512
VMEM_LIMIT = 60 * 1024 * 1024
NEG_BIG = -1e30
_LOG2E = 1.4426950408889634

_NT = (((1,), (1,)), ((), ()))


def _cparams(*sem):
    return pltpu.CompilerParams(dimension_semantics=sem, vmem_limit_bytes=VMEM_LIMIT)


def _round_up(n, m):
    return -(-n // m) * m


def _pad_rows(a, rows, axis=0):
    pad = [(0, 0)] * a.ndim
    pad[axis] = (0, rows - a.shape[axis])
    return jnp.pad(a, pad)


def _norm_proj_body(x_ref, g_ref, w_ref, b_ref, *out_refs, offs, outs, logsig_split, head_stats):
    x = x_ref[...]
    xn = x * lax.rsqrt(jnp.mean(x * x, axis=-1, keepdims=True) + RMS_EPS) * g_ref[...]
    xb = xn.astype(BF16)
    cols = {}
    for (i, kind), o_ref in zip(outs, out_refs):
        if i not in cols:
            y = jnp.dot(xb, w_ref[:, offs[i]:offs[i + 1]], preferred_element_type=F32)
            cols[i] = jax.nn.log_sigmoid(y + b_ref[...]) if i == logsig_split else y
        y = cols[i]
        if kind == "bf16_t":
            o_ref[0] = y.T.astype(BF16)
        else:
            o_ref[...] = y.astype(o_ref.dtype)
    if head_stats is not None:
        iq, ik, dim = head_stats
        n = offs[iq + 1] - offs[iq]
        group = (lax.broadcasted_iota(jnp.int32, (n, LANES), 0) // dim
                 == lax.broadcasted_iota(jnp.int32, (n, LANES), 1)).astype(BF16)
        yq, yk = cols[iq], cols[ik]
        for o_ref, prod in zip(out_refs[len(outs):], (yq * yq, yk * yk, yq * yk)):
            o_ref[...] = jnp.dot(prod.astype(BF16), group, preferred_element_type=F32)


def norm_proj(x, g, w_bf, splits, outs=None, bias=None, rows_out=None, head_stats=None, name="norm_proj"):
    m, d = x.shape
    n = w_bf.shape[1]
    offs = [0]
    for s in splits:
        offs.append(offs[-1] + s)
    tm = min(ROW_TILE, m)
    assert offs[-1] == n and m % tm == 0
    outs = tuple((i, "f32") for i in range(len(splits))) if outs is None else tuple(outs)
    logsig_split = len(splits) - 1 if bias is not None else -1
    if bias is None:
        bias = jnp.zeros((1, splits[-1]), F32)
    out_specs, out_shape = [], []
    for i, kind in outs:
        s = splits[i]
        if kind == "bf16_t":
            out_specs.append(pl.BlockSpec((1, s, tm), lambda r: (r, 0, 0)))
            out_shape.append(jax.ShapeDtypeStruct((m // tm, s, tm), BF16))
        else:
            rows = rows_out if kind == "f32_rows" else m
            assert m - rows < tm
            out_specs.append(pl.BlockSpec((tm, s), lambda r: (r, 0)))
            out_shape.append(jax.ShapeDtypeStruct((rows, s), BF16 if kind == "bf16" else F32))
    if head_stats is not None:
        assert {head_stats[0], head_stats[1]} <= {i for i, _ in outs}
        out_specs += [pl.BlockSpec((tm, LANES), lambda r: (r, 0))] * 3
        out_shape += [jax.ShapeDtypeStruct((m, LANES), F32)] * 3
    return pl.pallas_call(
        functools.partial(_norm_proj_body, offs=tuple(offs), outs=outs, logsig_split=logsig_split,
                          head_stats=head_stats),
        grid=(m // tm,),
        in_specs=[
            pl.BlockSpec((tm, d), lambda i: (i, 0)),
            pl.BlockSpec((1, d), lambda i: (0, 0)),
            pl.BlockSpec((d, n), lambda i: (0, 0)),
            pl.BlockSpec((1, splits[-1]), lambda i: (0, 0)),
        ],
        out_specs=out_specs,
        out_shape=out_shape,
        compiler_params=_cparams("parallel"),
        name=name,
    )(x, g.reshape(1, d), w_bf, bias)


def _out_proj_body(*refs, n_in, offs):
    a_refs = refs[:n_in]
    h_ref, w_ref, o_ref = refs[n_in:]
    acc = h_ref[...]
    for i, a_ref in enumerate(a_refs):
        acc = acc + jnp.dot(a_ref[...].astype(BF16), w_ref[offs[i]:offs[i + 1], :], preferred_element_type=F32)
    o_ref[...] = acc


def out_proj_residual(parts, h, w_bf, name="out_proj"):
    m, d = h.shape
    offs = [0]
    for a in parts:
        offs.append(offs[-1] + a.shape[1])
    tm = min(ROW_TILE, m)
    assert offs[-1] == w_bf.shape[0] and m % tm == 0
    return pl.pallas_call(
        functools.partial(_out_proj_body, n_in=len(parts), offs=tuple(offs)),
        grid=(m // tm,),
        in_specs=[pl.BlockSpec((tm, a.shape[1]), lambda i: (i, 0)) for a in parts] + [
            pl.BlockSpec((tm, d), lambda i: (i, 0)),
            pl.BlockSpec(w_bf.shape, lambda i: (0, 0)),
        ],
        out_specs=pl.BlockSpec((tm, d), lambda i: (i, 0)),
        out_shape=jax.ShapeDtypeStruct((m, d), F32),
        compiler_params=_cparams("parallel"),
        name=name,
    )(*parts, h, w_bf)


def _cumsum_body(x_ref, o_ref, carry_ref):
    @pl.when(pl.program_id(1) == 0)
    def _():
        carry_ref[...] = jnp.zeros_like(carry_ref)

    x = x_ref[0]
    t = x.shape[0]
    tri = (lax.broadcasted_iota(jnp.int32, (t, t), 0) >= lax.broadcasted_iota(jnp.int32, (t, t), 1)).astype(F32)
    c = jnp.dot(tri, x, preferred_element_type=F32, precision=lax.Precision.HIGHEST) + carry_ref[...]
    o_ref[0] = c
    carry_ref[...] = c[t - 1:t, :]


def cumsum_rows(x, name="cumsum_rows"):
    b, l, c = x.shape
    tm = ROW_TILE
    assert l % tm == 0
    return pl.pallas_call(
        _cumsum_body,
        grid=(b, l // tm),
        in_specs=[pl.BlockSpec((1, tm, c), lambda i, j: (i, j, 0))],
        out_specs=pl.BlockSpec((1, tm, c), lambda i, j: (i, j, 0)),
        out_shape=jax.ShapeDtypeStruct((b, l, c), F32),
        scratch_shapes=[pltpu.VMEM((1, c), F32)],
        compiler_params=_cparams("parallel", "arbitrary"),
        name=name,
    )(x)


def _fox_body(nsteps_ref, q_ref, k_ref, vt_ref, fk_ref, fq_ref, o_ref, acc_ref, s_ref, p_ref, *, q_offset, tq, tk):
    bi = pl.program_id(0)
    hp = pl.program_id(1)
    qi = pl.program_id(2)
    q = (q_ref[0].astype(F32) * (FOX_DIM ** -0.5 * _LOG2E)).astype(BF16)
    lane = lax.broadcasted_iota(jnp.int32, (1, LANES), 1)
    q_heads = (jnp.where(lane < FOX_DIM, q, jnp.zeros_like(q)), jnp.where(lane >= FOX_DIM, q, jnp.zeros_like(q)))
    q0 = q_offset + qi * tq
    q_pos = q0 + lax.broadcasted_iota(jnp.int32, (1, tq), 1)
    kb_diag = jnp.minimum((q0 + tq - 1) // tk, k_ref.shape[1] // tk - 1)
    n_causal = max(tq // tk, 1)
    fq_all = fq_ref[0]
    qf_heads = []
    for h in range(2):
        lo = _F_PARTS * (2 * hp + h)
        mine = ((lane >= lo) & (lane < lo + _F_PARTS)) | ((lane >= _F_ONES + lo) & (lane < _F_ONES + lo + _F_PARTS))
        qf_heads.append(jnp.concatenate([q_heads[h], jnp.where(mine, fq_all, jnp.zeros_like(fq_all))], axis=1))
    acc_ref[...] = jnp.zeros_like(acc_ref)

    def scores(kb):
        k0 = pl.multiple_of(kb * tk, tk)
        kf = jnp.concatenate([k_ref[0, pl.ds(k0, tk), :], fk_ref[0, pl.ds(k0, tk), :]], axis=1)
        for h in range(2):
            s_ref[h] = lax.dot_general(kf, qf_heads[h], _NT, preferred_element_type=F32)

    def accumulate(kb, alphas):
        vt = vt_ref[0, kb]
        for h in range(2):
            acc_ref[h] = alphas[h] * acc_ref[h] + jnp.dot(vt, p_ref[h], preferred_element_type=F32)

    def softmax(kb, carry, causal):
        if causal:
            mask = (kb * tk + lax.broadcasted_iota(jnp.int32, (tk, 1), 0)) <= q_pos
        stats, alphas = [], []
        for h in range(2):
            m_prev, l_prev = carry[2 * h], carry[2 * h + 1]
            s = s_ref[h]
            if causal:
                s = jnp.where(mask, s, NEG_BIG)
            m_new = jnp.maximum(m_prev, jnp.max(s, axis=0, keepdims=True))
            p = jnp.exp2(s - m_new)
            alpha = jnp.exp2(m_prev - m_new)
            p_ref[h] = p.astype(BF16)
            stats += [m_new, alpha * l_prev + jnp.sum(p, axis=0, keepdims=True)]
            alphas.append(alpha)
        return tuple(stats + alphas)

    def step(j, carry):
        kb = kb_diag - j
        accumulate(kb + 1, carry[4:6])
        carry = softmax(kb, carry, causal=False)
        scores(jnp.maximum(kb - 1, 0))
        return carry

    ones = jnp.ones((1, tq), F32)
    init = (jnp.full((1, tq), NEG_BIG, F32), jnp.zeros((1, tq), F32),
            jnp.full((1, tq), NEG_BIG, F32), jnp.zeros((1, tq), F32), ones, ones)
    n_steps = nsteps_ref[bi, hp, qi]
    scores(kb_diag)
    carry = init
    for c in range(n_causal):
        if c > 0:
            accumulate(kb_diag - c + 1, carry[4:6])
        carry = softmax(kb_diag - c, carry, causal=True)
        scores(jnp.maximum(kb_diag - c - 1, 0))
    fin = lax.fori_loop(n_causal, n_steps, step, carry)
    accumulate(kb_diag - (n_steps - 1), fin[4:6])
    row = lax.broadcasted_iota(jnp.int32, (LANES, 1), 0)
    ot = jnp.where(row < FOX_DIM, acc_ref[0] / fin[1], acc_ref[1] / fin[3])
    o_ref[0] = ot.T


_UNDERFLOW = 104.0


def _fox_block_counts(stats, f_keys, f_q, q_offset, t_valid, tq_all, tq, tk):
    qq, kk_max, qk = stats
    b, _, heads = qq.shape
    tk_all = f_keys.shape[1]
    nq, nk = pl.cdiv(tq_all, tq), tk_all // tk
    scale = FOX_DIM ** -0.5
    reach = jnp.sqrt(qq * kk_max[:, None, :]) * scale
    bound = _pad_rows(1.02 * reach - qk * scale, tq_all, 1) + f_q
    valid = (jnp.arange(tq_all) < t_valid)[None, :, None]
    bound = jnp.pad(jnp.where(valid, bound, -jnp.inf), ((0, 0), (0, nq * tq - tq_all), (0, 0)),
                    constant_values=-jnp.inf)
    cq = jnp.max(bound.reshape(b, nq, tq, heads), axis=2)
    f_end = f_keys[:, tk - 1::tk, :heads]
    kb_diag = jnp.minimum((q_offset + jnp.arange(nq) * tq + tq - 1) // tk, nk - 1)
    need = (cq[:, :, None, :] - f_end[:, None, :, :]) > -_UNDERFLOW
    need = need & (jnp.arange(nk)[None, None, :, None] <= kb_diag[None, :, None, None])
    first = jnp.min(jnp.where(need, jnp.arange(nk)[None, None, :, None], nk), axis=2)
    first = jnp.min(first.reshape(b, nq, heads // 2, 2), axis=-1)
    most = kb_diag[None, :, None] + 1
    steps = jnp.clip(kb_diag[None, :, None] - first + 1, jnp.minimum(max(tq // tk, 1), most), most)
    return steps.transpose(0, 2, 1).astype(jnp.int32)


_F_PARTS = 3
_F_ONES = _F_PARTS * FOX_HEADS


def _bf16_head(x):
    bits = lax.bitcast_convert_type(x, jnp.uint32) & jnp.uint32(0xFFFF0000)
    return lax.bitcast_convert_type(bits, F32)


def _split_f_operand(f, key_side):
    b, t, heads = f.shape
    f = f.astype(F32) * _LOG2E
    hi = _bf16_head(f)
    mid = _bf16_head(f - hi)
    lo = f - hi - mid
    pieces = jnp.stack([hi, mid, lo], axis=-1).reshape(b, t, heads * _F_PARTS).astype(BF16)
    const = jnp.full((b, t, heads * _F_PARTS), 1.0 if key_side else -1.0, BF16)
    both = [pieces, const] if key_side else [const, pieces]
    return jnp.pad(jnp.concatenate(both, axis=-1), ((0, 0), (0, 0), (0, LANES - 2 * heads * _F_PARTS)))


def fox_attention(q_bf, k_bf, vt_bf, f_keys, f_q, stats, q_offset, t_valid, name="fox_attention"):
    b, tq_all, w = q_bf.shape
    tk_all = k_bf.shape[1]
    tq = min(ATT_TQ, tq_all)
    tk = ROW_TILE
    assert tk_all % tk == 0
    assert (tq % tk == 0 and q_offset % tk == 0) or (tk % tq == 0 and q_offset % tq == 0)
    nsteps = _fox_block_counts(stats, f_keys, f_q, q_offset, t_valid, tq_all, tq, tk)
    fk_aug = _split_f_operand(f_keys[:, :, :FOX_HEADS], key_side=True)
    fq_aug = _split_f_operand(f_q, key_side=False)
    return pl.pallas_call(
        functools.partial(_fox_body, q_offset=q_offset, tq=tq, tk=tk),
        grid_spec=pltpu.PrefetchScalarGridSpec(
            num_scalar_prefetch=1,
            grid=(b, w // LANES, pl.cdiv(tq_all, tq)),
            in_specs=[
                pl.BlockSpec((1, tq, LANES), lambda i, h, j, n: (i, j, h)),
                pl.BlockSpec((1, tk_all, LANES), lambda i, h, j, n: (i, 0, h)),
                pl.BlockSpec((1, tk_all // tk, LANES, tk), lambda i, h, j, n: (i, 0, h, 0)),
                pl.BlockSpec((1, tk_all, LANES), lambda i, h, j, n: (i, 0, 0)),
                pl.BlockSpec((1, tq, LANES), lambda i, h, j, n: (i, j, 0)),
            ],
            out_specs=pl.BlockSpec((1, tq, LANES), lambda i, h, j, n: (i, j, h)),
            scratch_shapes=[pltpu.VMEM((2, LANES, tq), F32), pltpu.VMEM((2, tk, tq), F32),
                            pltpu.VMEM((2, tk, tq), BF16)],
        ),
        out_shape=jax.ShapeDtypeStruct((b, tq_all, w), F32),
        compiler_params=_cparams("parallel", "parallel", "arbitrary"),
        name=name,
    )(nsteps, q_bf, k_bf, vt_bf, fk_aug, fq_aug)


_SB_BLOCKS = 2


def _sb_body(q_ref, k_ref, vt_ref, o_ref, acc_ref, *, q_offset, tq, tk, tq_all):
    qi = pl.program_id(2)
    q = q_ref[0] * jnp.asarray(SB_DIM ** -0.5, BF16)
    lane = lax.broadcasted_iota(jnp.int32, (1, LANES), 1)
    q_heads = (jnp.where(lane < SB_DIM, q, jnp.zeros_like(q)), jnp.where(lane >= SB_DIM, q, jnp.zeros_like(q)))
    q0 = q_offset + qi * tq
    q_pos = q0 + lax.broadcasted_iota(jnp.int32, (1, tq), 1)
    n_kb = jnp.minimum(jnp.maximum(q0 + tq - 2, 0) // tk + 1, k_ref.shape[1] // tk)
    real_query = q_pos < q_offset + tq_all
    upper = (lax.broadcasted_iota(jnp.int32, (tk, tk), 1) > lax.broadcasted_iota(jnp.int32, (tk, tk), 0)).astype(BF16)
    acc_ref[...] = jnp.zeros_like(acc_ref)

    def block_terms(kb, h):
        kbc = jnp.maximum(kb, 0)
        k0 = pl.multiple_of(kbc * tk, tk)
        k = k_ref[0, pl.ds(k0, tk), :]
        edge = jnp.where(kb >= 0, q_pos, -1)
        mask = (k0 + lax.broadcasted_iota(jnp.int32, (tk, 1), 0)) < edge
        z = lax.dot_general(k, q_heads[h], _NT, preferred_element_type=F32)
        sp = jnp.maximum(z, 0.0) + jnp.log(1.0 + jnp.exp(-jnp.abs(z)))
        l = jnp.where(mask, -sp, 0.0)
        l_hi = l.astype(BF16)
        l_lo = (l - l_hi.astype(F32)).astype(BF16)
        later = (jnp.dot(upper, l_hi, preferred_element_type=F32)
                 + jnp.dot(upper, l_lo, preferred_element_type=F32))
        return mask, (z - sp) + later, later[0:1, :] + l[0:1, :], vt_ref[0, kbc]

    def step(carry):
        j = carry[0]
        kb = n_kb - 1 - _SB_BLOCKS * j
        out = []
        for h in range(2):
            r = carry[2 + h]
            terms = [block_terms(kb - u, h) for u in range(_SB_BLOCKS)]
            pv = None
            for mask, log_w, total, vt in terms:
                w = jnp.where(mask, jnp.exp(log_w + r), 0.0)
                part = jnp.dot(vt, w.astype(BF16), preferred_element_type=F32)
                pv = part if pv is None else pv + part
                r = r + total
            acc_ref[h] += pv
            out.append(r)
        live = jnp.where(real_query, jnp.maximum(out[0], out[1]), -jnp.inf)
        return (j + 1, jnp.max(live), out[0], out[1])

    def more(carry):
        return (_SB_BLOCKS * carry[0] < n_kb) & (carry[1] > -_UNDERFLOW)

    lax.while_loop(more, step, (jnp.int32(0), jnp.float32(0.0), jnp.zeros((1, tq), F32), jnp.zeros((1, tq), F32)))
    row = lax.broadcasted_iota(jnp.int32, (LANES, 1), 0)
    o_ref[0] = jnp.where(row < SB_DIM, acc_ref[0], acc_ref[1]).T


def sb_attention(q_bf, k_bf, vt_bf, q_offset, name="sb_attention"):
    b, tq_all, w = q_bf.shape
    tk_all = k_bf.shape[1]
    tq = min(ROW_TILE, tq_all)
    tk = ROW_TILE
    assert tk_all % tk == 0
    return pl.pallas_call(
        functools.partial(_sb_body, q_offset=q_offset, tq=tq, tk=tk, tq_all=tq_all),
        grid=(b, w // LANES, pl.cdiv(tq_all, tq)),
        in_specs=[
            pl.BlockSpec((1, tq, LANES), lambda i, h, j: (i, j, h)),
            pl.BlockSpec((1, tk_all, LANES), lambda i, h, j: (i, 0, h)),
            pl.BlockSpec((1, tk_all // tk, LANES, tk), lambda i, h, j: (i, 0, h, 0)),
        ],
        out_specs=pl.BlockSpec((1, tq, LANES), lambda i, h, j: (i, j, h)),
        out_shape=jax.ShapeDtypeStruct((b, tq_all, w), F32),
        scratch_shapes=[pltpu.VMEM((2, LANES, tq), F32)],
        compiler_params=_cparams("parallel", "parallel", "arbitrary"),
        name=name,
    )(q_bf, k_bf, vt_bf)


_HI = lax.Precision.HIGHEST
_CONV_PAD = 8
_GDN_CPS = 5
_GDN_ROWS = _GDN_CPS * CHUNK


def _dot_hi(a, b):
    return jnp.dot(a, b, preferred_element_type=F32, precision=_HI)


def _split_bf16(x):
    hi = x.astype(BF16)
    return hi, (x - hi.astype(F32)).astype(BF16)


def _einsum3(spec, a, b):
    ah, al = a if isinstance(a, tuple) else _split_bf16(a)
    bh, bl = b if isinstance(b, tuple) else _split_bf16(b)
    prod = functools.partial(jnp.einsum, spec, preferred_element_type=F32)
    return prod(ah, bh) + (prod(ah, bl) + prod(al, bh))


def _bmm(a, b):
    return _einsum3('hij,hjk->hik', a, b)


def _bmm_nt(a, b):
    return _einsum3('hik,hjk->hij', a, b)


def _softplus(x):
    return jnp.maximum(x, 0.0) + jnp.log1p(jnp.exp(-jnp.abs(x)))


def _silu(x):
    return x / (1.0 + jnp.exp(-x))


def _unit_lower_inverse(m):
    c_len = m.shape[-1]
    ri = lax.broadcasted_iota(jnp.int32, (c_len, c_len), 0)
    ci = lax.broadcasted_iota(jnp.int32, (c_len, c_len), 1)
    d = jnp.broadcast_to((ri == ci).astype(F32), m.shape)
    s = 1
    while s < c_len:
        join = (ri // (2 * s) == ci // (2 * s)) & (ri % (2 * s) >= s) & (ci % (2 * s) < s)
        c = jnp.where(join, m, 0.0)
        if s == 1:
            d = d - c
        else:
            d_s = _split_bf16(d)
            d = d - _bmm(_bmm(d_s, c), d_s)
        s *= 2
    return d


def _gdn_body(x_ref, z_ref, ab_ref, buf_ref, s0_ref, cw_ref, alog_ref, dt_ref, gn_ref,
              o_ref, sfin_ref, xwin_ref, s_ref, *, t_valid):
    c = pl.program_id(1)
    n_c = pl.num_programs(1)
    hist = GDN_CONV - 1

    @pl.when(c == 0)
    def _():
        xwin_ref[_CONV_PAD - hist:_CONV_PAD, :] = buf_ref[0]
        s_ref[...] = s0_ref[0]

    rows = _GDN_ROWS
    xwin_ref[_CONV_PAD:_CONV_PAD + rows, :] = x_ref[0]
    conv = xwin_ref[_CONV_PAD - hist:_CONV_PAD - hist + rows, :] * cw_ref[0:1, :]
    for i in range(1, GDN_CONV):
        conv = conv + xwin_ref[_CONV_PAD - hist + i:_CONV_PAD - hist + i + rows, :] * cw_ref[i:i + 1, :]
    tail = xwin_ref[_CONV_PAD + rows - hist:_CONV_PAD + rows, :]
    xwin_ref[_CONV_PAD - hist:_CONV_PAD, :] = tail
    act = _silu(conv)

    ab = ab_ref[0]
    row_ok = (c * rows + lax.broadcasted_iota(jnp.int32, (rows, 1), 0)) < t_valid
    g_all = jnp.where(row_ok, -jnp.exp(alog_ref[...]) * _softplus(ab + dt_ref[...]), 0.0)
    beta_all = jnp.where(row_ok, 1.0 / (1.0 + jnp.exp(-ab)), 0.0)
    rr = lax.broadcasted_iota(jnp.int32, (rows, rows), 0)
    rc = lax.broadcasted_iota(jnp.int32, (rows, rows), 1)
    tri_chunks = ((rr >= rc) & (rr // CHUNK == rc // CHUNK)).astype(F32)
    gcum_all = _dot_hi(tri_chunks, g_all)
    sel = (lax.broadcasted_iota(jnp.int32, (8, LANES), 0) == lax.broadcasted_iota(jnp.int32, (8, LANES), 1)).astype(F32)
    gcum_rows = lax.dot_general(sel, gcum_all, _NT, preferred_element_type=F32, precision=_HI)
    ri = lax.broadcasted_iota(jnp.int32, (CHUNK, CHUNK), 0)
    ci = lax.broadcasted_iota(jnp.int32, (CHUNK, CHUNK), 1)
    tri = ri >= ci
    strict = ri > ci

    pairs = [(ck, h) for ck in range(_GDN_CPS) for h in range(GDN_HEADS)]
    rows_of = lambda ck: slice(ck * CHUNK, (ck + 1) * CHUNK)
    q4 = jnp.stack([act[rows_of(ck), h * GDN_DK:(h + 1) * GDN_DK] for ck, h in pairs])
    k4 = jnp.stack([act[rows_of(ck), GDN_QK + h * GDN_DK:GDN_QK + (h + 1) * GDN_DK] for ck, h in pairs])
    v4 = jnp.stack([act[rows_of(ck), 2 * GDN_QK + h * GDN_DV:2 * GDN_QK + (h + 1) * GDN_DV] for ck, h in pairs])
    q4 = q4 * lax.rsqrt(jnp.sum(q4 * q4, axis=-1, keepdims=True) + RMS_EPS) * (GDN_DK ** -0.5)
    k4 = k4 * lax.rsqrt(jnp.sum(k4 * k4, axis=-1, keepdims=True) + RMS_EPS)
    beta = jnp.stack([beta_all[rows_of(ck), GDN_HEADS + h:GDN_HEADS + h + 1] for ck, h in pairs])
    gc = jnp.stack([gcum_all[rows_of(ck), h:h + 1] for ck, h in pairs])
    gr = jnp.stack([gcum_rows[h:h + 1, rows_of(ck)] for ck, h in pairs])
    decay = jnp.exp(jnp.where(tri, gc - gr, NEG_BIG))
    kb = k4 * beta
    k4_s = _split_bf16(k4)
    m = jnp.where(strict, _bmm_nt(kb, k4_s) * decay, 0.0)
    tinv = _split_bf16(_unit_lower_inverse(m))
    eg = jnp.exp(gc)
    u = _bmm(tinv, v4 * beta)
    w = _split_bf16(_bmm(tinv, kb * eg))
    attn = _split_bf16(_bmm_nt(q4, k4_s) * decay)
    qe = _split_bf16(q4 * eg)
    g_last = gc[:, CHUNK - 1:CHUNK, :]
    k_dec = _split_bf16(k4 * jnp.exp(g_last - gc))
    s_scale = jnp.exp(g_last)

    s4 = s_ref[...]
    outs = []
    for ck in range(_GDN_CPS):
        sl = slice(ck * GDN_HEADS, (ck + 1) * GDN_HEADS)
        part = lambda pair: (pair[0][sl], pair[1][sl])
        s4_s = _split_bf16(s4)
        v_new_s = _split_bf16(u[sl] - _bmm(part(w), s4_s))
        outs.append(_bmm(part(qe), s4_s) + _bmm(part(attn), v_new_s))
        s4 = s4 * s_scale[sl] + _einsum3('hck,hcv->hkv', part(k_dec), v_new_s)
    s_ref[...] = s4
    for ck in range(_GDN_CPS):
        o = outs[ck]
        o = o * lax.rsqrt(jnp.mean(o * o, axis=-1, keepdims=True) + RMS_EPS) * gn_ref[...]
        for h in range(GDN_HEADS):
            cols = slice(h * GDN_DV, (h + 1) * GDN_DV)
            o_ref[0, rows_of(ck), cols] = o[h] * _silu(z_ref[0, rows_of(ck), cols])

    @pl.when(c == n_c - 1)
    def _():
        sfin_ref[0] = s_ref[...]


def gdn_heads(qkv_pre, z, ab, conv_buf, s0, conv_w, a_log, dt_bias, gnorm, t_valid, name="gdn"):
    b, t, cd = qkv_pre.shape
    rows = _GDN_ROWS
    assert t % rows == 0
    n_c = t // rows
    pad_l = lambda a: jnp.pad(a.astype(F32), (0, LANES - a.shape[0])).reshape(1, LANES)
    const = lambda *shape: pl.BlockSpec(shape, lambda i, j: (0,) * len(shape))
    return pl.pallas_call(
        functools.partial(_gdn_body, t_valid=t_valid),
        grid=(b, n_c),
        in_specs=[
            pl.BlockSpec((1, rows, cd), lambda i, j: (i, j, 0)),
            pl.BlockSpec((1, rows, GDN_V), lambda i, j: (i, j, 0)),
            pl.BlockSpec((1, rows, LANES), lambda i, j: (i, j, 0)),
            pl.BlockSpec((1, GDN_CONV - 1, cd), lambda i, j: (i, 0, 0)),
            pl.BlockSpec((1, GDN_HEADS, GDN_DK, GDN_DV), lambda i, j: (i, 0, 0, 0)),
            const(GDN_CONV, cd), const(1, LANES), const(1, LANES), const(1, GDN_DV),
        ],
        out_specs=[
            pl.BlockSpec((1, rows, GDN_V), lambda i, j: (i, j, 0)),
            pl.BlockSpec((1, GDN_HEADS, GDN_DK, GDN_DV), lambda i, j: (i, 0, 0, 0)),
        ],
        out_shape=[jax.ShapeDtypeStruct((b, t, GDN_V), F32),
                   jax.ShapeDtypeStruct((b, GDN_HEADS, GDN_DK, GDN_DV), F32)],
        scratch_shapes=[pltpu.VMEM((_CONV_PAD + rows, cd), F32), pltpu.VMEM((GDN_HEADS, GDN_DK, GDN_DV), F32)],
        compiler_params=_cparams("parallel", "arbitrary"),
        name=name,
    )(qkv_pre, z, ab, conv_buf.astype(F32), s0.astype(F32), conv_w.astype(F32), pad_l(a_log), pad_l(dt_bias),
      gnorm.astype(F32).reshape(1, GDN_DV))


def even_mixer(h, g, w_in, w_out, conv_w, a_log, dt_bias, gnorm, sb_k_past, sb_v_past, gdn_s0, conv_buf, t_valid):
    b, t, d = h.shape
    p = sb_k_past.shape[1]
    rows = b * t
    o0 = 3 * SB_W
    w_ab = jnp.pad(w_in[:, o0 + GDN_CONV_DIM + GDN_V:], ((0, 0), (0, LANES - 2 * GDN_HEADS)))
    w_bf = jnp.concatenate([w_in[:, :o0 + GDN_CONV_DIM + GDN_V], w_ab], axis=1).astype(BF16)
    splits = (SB_W, SB_W, SB_W, GDN_CONV_DIM, GDN_V, LANES)
    tk = ROW_TILE
    if p == 0 and b == 1 and t % tk == 0:
        q_bf, k, v, k_bf, vt_bf, qkv_pre, z, ab = norm_proj(
            h.reshape(rows, d), g, w_bf, splits, rows_out=t_valid, name="even_in_proj",
            outs=((0, "bf16"), (1, "f32_rows"), (2, "f32_rows"), (1, "bf16"), (2, "bf16_t"), (3, "f32"), (4, "f32"), (5, "f32")))
        o_sb = sb_attention(q_bf[None], k_bf[None], vt_bf[None], 0)
        k = k[None]
        v = v[None]
    else:
        q, k, v, qkv_pre, z, ab = norm_proj(h.reshape(rows, d), g, w_bf, splits, name="even_in_proj")
        q = q.reshape(b, t, SB_W)
        k = k.reshape(b, t, SB_W)
        v = v.reshape(b, t, SB_W)
        tq_pad = _round_up(t, LANES)
        tk_pad = _round_up(p + t, tk)
        k_all = _pad_rows(jnp.concatenate([sb_k_past.reshape(b, p, SB_W), k], axis=1), tk_pad, 1)
        v_all = _pad_rows(jnp.concatenate([sb_v_past.reshape(b, p, SB_W), v], axis=1), tk_pad, 1)
        k_bf, vt_bf = _kv_layouts(k_all, v_all, tk)
        q_bf = _pad_rows(q, tq_pad, 1).astype(BF16)
        o_sb = sb_attention(q_bf, k_bf, vt_bf, p)[:, :t]
        k = k[:, :t_valid]
        v = v[:, :t_valid]

    t_c = _round_up(t, _GDN_ROWS)
    qkv_pre = qkv_pre.reshape(b, t, GDN_CONV_DIM)
    o_gdn, s_fin = gdn_heads(_pad_rows(qkv_pre, t_c, 1), _pad_rows(z.reshape(b, t, GDN_V), t_c, 1),
                             _pad_rows(ab.reshape(b, t, LANES), t_c, 1), conv_buf, gdn_s0,
                             conv_w, a_log, dt_bias, gnorm, t_valid)
    o_gdn = o_gdn[:, :t]
    hist = GDN_CONV - 1
    assert t_valid >= hist
    xp_tail = qkv_pre[:, t_valid - hist:t_valid]
    h_new = out_proj_residual([o_sb.reshape(rows, SB_W), o_gdn.reshape(rows, GDN_V)], h.reshape(rows, d),
                              w_out.astype(BF16), name="even_out_proj")
    return (h_new.reshape(b, t, d), k.reshape(b, t_valid, SB_HEADS, SB_DIM), v.reshape(b, t_valid, SB_HEADS, SB_DIM),
            s_fin, xp_tail)


PEER_HALF = PEER_QDIM // 2
_NSEL = PEER_TOPK + 1
_SUB = 512
_CAND = tuple((a, b) for a in range(_NSEL) for b in range(_NSEL) if (a + 1) * (b + 1) <= _NSEL)
_NCAND = _round_up(len(_CAND), 8)


_SUBLANES = 8


def _sorting_network(n):
    pairs = []

    def merge(lo, m, r):
        step = 2 * r
        if step < m:
            merge(lo, m, step)
            merge(lo + r, m, step)
            pairs.extend((i, i + r) for i in range(lo + r, lo + m - r, step))
        else:
            pairs.append((lo, lo + r))

    def sort(lo, m):
        if m > 1:
            sort(lo, m // 2)
            sort(lo + m // 2, m // 2)
            merge(lo, m, 1)

    sort(0, n)
    return pairs


def _top_values(x, n, out_ref):
    rows, tn = x.shape
    groups = rows // _SUBLANES
    width = 1 << (groups - 1).bit_length()
    minus_inf = jnp.full((_SUBLANES, tn), -jnp.inf, F32)
    lists = [x[r * _SUBLANES:(r + 1) * _SUBLANES, :] for r in range(groups)] + [minus_inf] * (width - groups)
    for i, j in _sorting_network(width):
        lists[i], lists[j] = jnp.maximum(lists[i], lists[j]), jnp.minimum(lists[i], lists[j])
    lists = lists[:groups]
    sub = lax.broadcasted_iota(jnp.int32, (_SUBLANES, 1), 0)
    for it in range(n):
        head = lists[0]
        m = jnp.max(head, axis=0, keepdims=True)
        out_ref[it:it + 1, :] = m
        still_needed = n - it - 1
        if still_needed == 0:
            break
        first = jnp.min(jnp.where(head == m, sub, _SUBLANES), axis=0, keepdims=True)
        won = sub == first
        for r in range(min(groups, still_needed)):
            below = lists[r + 1] if r + 1 < groups else minus_inf
            lists[r] = jnp.where(won, below, lists[r])


def _gelu_tanh(x):
    return 0.5 * x * (1.0 + jnp.tanh(0.7978845608028654 * (x + 0.044715 * (x * x * x))))


def _peer_body(h_ref, g_ref, wq_ref, k1_ref, k2_ref, u_ref, vt_ref, gf_ref, o_ref,
               xn_ref, q_ref, ns1_ref, s2m_ref, e1_ref, e2_ref, t1_ref, t2_ref, cand_ref, csort_ref, acc_ref,
               *, te, final_norm):
    e = pl.program_id(1)
    n_e = pl.num_programs(1)
    tn = h_ref.shape[0]

    @pl.when(e == 0)
    def _prologue():
        x = h_ref[...]
        xn = x * lax.rsqrt(jnp.mean(x * x, axis=-1, keepdims=True) + RMS_EPS) * g_ref[...]
        xb = xn.astype(BF16)
        xn_ref[...] = xb
        q = jnp.dot(xb, wq_ref[...], preferred_element_type=F32)
        for j in range(2 * PEER_HEADS):
            q_ref[j] = q[:, j * PEER_HALF:(j + 1) * PEER_HALF]
        acc_ref[...] = jnp.zeros_like(acc_ref)
        cand_ref[...] = jnp.full(cand_ref.shape, -jnp.inf, F32)

        def per_head(h, _):
            s1 = lax.dot_general(k1_ref[h], q_ref[2 * h], _NT, preferred_element_type=F32)
            s2 = lax.dot_general(k2_ref[h], q_ref[2 * h + 1], _NT, preferred_element_type=F32)
            _top_values(s1, _NSEL, t1_ref)
            _top_values(s2, _NSEL, t2_ref)
            for r, (a, b) in enumerate(_CAND):
                cand_ref[r:r + 1, :] = t1_ref[a:a + 1, :] + t2_ref[b:b + 1, :]
            _top_values(cand_ref[...], _NSEL, csort_ref)
            thr = 0.5 * (csort_ref[PEER_TOPK - 1:PEER_TOPK, :] + csort_ref[PEER_TOPK:PEER_TOPK + 1, :])
            s_max = t1_ref[0:1, :] + t2_ref[0:1, :]
            cand = cand_ref[...]
            zsum = jnp.sum(jnp.where(cand >= thr, jnp.exp(cand - s_max), 0.0), axis=0, keepdims=True)
            ns1_ref[h] = -s1
            s2m_ref[h] = s2 - thr
            e1_ref[h] = jnp.exp(s1 - t1_ref[0:1, :]) / zsum
            e2_ref[h] = jnp.exp(s2 - t2_ref[0:1, :])
            return 0

        lax.fori_loop(0, PEER_HEADS, per_head, 0)

    xb = xn_ref[...]

    w_tiles = []
    for j in range(te // _SUB):
        r0 = j * _SUB
        a_t = lax.dot_general(u_ref[r0:r0 + _SUB, :], xb, _NT, preferred_element_type=F32)
        n_i1 = _SUB // PEER_NKEYS
        i1s = [e * (te // PEER_NKEYS) + j * n_i1 + r for r in range(n_i1)]
        ns1_rows = [[ns1_ref[h, pl.ds(i1, 1), :] for h in range(PEER_HEADS)] for i1 in i1s]
        e1_rows = [[e1_ref[h, pl.ds(i1, 1), :] for h in range(PEER_HEADS)] for i1 in i1s]
        tiles = [[] for _ in i1s]
        for c0 in range(0, tn, LANES):
            gsums = [None] * n_i1
            for h in range(PEER_HEADS):
                s2m_t = s2m_ref[h, :, c0:c0 + LANES]
                e2_t = e2_ref[h, :, c0:c0 + LANES]
                for r in range(n_i1):
                    term = jnp.where(s2m_t >= ns1_rows[r][h][:, c0:c0 + LANES],
                                     e2_t * e1_rows[r][h][:, c0:c0 + LANES], 0.0)
                    gsums[r] = term if gsums[r] is None else gsums[r] + term
            for r in range(n_i1):
                tiles[r].append(gsums[r])
        gates = jnp.concatenate([jnp.concatenate(t, axis=1) for t in tiles], axis=0)
        w_tiles.append((_gelu_tanh(a_t) * gates).astype(BF16))
    acc_ref[...] += jnp.dot(vt_ref[...], jnp.concatenate(w_tiles, axis=0), preferred_element_type=F32)

    @pl.when(e == n_e - 1)
    def _epilogue():
        y = h_ref[...] + acc_ref[...].T
        if final_norm:
            y = y * lax.rsqrt(jnp.mean(y * y, axis=-1, keepdims=True) + RMS_EPS) * gf_ref[...]
        o_ref[...] = y


PEER_TN = 640


def peer_residual(h, g, wq, k1, k2, u_bf, vt_bf, layer, final_g=None, tn=PEER_TN, te=2048, name="peer"):
    m, d = h.shape
    tn = min(tn, m)
    assert m % tn == 0 and N_EXPERTS % te == 0 and te % _SUB == 0
    final_norm = final_g is not None
    gf = (final_g if final_norm else jnp.ones((d,), F32)).astype(F32).reshape(1, d)
    const = lambda *shape: pl.BlockSpec(shape, lambda i, j: (0,) * len(shape))
    big = lambda: pltpu.VMEM((PEER_HEADS, PEER_NKEYS, tn), F32)
    return pl.pallas_call(
        functools.partial(_peer_body, te=te, final_norm=final_norm),
        grid=(m // tn, N_EXPERTS // te),
        in_specs=[
            pl.BlockSpec((tn, d), lambda i, j: (i, 0)),
            const(1, d),
            const(d, PEER_HEADS * PEER_QDIM),
            const(PEER_HEADS, PEER_NKEYS, PEER_HALF),
            const(PEER_HEADS, PEER_NKEYS, PEER_HALF),
            pl.BlockSpec((None, te, d), lambda i, j: (layer, j, 0)),
            pl.BlockSpec((None, d, te), lambda i, j: (layer, 0, j)),
            const(1, d),
        ],
        out_specs=pl.BlockSpec((tn, d), lambda i, j: (i, 0)),
        out_shape=jax.ShapeDtypeStruct((m, d), F32),
        scratch_shapes=[
            pltpu.VMEM((tn, d), BF16),
            pltpu.VMEM((2 * PEER_HEADS, tn, PEER_HALF), F32),
            big(), big(), big(), big(),
            pltpu.VMEM((_round_up(_NSEL, 8), tn), F32),
            pltpu.VMEM((_round_up(_NSEL, 8), tn), F32),
            pltpu.VMEM((_NCAND, tn), F32),
            pltpu.VMEM((_round_up(_NSEL, 8), tn), F32),
            pltpu.VMEM((d, tn), F32),
        ],
        compiler_params=_cparams("parallel", "arbitrary"),
        name=name,
    )(h, g.astype(F32).reshape(1, d), wq.astype(BF16), k1.astype(F32), k2.astype(F32), u_bf, vt_bf, gf)


def _kv_layouts(k_all, v_all, tk):
    b, t, w = k_all.shape
    vt = v_all.astype(BF16).reshape(b, t // tk, tk, w).transpose(0, 1, 3, 2)
    return k_all.astype(BF16), vt


def odd_mixer(h, g, w_in, b_f, w_out, k_past, v_past, logf_past, t_valid):
    b, t, d = h.shape
    p = k_past.shape[1]
    rows = b * t
    w_f = jnp.pad(w_in[:, 3 * FOX_W:], ((0, 0), (0, LANES - FOX_HEADS)))
    w_bf = jnp.concatenate([w_in[:, :3 * FOX_W], w_f], axis=1).astype(BF16)
    bias = jnp.pad(b_f.astype(F32), (0, LANES - FOX_HEADS)).reshape(1, LANES)
    splits = (FOX_W, FOX_W, FOX_W, LANES)
    tk = ROW_TILE
    if p == 0 and b == 1 and t % tk == 0:
        q_bf, k, v, k_bf, vt_bf, logf, qq, kk, qk = norm_proj(
            h.reshape(rows, d), g, w_bf, splits, bias=bias, rows_out=t_valid, name="odd_in_proj",
            outs=((0, "bf16"), (1, "f32_rows"), (2, "f32_rows"), (1, "bf16"), (2, "bf16_t"), (3, "f32")),
            head_stats=(0, 1, FOX_DIM))
        logf = logf[None]
        f_cum = cumsum_rows(logf, name="fox_cumsum")
        stats = (qq[None, :, :FOX_HEADS], jnp.max(kk[:, :FOX_HEADS], axis=0)[None], qk[None, :, :FOX_HEADS])
        o = fox_attention(q_bf[None], k_bf[None], vt_bf[None], f_cum, f_cum[:, :, :FOX_HEADS], stats, 0, t_valid)
        k = k[None]
        v = v[None]
    else:
        q, k, v, logf = norm_proj(h.reshape(rows, d), g, w_bf, splits, bias=bias, name="odd_in_proj")
        q = q.reshape(b, t, FOX_W)
        k = k.reshape(b, t, FOX_W)
        v = v.reshape(b, t, FOX_W)
        logf = logf.reshape(b, t, LANES)
        tq_pad = _round_up(t, LANES)
        tk_pad = _round_up(p + t, tk)
        k_all = _pad_rows(jnp.concatenate([k_past.reshape(b, p, FOX_W), k], axis=1), tk_pad, 1)
        v_all = _pad_rows(jnp.concatenate([v_past.reshape(b, p, FOX_W), v], axis=1), tk_pad, 1)
        logf_past = jnp.pad(logf_past.astype(F32), ((0, 0), (0, 0), (0, LANES - FOX_HEADS)))
        logf_all = _pad_rows(jnp.concatenate([logf_past, logf], axis=1), tk_pad, 1)
        f_cum = cumsum_rows(logf_all, name="fox_cumsum")
        f_q = _pad_rows(f_cum[:, p:p + t, :FOX_HEADS], tq_pad, 1)
        k_bf, vt_bf = _kv_layouts(k_all, v_all, tk)
        q_bf = _pad_rows(q, tq_pad, 1).astype(BF16)
        per_head = lambda a: a.reshape(*a.shape[:2], FOX_HEADS, FOX_DIM)
        stats = (jnp.sum(jnp.square(per_head(q)), axis=-1), jnp.max(jnp.sum(jnp.square(per_head(k_all)), axis=-1), axis=1),
                 jnp.sum(per_head(q) * per_head(k), axis=-1))
        o = fox_attention(q_bf, k_bf, vt_bf, f_cum, f_q, stats, p, t_valid)[:, :t]
        k = k[:, :t_valid]
        v = v[:, :t_valid]
    h_new = out_proj_residual([o.reshape(rows, FOX_W)], h.reshape(rows, d), w_out.astype(BF16), name="odd_out_proj")
    return (h_new.reshape(b, t, d), k.reshape(b, t_valid, FOX_HEADS, FOX_DIM), v.reshape(b, t_valid, FOX_HEADS, FOX_DIM),
            logf[:, :t_valid, :FOX_HEADS])


def kernel(x_prompt, x_sample, cache_sb_k, cache_sb_v, state_gdn, state_gdn_conv, cache_fox_k, cache_fox_v, cache_fox_logf, meta_tokens, norm_mix, norm_ffn, norm_final, w_in_even, w_out_even, gdn_conv_w, gdn_a_log, gdn_dt_bias, gdn_norm, w_in_odd, b_forget, w_out_odd, peer_wq, peer_k1, peer_k2, peer_u, peer_v):
    bsz, seq, d = x_prompt.shape
    dec_b, dec_t, _ = x_sample.shape
    depth = norm_mix.shape[0]
    dt = x_prompt.dtype
    t_p = N_META + seq
    t_pad = _round_up(t_p, PEER_TN)

    meta = jnp.broadcast_to(meta_tokens.astype(dt)[None], (bsz, N_META, d))
    hp = _pad_rows(jnp.concatenate([meta, x_prompt], axis=1), t_pad, 1)
    hs = x_sample

    empty_sb = jnp.zeros((bsz, 0, SB_HEADS, SB_DIM), dt)
    zero_s = jnp.zeros((bsz, GDN_HEADS, GDN_DK, GDN_DV), dt)
    zero_buf = jnp.zeros((bsz, GDN_CONV - 1, GDN_CONV_DIM), dt)
    empty_fox = jnp.zeros((bsz, 0, FOX_HEADS, FOX_DIM), dt)
    empty_logf = jnp.zeros((bsz, 0, FOX_HEADS), dt)

    sbk_p, sbv_p, sbk_s, sbv_s = [], [], [], []
    gs_p, gs_s, gc_p, gc_s = [], [], [], []
    fk_p, fv_p, ff_p, fk_s, fv_s, ff_s = [], [], [], [], [], []

    u_bf = peer_u.astype(BF16)
    vt_bf = jnp.swapaxes(peer_v, 1, 2).astype(BF16)

    for layer in range(depth):
        if layer % 2 == 0:
            e = layer // 2
            w = (norm_mix[layer], w_in_even[e], w_out_even[e], gdn_conv_w[e], gdn_a_log[e], gdn_dt_bias[e], gdn_norm[e])
            hp, kp, vp, sp, bp = even_mixer(hp, *w, empty_sb, empty_sb, zero_s, zero_buf, t_p)
            hs, ks_, vs_, ss, bs = even_mixer(hs, *w, cache_sb_k[e], cache_sb_v[e], state_gdn[e], state_gdn_conv[e], dec_t)
            sbk_p.append(kp); sbv_p.append(vp); sbk_s.append(ks_); sbv_s.append(vs_)
            gs_p.append(sp); gs_s.append(ss); gc_p.append(bp); gc_s.append(bs)
        else:
            o = layer // 2
            w = (norm_mix[layer], w_in_odd[o], b_forget[o], w_out_odd[o])
            hp, kp, vp, fp = odd_mixer(hp, *w, empty_fox, empty_fox, empty_logf, t_p)
            hs, ks_, vs_, fs = odd_mixer(hs, *w, cache_fox_k[o], cache_fox_v[o], cache_fox_logf[o], dec_t)
            fk_p.append(kp); fv_p.append(vp); ff_p.append(fp)
            fk_s.append(ks_); fv_s.append(vs_); ff_s.append(fs)
        last = layer == depth - 1
        pw = (norm_ffn[layer], peer_wq[layer], peer_k1[layer], peer_k2[layer], u_bf, vt_bf, layer,
              norm_final if last else None)
        n_s = dec_b * dec_t
        if bsz == 1 and t_pad - t_p >= n_s:
            both = lax.dynamic_update_slice(hp[0], hs.reshape(n_s, d), (t_p, 0))
            both = peer_residual(both, *pw, name="peer")
            hp = both[None]
            hs = both[t_p:t_p + n_s].reshape(dec_b, dec_t, d)
        else:
            hp = peer_residual(hp.reshape(bsz * t_pad, d), *pw, name="peer_prompt").reshape(bsz, t_pad, d)
            hs = peer_residual(hs.reshape(n_s, d), *pw, name="peer_sample").reshape(dec_b, dec_t, d)

    y_prompt = hp[:, N_META:t_p]
    y_sample = hs
    return (y_prompt, y_sample,
            jnp.stack(sbk_p), jnp.stack(sbv_p), jnp.stack(sbk_s), jnp.stack(sbv_s),
            jnp.stack(gs_p), jnp.stack(gs_s), jnp.stack(gc_p), jnp.stack(gc_s),
            jnp.stack(fk_p), jnp.stack(fv_p), jnp.stack(ff_p),
            jnp.stack(fk_s), jnp.stack(fv_s), jnp.stack(ff_s))
```

```python
import functools

import jax
import jax.numpy as jnp
from jax import lax
from jax.experimental import pallas as pl
from jax.experimental.pallas import tpu as pltpu

F32 = jnp.float32
BF16 = jnp.bfloat16

D_MODEL = 1024
N_META = 16
CHUNK = 64
SB_DIM = 64
SB_HEADS = 8
SB_W = SB_HEADS * SB_DIM
GDN_DK = 128
GDN_DV = 128
GDN_HEADS = 4
GDN_QK = GDN_HEADS * GDN_DK
GDN_V = GDN_HEADS * GDN_DV
GDN_CONV = 4
GDN_CONV_DIM = 2 * GDN_QK + GDN_V
FOX_DIM = 64
FOX_HEADS = 16
FOX_W = FOX_HEADS * FOX_DIM
PEER_HEADS = 8
PEER_NKEYS = 128
PEER_TOPK = 16
PEER_QDIM = 256
N_EXPERTS = PEER_NKEYS ** 2
RMS_EPS = 1e-6

LANES = 128
ROW_TILE = 256
ATT_TQ = 512
VMEM_LIMIT = 60 * 1024 * 1024
NEG_BIG = -1e30
_LOG2E = 1.4426950408889634

_NT = (((1,), (1,)), ((), ()))


def _cparams(*sem):
    return pltpu.CompilerParams(dimension_semantics=sem, vmem_limit_bytes=VMEM_LIMIT)


def _round_up(n, m):
    return -(-n // m) * m


def _pad_rows(a, rows, axis=0):
    pad = [(0, 0)] * a.ndim
    pad[axis] = (0, rows - a.shape[axis])
    return jnp.pad(a, pad)


def _norm_proj_body(x_ref, g_ref, w_ref, b_ref, *out_refs, offs, outs, logsig_split, head_stats):
    x = x_ref[...]
    xn = x * lax.rsqrt(jnp.mean(x * x, axis=-1, keepdims=True) + RMS_EPS) * g_ref[...]
    xb = xn.astype(BF16)
    cols = {}
    for (i, kind), o_ref in zip(outs, out_refs):
        if i not in cols:
            y = jnp.dot(xb, w_ref[:, offs[i]:offs[i + 1]], preferred_element_type=F32)
            cols[i] = jax.nn.log_sigmoid(y + b_ref[...]) if i == logsig_split else y
        y = cols[i]
        if kind == "bf16_t":
            o_ref[0] = y.T.astype(BF16)
        else:
            o_ref[...] = y.astype(o_ref.dtype)
    if head_stats is not None:
        iq, ik, dim = head_stats
        n = offs[iq + 1] - offs[iq]
        group = (lax.broadcasted_iota(jnp.int32, (n, LANES), 0) // dim
                 == lax.broadcasted_iota(jnp.int32, (n, LANES), 1)).astype(BF16)
        yq, yk = cols[iq], cols[ik]
        for o_ref, prod in zip(out_refs[len(outs):], (yq * yq, yk * yk, yq * yk)):
            o_ref[...] = jnp.dot(prod.astype(BF16), group, preferred_element_type=F32)


def norm_proj(x, g, w_bf, splits, outs=None, bias=None, rows_out=None, head_stats=None, name="norm_proj"):
    m, d = x.shape
    n = w_bf.shape[1]
    offs = [0]
    for s in splits:
        offs.append(offs[-1] + s)
    tm = min(ROW_TILE, m)
    assert offs[-1] == n and m % tm == 0
    outs = tuple((i, "f32") for i in range(len(splits))) if outs is None else tuple(outs)
    logsig_split = len(splits) - 1 if bias is not None else -1
    if bias is None:
        bias = jnp.zeros((1, splits[-1]), F32)
    out_specs, out_shape = [], []
    for i, kind in outs:
        s = splits[i]
        if kind == "bf16_t":
            out_specs.append(pl.BlockSpec((1, s, tm), lambda r: (r, 0, 0)))
            out_shape.append(jax.ShapeDtypeStruct((m // tm, s, tm), BF16))
        else:
            rows = rows_out if kind == "f32_rows" else m
            assert m - rows < tm
            out_specs.append(pl.BlockSpec((tm, s), lambda r: (r, 0)))
            out_shape.append(jax.ShapeDtypeStruct((rows, s), BF16 if kind == "bf16" else F32))
    if head_stats is not None:
        assert {head_stats[0], head_stats[1]} <= {i for i, _ in outs}
        out_specs += [pl.BlockSpec((tm, LANES), lambda r: (r, 0))] * 3
        out_shape += [jax.ShapeDtypeStruct((m, LANES), F32)] * 3
    return pl.pallas_call(
        functools.partial(_norm_proj_body, offs=tuple(offs), outs=outs, logsig_split=logsig_split,
                          head_stats=head_stats),
        grid=(m // tm,),
        in_specs=[
            pl.BlockSpec((tm, d), lambda i: (i, 0)),
            pl.BlockSpec((1, d), lambda i: (0, 0)),
            pl.BlockSpec((d, n), lambda i: (0, 0)),
            pl.BlockSpec((1, splits[-1]), lambda i: (0, 0)),
        ],
        out_specs=out_specs,
        out_shape=out_shape,
        compiler_params=_cparams("parallel"),
        name=name,
    )(x, g.reshape(1, d), w_bf, bias)


def _out_proj_body(*refs, n_in, offs):
    a_refs = refs[:n_in]
    h_ref, w_ref, o_ref = refs[n_in:]
    acc = h_ref[...]
    for i, a_ref in enumerate(a_refs):
        acc = acc + jnp.dot(a_ref[...].astype(BF16), w_ref[offs[i]:offs[i + 1], :], preferred_element_type=F32)
    o_ref[...] = acc


def out_proj_residual(parts, h, w_bf, name="out_proj"):
    m, d = h.shape
    offs = [0]
    for a in parts:
        offs.append(offs[-1] + a.shape[1])
    tm = min(ROW_TILE, m)
    assert offs[-1] == w_bf.shape[0] and m % tm == 0
    return pl.pallas_call(
        functools.partial(_out_proj_body, n_in=len(parts), offs=tuple(offs)),
        grid=(m // tm,),
        in_specs=[pl.BlockSpec((tm, a.shape[1]), lambda i: (i, 0)) for a in parts] + [
            pl.BlockSpec((tm, d), lambda i: (i, 0)),
            pl.BlockSpec(w_bf.shape, lambda i: (0, 0)),
        ],
        out_specs=pl.BlockSpec((tm, d), lambda i: (i, 0)),
        out_shape=jax.ShapeDtypeStruct((m, d), F32),
        compiler_params=_cparams("parallel"),
        name=name,
    )(*parts, h, w_bf)


def _cumsum_body(x_ref, o_ref, carry_ref):
    @pl.when(pl.program_id(1) == 0)
    def _():
        carry_ref[...] = jnp.zeros_like(carry_ref)

    x = x_ref[0]
    t = x.shape[0]
    tri = (lax.broadcasted_iota(jnp.int32, (t, t), 0) >= lax.broadcasted_iota(jnp.int32, (t, t), 1)).astype(F32)
    c = jnp.dot(tri, x, preferred_element_type=F32, precision=lax.Precision.HIGHEST) + carry_ref[...]
    o_ref[0] = c
    carry_ref[...] = c[t - 1:t, :]


def cumsum_rows(x, name="cumsum_rows"):
    b, l, c = x.shape
    tm = ROW_TILE
    assert l % tm == 0
    return pl.pallas_call(
        _cumsum_body,
        grid=(b, l // tm),
        in_specs=[pl.BlockSpec((1, tm, c), lambda i, j: (i, j, 0))],
        out_specs=pl.BlockSpec((1, tm, c), lambda i, j: (i, j, 0)),
        out_shape=jax.ShapeDtypeStruct((b, l, c), F32),
        scratch_shapes=[pltpu.VMEM((1, c), F32)],
        compiler_params=_cparams("parallel", "arbitrary"),
        name=name,
    )(x)


def _fox_body(nsteps_ref, q_ref, k_ref, vt_ref, fk_ref, fq_ref, o_ref, acc_ref, s_ref, p_ref, *, q_offset, tq, tk):
    bi = pl.program_id(0)
    hp = pl.program_id(1)
    qi = pl.program_id(2)
    q = (q_ref[0].astype(F32) * (FOX_DIM ** -0.5 * _LOG2E)).astype(BF16)
    lane = lax.broadcasted_iota(jnp.int32, (1, LANES), 1)
    q_heads = (jnp.where(lane < FOX_DIM, q, jnp.zeros_like(q)), jnp.where(lane >= FOX_DIM, q, jnp.zeros_like(q)))
    q0 = q_offset + qi * tq
    q_pos = q0 + lax.broadcasted_iota(jnp.int32, (1, tq), 1)
    kb_diag = jnp.minimum((q0 + tq - 1) // tk, k_ref.shape[1] // tk - 1)
    n_causal = max(tq // tk, 1)
    fq_all = fq_ref[0]
    qf_heads = []
    for h in range(2):
        lo = _F_PARTS * (2 * hp + h)
        mine = ((lane >= lo) & (lane < lo + _F_PARTS)) | ((lane >= _F_ONES + lo) & (lane < _F_ONES + lo + _F_PARTS))
        qf_heads.append(jnp.concatenate([q_heads[h], jnp.where(mine, fq_all, jnp.zeros_like(fq_all))], axis=1))
    acc_ref[...] = jnp.zeros_like(acc_ref)

    def scores(kb):
        k0 = pl.multiple_of(kb * tk, tk)
        kf = jnp.concatenate([k_ref[0, pl.ds(k0, tk), :], fk_ref[0, pl.ds(k0, tk), :]], axis=1)
        for h in range(2):
            s_ref[h] = lax.dot_general(kf, qf_heads[h], _NT, preferred_element_type=F32)

    def accumulate(kb, alphas):
        vt = vt_ref[0, kb]
        for h in range(2):
            acc_ref[h] = alphas[h] * acc_ref[h] + jnp.dot(vt, p_ref[h], preferred_element_type=F32)

    def softmax(kb, carry, causal):
        if causal:
            mask = (kb * tk + lax.broadcasted_iota(jnp.int32, (tk, 1), 0)) <= q_pos
        stats, alphas = [], []
        for h in range(2):
            m_prev, l_prev = carry[2 * h], carry[2 * h + 1]
            s = s_ref[h]
            if causal:
                s = jnp.where(mask, s, NEG_BIG)
            m_new = jnp.maximum(m_prev, jnp.max(s, axis=0, keepdims=True))
            p = jnp.exp2(s - m_new)
            alpha = jnp.exp2(m_prev - m_new)
            p_ref[h] = p.astype(BF16)
            stats += [m_new, alpha * l_prev + jnp.sum(p, axis=0, keepdims=True)]
            alphas.append(alpha)
        return tuple(stats + alphas)

    def step(j, carry):
        kb = kb_diag - j
        accumulate(kb + 1, carry[4:6])
        carry = softmax(kb, carry, causal=False)
        scores(jnp.maximum(kb - 1, 0))
        return carry

    ones = jnp.ones((1, tq), F32)
    init = (jnp.full((1, tq), NEG_BIG, F32), jnp.zeros((1, tq), F32),
            jnp.full((1, tq), NEG_BIG, F32), jnp.zeros((1, tq), F32), ones, ones)
    n_steps = nsteps_ref[bi, hp, qi]
    scores(kb_diag)
    carry = init
    for c in range(n_causal):
        if c > 0:
            accumulate(kb_diag - c + 1, carry[4:6])
        carry = softmax(kb_diag - c, carry, causal=True)
        scores(jnp.maximum(kb_diag - c - 1, 0))
    fin = lax.fori_loop(n_causal, n_steps, step, carry)
    accumulate(kb_diag - (n_steps - 1), fin[4:6])
    row = lax.broadcasted_iota(jnp.int32, (LANES, 1), 0)
    ot = jnp.where(row < FOX_DIM, acc_ref[0] / fin[1], acc_ref[1] / fin[3])
    o_ref[0] = ot.T


_UNDERFLOW = 104.0


def _fox_block_counts(stats, f_keys, f_q, q_offset, t_valid, tq_all, tq, tk):
    b, tk_all = f_keys.shape[:2]
    nq, nk = pl.cdiv(tq_all, tq), tk_all // tk
    if stats is None:
        every = jnp.minimum((q_offset + jnp.arange(nq) * tq + tq - 1) // tk, nk - 1) + 1
        return jnp.broadcast_to(every[None, None, :], (b, FOX_HEADS // 2, nq)).astype(jnp.int32)
    qq, kk_max, qk = stats
    heads = qq.shape[-1]
    scale = FOX_DIM ** -0.5
    reach = jnp.sqrt(qq * kk_max[:, None, :]) * scale
    bound = _pad_rows(1.02 * reach - qk * scale, tq_all, 1) + f_q
    valid = (jnp.arange(tq_all) < t_valid)[None, :, None]
    bound = jnp.pad(jnp.where(valid, bound, -jnp.inf), ((0, 0), (0, nq * tq - tq_all), (0, 0)),
                    constant_values=-jnp.inf)
    cq = jnp.max(bound.reshape(b, nq, tq, heads), axis=2)
    f_end = f_keys[:, tk - 1::tk, :heads]
    kb_diag = jnp.minimum((q_offset + jnp.arange(nq) * tq + tq - 1) // tk, nk - 1)
    need = (cq[:, :, None, :] - f_end[:, None, :, :]) > -_UNDERFLOW
    need = need & (jnp.arange(nk)[None, None, :, None] <= kb_diag[None, :, None, None])
    first = jnp.min(jnp.where(need, jnp.arange(nk)[None, None, :, None], nk), axis=2)
    first = jnp.min(first.reshape(b, nq, heads // 2, 2), axis=-1)
    most = kb_diag[None, :, None] + 1
    steps = jnp.clip(kb_diag[None, :, None] - first + 1, jnp.minimum(max(tq // tk, 1), most), most)
    return steps.transpose(0, 2, 1).astype(jnp.int32)


_F_PARTS = 3
_F_ONES = _F_PARTS * FOX_HEADS


def _bf16_head(x):
    bits = lax.bitcast_convert_type(x, jnp.uint32) & jnp.uint32(0xFFFF0000)
    return lax.bitcast_convert_type(bits, F32)


def _split_f_operand(f, key_side):
    b, t, heads = f.shape
    f = f.astype(F32) * _LOG2E
    hi = _bf16_head(f)
    mid = _bf16_head(f - hi)
    lo = f - hi - mid
    pieces = jnp.stack([hi, mid, lo], axis=-1).reshape(b, t, heads * _F_PARTS).astype(BF16)
    const = jnp.full((b, t, heads * _F_PARTS), 1.0 if key_side else -1.0, BF16)
    both = [pieces, const] if key_side else [const, pieces]
    return jnp.pad(jnp.concatenate(both, axis=-1), ((0, 0), (0, 0), (0, LANES - 2 * heads * _F_PARTS)))


def fox_attention(q_bf, k_bf, vt_bf, f_keys, f_q, stats, q_offset, t_valid, name="fox_attention"):
    b, tq_all, w = q_bf.shape
    tk_all = k_bf.shape[1]
    tq = min(ATT_TQ, tq_all)
    tk = ROW_TILE
    assert tk_all % tk == 0
    assert (tq % tk == 0 and q_offset % tk == 0) or (tk % tq == 0 and q_offset % tq == 0)
    nsteps = _fox_block_counts(stats, f_keys, f_q, q_offset, t_valid, tq_all, tq, tk)
    fk_aug = _split_f_operand(f_keys[:, :, :FOX_HEADS], key_side=True)
    fq_aug = _split_f_operand(f_q, key_side=False)
    return pl.pallas_call(
        functools.partial(_fox_body, q_offset=q_offset, tq=tq, tk=tk),
        grid_spec=pltpu.PrefetchScalarGridSpec(
            num_scalar_prefetch=1,
            grid=(b, w // LANES, pl.cdiv(tq_all, tq)),
            in_specs=[
                pl.BlockSpec((1, tq, LANES), lambda i, h, j, n: (i, j, h)),
                pl.BlockSpec((1, tk_all, LANES), lambda i, h, j, n: (i, 0, h)),
                pl.BlockSpec((1, tk_all // tk, LANES, tk), lambda i, h, j, n: (i, 0, h, 0)),
                pl.BlockSpec((1, tk_all, LANES), lambda i, h, j, n: (i, 0, 0)),
                pl.BlockSpec((1, tq, LANES), lambda i, h, j, n: (i, j, 0)),
            ],
            out_specs=pl.BlockSpec((1, tq, LANES), lambda i, h, j, n: (i, j, h)),
            scratch_shapes=[pltpu.VMEM((2, LANES, tq), F32), pltpu.VMEM((2, tk, tq), F32),
                            pltpu.VMEM((2, tk, tq), BF16)],
        ),
        out_shape=jax.ShapeDtypeStruct((b, tq_all, w), F32),
        compiler_params=_cparams("parallel", "parallel", "arbitrary"),
        name=name,
    )(nsteps, q_bf, k_bf, vt_bf, fk_aug, fq_aug)


_SB_BLOCKS = 2


def _sb_body(q_ref, k_ref, vt_ref, o_ref, acc_ref, *, q_offset, tq, tk, tq_all):
    qi = pl.program_id(2)
    q = q_ref[0] * jnp.asarray(SB_DIM ** -0.5, BF16)
    lane = lax.broadcasted_iota(jnp.int32, (1, LANES), 1)
    q_heads = (jnp.where(lane < SB_DIM, q, jnp.zeros_like(q)), jnp.where(lane >= SB_DIM, q, jnp.zeros_like(q)))
    q0 = q_offset + qi * tq
    q_pos = q0 + lax.broadcasted_iota(jnp.int32, (1, tq), 1)
    n_kb = jnp.minimum(jnp.maximum(q0 + tq - 2, 0) // tk + 1, k_ref.shape[1] // tk)
    real_query = q_pos < q_offset + tq_all
    upper = (lax.broadcasted_iota(jnp.int32, (tk, tk), 1) > lax.broadcasted_iota(jnp.int32, (tk, tk), 0)).astype(BF16)
    acc_ref[...] = jnp.zeros_like(acc_ref)

    def block_terms(kb, h):
        kbc = jnp.maximum(kb, 0)
        k0 = pl.multiple_of(kbc * tk, tk)
        k = k_ref[0, pl.ds(k0, tk), :]
        edge = jnp.where(kb >= 0, q_pos, -1)
        mask = (k0 + lax.broadcasted_iota(jnp.int32, (tk, 1), 0)) < edge
        z = lax.dot_general(k, q_heads[h], _NT, preferred_element_type=F32)
        sp = jnp.maximum(z, 0.0) + jnp.log(1.0 + jnp.exp(-jnp.abs(z)))
        l = jnp.where(mask, -sp, 0.0)
        l_hi = l.astype(BF16)
        l_lo = (l - l_hi.astype(F32)).astype(BF16)
        later = (jnp.dot(upper, l_hi, preferred_element_type=F32)
                 + jnp.dot(upper, l_lo, preferred_element_type=F32))
        return mask, (z - sp) + later, later[0:1, :] + l[0:1, :], vt_ref[0, kbc]

    def step(carry):
        j = carry[0]
        kb = n_kb - 1 - _SB_BLOCKS * j
        out = []
        for h in range(2):
            r = carry[2 + h]
            terms = [block_terms(kb - u, h) for u in range(_SB_BLOCKS)]
            pv = None
            for mask, log_w, total, vt in terms:
                w = jnp.where(mask, jnp.exp(log_w + r), 0.0)
                part = jnp.dot(vt, w.astype(BF16), preferred_element_type=F32)
                pv = part if pv is None else pv + part
                r = r + total
            acc_ref[h] += pv
            out.append(r)
        live = jnp.where(real_query, jnp.maximum(out[0], out[1]), -jnp.inf)
        return (j + 1, jnp.max(live), out[0], out[1])

    def more(carry):
        return (_SB_BLOCKS * carry[0] < n_kb) & (carry[1] > -_UNDERFLOW)

    lax.while_loop(more, step, (jnp.int32(0), jnp.float32(0.0), jnp.zeros((1, tq), F32), jnp.zeros((1, tq), F32)))
    row = lax.broadcasted_iota(jnp.int32, (LANES, 1), 0)
    o_ref[0] = jnp.where(row < SB_DIM, acc_ref[0], acc_ref[1]).T


def sb_attention(q_bf, k_bf, vt_bf, q_offset, name="sb_attention"):
    b, tq_all, w = q_bf.shape
    tk_all = k_bf.shape[1]
    tq = min(ROW_TILE, tq_all)
    tk = ROW_TILE
    assert tk_all % tk == 0
    return pl.pallas_call(
        functools.partial(_sb_body, q_offset=q_offset, tq=tq, tk=tk, tq_all=tq_all),
        grid=(b, w // LANES, pl.cdiv(tq_all, tq)),
        in_specs=[
            pl.BlockSpec((1, tq, LANES), lambda i, h, j: (i, j, h)),
            pl.BlockSpec((1, tk_all, LANES), lambda i, h, j: (i, 0, h)),
            pl.BlockSpec((1, tk_all // tk, LANES, tk), lambda i, h, j: (i, 0, h, 0)),
        ],
        out_specs=pl.BlockSpec((1, tq, LANES), lambda i, h, j: (i, j, h)),
        out_shape=jax.ShapeDtypeStruct((b, tq_all, w), F32),
        scratch_shapes=[pltpu.VMEM((2, LANES, tq), F32)],
        compiler_params=_cparams("parallel", "parallel", "arbitrary"),
        name=name,
    )(q_bf, k_bf, vt_bf)


_HI = lax.Precision.HIGHEST
_CONV_PAD = 8
_GDN_CPS = 5


def _dot_hi(a, b):
    return jnp.dot(a, b, preferred_element_type=F32, precision=_HI)


def _split_bf16(x):
    hi = x.astype(BF16)
    return hi, (x - hi.astype(F32)).astype(BF16)


def _einsum3(spec, a, b):
    ah, al = a if isinstance(a, tuple) else _split_bf16(a)
    bh, bl = b if isinstance(b, tuple) else _split_bf16(b)
    prod = functools.partial(jnp.einsum, spec, preferred_element_type=F32)
    return prod(ah, bh) + (prod(ah, bl) + prod(al, bh))


def _bmm(a, b):
    return _einsum3('hij,hjk->hik', a, b)


def _bmm_nt(a, b):
    return _einsum3('hik,hjk->hij', a, b)


def _softplus(x):
    return jnp.maximum(x, 0.0) + jnp.log1p(jnp.exp(-jnp.abs(x)))


def _silu(x):
    return x / (1.0 + jnp.exp(-x))


def _unit_lower_inverse(m):
    c_len = m.shape[-1]
    ri = lax.broadcasted_iota(jnp.int32, (c_len, c_len), 0)
    ci = lax.broadcasted_iota(jnp.int32, (c_len, c_len), 1)
    d = jnp.broadcast_to((ri == ci).astype(F32), m.shape)
    s = 1
    while s < c_len:
        join = (ri // (2 * s) == ci // (2 * s)) & (ri % (2 * s) >= s) & (ci % (2 * s) < s)
        c = jnp.where(join, m, 0.0)
        if s == 1:
            d = d - c
        else:
            d_s = _split_bf16(d)
            d = d - _bmm(_bmm(d_s, c), d_s)
        s *= 2
    return d


def _gdn_body(x_ref, z_ref, ab_ref, buf_ref, s0_ref, cw_ref, alog_ref, dt_ref, gn_ref,
              o_ref, sfin_ref, xwin_ref, s_ref, *, t_valid, cps):
    c = pl.program_id(1)
    n_c = pl.num_programs(1)
    hist = GDN_CONV - 1

    @pl.when(c == 0)
    def _():
        xwin_ref[_CONV_PAD - hist:_CONV_PAD, :] = buf_ref[0]
        s_ref[...] = s0_ref[0]

    rows = cps * CHUNK
    xwin_ref[_CONV_PAD:_CONV_PAD + rows, :] = x_ref[0]
    conv = xwin_ref[_CONV_PAD - hist:_CONV_PAD - hist + rows, :] * cw_ref[0:1, :]
    for i in range(1, GDN_CONV):
        conv = conv + xwin_ref[_CONV_PAD - hist + i:_CONV_PAD - hist + i + rows, :] * cw_ref[i:i + 1, :]
    tail = xwin_ref[_CONV_PAD + rows - hist:_CONV_PAD + rows, :]
    xwin_ref[_CONV_PAD - hist:_CONV_PAD, :] = tail
    act = _silu(conv)

    ab = ab_ref[0]
    row_ok = (c * rows + lax.broadcasted_iota(jnp.int32, (rows, 1), 0)) < t_valid
    g_all = jnp.where(row_ok, -jnp.exp(alog_ref[...]) * _softplus(ab + dt_ref[...]), 0.0)
    beta_all = jnp.where(row_ok, 1.0 / (1.0 + jnp.exp(-ab)), 0.0)
    rr = lax.broadcasted_iota(jnp.int32, (rows, rows), 0)
    rc = lax.broadcasted_iota(jnp.int32, (rows, rows), 1)
    tri_chunks = ((rr >= rc) & (rr // CHUNK == rc // CHUNK)).astype(F32)
    gcum_all = _dot_hi(tri_chunks, g_all)
    sel = (lax.broadcasted_iota(jnp.int32, (8, LANES), 0) == lax.broadcasted_iota(jnp.int32, (8, LANES), 1)).astype(F32)
    gcum_rows = lax.dot_general(sel, gcum_all, _NT, preferred_element_type=F32, precision=_HI)
    ri = lax.broadcasted_iota(jnp.int32, (CHUNK, CHUNK), 0)
    ci = lax.broadcasted_iota(jnp.int32, (CHUNK, CHUNK), 1)
    tri = ri >= ci
    strict = ri > ci

    pairs = [(ck, h) for ck in range(cps) for h in range(GDN_HEADS)]
    rows_of = lambda ck: slice(ck * CHUNK, (ck + 1) * CHUNK)
    q4 = jnp.stack([act[rows_of(ck), h * GDN_DK:(h + 1) * GDN_DK] for ck, h in pairs])
    k4 = jnp.stack([act[rows_of(ck), GDN_QK + h * GDN_DK:GDN_QK + (h + 1) * GDN_DK] for ck, h in pairs])
    v4 = jnp.stack([act[rows_of(ck), 2 * GDN_QK + h * GDN_DV:2 * GDN_QK + (h + 1) * GDN_DV] for ck, h in pairs])
    q4 = q4 * lax.rsqrt(jnp.sum(q4 * q4, axis=-1, keepdims=True) + RMS_EPS) * (GDN_DK ** -0.5)
    k4 = k4 * lax.rsqrt(jnp.sum(k4 * k4, axis=-1, keepdims=True) + RMS_EPS)
    beta = jnp.stack([beta_all[rows_of(ck), GDN_HEADS + h:GDN_HEADS + h + 1] for ck, h in pairs])
    gc = jnp.stack([gcum_all[rows_of(ck), h:h + 1] for ck, h in pairs])
    gr = jnp.stack([gcum_rows[h:h + 1, rows_of(ck)] for ck, h in pairs])
    decay = jnp.exp(jnp.where(tri, gc - gr, NEG_BIG))
    kb = k4 * beta
    k4_s = _split_bf16(k4)
    m = jnp.where(strict, _bmm_nt(kb, k4_s) * decay, 0.0)
    tinv = _split_bf16(_unit_lower_inverse(m))
    eg = jnp.exp(gc)
    u = _bmm(tinv, v4 * beta)
    w = _split_bf16(_bmm(tinv, kb * eg))
    attn = _split_bf16(_bmm_nt(q4, k4_s) * decay)
    qe = _split_bf16(q4 * eg)
    g_last = gc[:, CHUNK - 1:CHUNK, :]
    k_dec = _split_bf16(k4 * jnp.exp(g_last - gc))
    s_scale = jnp.exp(g_last)

    s4 = s_ref[...]
    outs = []
    for ck in range(cps):
        sl = slice(ck * GDN_HEADS, (ck + 1) * GDN_HEADS)
        part = lambda pair: (pair[0][sl], pair[1][sl])
        s4_s = _split_bf16(s4)
        v_new_s = _split_bf16(u[sl] - _bmm(part(w), s4_s))
        outs.append(_bmm(part(qe), s4_s) + _bmm(part(attn), v_new_s))
        s4 = s4 * s_scale[sl] + _einsum3('hck,hcv->hkv', part(k_dec), v_new_s)
    s_ref[...] = s4
    for ck in range(cps):
        o = outs[ck]
        o = o * lax.rsqrt(jnp.mean(o * o, axis=-1, keepdims=True) + RMS_EPS) * gn_ref[...]
        for h in range(GDN_HEADS):
            cols = slice(h * GDN_DV, (h + 1) * GDN_DV)
            o_ref[0, rows_of(ck), cols] = o[h] * _silu(z_ref[0, rows_of(ck), cols])

    @pl.when(c == n_c - 1)
    def _():
        sfin_ref[0] = s_ref[...]


def gdn_heads(qkv_pre, z, ab, conv_buf, s0, conv_w, a_log, dt_bias, gnorm, t_valid, name="gdn"):
    b, t, cd = qkv_pre.shape
    cps = max(c for c in range(1, _GDN_CPS + 1) if (t // CHUNK) % c == 0)
    rows = cps * CHUNK
    assert t % rows == 0
    n_c = t // rows
    pad_l = lambda a: jnp.pad(a.astype(F32), (0, LANES - a.shape[0])).reshape(1, LANES)
    const = lambda *shape: pl.BlockSpec(shape, lambda i, j: (0,) * len(shape))
    return pl.pallas_call(
        functools.partial(_gdn_body, t_valid=t_valid, cps=cps),
        grid=(b, n_c),
        in_specs=[
            pl.BlockSpec((1, rows, cd), lambda i, j: (i, j, 0)),
            pl.BlockSpec((1, rows, GDN_V), lambda i, j: (i, j, 0)),
            pl.BlockSpec((1, rows, LANES), lambda i, j: (i, j, 0)),
            pl.BlockSpec((1, GDN_CONV - 1, cd), lambda i, j: (i, 0, 0)),
            pl.BlockSpec((1, GDN_HEADS, GDN_DK, GDN_DV), lambda i, j: (i, 0, 0, 0)),
            const(GDN_CONV, cd), const(1, LANES), const(1, LANES), const(1, GDN_DV),
        ],
        out_specs=[
            pl.BlockSpec((1, rows, GDN_V), lambda i, j: (i, j, 0)),
            pl.BlockSpec((1, GDN_HEADS, GDN_DK, GDN_DV), lambda i, j: (i, 0, 0, 0)),
        ],
        out_shape=[jax.ShapeDtypeStruct((b, t, GDN_V), F32),
                   jax.ShapeDtypeStruct((b, GDN_HEADS, GDN_DK, GDN_DV), F32)],
        scratch_shapes=[pltpu.VMEM((_CONV_PAD + rows, cd), F32), pltpu.VMEM((GDN_HEADS, GDN_DK, GDN_DV), F32)],
        compiler_params=_cparams("parallel", "arbitrary"),
        name=name,
    )(qkv_pre, z, ab, conv_buf.astype(F32), s0.astype(F32), conv_w.astype(F32), pad_l(a_log), pad_l(dt_bias),
      gnorm.astype(F32).reshape(1, GDN_DV))


def even_mixer(h, g, w_in, w_out, conv_w, a_log, dt_bias, gnorm, sb_k_past, sb_v_past, gdn_s0, conv_buf, t_valid):
    b, t, d = h.shape
    p = sb_k_past.shape[1]
    rows = b * t
    o0 = 3 * SB_W
    w_ab = jnp.pad(w_in[:, o0 + GDN_CONV_DIM + GDN_V:], ((0, 0), (0, LANES - 2 * GDN_HEADS)))
    w_bf = jnp.concatenate([w_in[:, :o0 + GDN_CONV_DIM + GDN_V], w_ab], axis=1).astype(BF16)
    splits = (SB_W, SB_W, SB_W, GDN_CONV_DIM, GDN_V, LANES)
    tk = ROW_TILE
    if p == 0 and b == 1 and t % tk == 0:
        q_bf, k, v, k_bf, vt_bf, qkv_pre, z, ab = norm_proj(
            h.reshape(rows, d), g, w_bf, splits, rows_out=t_valid, name="even_in_proj",
            outs=((0, "bf16"), (1, "f32_rows"), (2, "f32_rows"), (1, "bf16"), (2, "bf16_t"), (3, "f32"), (4, "f32"), (5, "f32")))
        o_sb = sb_attention(q_bf[None], k_bf[None], vt_bf[None], 0)
        k = k[None]
        v = v[None]
    else:
        q, k, v, qkv_pre, z, ab = norm_proj(h.reshape(rows, d), g, w_bf, splits, name="even_in_proj")
        q = q.reshape(b, t, SB_W)
        k = k.reshape(b, t, SB_W)
        v = v.reshape(b, t, SB_W)
        tq_pad = _round_up(t, LANES)
        tk_pad = _round_up(p + t, tk)
        k_all = _pad_rows(jnp.concatenate([sb_k_past.reshape(b, p, SB_W), k], axis=1), tk_pad, 1)
        v_all = _pad_rows(jnp.concatenate([sb_v_past.reshape(b, p, SB_W), v], axis=1), tk_pad, 1)
        k_bf, vt_bf = _kv_layouts(k_all, v_all, tk)
        q_bf = _pad_rows(q, tq_pad, 1).astype(BF16)
        o_sb = sb_attention(q_bf, k_bf, vt_bf, p)[:, :t]
        k = k[:, :t_valid]
        v = v[:, :t_valid]

    t_c = _round_up(t, CHUNK)
    qkv_pre = qkv_pre.reshape(b, t, GDN_CONV_DIM)
    o_gdn, s_fin = gdn_heads(_pad_rows(qkv_pre, t_c, 1), _pad_rows(z.reshape(b, t, GDN_V), t_c, 1),
                             _pad_rows(ab.reshape(b, t, LANES), t_c, 1), conv_buf, gdn_s0,
                             conv_w, a_log, dt_bias, gnorm, t_valid)
    o_gdn = o_gdn[:, :t]
    hist = GDN_CONV - 1
    assert t_valid >= hist
    xp_tail = qkv_pre[:, t_valid - hist:t_valid]
    h_new = out_proj_residual([o_sb.reshape(rows, SB_W), o_gdn.reshape(rows, GDN_V)], h.reshape(rows, d),
                              w_out.astype(BF16), name="even_out_proj")
    return (h_new.reshape(b, t, d), k.reshape(b, t_valid, SB_HEADS, SB_DIM), v.reshape(b, t_valid, SB_HEADS, SB_DIM),
            s_fin, xp_tail)


PEER_HALF = PEER_QDIM // 2
_NSEL = PEER_TOPK + 1
_SUB = 512
_CAND = tuple((a, b) for a in range(_NSEL) for b in range(_NSEL) if (a + 1) * (b + 1) <= _NSEL)
_NCAND = _round_up(len(_CAND), 8)


_SUBLANES = 8


def _sorting_network(n):
    pairs = []

    def merge(lo, m, r):
        step = 2 * r
        if step < m:
            merge(lo, m, step)
            merge(lo + r, m, step)
            pairs.extend((i, i + r) for i in range(lo + r, lo + m - r, step))
        else:
            pairs.append((lo, lo + r))

    def sort(lo, m):
        if m > 1:
            sort(lo, m // 2)
            sort(lo + m // 2, m // 2)
            merge(lo, m, 1)

    sort(0, n)
    return pairs


def _top_values(x, n, out_ref):
    rows, tn = x.shape
    groups = rows // _SUBLANES
    width = 1 << (groups - 1).bit_length()
    minus_inf = jnp.full((_SUBLANES, tn), -jnp.inf, F32)
    lists = [x[r * _SUBLANES:(r + 1) * _SUBLANES, :] for r in range(groups)] + [minus_inf] * (width - groups)
    for i, j in _sorting_network(width):
        lists[i], lists[j] = jnp.maximum(lists[i], lists[j]), jnp.minimum(lists[i], lists[j])
    lists = lists[:groups]
    sub = lax.broadcasted_iota(jnp.int32, (_SUBLANES, 1), 0)
    for it in range(n):
        head = lists[0]
        m = jnp.max(head, axis=0, keepdims=True)
        out_ref[it:it + 1, :] = m
        still_needed = n - it - 1
        if still_needed == 0:
            break
        first = jnp.min(jnp.where(head == m, sub, _SUBLANES), axis=0, keepdims=True)
        won = sub == first
        for r in range(min(groups, still_needed)):
            below = lists[r + 1] if r + 1 < groups else minus_inf
            lists[r] = jnp.where(won, below, lists[r])


def _gelu_tanh(x):
    return 0.5 * x * (1.0 + jnp.tanh(0.7978845608028654 * (x + 0.044715 * (x * x * x))))


def _peer_body(h_ref, g_ref, wq_ref, k1_ref, k2_ref, u_ref, vt_ref, gf_ref, o_ref,
               xn_ref, q_ref, ns1_ref, s2m_ref, e1_ref, e2_ref, t1_ref, t2_ref, cand_ref, csort_ref, acc_ref,
               *, te, final_norm):
    e = pl.program_id(1)
    n_e = pl.num_programs(1)
    tn = h_ref.shape[0]

    @pl.when(e == 0)
    def _prologue():
        x = h_ref[...]
        xn = x * lax.rsqrt(jnp.mean(x * x, axis=-1, keepdims=True) + RMS_EPS) * g_ref[...]
        xb = xn.astype(BF16)
        xn_ref[...] = xb
        q = jnp.dot(xb, wq_ref[...], preferred_element_type=F32)
        for j in range(2 * PEER_HEADS):
            q_ref[j] = q[:, j * PEER_HALF:(j + 1) * PEER_HALF]
        acc_ref[...] = jnp.zeros_like(acc_ref)
        cand_ref[...] = jnp.full(cand_ref.shape, -jnp.inf, F32)

        def per_head(h, _):
            s1 = lax.dot_general(k1_ref[h], q_ref[2 * h], _NT, preferred_element_type=F32)
            s2 = lax.dot_general(k2_ref[h], q_ref[2 * h + 1], _NT, preferred_element_type=F32)
            _top_values(s1, _NSEL, t1_ref)
            _top_values(s2, _NSEL, t2_ref)
            for r, (a, b) in enumerate(_CAND):
                cand_ref[r:r + 1, :] = t1_ref[a:a + 1, :] + t2_ref[b:b + 1, :]
            _top_values(cand_ref[...], _NSEL, csort_ref)
            thr = 0.5 * (csort_ref[PEER_TOPK - 1:PEER_TOPK, :] + csort_ref[PEER_TOPK:PEER_TOPK + 1, :])
            s_max = t1_ref[0:1, :] + t2_ref[0:1, :]
            cand = cand_ref[...]
            zsum = jnp.sum(jnp.where(cand >= thr, jnp.exp(cand - s_max), 0.0), axis=0, keepdims=True)
            ns1_ref[h] = -s1
            s2m_ref[h] = s2 - thr
            e1_ref[h] = jnp.exp(s1 - t1_ref[0:1, :]) / zsum
            e2_ref[h] = jnp.exp(s2 - t2_ref[0:1, :])
            return 0

        lax.fori_loop(0, PEER_HEADS, per_head, 0)

    xb = xn_ref[...]

    w_tiles = []
    for j in range(te // _SUB):
        r0 = j * _SUB
        a_t = lax.dot_general(u_ref[r0:r0 + _SUB, :], xb, _NT, preferred_element_type=F32)
        n_i1 = _SUB // PEER_NKEYS
        i1s = [e * (te // PEER_NKEYS) + j * n_i1 + r for r in range(n_i1)]
        ns1_rows = [[ns1_ref[h, pl.ds(i1, 1), :] for h in range(PEER_HEADS)] for i1 in i1s]
        e1_rows = [[e1_ref[h, pl.ds(i1, 1), :] for h in range(PEER_HEADS)] for i1 in i1s]
        tiles = [[] for _ in i1s]
        for c0 in range(0, tn, LANES):
            gsums = [None] * n_i1
            for h in range(PEER_HEADS):
                s2m_t = s2m_ref[h, :, c0:c0 + LANES]
                e2_t = e2_ref[h, :, c0:c0 + LANES]
                for r in range(n_i1):
                    term = jnp.where(s2m_t >= ns1_rows[r][h][:, c0:c0 + LANES],
                                     e2_t * e1_rows[r][h][:, c0:c0 + LANES], 0.0)
                    gsums[r] = term if gsums[r] is None else gsums[r] + term
            for r in range(n_i1):
                tiles[r].append(gsums[r])
        gates = jnp.concatenate([jnp.concatenate(t, axis=1) for t in tiles], axis=0)
        w_tiles.append((_gelu_tanh(a_t) * gates).astype(BF16))
    acc_ref[...] += jnp.dot(vt_ref[...], jnp.concatenate(w_tiles, axis=0), preferred_element_type=F32)

    @pl.when(e == n_e - 1)
    def _epilogue():
        y = h_ref[...] + acc_ref[...].T
        if final_norm:
            y = y * lax.rsqrt(jnp.mean(y * y, axis=-1, keepdims=True) + RMS_EPS) * gf_ref[...]
        o_ref[...] = y


PEER_TN = 640


def peer_residual(h, g, wq, k1, k2, u_bf, vt_bf, layer, final_g=None, tn=PEER_TN, te=2048, name="peer"):
    m, d = h.shape
    tn = min(tn, m)
    assert m % tn == 0 and N_EXPERTS % te == 0 and te % _SUB == 0
    final_norm = final_g is not None
    gf = (final_g if final_norm else jnp.ones((d,), F32)).astype(F32).reshape(1, d)
    const = lambda *shape: pl.BlockSpec(shape, lambda i, j: (0,) * len(shape))
    big = lambda: pltpu.VMEM((PEER_HEADS, PEER_NKEYS, tn), F32)
    return pl.pallas_call(
        functools.partial(_peer_body, te=te, final_norm=final_norm),
        grid=(m // tn, N_EXPERTS // te),
        in_specs=[
            pl.BlockSpec((tn, d), lambda i, j: (i, 0)),
            const(1, d),
            const(d, PEER_HEADS * PEER_QDIM),
            const(PEER_HEADS, PEER_NKEYS, PEER_HALF),
            const(PEER_HEADS, PEER_NKEYS, PEER_HALF),
            pl.BlockSpec((None, te, d), lambda i, j: (layer, j, 0)),
            pl.BlockSpec((None, d, te), lambda i, j: (layer, 0, j)),
            const(1, d),
        ],
        out_specs=pl.BlockSpec((tn, d), lambda i, j: (i, 0)),
        out_shape=jax.ShapeDtypeStruct((m, d), F32),
        scratch_shapes=[
            pltpu.VMEM((tn, d), BF16),
            pltpu.VMEM((2 * PEER_HEADS, tn, PEER_HALF), F32),
            big(), big(), big(), big(),
            pltpu.VMEM((_round_up(_NSEL, 8), tn), F32),
            pltpu.VMEM((_round_up(_NSEL, 8), tn), F32),
            pltpu.VMEM((_NCAND, tn), F32),
            pltpu.VMEM((_round_up(_NSEL, 8), tn), F32),
            pltpu.VMEM((d, tn), F32),
        ],
        compiler_params=_cparams("parallel", "arbitrary"),
        name=name,
    )(h, g.astype(F32).reshape(1, d), wq.astype(BF16), k1.astype(F32), k2.astype(F32), u_bf, vt_bf, gf)


def _kv_layouts(k_all, v_all, tk):
    b, t, w = k_all.shape
    vt = v_all.astype(BF16).reshape(b, t // tk, tk, w).transpose(0, 1, 3, 2)
    return k_all.astype(BF16), vt


def odd_mixer(h, g, w_in, b_f, w_out, k_past, v_past, logf_past, t_valid):
    b, t, d = h.shape
    p = k_past.shape[1]
    rows = b * t
    w_f = jnp.pad(w_in[:, 3 * FOX_W:], ((0, 0), (0, LANES - FOX_HEADS)))
    w_bf = jnp.concatenate([w_in[:, :3 * FOX_W], w_f], axis=1).astype(BF16)
    bias = jnp.pad(b_f.astype(F32), (0, LANES - FOX_HEADS)).reshape(1, LANES)
    splits = (FOX_W, FOX_W, FOX_W, LANES)
    tk = ROW_TILE
    if p == 0 and b == 1 and t % tk == 0:
        q_bf, k, v, k_bf, vt_bf, logf, qq, kk, qk = norm_proj(
            h.reshape(rows, d), g, w_bf, splits, bias=bias, rows_out=t_valid, name="odd_in_proj",
            outs=((0, "bf16"), (1, "f32_rows"), (2, "f32_rows"), (1, "bf16"), (2, "bf16_t"), (3, "f32")),
            head_stats=(0, 1, FOX_DIM))
        logf = logf[None]
        f_cum = cumsum_rows(logf, name="fox_cumsum")
        stats = (qq[None, :, :FOX_HEADS], jnp.max(kk[:, :FOX_HEADS], axis=0)[None], qk[None, :, :FOX_HEADS])
        o = fox_attention(q_bf[None], k_bf[None], vt_bf[None], f_cum, f_cum[:, :, :FOX_HEADS], stats, 0, t_valid)
        k = k[None]
        v = v[None]
    else:
        q, k, v, logf = norm_proj(h.reshape(rows, d), g, w_bf, splits, bias=bias, name="odd_in_proj")
        q = q.reshape(b, t, FOX_W)
        k = k.reshape(b, t, FOX_W)
        v = v.reshape(b, t, FOX_W)
        logf = logf.reshape(b, t, LANES)
        tq_pad = _round_up(t, LANES)
        tk_pad = _round_up(p + t, tk)
        k_all = _pad_rows(jnp.concatenate([k_past.reshape(b, p, FOX_W), k], axis=1), tk_pad, 1)
        v_all = _pad_rows(jnp.concatenate([v_past.reshape(b, p, FOX_W), v], axis=1), tk_pad, 1)
        logf_past = jnp.pad(logf_past.astype(F32), ((0, 0), (0, 0), (0, LANES - FOX_HEADS)))
        logf_all = _pad_rows(jnp.concatenate([logf_past, logf], axis=1), tk_pad, 1)
        f_cum = cumsum_rows(logf_all, name="fox_cumsum")
        f_q = _pad_rows(f_cum[:, p:p + t, :FOX_HEADS], tq_pad, 1)
        k_bf, vt_bf = _kv_layouts(k_all, v_all, tk)
        q_bf = _pad_rows(q, tq_pad, 1).astype(BF16)
        o = fox_attention(q_bf, k_bf, vt_bf, f_cum, f_q, None, p, t_valid)[:, :t]
        k = k[:, :t_valid]
        v = v[:, :t_valid]
    h_new = out_proj_residual([o.reshape(rows, FOX_W)], h.reshape(rows, d), w_out.astype(BF16), name="odd_out_proj")
    return (h_new.reshape(b, t, d), k.reshape(b, t_valid, FOX_HEADS, FOX_DIM), v.reshape(b, t_valid, FOX_HEADS, FOX_DIM),
            logf[:, :t_valid, :FOX_HEADS])


def kernel(x_prompt, x_sample, cache_sb_k, cache_sb_v, state_gdn, state_gdn_conv, cache_fox_k, cache_fox_v, cache_fox_logf, meta_tokens, norm_mix, norm_ffn, norm_final, w_in_even, w_out_even, gdn_conv_w, gdn_a_log, gdn_dt_bias, gdn_norm, w_in_odd, b_forget, w_out_odd, peer_wq, peer_k1, peer_k2, peer_u, peer_v):
    bsz, seq, d = x_prompt.shape
    dec_b, dec_t, _ = x_sample.shape
    depth = norm_mix.shape[0]
    dt = x_prompt.dtype
    t_p = N_META + seq
    t_pad = _round_up(t_p, PEER_TN)

    meta = jnp.broadcast_to(meta_tokens.astype(dt)[None], (bsz, N_META, d))
    hp = _pad_rows(jnp.concatenate([meta, x_prompt], axis=1), t_pad, 1)
    hs = x_sample

    empty_sb = jnp.zeros((bsz, 0, SB_HEADS, SB_DIM), dt)
    zero_s = jnp.zeros((bsz, GDN_HEADS, GDN_DK, GDN_DV), dt)
    zero_buf = jnp.zeros((bsz, GDN_CONV - 1, GDN_CONV_DIM), dt)
    empty_fox = jnp.zeros((bsz, 0, FOX_HEADS, FOX_DIM), dt)
    empty_logf = jnp.zeros((bsz, 0, FOX_HEADS), dt)

    sbk_p, sbv_p, sbk_s, sbv_s = [], [], [], []
    gs_p, gs_s, gc_p, gc_s = [], [], [], []
    fk_p, fv_p, ff_p, fk_s, fv_s, ff_s = [], [], [], [], [], []

    u_bf = peer_u.astype(BF16)
    vt_bf = jnp.swapaxes(peer_v, 1, 2).astype(BF16)

    for layer in range(depth):
        if layer % 2 == 0:
            e = layer // 2
            w = (norm_mix[layer], w_in_even[e], w_out_even[e], gdn_conv_w[e], gdn_a_log[e], gdn_dt_bias[e], gdn_norm[e])
            hp, kp, vp, sp, bp = even_mixer(hp, *w, empty_sb, empty_sb, zero_s, zero_buf, t_p)
            hs, ks_, vs_, ss, bs = even_mixer(hs, *w, cache_sb_k[e], cache_sb_v[e], state_gdn[e], state_gdn_conv[e], dec_t)
            sbk_p.append(kp); sbv_p.append(vp); sbk_s.append(ks_); sbv_s.append(vs_)
            gs_p.append(sp); gs_s.append(ss); gc_p.append(bp); gc_s.append(bs)
        else:
            o = layer // 2
            w = (norm_mix[layer], w_in_odd[o], b_forget[o], w_out_odd[o])
            hp, kp, vp, fp = odd_mixer(hp, *w, empty_fox, empty_fox, empty_logf, t_p)
            hs, ks_, vs_, fs = odd_mixer(hs, *w, cache_fox_k[o], cache_fox_v[o], cache_fox_logf[o], dec_t)
            fk_p.append(kp); fv_p.append(vp); ff_p.append(fp)
            fk_s.append(ks_); fv_s.append(vs_); ff_s.append(fs)
        last = layer == depth - 1
        pw = (norm_ffn[layer], peer_wq[layer], peer_k1[layer], peer_k2[layer], u_bf, vt_bf, layer,
              norm_final if last else None)
        n_s = dec_b * dec_t
        if bsz == 1 and t_pad - t_p >= n_s:
            both = lax.dynamic_update_slice(hp[0], hs.reshape(n_s, d), (t_p, 0))
            both = peer_residual(both, *pw, name="peer")
            hp = both[None]
            hs = both[t_p:t_p + n_s].reshape(dec_b, dec_t, d)
        else:
            hp = peer_residual(hp.reshape(bsz * t_pad, d), *pw, name="peer_prompt").reshape(bsz, t_pad, d)
            hs = peer_residual(hs.reshape(n_s, d), *pw, name="peer_sample").reshape(dec_b, dec_t, d)

    y_prompt = hp[:, N_META:t_p]
    y_sample = hs
    return (y_prompt, y_sample,
            jnp.stack(sbk_p), jnp.stack(sbv_p), jnp.stack(sbk_s), jnp.stack(sbv_s),
            jnp.stack(gs_p), jnp.stack(gs_s), jnp.stack(gc_p), jnp.stack(gc_s),
            jnp.stack(fk_p), jnp.stack(fv_p), jnp.stack(ff_p),
            jnp.stack(fk_s), jnp.stack(fv_s), jnp.stack(ff_s))
```

```python
import functools

import jax
import jax.numpy as jnp
from jax import lax
from jax.experimental import pallas as pl
from jax.experimental.pallas import tpu as pltpu

F32 = jnp.float32
BF16 = jnp.bfloat16

D_MODEL = 1024
N_META = 16
CHUNK = 64
SB_DIM = 64
SB_HEADS = 8
SB_W = SB_HEADS * SB_DIM
GDN_DK = 128
GDN_DV = 128
GDN_HEADS = 4
GDN_QK = GDN_HEADS * GDN_DK
GDN_V = GDN_HEADS * GDN_DV
GDN_CONV = 4
GDN_CONV_DIM = 2 * GDN_QK + GDN_V
FOX_DIM = 64
FOX_HEADS = 16
FOX_W = FOX_HEADS * FOX_DIM
PEER_HEADS = 8
PEER_NKEYS = 128
PEER_TOPK = 16
PEER_QDIM = 256
N_EXPERTS = PEER_NKEYS ** 2
RMS_EPS = 1e-6

LANES = 128
ROW_TILE = 256
ATT_TQ = 512
VMEM_LIMIT = 60 * 1024 * 1024
NEG_BIG = -1e30
_LOG2E = 1.4426950408889634

_NT = (((1,), (1,)), ((), ()))


def _cparams(*sem):
    return pltpu.CompilerParams(dimension_semantics=sem, vmem_limit_bytes=VMEM_LIMIT)


def _round_up(n, m):
    return -(-n // m) * m


def _pad_rows(a, rows, axis=0):
    pad = [(0, 0)] * a.ndim
    pad[axis] = (0, rows - a.shape[axis])
    return jnp.pad(a, pad)


def _norm_proj_body(x_ref, g_ref, w_ref, b_ref, *out_refs, offs, outs, logsig_split, head_stats):
    x = x_ref[...]
    xn = x * lax.rsqrt(jnp.mean(x * x, axis=-1, keepdims=True) + RMS_EPS) * g_ref[...]
    xb = xn.astype(BF16)
    cols = {}
    for (i, kind), o_ref in zip(outs, out_refs):
        if i not in cols:
            y = jnp.dot(xb, w_ref[:, offs[i]:offs[i + 1]], preferred_element_type=F32)
            cols[i] = jax.nn.log_sigmoid(y + b_ref[...]) if i == logsig_split else y
        y = cols[i]
        if kind == "bf16_t":
            o_ref[0] = y.T.astype(BF16)
        else:
            o_ref[...] = y.astype(o_ref.dtype)
    if head_stats is not None:
        iq, ik, dim = head_stats
        n = offs[iq + 1] - offs[iq]
        group = (lax.broadcasted_iota(jnp.int32, (n, LANES), 0) // dim
                 == lax.broadcasted_iota(jnp.int32, (n, LANES), 1)).astype(BF16)
        yq, yk = cols[iq], cols[ik]
        for o_ref, prod in zip(out_refs[len(outs):], (yq * yq, yk * yk, yq * yk)):
            o_ref[...] = jnp.dot(prod.astype(BF16), group, preferred_element_type=F32)


def norm_proj(x, g, w_bf, splits, outs=None, bias=None, rows_out=None, head_stats=None, name="norm_proj"):
    m, d = x.shape
    n = w_bf.shape[1]
    offs = [0]
    for s in splits:
        offs.append(offs[-1] + s)
    tm = min(ROW_TILE, m)
    assert offs[-1] == n and m % tm == 0
    outs = tuple((i, "f32") for i in range(len(splits))) if outs is None else tuple(outs)
    logsig_split = len(splits) - 1 if bias is not None else -1
    if bias is None:
        bias = jnp.zeros((1, splits[-1]), F32)
    out_specs, out_shape = [], []
    for i, kind in outs:
        s = splits[i]
        if kind == "bf16_t":
            out_specs.append(pl.BlockSpec((1, s, tm), lambda r: (r, 0, 0)))
            out_shape.append(jax.ShapeDtypeStruct((m // tm, s, tm), BF16))
        else:
            rows = rows_out if kind == "f32_rows" else m
            assert m - rows < tm
            out_specs.append(pl.BlockSpec((tm, s), lambda r: (r, 0)))
            out_shape.append(jax.ShapeDtypeStruct((rows, s), BF16 if kind == "bf16" else F32))
    if head_stats is not None:
        assert {head_stats[0], head_stats[1]} <= {i for i, _ in outs}
        out_specs += [pl.BlockSpec((tm, LANES), lambda r: (r, 0))] * 3
        out_shape += [jax.ShapeDtypeStruct((m, LANES), F32)] * 3
    return pl.pallas_call(
        functools.partial(_norm_proj_body, offs=tuple(offs), outs=outs, logsig_split=logsig_split,
                          head_stats=head_stats),
        grid=(m // tm,),
        in_specs=[
            pl.BlockSpec((tm, d), lambda i: (i, 0)),
            pl.BlockSpec((1, d), lambda i: (0, 0)),
            pl.BlockSpec((d, n), lambda i: (0, 0)),
            pl.BlockSpec((1, splits[-1]), lambda i: (0, 0)),
        ],
        out_specs=out_specs,
        out_shape=out_shape,
        compiler_params=_cparams("parallel"),
        name=name,
    )(x, g.reshape(1, d), w_bf, bias)


def _out_proj_body(*refs, n_in, offs):
    a_refs = refs[:n_in]
    h_ref, w_ref, o_ref = refs[n_in:]
    acc = h_ref[...]
    for i, a_ref in enumerate(a_refs):
        acc = acc + jnp.dot(a_ref[...].astype(BF16), w_ref[offs[i]:offs[i + 1], :], preferred_element_type=F32)
    o_ref[...] = acc


def out_proj_residual(parts, h, w_bf, name="out_proj"):
    m, d = h.shape
    offs = [0]
    for a in parts:
        offs.append(offs[-1] + a.shape[1])
    tm = min(ROW_TILE, m)
    assert offs[-1] == w_bf.shape[0] and m % tm == 0
    return pl.pallas_call(
        functools.partial(_out_proj_body, n_in=len(parts), offs=tuple(offs)),
        grid=(m // tm,),
        in_specs=[pl.BlockSpec((tm, a.shape[1]), lambda i: (i, 0)) for a in parts] + [
            pl.BlockSpec((tm, d), lambda i: (i, 0)),
            pl.BlockSpec(w_bf.shape, lambda i: (0, 0)),
        ],
        out_specs=pl.BlockSpec((tm, d), lambda i: (i, 0)),
        out_shape=jax.ShapeDtypeStruct((m, d), F32),
        compiler_params=_cparams("parallel"),
        name=name,
    )(*parts, h, w_bf)


def _cumsum_body(x_ref, o_ref, carry_ref):
    @pl.when(pl.program_id(1) == 0)
    def _():
        carry_ref[...] = jnp.zeros_like(carry_ref)

    x = x_ref[0]
    t = x.shape[0]
    tri = (lax.broadcasted_iota(jnp.int32, (t, t), 0) >= lax.broadcasted_iota(jnp.int32, (t, t), 1)).astype(F32)
    c = jnp.dot(tri, x, preferred_element_type=F32, precision=lax.Precision.HIGHEST) + carry_ref[...]
    o_ref[0] = c
    carry_ref[...] = c[t - 1:t, :]


def cumsum_rows(x, name="cumsum_rows"):
    b, l, c = x.shape
    tm = ROW_TILE
    assert l % tm == 0
    return pl.pallas_call(
        _cumsum_body,
        grid=(b, l // tm),
        in_specs=[pl.BlockSpec((1, tm, c), lambda i, j: (i, j, 0))],
        out_specs=pl.BlockSpec((1, tm, c), lambda i, j: (i, j, 0)),
        out_shape=jax.ShapeDtypeStruct((b, l, c), F32),
        scratch_shapes=[pltpu.VMEM((1, c), F32)],
        compiler_params=_cparams("parallel", "arbitrary"),
        name=name,
    )(x)


def _fox_body(nsteps_ref, q_ref, k_ref, vt_ref, fk_ref, fq_ref, o_ref, acc_ref, s_ref, p_ref, *, q_offset, tq, tk):
    bi = pl.program_id(0)
    hp = pl.program_id(1)
    qi = pl.program_id(2)
    q = (q_ref[0].astype(F32) * (FOX_DIM ** -0.5 * _LOG2E)).astype(BF16)
    lane = lax.broadcasted_iota(jnp.int32, (1, LANES), 1)
    q_heads = (jnp.where(lane < FOX_DIM, q, jnp.zeros_like(q)), jnp.where(lane >= FOX_DIM, q, jnp.zeros_like(q)))
    q0 = q_offset + qi * tq
    q_pos = q0 + lax.broadcasted_iota(jnp.int32, (1, tq), 1)
    kb_diag = jnp.minimum((q0 + tq - 1) // tk, k_ref.shape[1] // tk - 1)
    n_causal = max(tq // tk, 1)
    fq_all = fq_ref[0]
    qf_heads = []
    for h in range(2):
        lo = _F_PARTS * (2 * hp + h)
        mine = ((lane >= lo) & (lane < lo + _F_PARTS)) | ((lane >= _F_ONES + lo) & (lane < _F_ONES + lo + _F_PARTS))
        qf_heads.append(jnp.concatenate([q_heads[h], jnp.where(mine, fq_all, jnp.zeros_like(fq_all))], axis=1))
    acc_ref[...] = jnp.zeros_like(acc_ref)

    def scores(kb, buf):
        k0 = pl.multiple_of(kb * tk, tk)
        kf = jnp.concatenate([k_ref[0, pl.ds(k0, tk), :], fk_ref[0, pl.ds(k0, tk), :]], axis=1)
        for h in range(2):
            s_ref[buf, h] = lax.dot_general(kf, qf_heads[h], _NT, preferred_element_type=F32)

    def accumulate(kb, alphas, buf):
        vt = vt_ref[0, kb]
        for h in range(2):
            acc_ref[h] = alphas[h] * acc_ref[h] + jnp.dot(vt, p_ref[buf, h], preferred_element_type=F32)

    def softmax(kb, carry, causal, buf):
        if causal:
            mask = (kb * tk + lax.broadcasted_iota(jnp.int32, (tk, 1), 0)) <= q_pos
        stats, alphas = [], []
        for h in range(2):
            m_prev, l_prev = carry[2 * h], carry[2 * h + 1]
            s = s_ref[buf, h]
            if causal:
                s = jnp.where(mask, s, NEG_BIG)
            m_new = jnp.maximum(m_prev, jnp.max(s, axis=0, keepdims=True))
            p = jnp.exp2(s - m_new)
            alpha = jnp.exp2(m_prev - m_new)
            p_ref[buf, h] = p.astype(BF16)
            stats += [m_new, alpha * l_prev + jnp.sum(p, axis=0, keepdims=True)]
            alphas.append(alpha)
        return tuple(stats + alphas)

    def trip(j, carry, buf):
        kb = kb_diag - j
        accumulate(kb + 1, carry[4:6], 1 - buf)
        carry = softmax(kb, carry, False, buf)
        scores(jnp.maximum(kb - 1, 0), 1 - buf)
        return carry

    ones = jnp.ones((1, tq), F32)
    init = (jnp.full((1, tq), NEG_BIG, F32), jnp.zeros((1, tq), F32),
            jnp.full((1, tq), NEG_BIG, F32), jnp.zeros((1, tq), F32), ones, ones)
    n_steps = nsteps_ref[bi, hp, qi]
    scores(kb_diag, 0)
    carry = init
    for c in range(n_causal):
        if c > 0:
            accumulate(kb_diag - c + 1, carry[4:6], (c - 1) % 2)
        carry = softmax(kb_diag - c, carry, True, c % 2)
        scores(jnp.maximum(kb_diag - c - 1, 0), (c + 1) % 2)
    first = n_causal % 2
    n_pairs = (n_steps - n_causal) // 2

    def two_trips(i, carry):
        j = n_causal + 2 * i
        return trip(j + 1, trip(j, carry, first), 1 - first)

    carry = lax.fori_loop(0, n_pairs, two_trips, carry)
    j_tail = n_causal + 2 * n_pairs
    fin = lax.cond(j_tail < n_steps, lambda cr: trip(j_tail, cr, first), lambda cr: cr, carry)
    last = n_steps - 1
    accumulate(kb_diag - last, fin[4:6], last % 2)
    row = lax.broadcasted_iota(jnp.int32, (LANES, 1), 0)
    ot = jnp.where(row < FOX_DIM, acc_ref[0] / fin[1], acc_ref[1] / fin[3])
    o_ref[0] = ot.T


_UNDERFLOW = 104.0


def _fox_block_counts(stats, f_keys, f_q, q_offset, t_valid, tq_all, tq, tk):
    b, tk_all = f_keys.shape[:2]
    nq, nk = pl.cdiv(tq_all, tq), tk_all // tk
    if stats is None:
        every = jnp.minimum((q_offset + jnp.arange(nq) * tq + tq - 1) // tk, nk - 1) + 1
        return jnp.broadcast_to(every[None, None, :], (b, FOX_HEADS // 2, nq)).astype(jnp.int32)
    qq, kk_max, qk = stats
    heads = qq.shape[-1]
    scale = FOX_DIM ** -0.5
    reach = jnp.sqrt(qq * kk_max[:, None, :]) * scale
    bound = _pad_rows(1.02 * reach - qk * scale, tq_all, 1) + f_q
    valid = (jnp.arange(tq_all) < t_valid)[None, :, None]
    bound = jnp.pad(jnp.where(valid, bound, -jnp.inf), ((0, 0), (0, nq * tq - tq_all), (0, 0)),
                    constant_values=-jnp.inf)
    cq = jnp.max(bound.reshape(b, nq, tq, heads), axis=2)
    f_end = f_keys[:, tk - 1::tk, :heads]
    kb_diag = jnp.minimum((q_offset + jnp.arange(nq) * tq + tq - 1) // tk, nk - 1)
    need = (cq[:, :, None, :] - f_end[:, None, :, :]) > -_UNDERFLOW
    need = need & (jnp.arange(nk)[None, None, :, None] <= kb_diag[None, :, None, None])
    first = jnp.min(jnp.where(need, jnp.arange(nk)[None, None, :, None], nk), axis=2)
    first = jnp.min(first.reshape(b, nq, heads // 2, 2), axis=-1)
    most = kb_diag[None, :, None] + 1
    steps = jnp.clip(kb_diag[None, :, None] - first + 1, jnp.minimum(max(tq // tk, 1), most), most)
    return steps.transpose(0, 2, 1).astype(jnp.int32)


_F_PARTS = 3
_F_ONES = _F_PARTS * FOX_HEADS


def _bf16_head(x):
    bits = lax.bitcast_convert_type(x, jnp.uint32) & jnp.uint32(0xFFFF0000)
    return lax.bitcast_convert_type(bits, F32)


def _split_f_operand(f, key_side):
    b, t, heads = f.shape
    f = f.astype(F32) * _LOG2E
    hi = _bf16_head(f)
    mid = _bf16_head(f - hi)
    lo = f - hi - mid
    pieces = jnp.stack([hi, mid, lo], axis=-1).reshape(b, t, heads * _F_PARTS).astype(BF16)
    const = jnp.full((b, t, heads * _F_PARTS), 1.0 if key_side else -1.0, BF16)
    both = [pieces, const] if key_side else [const, pieces]
    return jnp.pad(jnp.concatenate(both, axis=-1), ((0, 0), (0, 0), (0, LANES - 2 * heads * _F_PARTS)))


def fox_attention(q_bf, k_bf, vt_bf, f_keys, f_q, stats, q_offset, t_valid, name="fox_attention"):
    b, tq_all, w = q_bf.shape
    tk_all = k_bf.shape[1]
    tq = min(ATT_TQ, tq_all)
    tk = ROW_TILE
    assert tk_all % tk == 0
    assert (tq % tk == 0 and q_offset % tk == 0) or (tk % tq == 0 and q_offset % tq == 0)
    nsteps = _fox_block_counts(stats, f_keys, f_q, q_offset, t_valid, tq_all, tq, tk)
    fk_aug = _split_f_operand(f_keys[:, :, :FOX_HEADS], key_side=True)
    fq_aug = _split_f_operand(f_q, key_side=False)
    return pl.pallas_call(
        functools.partial(_fox_body, q_offset=q_offset, tq=tq, tk=tk),
        grid_spec=pltpu.PrefetchScalarGridSpec(
            num_scalar_prefetch=1,
            grid=(b, w // LANES, pl.cdiv(tq_all, tq)),
            in_specs=[
                pl.BlockSpec((1, tq, LANES), lambda i, h, j, n: (i, j, h)),
                pl.BlockSpec((1, tk_all, LANES), lambda i, h, j, n: (i, 0, h)),
                pl.BlockSpec((1, tk_all // tk, LANES, tk), lambda i, h, j, n: (i, 0, h, 0)),
                pl.BlockSpec((1, tk_all, LANES), lambda i, h, j, n: (i, 0, 0)),
                pl.BlockSpec((1, tq, LANES), lambda i, h, j, n: (i, j, 0)),
            ],
            out_specs=pl.BlockSpec((1, tq, LANES), lambda i, h, j, n: (i, j, h)),
            scratch_shapes=[pltpu.VMEM((2, LANES, tq), F32), pltpu.VMEM((2, 2, tk, tq), F32),
                            pltpu.VMEM((2, 2, tk, tq), BF16)],
        ),
        out_shape=jax.ShapeDtypeStruct((b, tq_all, w), F32),
        compiler_params=_cparams("parallel", "parallel", "arbitrary"),
        name=name,
    )(nsteps, q_bf, k_bf, vt_bf, fk_aug, fq_aug)


_SB_BLOCKS = 2


def _sb_body(q_ref, k_ref, vt_ref, o_ref, acc_ref, *, q_offset, tq, tk, tq_all):
    qi = pl.program_id(2)
    q = q_ref[0] * jnp.asarray(SB_DIM ** -0.5, BF16)
    lane = lax.broadcasted_iota(jnp.int32, (1, LANES), 1)
    q_heads = (jnp.where(lane < SB_DIM, q, jnp.zeros_like(q)), jnp.where(lane >= SB_DIM, q, jnp.zeros_like(q)))
    q0 = q_offset + qi * tq
    q_pos = q0 + lax.broadcasted_iota(jnp.int32, (1, tq), 1)
    n_kb = jnp.minimum(jnp.maximum(q0 + tq - 2, 0) // tk + 1, k_ref.shape[1] // tk)
    real_query = q_pos < q_offset + tq_all
    upper = (lax.broadcasted_iota(jnp.int32, (tk, tk), 1) > lax.broadcasted_iota(jnp.int32, (tk, tk), 0)).astype(BF16)
    acc_ref[...] = jnp.zeros_like(acc_ref)

    def block_terms(kb, h):
        kbc = jnp.maximum(kb, 0)
        k0 = pl.multiple_of(kbc * tk, tk)
        k = k_ref[0, pl.ds(k0, tk), :]
        edge = jnp.where(kb >= 0, q_pos, -1)
        mask = (k0 + lax.broadcasted_iota(jnp.int32, (tk, 1), 0)) < edge
        z = lax.dot_general(k, q_heads[h], _NT, preferred_element_type=F32)
        sp = jnp.maximum(z, 0.0) + jnp.log(1.0 + jnp.exp(-jnp.abs(z)))
        l = jnp.where(mask, -sp, 0.0)
        l_hi = l.astype(BF16)
        l_lo = (l - l_hi.astype(F32)).astype(BF16)
        later = (jnp.dot(upper, l_hi, preferred_element_type=F32)
                 + jnp.dot(upper, l_lo, preferred_element_type=F32))
        return mask, (z - sp) + later, later[0:1, :] + l[0:1, :], vt_ref[0, kbc]

    def step(carry):
        j = carry[0]
        kb = n_kb - 1 - _SB_BLOCKS * j
        out = []
        for h in range(2):
            r = carry[2 + h]
            terms = [block_terms(kb - u, h) for u in range(_SB_BLOCKS)]
            pv = None
            for mask, log_w, total, vt in terms:
                w = jnp.where(mask, jnp.exp(log_w + r), 0.0)
                part = jnp.dot(vt, w.astype(BF16), preferred_element_type=F32)
                pv = part if pv is None else pv + part
                r = r + total
            acc_ref[h] += pv
            out.append(r)
        live = jnp.where(real_query, jnp.maximum(out[0], out[1]), -jnp.inf)
        return (j + 1, jnp.max(live), out[0], out[1])

    def more(carry):
        return (_SB_BLOCKS * carry[0] < n_kb) & (carry[1] > -_UNDERFLOW)

    lax.while_loop(more, step, (jnp.int32(0), jnp.float32(0.0), jnp.zeros((1, tq), F32), jnp.zeros((1, tq), F32)))
    row = lax.broadcasted_iota(jnp.int32, (LANES, 1), 0)
    o_ref[0] = jnp.where(row < SB_DIM, acc_ref[0], acc_ref[1]).T


def sb_attention(q_bf, k_bf, vt_bf, q_offset, name="sb_attention"):
    b, tq_all, w = q_bf.shape
    tk_all = k_bf.shape[1]
    tq = min(ROW_TILE, tq_all)
    tk = ROW_TILE
    assert tk_all % tk == 0
    return pl.pallas_call(
        functools.partial(_sb_body, q_offset=q_offset, tq=tq, tk=tk, tq_all=tq_all),
        grid=(b, w // LANES, pl.cdiv(tq_all, tq)),
        in_specs=[
            pl.BlockSpec((1, tq, LANES), lambda i, h, j: (i, j, h)),
            pl.BlockSpec((1, tk_all, LANES), lambda i, h, j: (i, 0, h)),
            pl.BlockSpec((1, tk_all // tk, LANES, tk), lambda i, h, j: (i, 0, h, 0)),
        ],
        out_specs=pl.BlockSpec((1, tq, LANES), lambda i, h, j: (i, j, h)),
        out_shape=jax.ShapeDtypeStruct((b, tq_all, w), F32),
        scratch_shapes=[pltpu.VMEM((2, LANES, tq), F32)],
        compiler_params=_cparams("parallel", "parallel", "arbitrary"),
        name=name,
    )(q_bf, k_bf, vt_bf)


_HI = lax.Precision.HIGHEST
_CONV_PAD = 8
_GDN_CPS = 5


def _dot_hi(a, b):
    return jnp.dot(a, b, preferred_element_type=F32, precision=_HI)


def _split_bf16(x):
    hi = x.astype(BF16)
    return hi, (x - hi.astype(F32)).astype(BF16)


def _einsum3(spec, a, b):
    ah, al = a if isinstance(a, tuple) else _split_bf16(a)
    bh, bl = b if isinstance(b, tuple) else _split_bf16(b)
    prod = functools.partial(jnp.einsum, spec, preferred_element_type=F32)
    return prod(ah, bh) + (prod(ah, bl) + prod(al, bh))


def _bmm(a, b):
    return _einsum3('hij,hjk->hik', a, b)


def _bmm_nt(a, b):
    return _einsum3('hik,hjk->hij', a, b)


def _softplus(x):
    return jnp.maximum(x, 0.0) + jnp.log1p(jnp.exp(-jnp.abs(x)))


def _silu(x):
    return x / (1.0 + jnp.exp(-x))


def _unit_lower_inverse(m):
    c_len = m.shape[-1]
    ri = lax.broadcasted_iota(jnp.int32, (c_len, c_len), 0)
    ci = lax.broadcasted_iota(jnp.int32, (c_len, c_len), 1)
    d = jnp.broadcast_to((ri == ci).astype(F32), m.shape)
    s = 1
    while s < c_len:
        join = (ri // (2 * s) == ci // (2 * s)) & (ri % (2 * s) >= s) & (ci % (2 * s) < s)
        c = jnp.where(join, m, 0.0)
        if s == 1:
            d = d - c
        else:
            d_s = _split_bf16(d)
            d = d - _bmm(_bmm(d_s, c), d_s)
        s *= 2
    return d


def _gdn_body(x_ref, z_ref, ab_ref, buf_ref, s0_ref, cw_ref, alog_ref, dt_ref, gn_ref,
              o_ref, sfin_ref, xwin_ref, s_ref, *, t_valid, cps):
    c = pl.program_id(1)
    n_c = pl.num_programs(1)
    hist = GDN_CONV - 1

    @pl.when(c == 0)
    def _():
        xwin_ref[_CONV_PAD - hist:_CONV_PAD, :] = buf_ref[0]
        s_ref[...] = s0_ref[0]

    rows = cps * CHUNK
    xwin_ref[_CONV_PAD:_CONV_PAD + rows, :] = x_ref[0]
    conv = xwin_ref[_CONV_PAD - hist:_CONV_PAD - hist + rows, :] * cw_ref[0:1, :]
    for i in range(1, GDN_CONV):
        conv = conv + xwin_ref[_CONV_PAD - hist + i:_CONV_PAD - hist + i + rows, :] * cw_ref[i:i + 1, :]
    tail = xwin_ref[_CONV_PAD + rows - hist:_CONV_PAD + rows, :]
    xwin_ref[_CONV_PAD - hist:_CONV_PAD, :] = tail
    act = _silu(conv)

    ab = ab_ref[0]
    row_ok = (c * rows + lax.broadcasted_iota(jnp.int32, (rows, 1), 0)) < t_valid
    g_all = jnp.where(row_ok, -jnp.exp(alog_ref[...]) * _softplus(ab + dt_ref[...]), 0.0)
    beta_all = jnp.where(row_ok, 1.0 / (1.0 + jnp.exp(-ab)), 0.0)
    rr = lax.broadcasted_iota(jnp.int32, (rows, rows), 0)
    rc = lax.broadcasted_iota(jnp.int32, (rows, rows), 1)
    tri_chunks = ((rr >= rc) & (rr // CHUNK == rc // CHUNK)).astype(F32)
    gcum_all = _dot_hi(tri_chunks, g_all)
    sel = (lax.broadcasted_iota(jnp.int32, (8, LANES), 0) == lax.broadcasted_iota(jnp.int32, (8, LANES), 1)).astype(F32)
    gcum_rows = lax.dot_general(sel, gcum_all, _NT, preferred_element_type=F32, precision=_HI)
    ri = lax.broadcasted_iota(jnp.int32, (CHUNK, CHUNK), 0)
    ci = lax.broadcasted_iota(jnp.int32, (CHUNK, CHUNK), 1)
    tri = ri >= ci
    strict = ri > ci

    pairs = [(ck, h) for ck in range(cps) for h in range(GDN_HEADS)]
    rows_of = lambda ck: slice(ck * CHUNK, (ck + 1) * CHUNK)
    q4 = jnp.stack([act[rows_of(ck), h * GDN_DK:(h + 1) * GDN_DK] for ck, h in pairs])
    k4 = jnp.stack([act[rows_of(ck), GDN_QK + h * GDN_DK:GDN_QK + (h + 1) * GDN_DK] for ck, h in pairs])
    v4 = jnp.stack([act[rows_of(ck), 2 * GDN_QK + h * GDN_DV:2 * GDN_QK + (h + 1) * GDN_DV] for ck, h in pairs])
    q4 = q4 * lax.rsqrt(jnp.sum(q4 * q4, axis=-1, keepdims=True) + RMS_EPS) * (GDN_DK ** -0.5)
    k4 = k4 * lax.rsqrt(jnp.sum(k4 * k4, axis=-1, keepdims=True) + RMS_EPS)
    beta = jnp.stack([beta_all[rows_of(ck), GDN_HEADS + h:GDN_HEADS + h + 1] for ck, h in pairs])
    gc = jnp.stack([gcum_all[rows_of(ck), h:h + 1] for ck, h in pairs])
    gr = jnp.stack([gcum_rows[h:h + 1, rows_of(ck)] for ck, h in pairs])
    decay = jnp.exp(jnp.where(tri, gc - gr, NEG_BIG))
    kb = k4 * beta
    k4_s = _split_bf16(k4)
    m = jnp.where(strict, _bmm_nt(kb, k4_s) * decay, 0.0)
    tinv = _split_bf16(_unit_lower_inverse(m))
    eg = jnp.exp(gc)
    u = _bmm(tinv, v4 * beta)
    w = _split_bf16(_bmm(tinv, kb * eg))
    attn = _split_bf16(_bmm_nt(q4, k4_s) * decay)
    qe = _split_bf16(q4 * eg)
    g_last = gc[:, CHUNK - 1:CHUNK, :]
    k_dec = _split_bf16(k4 * jnp.exp(g_last - gc))
    s_scale = jnp.exp(g_last)

    s4 = s_ref[...]
    outs = []
    for ck in range(cps):
        sl = slice(ck * GDN_HEADS, (ck + 1) * GDN_HEADS)
        part = lambda pair: (pair[0][sl], pair[1][sl])
        s4_s = _split_bf16(s4)
        v_new_s = _split_bf16(u[sl] - _bmm(part(w), s4_s))
        outs.append(_bmm(part(qe), s4_s) + _bmm(part(attn), v_new_s))
        s4 = s4 * s_scale[sl] + _einsum3('hck,hcv->hkv', part(k_dec), v_new_s)
    s_ref[...] = s4
    for ck in range(cps):
        o = outs[ck]
        o = o * lax.rsqrt(jnp.mean(o * o, axis=-1, keepdims=True) + RMS_EPS) * gn_ref[...]
        for h in range(GDN_HEADS):
            cols = slice(h * GDN_DV, (h + 1) * GDN_DV)
            o_ref[0, rows_of(ck), cols] = o[h] * _silu(z_ref[0, rows_of(ck), cols])

    @pl.when(c == n_c - 1)
    def _():
        sfin_ref[0] = s_ref[...]


def gdn_heads(qkv_pre, z, ab, conv_buf, s0, conv_w, a_log, dt_bias, gnorm, t_valid, name="gdn"):
    b, t, cd = qkv_pre.shape
    cps = max(c for c in range(1, _GDN_CPS + 1) if (t // CHUNK) % c == 0)
    rows = cps * CHUNK
    assert t % rows == 0
    n_c = t // rows
    pad_l = lambda a: jnp.pad(a.astype(F32), (0, LANES - a.shape[0])).reshape(1, LANES)
    const = lambda *shape: pl.BlockSpec(shape, lambda i, j: (0,) * len(shape))
    return pl.pallas_call(
        functools.partial(_gdn_body, t_valid=t_valid, cps=cps),
        grid=(b, n_c),
        in_specs=[
            pl.BlockSpec((1, rows, cd), lambda i, j: (i, j, 0)),
            pl.BlockSpec((1, rows, GDN_V), lambda i, j: (i, j, 0)),
            pl.BlockSpec((1, rows, LANES), lambda i, j: (i, j, 0)),
            pl.BlockSpec((1, GDN_CONV - 1, cd), lambda i, j: (i, 0, 0)),
            pl.BlockSpec((1, GDN_HEADS, GDN_DK, GDN_DV), lambda i, j: (i, 0, 0, 0)),
            const(GDN_CONV, cd), const(1, LANES), const(1, LANES), const(1, GDN_DV),
        ],
        out_specs=[
            pl.BlockSpec((1, rows, GDN_V), lambda i, j: (i, j, 0)),
            pl.BlockSpec((1, GDN_HEADS, GDN_DK, GDN_DV), lambda i, j: (i, 0, 0, 0)),
        ],
        out_shape=[jax.ShapeDtypeStruct((b, t, GDN_V), F32),
                   jax.ShapeDtypeStruct((b, GDN_HEADS, GDN_DK, GDN_DV), F32)],
        scratch_shapes=[pltpu.VMEM((_CONV_PAD + rows, cd), F32), pltpu.VMEM((GDN_HEADS, GDN_DK, GDN_DV), F32)],
        compiler_params=_cparams("parallel", "arbitrary"),
        name=name,
    )(qkv_pre, z, ab, conv_buf.astype(F32), s0.astype(F32), conv_w.astype(F32), pad_l(a_log), pad_l(dt_bias),
      gnorm.astype(F32).reshape(1, GDN_DV))


def even_mixer(h, g, w_in, w_out, conv_w, a_log, dt_bias, gnorm, sb_k_past, sb_v_past, gdn_s0, conv_buf, t_valid):
    b, t, d = h.shape
    p = sb_k_past.shape[1]
    rows = b * t
    o0 = 3 * SB_W
    w_ab = jnp.pad(w_in[:, o0 + GDN_CONV_DIM + GDN_V:], ((0, 0), (0, LANES - 2 * GDN_HEADS)))
    w_bf = jnp.concatenate([w_in[:, :o0 + GDN_CONV_DIM + GDN_V], w_ab], axis=1).astype(BF16)
    splits = (SB_W, SB_W, SB_W, GDN_CONV_DIM, GDN_V, LANES)
    tk = ROW_TILE
    if p == 0 and b == 1 and t % tk == 0:
        q_bf, k, v, k_bf, vt_bf, qkv_pre, z, ab = norm_proj(
            h.reshape(rows, d), g, w_bf, splits, rows_out=t_valid, name="even_in_proj",
            outs=((0, "bf16"), (1, "f32_rows"), (2, "f32_rows"), (1, "bf16"), (2, "bf16_t"), (3, "f32"), (4, "f32"), (5, "f32")))
        o_sb = sb_attention(q_bf[None], k_bf[None], vt_bf[None], 0)
        k = k[None]
        v = v[None]
    else:
        q, k, v, qkv_pre, z, ab = norm_proj(h.reshape(rows, d), g, w_bf, splits, name="even_in_proj")
        q = q.reshape(b, t, SB_W)
        k = k.reshape(b, t, SB_W)
        v = v.reshape(b, t, SB_W)
        tq_pad = _round_up(t, LANES)
        tk_pad = _round_up(p + t, tk)
        k_all = _pad_rows(jnp.concatenate([sb_k_past.reshape(b, p, SB_W), k], axis=1), tk_pad, 1)
        v_all = _pad_rows(jnp.concatenate([sb_v_past.reshape(b, p, SB_W), v], axis=1), tk_pad, 1)
        k_bf, vt_bf = _kv_layouts(k_all, v_all, tk)
        q_bf = _pad_rows(q, tq_pad, 1).astype(BF16)
        o_sb = sb_attention(q_bf, k_bf, vt_bf, p)[:, :t]
        k = k[:, :t_valid]
        v = v[:, :t_valid]

    t_c = _round_up(t, CHUNK)
    qkv_pre = qkv_pre.reshape(b, t, GDN_CONV_DIM)
    o_gdn, s_fin = gdn_heads(_pad_rows(qkv_pre, t_c, 1), _pad_rows(z.reshape(b, t, GDN_V), t_c, 1),
                             _pad_rows(ab.reshape(b, t, LANES), t_c, 1), conv_buf, gdn_s0,
                             conv_w, a_log, dt_bias, gnorm, t_valid)
    o_gdn = o_gdn[:, :t]
    hist = GDN_CONV - 1
    assert t_valid >= hist
    xp_tail = qkv_pre[:, t_valid - hist:t_valid]
    h_new = out_proj_residual([o_sb.reshape(rows, SB_W), o_gdn.reshape(rows, GDN_V)], h.reshape(rows, d),
                              w_out.astype(BF16), name="even_out_proj")
    return (h_new.reshape(b, t, d), k.reshape(b, t_valid, SB_HEADS, SB_DIM), v.reshape(b, t_valid, SB_HEADS, SB_DIM),
            s_fin, xp_tail)


PEER_HALF = PEER_QDIM // 2
_NSEL = PEER_TOPK + 1
_SUB = 512
_CAND = tuple((a, b) for a in range(_NSEL) for b in range(_NSEL) if (a + 1) * (b + 1) <= _NSEL)
_NCAND = _round_up(len(_CAND), 8)


_SUBLANES = 8


def _sorting_network(n):
    pairs = []

    def merge(lo, m, r):
        step = 2 * r
        if step < m:
            merge(lo, m, step)
            merge(lo + r, m, step)
            pairs.extend((i, i + r) for i in range(lo + r, lo + m - r, step))
        else:
            pairs.append((lo, lo + r))

    def sort(lo, m):
        if m > 1:
            sort(lo, m // 2)
            sort(lo + m // 2, m // 2)
            merge(lo, m, 1)

    sort(0, n)
    return pairs


def _top_values(x, n, out_ref):
    rows, tn = x.shape
    groups = rows // _SUBLANES
    width = 1 << (groups - 1).bit_length()
    minus_inf = jnp.full((_SUBLANES, tn), -jnp.inf, F32)
    lists = [x[r * _SUBLANES:(r + 1) * _SUBLANES, :] for r in range(groups)] + [minus_inf] * (width - groups)
    for i, j in _sorting_network(width):
        lists[i], lists[j] = jnp.maximum(lists[i], lists[j]), jnp.minimum(lists[i], lists[j])
    lists = lists[:groups]
    sub = lax.broadcasted_iota(jnp.int32, (_SUBLANES, 1), 0)
    for it in range(n):
        head = lists[0]
        m = jnp.max(head, axis=0, keepdims=True)
        out_ref[it:it + 1, :] = m
        still_needed = n - it - 1
        if still_needed == 0:
            break
        first = jnp.min(jnp.where(head == m, sub, _SUBLANES), axis=0, keepdims=True)
        won = sub == first
        for r in range(min(groups, still_needed)):
            below = lists[r + 1] if r + 1 < groups else minus_inf
            lists[r] = jnp.where(won, below, lists[r])


def _gelu_tanh(x):
    return 0.5 * x * (1.0 + jnp.tanh(0.7978845608028654 * (x + 0.044715 * (x * x * x))))


def _peer_body(h_ref, g_ref, wq_ref, k1_ref, k2_ref, u_ref, vt_ref, gf_ref, o_ref,
               xn_ref, q_ref, ns1_ref, s2m_ref, e1_ref, e2_ref, t1_ref, t2_ref, cand_ref, csort_ref, acc_ref,
               *, te, final_norm):
    e = pl.program_id(1)
    n_e = pl.num_programs(1)
    tn = h_ref.shape[0]

    @pl.when(e == 0)
    def _prologue():
        x = h_ref[...]
        xn = x * lax.rsqrt(jnp.mean(x * x, axis=-1, keepdims=True) + RMS_EPS) * g_ref[...]
        xb = xn.astype(BF16)
        xn_ref[...] = xb
        q = jnp.dot(xb, wq_ref[...], preferred_element_type=F32)
        for j in range(2 * PEER_HEADS):
            q_ref[j] = q[:, j * PEER_HALF:(j + 1) * PEER_HALF]
        acc_ref[...] = jnp.zeros_like(acc_ref)
        cand_ref[...] = jnp.full(cand_ref.shape, -jnp.inf, F32)

        def per_head(h, _):
            s1 = lax.dot_general(k1_ref[h], q_ref[2 * h], _NT, preferred_element_type=F32)
            s2 = lax.dot_general(k2_ref[h], q_ref[2 * h + 1], _NT, preferred_element_type=F32)
            _top_values(s1, _NSEL, t1_ref)
            _top_values(s2, _NSEL, t2_ref)
            for r, (a, b) in enumerate(_CAND):
                cand_ref[r:r + 1, :] = t1_ref[a:a + 1, :] + t2_ref[b:b + 1, :]
            _top_values(cand_ref[...], _NSEL, csort_ref)
            thr = 0.5 * (csort_ref[PEER_TOPK - 1:PEER_TOPK, :] + csort_ref[PEER_TOPK:PEER_TOPK + 1, :])
            s_max = t1_ref[0:1, :] + t2_ref[0:1, :]
            cand = cand_ref[...]
            zsum = jnp.sum(jnp.where(cand >= thr, jnp.exp(cand - s_max), 0.0), axis=0, keepdims=True)
            ns1_ref[h] = -s1
            s2m_ref[h] = s2 - thr
            e1_ref[h] = jnp.exp(s1 - t1_ref[0:1, :]) / zsum
            e2_ref[h] = jnp.exp(s2 - t2_ref[0:1, :])
            return 0

        lax.fori_loop(0, PEER_HEADS, per_head, 0)

    xb = xn_ref[...]

    w_tiles = []
    for j in range(te // _SUB):
        r0 = j * _SUB
        a_t = lax.dot_general(u_ref[r0:r0 + _SUB, :], xb, _NT, preferred_element_type=F32)
        n_i1 = _SUB // PEER_NKEYS
        i1s = [e * (te // PEER_NKEYS) + j * n_i1 + r for r in range(n_i1)]
        ns1_rows = [[ns1_ref[h, pl.ds(i1, 1), :] for h in range(PEER_HEADS)] for i1 in i1s]
        e1_rows = [[e1_ref[h, pl.ds(i1, 1), :] for h in range(PEER_HEADS)] for i1 in i1s]
        tiles = [[] for _ in i1s]
        for c0 in range(0, tn, LANES):
            gsums = [None] * n_i1
            for h in range(PEER_HEADS):
                s2m_t = s2m_ref[h, :, c0:c0 + LANES]
                e2_t = e2_ref[h, :, c0:c0 + LANES]
                for r in range(n_i1):
                    term = jnp.where(s2m_t >= ns1_rows[r][h][:, c0:c0 + LANES],
                                     e2_t * e1_rows[r][h][:, c0:c0 + LANES], 0.0)
                    gsums[r] = term if gsums[r] is None else gsums[r] + term
            for r in range(n_i1):
                tiles[r].append(gsums[r])
        gates = jnp.concatenate([jnp.concatenate(t, axis=1) for t in tiles], axis=0)
        w_tiles.append((_gelu_tanh(a_t) * gates).astype(BF16))
    acc_ref[...] += jnp.dot(vt_ref[...], jnp.concatenate(w_tiles, axis=0), preferred_element_type=F32)

    @pl.when(e == n_e - 1)
    def _epilogue():
        y = h_ref[...] + acc_ref[...].T
        if final_norm:
            y = y * lax.rsqrt(jnp.mean(y * y, axis=-1, keepdims=True) + RMS_EPS) * gf_ref[...]
        o_ref[...] = y


PEER_TN = 640


def peer_residual(h, g, wq, k1, k2, u_bf, vt_bf, layer, final_g=None, tn=PEER_TN, te=2048, name="peer"):
    m, d = h.shape
    tn = min(tn, m)
    assert m % tn == 0 and N_EXPERTS % te == 0 and te % _SUB == 0
    final_norm = final_g is not None
    gf = (final_g if final_norm else jnp.ones((d,), F32)).astype(F32).reshape(1, d)
    const = lambda *shape: pl.BlockSpec(shape, lambda i, j: (0,) * len(shape))
    big = lambda: pltpu.VMEM((PEER_HEADS, PEER_NKEYS, tn), F32)
    return pl.pallas_call(
        functools.partial(_peer_body, te=te, final_norm=final_norm),
        grid=(m // tn, N_EXPERTS // te),
        in_specs=[
            pl.BlockSpec((tn, d), lambda i, j: (i, 0)),
            const(1, d),
            const(d, PEER_HEADS * PEER_QDIM),
            const(PEER_HEADS, PEER_NKEYS, PEER_HALF),
            const(PEER_HEADS, PEER_NKEYS, PEER_HALF),
            pl.BlockSpec((None, te, d), lambda i, j: (layer, j, 0)),
            pl.BlockSpec((None, d, te), lambda i, j: (layer, 0, j)),
            const(1, d),
        ],
        out_specs=pl.BlockSpec((tn, d), lambda i, j: (i, 0)),
        out_shape=jax.ShapeDtypeStruct((m, d), F32),
        scratch_shapes=[
            pltpu.VMEM((tn, d), BF16),
            pltpu.VMEM((2 * PEER_HEADS, tn, PEER_HALF), F32),
            big(), big(), big(), big(),
            pltpu.VMEM((_round_up(_NSEL, 8), tn), F32),
            pltpu.VMEM((_round_up(_NSEL, 8), tn), F32),
            pltpu.VMEM((_NCAND, tn), F32),
            pltpu.VMEM((_round_up(_NSEL, 8), tn), F32),
            pltpu.VMEM((d, tn), F32),
        ],
        compiler_params=_cparams("parallel", "arbitrary"),
        name=name,
    )(h, g.astype(F32).reshape(1, d), wq.astype(BF16), k1.astype(F32), k2.astype(F32), u_bf, vt_bf, gf)


def _kv_layouts(k_all, v_all, tk):
    b, t, w = k_all.shape
    vt = v_all.astype(BF16).reshape(b, t // tk, tk, w).transpose(0, 1, 3, 2)
    return k_all.astype(BF16), vt


def odd_mixer(h, g, w_in, b_f, w_out, k_past, v_past, logf_past, t_valid):
    b, t, d = h.shape
    p = k_past.shape[1]
    rows = b * t
    w_f = jnp.pad(w_in[:, 3 * FOX_W:], ((0, 0), (0, LANES - FOX_HEADS)))
    w_bf = jnp.concatenate([w_in[:, :3 * FOX_W], w_f], axis=1).astype(BF16)
    bias = jnp.pad(b_f.astype(F32), (0, LANES - FOX_HEADS)).reshape(1, LANES)
    splits = (FOX_W, FOX_W, FOX_W, LANES)
    tk = ROW_TILE
    if p == 0 and b == 1 and t % tk == 0:
        q_bf, k, v, k_bf, vt_bf, logf, qq, kk, qk = norm_proj(
            h.reshape(rows, d), g, w_bf, splits, bias=bias, rows_out=t_valid, name="odd_in_proj",
            outs=((0, "bf16"), (1, "f32_rows"), (2, "f32_rows"), (1, "bf16"), (2, "bf16_t"), (3, "f32")),
            head_stats=(0, 1, FOX_DIM))
        logf = logf[None]
        f_cum = cumsum_rows(logf, name="fox_cumsum")
        stats = (qq[None, :, :FOX_HEADS], jnp.max(kk[:, :FOX_HEADS], axis=0)[None], qk[None, :, :FOX_HEADS])
        o = fox_attention(q_bf[None], k_bf[None], vt_bf[None], f_cum, f_cum[:, :, :FOX_HEADS], stats, 0, t_valid)
        k = k[None]
        v = v[None]
    else:
        q, k, v, logf = norm_proj(h.reshape(rows, d), g, w_bf, splits, bias=bias, name="odd_in_proj")
        q = q.reshape(b, t, FOX_W)
        k = k.reshape(b, t, FOX_W)
        v = v.reshape(b, t, FOX_W)
        logf = logf.reshape(b, t, LANES)
        tq_pad = _round_up(t, LANES)
        tk_pad = _round_up(p + t, tk)
        k_all = _pad_rows(jnp.concatenate([k_past.reshape(b, p, FOX_W), k], axis=1), tk_pad, 1)
        v_all = _pad_rows(jnp.concatenate([v_past.reshape(b, p, FOX_W), v], axis=1), tk_pad, 1)
        logf_past = jnp.pad(logf_past.astype(F32), ((0, 0), (0, 0), (0, LANES - FOX_HEADS)))
        logf_all = _pad_rows(jnp.concatenate([logf_past, logf], axis=1), tk_pad, 1)
        f_cum = cumsum_rows(logf_all, name="fox_cumsum")
        f_q = _pad_rows(f_cum[:, p:p + t, :FOX_HEADS], tq_pad, 1)
        k_bf, vt_bf = _kv_layouts(k_all, v_all, tk)
        q_bf = _pad_rows(q, tq_pad, 1).astype(BF16)
        o = fox_attention(q_bf, k_bf, vt_bf, f_cum, f_q, None, p, t_valid)[:, :t]
        k = k[:, :t_valid]
        v = v[:, :t_valid]
    h_new = out_proj_residual([o.reshape(rows, FOX_W)], h.reshape(rows, d), w_out.astype(BF16), name="odd_out_proj")
    return (h_new.reshape(b, t, d), k.reshape(b, t_valid, FOX_HEADS, FOX_DIM), v.reshape(b, t_valid, FOX_HEADS, FOX_DIM),
            logf[:, :t_valid, :FOX_HEADS])


def kernel(x_prompt, x_sample, cache_sb_k, cache_sb_v, state_gdn, state_gdn_conv, cache_fox_k, cache_fox_v, cache_fox_logf, meta_tokens, norm_mix, norm_ffn, norm_final, w_in_even, w_out_even, gdn_conv_w, gdn_a_log, gdn_dt_bias, gdn_norm, w_in_odd, b_forget, w_out_odd, peer_wq, peer_k1, peer_k2, peer_u, peer_v):
    bsz, seq, d = x_prompt.shape
    dec_b, dec_t, _ = x_sample.shape
    depth = norm_mix.shape[0]
    dt = x_prompt.dtype
    t_p = N_META + seq
    t_pad = _round_up(t_p, PEER_TN)

    meta = jnp.broadcast_to(meta_tokens.astype(dt)[None], (bsz, N_META, d))
    hp = _pad_rows(jnp.concatenate([meta, x_prompt], axis=1), t_pad, 1)
    hs = x_sample

    empty_sb = jnp.zeros((bsz, 0, SB_HEADS, SB_DIM), dt)
    zero_s = jnp.zeros((bsz, GDN_HEADS, GDN_DK, GDN_DV), dt)
    zero_buf = jnp.zeros((bsz, GDN_CONV - 1, GDN_CONV_DIM), dt)
    empty_fox = jnp.zeros((bsz, 0, FOX_HEADS, FOX_DIM), dt)
    empty_logf = jnp.zeros((bsz, 0, FOX_HEADS), dt)

    sbk_p, sbv_p, sbk_s, sbv_s = [], [], [], []
    gs_p, gs_s, gc_p, gc_s = [], [], [], []
    fk_p, fv_p, ff_p, fk_s, fv_s, ff_s = [], [], [], [], [], []

    u_bf = peer_u.astype(BF16)
    vt_bf = jnp.swapaxes(peer_v, 1, 2).astype(BF16)

    for layer in range(depth):
        if layer % 2 == 0:
            e = layer // 2
            w = (norm_mix[layer], w_in_even[e], w_out_even[e], gdn_conv_w[e], gdn_a_log[e], gdn_dt_bias[e], gdn_norm[e])
            hp, kp, vp, sp, bp = even_mixer(hp, *w, empty_sb, empty_sb, zero_s, zero_buf, t_p)
            hs, ks_, vs_, ss, bs = even_mixer(hs, *w, cache_sb_k[e], cache_sb_v[e], state_gdn[e], state_gdn_conv[e], dec_t)
            sbk_p.append(kp); sbv_p.append(vp); sbk_s.append(ks_); sbv_s.append(vs_)
            gs_p.append(sp); gs_s.append(ss); gc_p.append(bp); gc_s.append(bs)
        else:
            o = layer // 2
            w = (norm_mix[layer], w_in_odd[o], b_forget[o], w_out_odd[o])
            hp, kp, vp, fp = odd_mixer(hp, *w, empty_fox, empty_fox, empty_logf, t_p)
            hs, ks_, vs_, fs = odd_mixer(hs, *w, cache_fox_k[o], cache_fox_v[o], cache_fox_logf[o], dec_t)
            fk_p.append(kp); fv_p.append(vp); ff_p.append(fp)
            fk_s.append(ks_); fv_s.append(vs_); ff_s.append(fs)
        last = layer == depth - 1
        pw = (norm_ffn[layer], peer_wq[layer], peer_k1[layer], peer_k2[layer], u_bf, vt_bf, layer,
              norm_final if last else None)
        n_s = dec_b * dec_t
        if bsz == 1 and t_pad - t_p >= n_s:
            both = lax.dynamic_update_slice(hp[0], hs.reshape(n_s, d), (t_p, 0))
            both = peer_residual(both, *pw, name="peer")
            hp = both[None]
            hs = both[t_p:t_p + n_s].reshape(dec_b, dec_t, d)
        else:
            hp = peer_residual(hp.reshape(bsz * t_pad, d), *pw, name="peer_prompt").reshape(bsz, t_pad, d)
            hs = peer_residual(hs.reshape(n_s, d), *pw, name="peer_sample").reshape(dec_b, dec_t, d)

    y_prompt = hp[:, N_META:t_p]
    y_sample = hs
    return (y_prompt, y_sample,
            jnp.stack(sbk_p), jnp.stack(sbv_p), jnp.stack(sbk_s), jnp.stack(sbv_s),
            jnp.stack(gs_p), jnp.stack(gs_s), jnp.stack(gc_p), jnp.stack(gc_s),
            jnp.stack(fk_p), jnp.stack(fv_p), jnp.stack(ff_p),
            jnp.stack(fk_s), jnp.stack(fv_s), jnp.stack(ff_s))
```

```python
import functools

import jax
import jax.numpy as jnp
from jax import lax
from jax.experimental import pallas as pl
from jax.experimental.pallas import tpu as pltpu

F32 = jnp.float32
BF16 = jnp.bfloat16

D_MODEL = 1024
N_META = 16
CHUNK = 64
SB_DIM = 64
SB_HEADS = 8
SB_W = SB_HEADS * SB_DIM
GDN_DK = 128
GDN_DV = 128
GDN_HEADS = 4
GDN_QK = GDN_HEADS * GDN_DK
GDN_V = GDN_HEADS * GDN_DV
GDN_CONV = 4
GDN_CONV_DIM = 2 * GDN_QK + GDN_V
FOX_DIM = 64
FOX_HEADS = 16
FOX_W = FOX_HEADS * FOX_DIM
PEER_HEADS = 8
PEER_NKEYS = 128
PEER_TOPK = 16
PEER_QDIM = 256
N_EXPERTS = PEER_NKEYS ** 2
RMS_EPS = 1e-6

LANES = 128
ROW_TILE = 256
ATT_TQ = 512
VMEM_LIMIT = 60 * 1024 * 1024
NEG_BIG = -1e30
_LOG2E = 1.4426950408889634

_NT = (((1,), (1,)), ((), ()))


def _cparams(*sem):
    return pltpu.CompilerParams(dimension_semantics=sem, vmem_limit_bytes=VMEM_LIMIT)


def _round_up(n, m):
    return -(-n // m) * m


def _pad_rows(a, rows, axis=0):
    pad = [(0, 0)] * a.ndim
    pad[axis] = (0, rows - a.shape[axis])
    return jnp.pad(a, pad)


def _norm_proj_body(x_ref, g_ref, w_ref, b_ref, *out_refs, offs, outs, logsig_split, head_stats):
    x = x_ref[...]
    xn = x * lax.rsqrt(jnp.mean(x * x, axis=-1, keepdims=True) + RMS_EPS) * g_ref[...]
    xb = xn.astype(BF16)
    cols = {}
    for (i, kind), o_ref in zip(outs, out_refs):
        if i not in cols:
            y = jnp.dot(xb, w_ref[:, offs[i]:offs[i + 1]], preferred_element_type=F32)
            cols[i] = jax.nn.log_sigmoid(y + b_ref[...]) if i == logsig_split else y
        y = cols[i]
        if kind == "bf16_t":
            o_ref[0] = y.T.astype(BF16)
        else:
            o_ref[...] = y.astype(o_ref.dtype)
    if head_stats is not None:
        iq, ik, dim = head_stats
        n = offs[iq + 1] - offs[iq]
        group = (lax.broadcasted_iota(jnp.int32, (n, LANES), 0) // dim
                 == lax.broadcasted_iota(jnp.int32, (n, LANES), 1)).astype(BF16)
        yq, yk = cols[iq], cols[ik]
        for o_ref, prod in zip(out_refs[len(outs):], (yq * yq, yk * yk, yq * yk)):
            o_ref[...] = jnp.dot(prod.astype(BF16), group, preferred_element_type=F32)


def norm_proj(x, g, w_bf, splits, outs=None, bias=None, rows_out=None, head_stats=None, name="norm_proj"):
    m, d = x.shape
    n = w_bf.shape[1]
    offs = [0]
    for s in splits:
        offs.append(offs[-1] + s)
    tm = min(ROW_TILE, m)
    assert offs[-1] == n and m % tm == 0
    outs = tuple((i, "f32") for i in range(len(splits))) if outs is None else tuple(outs)
    logsig_split = len(splits) - 1 if bias is not None else -1
    if bias is None:
        bias = jnp.zeros((1, splits[-1]), F32)
    out_specs, out_shape = [], []
    for i, kind in outs:
        s = splits[i]
        if kind == "bf16_t":
            out_specs.append(pl.BlockSpec((1, s, tm), lambda r: (r, 0, 0)))
            out_shape.append(jax.ShapeDtypeStruct((m // tm, s, tm), BF16))
        else:
            rows = rows_out if kind == "f32_rows" else m
            assert m - rows < tm
            out_specs.append(pl.BlockSpec((tm, s), lambda r: (r, 0)))
            out_shape.append(jax.ShapeDtypeStruct((rows, s), BF16 if kind == "bf16" else F32))
    if head_stats is not None:
        assert {head_stats[0], head_stats[1]} <= {i for i, _ in outs}
        out_specs += [pl.BlockSpec((tm, LANES), lambda r: (r, 0))] * 3
        out_shape += [jax.ShapeDtypeStruct((m, LANES), F32)] * 3
    return pl.pallas_call(
        functools.partial(_norm_proj_body, offs=tuple(offs), outs=outs, logsig_split=logsig_split,
                          head_stats=head_stats),
        grid=(m // tm,),
        in_specs=[
            pl.BlockSpec((tm, d), lambda i: (i, 0)),
            pl.BlockSpec((1, d), lambda i: (0, 0)),
            pl.BlockSpec((d, n), lambda i: (0, 0)),
            pl.BlockSpec((1, splits[-1]), lambda i: (0, 0)),
        ],
        out_specs=out_specs,
        out_shape=out_shape,
        compiler_params=_cparams("parallel"),
        name=name,
    )(x, g.reshape(1, d), w_bf, bias)


def _out_proj_body(*refs, n_in, offs):
    a_refs = refs[:n_in]
    h_ref, w_ref, o_ref = refs[n_in:]
    acc = h_ref[...]
    for i, a_ref in enumerate(a_refs):
        acc = acc + jnp.dot(a_ref[...].astype(BF16), w_ref[offs[i]:offs[i + 1], :], preferred_element_type=F32)
    o_ref[...] = acc


def out_proj_residual(parts, h, w_bf, name="out_proj"):
    m, d = h.shape
    offs = [0]
    for a in parts:
        offs.append(offs[-1] + a.shape[1])
    tm = min(ROW_TILE, m)
    assert offs[-1] == w_bf.shape[0] and m % tm == 0
    return pl.pallas_call(
        functools.partial(_out_proj_body, n_in=len(parts), offs=tuple(offs)),
        grid=(m // tm,),
        in_specs=[pl.BlockSpec((tm, a.shape[1]), lambda i: (i, 0)) for a in parts] + [
            pl.BlockSpec((tm, d), lambda i: (i, 0)),
            pl.BlockSpec(w_bf.shape, lambda i: (0, 0)),
        ],
        out_specs=pl.BlockSpec((tm, d), lambda i: (i, 0)),
        out_shape=jax.ShapeDtypeStruct((m, d), F32),
        compiler_params=_cparams("parallel"),
        name=name,
    )(*parts, h, w_bf)


def _cumsum_body(x_ref, o_ref, carry_ref):
    @pl.when(pl.program_id(1) == 0)
    def _():
        carry_ref[...] = jnp.zeros_like(carry_ref)

    x = x_ref[0]
    t = x.shape[0]
    tri = (lax.broadcasted_iota(jnp.int32, (t, t), 0) >= lax.broadcasted_iota(jnp.int32, (t, t), 1)).astype(F32)
    c = jnp.dot(tri, x, preferred_element_type=F32, precision=lax.Precision.HIGHEST) + carry_ref[...]
    o_ref[0] = c
    carry_ref[...] = c[t - 1:t, :]


def cumsum_rows(x, name="cumsum_rows"):
    b, l, c = x.shape
    tm = ROW_TILE
    assert l % tm == 0
    return pl.pallas_call(
        _cumsum_body,
        grid=(b, l // tm),
        in_specs=[pl.BlockSpec((1, tm, c), lambda i, j: (i, j, 0))],
        out_specs=pl.BlockSpec((1, tm, c), lambda i, j: (i, j, 0)),
        out_shape=jax.ShapeDtypeStruct((b, l, c), F32),
        scratch_shapes=[pltpu.VMEM((1, c), F32)],
        compiler_params=_cparams("parallel", "arbitrary"),
        name=name,
    )(x)


def _fox_body(nsteps_ref, q_ref, k_ref, vt_ref, fk_ref, fq_ref, o_ref, acc_ref, s_ref, p_ref, *, q_offset, tq, tk):
    bi = pl.program_id(0)
    hp = pl.program_id(1)
    qi = pl.program_id(2)
    q = (q_ref[0].astype(F32) * (FOX_DIM ** -0.5 * _LOG2E)).astype(BF16)
    lane = lax.broadcasted_iota(jnp.int32, (1, LANES), 1)
    q_heads = (jnp.where(lane < FOX_DIM, q, jnp.zeros_like(q)), jnp.where(lane >= FOX_DIM, q, jnp.zeros_like(q)))
    q0 = q_offset + qi * tq
    q_pos = q0 + lax.broadcasted_iota(jnp.int32, (1, tq), 1)
    kb_diag = jnp.minimum((q0 + tq - 1) // tk, k_ref.shape[1] // tk - 1)
    n_causal = max(tq // tk, 1)
    fq_all = fq_ref[0]
    qf_heads = []
    for h in range(2):
        lo = _F_PARTS * (2 * hp + h)
        mine = ((lane >= lo) & (lane < lo + _F_PARTS)) | ((lane >= _F_ONES + lo) & (lane < _F_ONES + lo + _F_PARTS))
        qf_heads.append(jnp.concatenate([q_heads[h], jnp.where(mine, fq_all, jnp.zeros_like(fq_all))], axis=1))
    acc_ref[...] = jnp.zeros_like(acc_ref)

    def scores(kb, buf):
        k0 = pl.multiple_of(kb * tk, tk)
        kf = jnp.concatenate([k_ref[0, pl.ds(k0, tk), :], fk_ref[0, pl.ds(k0, tk), :]], axis=1)
        for h in range(2):
            s_ref[buf, h] = lax.dot_general(kf, qf_heads[h], _NT, preferred_element_type=F32)

    def accumulate(kb, alphas, buf):
        vt = vt_ref[0, kb]
        for h in range(2):
            acc_ref[h] = alphas[h] * acc_ref[h] + jnp.dot(vt, p_ref[buf, h], preferred_element_type=F32)

    def softmax(kb, carry, causal, buf):
        if causal:
            mask = (kb * tk + lax.broadcasted_iota(jnp.int32, (tk, 1), 0)) <= q_pos
        stats, alphas = [], []
        for h in range(2):
            m_prev, l_prev = carry[2 * h], carry[2 * h + 1]
            s = s_ref[buf, h]
            if causal:
                s = jnp.where(mask, s, NEG_BIG)
            m_new = jnp.maximum(m_prev, jnp.max(s, axis=0, keepdims=True))
            p = jnp.exp2(s - m_new)
            alpha = jnp.exp2(m_prev - m_new)
            p_ref[buf, h] = p.astype(BF16)
            stats += [m_new, alpha * l_prev + jnp.sum(p, axis=0, keepdims=True)]
            alphas.append(alpha)
        return tuple(stats + alphas)

    def trip(j, carry, buf):
        kb = kb_diag - j
        accumulate(kb + 1, carry[4:6], 1 - buf)
        carry = softmax(kb, carry, False, buf)
        scores(jnp.maximum(kb - 1, 0), 1 - buf)
        return carry

    ones = jnp.ones((1, tq), F32)
    init = (jnp.full((1, tq), NEG_BIG, F32), jnp.zeros((1, tq), F32),
            jnp.full((1, tq), NEG_BIG, F32), jnp.zeros((1, tq), F32), ones, ones)
    n_steps = nsteps_ref[bi, hp, qi]
    scores(kb_diag, 0)
    carry = init
    for c in range(n_causal):
        if c > 0:
            accumulate(kb_diag - c + 1, carry[4:6], (c - 1) % 2)
        carry = softmax(kb_diag - c, carry, True, c % 2)
        scores(jnp.maximum(kb_diag - c - 1, 0), (c + 1) % 2)
    first = n_causal % 2
    n_pairs = (n_steps - n_causal) // 2

    def two_trips(i, carry):
        j = n_causal + 2 * i
        return trip(j + 1, trip(j, carry, first), 1 - first)

    carry = lax.fori_loop(0, n_pairs, two_trips, carry)
    j_tail = n_causal + 2 * n_pairs
    fin = lax.cond(j_tail < n_steps, lambda cr: trip(j_tail, cr, first), lambda cr: cr, carry)
    last = n_steps - 1
    accumulate(kb_diag - last, fin[4:6], last % 2)
    row = lax.broadcasted_iota(jnp.int32, (LANES, 1), 0)
    ot = jnp.where(row < FOX_DIM, acc_ref[0] / fin[1], acc_ref[1] / fin[3])
    o_ref[0] = ot.T.astype(o_ref.dtype)


_UNDERFLOW = 104.0


def _fox_block_counts(stats, f_keys, f_q, q_offset, t_valid, tq_all, tq, tk):
    b, tk_all = f_keys.shape[:2]
    nq, nk = pl.cdiv(tq_all, tq), tk_all // tk
    if stats is None:
        every = jnp.minimum((q_offset + jnp.arange(nq) * tq + tq - 1) // tk, nk - 1) + 1
        return jnp.broadcast_to(every[None, None, :], (b, FOX_HEADS // 2, nq)).astype(jnp.int32)
    qq, kk_max, qk = stats
    heads = qq.shape[-1]
    scale = FOX_DIM ** -0.5
    reach = jnp.sqrt(qq * kk_max[:, None, :]) * scale
    bound = _pad_rows(1.02 * reach - qk * scale, tq_all, 1) + f_q
    valid = (jnp.arange(tq_all) < t_valid)[None, :, None]
    bound = jnp.pad(jnp.where(valid, bound, -jnp.inf), ((0, 0), (0, nq * tq - tq_all), (0, 0)),
                    constant_values=-jnp.inf)
    cq = jnp.max(bound.reshape(b, nq, tq, heads), axis=2)
    f_end = f_keys[:, tk - 1::tk, :heads]
    kb_diag = jnp.minimum((q_offset + jnp.arange(nq) * tq + tq - 1) // tk, nk - 1)
    need = (cq[:, :, None, :] - f_end[:, None, :, :]) > -_UNDERFLOW
    need = need & (jnp.arange(nk)[None, None, :, None] <= kb_diag[None, :, None, None])
    first = jnp.min(jnp.where(need, jnp.arange(nk)[None, None, :, None], nk), axis=2)
    first = jnp.min(first.reshape(b, nq, heads // 2, 2), axis=-1)
    most = kb_diag[None, :, None] + 1
    steps = jnp.clip(kb_diag[None, :, None] - first + 1, jnp.minimum(max(tq // tk, 1), most), most)
    return steps.transpose(0, 2, 1).astype(jnp.int32)


_F_PARTS = 3
_F_ONES = _F_PARTS * FOX_HEADS


def _bf16_head(x):
    bits = lax.bitcast_convert_type(x, jnp.uint32) & jnp.uint32(0xFFFF0000)
    return lax.bitcast_convert_type(bits, F32)


def _split_f_operand(f, key_side):
    b, t, heads = f.shape
    f = f.astype(F32) * _LOG2E
    hi = _bf16_head(f)
    mid = _bf16_head(f - hi)
    lo = f - hi - mid
    pieces = jnp.stack([hi, mid, lo], axis=-1).reshape(b, t, heads * _F_PARTS).astype(BF16)
    const = jnp.full((b, t, heads * _F_PARTS), 1.0 if key_side else -1.0, BF16)
    both = [pieces, const] if key_side else [const, pieces]
    return jnp.pad(jnp.concatenate(both, axis=-1), ((0, 0), (0, 0), (0, LANES - 2 * heads * _F_PARTS)))


def fox_attention(q_bf, k_bf, vt_bf, f_keys, f_q, stats, q_offset, t_valid, name="fox_attention"):
    b, tq_all, w = q_bf.shape
    tk_all = k_bf.shape[1]
    tq = min(ATT_TQ, tq_all)
    tk = ROW_TILE
    assert tk_all % tk == 0
    assert (tq % tk == 0 and q_offset % tk == 0) or (tk % tq == 0 and q_offset % tq == 0)
    nsteps = _fox_block_counts(stats, f_keys, f_q, q_offset, t_valid, tq_all, tq, tk)
    fk_aug = _split_f_operand(f_keys[:, :, :FOX_HEADS], key_side=True)
    fq_aug = _split_f_operand(f_q, key_side=False)
    return pl.pallas_call(
        functools.partial(_fox_body, q_offset=q_offset, tq=tq, tk=tk),
        grid_spec=pltpu.PrefetchScalarGridSpec(
            num_scalar_prefetch=1,
            grid=(b, w // LANES, pl.cdiv(tq_all, tq)),
            in_specs=[
                pl.BlockSpec((1, tq, LANES), lambda i, h, j, n: (i, j, h)),
                pl.BlockSpec((1, tk_all, LANES), lambda i, h, j, n: (i, 0, h)),
                pl.BlockSpec((1, tk_all // tk, LANES, tk), lambda i, h, j, n: (i, 0, h, 0)),
                pl.BlockSpec((1, tk_all, LANES), lambda i, h, j, n: (i, 0, 0)),
                pl.BlockSpec((1, tq, LANES), lambda i, h, j, n: (i, j, 0)),
            ],
            out_specs=pl.BlockSpec((1, tq, LANES), lambda i, h, j, n: (i, j, h)),
            scratch_shapes=[pltpu.VMEM((2, LANES, tq), F32), pltpu.VMEM((2, 2, tk, tq), F32),
                            pltpu.VMEM((2, 2, tk, tq), BF16)],
        ),
        out_shape=jax.ShapeDtypeStruct((b, tq_all, w), BF16),
        compiler_params=_cparams("parallel", "parallel", "arbitrary"),
        name=name,
    )(nsteps, q_bf, k_bf, vt_bf, fk_aug, fq_aug)


_SB_BLOCKS = 2


def _sb_body(q_ref, k_ref, vt_ref, o_ref, acc_ref, *, q_offset, tq, tk, tq_all):
    qi = pl.program_id(2)
    q = q_ref[0] * jnp.asarray(SB_DIM ** -0.5, BF16)
    lane = lax.broadcasted_iota(jnp.int32, (1, LANES), 1)
    q_heads = (jnp.where(lane < SB_DIM, q, jnp.zeros_like(q)), jnp.where(lane >= SB_DIM, q, jnp.zeros_like(q)))
    q0 = q_offset + qi * tq
    q_pos = q0 + lax.broadcasted_iota(jnp.int32, (1, tq), 1)
    n_kb = jnp.minimum(jnp.maximum(q0 + tq - 2, 0) // tk + 1, k_ref.shape[1] // tk)
    real_query = q_pos < q_offset + tq_all
    upper = (lax.broadcasted_iota(jnp.int32, (tk, tk), 1) > lax.broadcasted_iota(jnp.int32, (tk, tk), 0)).astype(BF16)
    acc_ref[...] = jnp.zeros_like(acc_ref)

    def block_terms(kb, h):
        kbc = jnp.maximum(kb, 0)
        k0 = pl.multiple_of(kbc * tk, tk)
        k = k_ref[0, pl.ds(k0, tk), :]
        edge = jnp.where(kb >= 0, q_pos, -1)
        mask = (k0 + lax.broadcasted_iota(jnp.int32, (tk, 1), 0)) < edge
        z = lax.dot_general(k, q_heads[h], _NT, preferred_element_type=F32)
        sp = jnp.maximum(z, 0.0) + jnp.log(1.0 + jnp.exp(-jnp.abs(z)))
        l = jnp.where(mask, -sp, 0.0)
        l_hi = l.astype(BF16)
        l_lo = (l - l_hi.astype(F32)).astype(BF16)
        later = (jnp.dot(upper, l_hi, preferred_element_type=F32)
                 + jnp.dot(upper, l_lo, preferred_element_type=F32))
        return mask, (z - sp) + later, later[0:1, :] + l[0:1, :], vt_ref[0, kbc]

    def step(carry):
        j = carry[0]
        kb = n_kb - 1 - _SB_BLOCKS * j
        out = []
        for h in range(2):
            r = carry[2 + h]
            terms = [block_terms(kb - u, h) for u in range(_SB_BLOCKS)]
            pv = None
            for mask, log_w, total, vt in terms:
                w = jnp.where(mask, jnp.exp(log_w + r), 0.0)
                part = jnp.dot(vt, w.astype(BF16), preferred_element_type=F32)
                pv = part if pv is None else pv + part
                r = r + total
            acc_ref[h] += pv
            out.append(r)
        live = jnp.where(real_query, jnp.maximum(out[0], out[1]), -jnp.inf)
        return (j + 1, jnp.max(live), out[0], out[1])

    def more(carry):
        return (_SB_BLOCKS * carry[0] < n_kb) & (carry[1] > -_UNDERFLOW)

    lax.while_loop(more, step, (jnp.int32(0), jnp.float32(0.0), jnp.zeros((1, tq), F32), jnp.zeros((1, tq), F32)))
    row = lax.broadcasted_iota(jnp.int32, (LANES, 1), 0)
    o_ref[0] = jnp.where(row < SB_DIM, acc_ref[0], acc_ref[1]).T.astype(o_ref.dtype)


def sb_attention(q_bf, k_bf, vt_bf, q_offset, name="sb_attention"):
    b, tq_all, w = q_bf.shape
    tk_all = k_bf.shape[1]
    tq = min(ROW_TILE, tq_all)
    tk = ROW_TILE
    assert tk_all % tk == 0
    return pl.pallas_call(
        functools.partial(_sb_body, q_offset=q_offset, tq=tq, tk=tk, tq_all=tq_all),
        grid=(b, w // LANES, pl.cdiv(tq_all, tq)),
        in_specs=[
            pl.BlockSpec((1, tq, LANES), lambda i, h, j: (i, j, h)),
            pl.BlockSpec((1, tk_all, LANES), lambda i, h, j: (i, 0, h)),
            pl.BlockSpec((1, tk_all // tk, LANES, tk), lambda i, h, j: (i, 0, h, 0)),
        ],
        out_specs=pl.BlockSpec((1, tq, LANES), lambda i, h, j: (i, j, h)),
        out_shape=jax.ShapeDtypeStruct((b, tq_all, w), BF16),
        scratch_shapes=[pltpu.VMEM((2, LANES, tq), F32)],
        compiler_params=_cparams("parallel", "parallel", "arbitrary"),
        name=name,
    )(q_bf, k_bf, vt_bf)


_HI = lax.Precision.HIGHEST
_CONV_PAD = 8
_GDN_CPS = 5


def _dot_hi(a, b):
    return jnp.dot(a, b, preferred_element_type=F32, precision=_HI)


def _split_bf16(x):
    hi = x.astype(BF16)
    return hi, (x - hi.astype(F32)).astype(BF16)


def _einsum3(spec, a, b):
    ah, al = a if isinstance(a, tuple) else _split_bf16(a)
    bh, bl = b if isinstance(b, tuple) else _split_bf16(b)
    prod = functools.partial(jnp.einsum, spec, preferred_element_type=F32)
    return prod(ah, bh) + (prod(ah, bl) + prod(al, bh))


def _bmm(a, b):
    return _einsum3('hij,hjk->hik', a, b)


def _bmm_nt(a, b):
    return _einsum3('hik,hjk->hij', a, b)


def _softplus(x):
    return jnp.maximum(x, 0.0) + jnp.log1p(jnp.exp(-jnp.abs(x)))


def _silu(x):
    return x / (1.0 + jnp.exp(-x))


def _unit_lower_inverse(m):
    c_len = m.shape[-1]
    ri = lax.broadcasted_iota(jnp.int32, (c_len, c_len), 0)
    ci = lax.broadcasted_iota(jnp.int32, (c_len, c_len), 1)
    d = jnp.broadcast_to((ri == ci).astype(F32), m.shape)
    s = 1
    while s < c_len:
        join = (ri // (2 * s) == ci // (2 * s)) & (ri % (2 * s) >= s) & (ci % (2 * s) < s)
        c = jnp.where(join, m, 0.0)
        if s == 1:
            d = d - c
        else:
            d_s = _split_bf16(d)
            d = d - _bmm(_bmm(d_s, c), d_s)
        s *= 2
    return d


def _gdn_body(x_ref, z_ref, ab_ref, buf_ref, s0_ref, cw_ref, alog_ref, dt_ref, gn_ref,
              o_ref, sfin_ref, xwin_ref, s_ref, *, t_valid, cps):
    c = pl.program_id(1)
    n_c = pl.num_programs(1)
    hist = GDN_CONV - 1

    @pl.when(c == 0)
    def _():
        xwin_ref[_CONV_PAD - hist:_CONV_PAD, :] = buf_ref[0]
        s_ref[...] = s0_ref[0]

    rows = cps * CHUNK
    xwin_ref[_CONV_PAD:_CONV_PAD + rows, :] = x_ref[0]
    conv = xwin_ref[_CONV_PAD - hist:_CONV_PAD - hist + rows, :] * cw_ref[0:1, :]
    for i in range(1, GDN_CONV):
        conv = conv + xwin_ref[_CONV_PAD - hist + i:_CONV_PAD - hist + i + rows, :] * cw_ref[i:i + 1, :]
    tail = xwin_ref[_CONV_PAD + rows - hist:_CONV_PAD + rows, :]
    xwin_ref[_CONV_PAD - hist:_CONV_PAD, :] = tail
    act = _silu(conv)

    ab = ab_ref[0]
    row_ok = (c * rows + lax.broadcasted_iota(jnp.int32, (rows, 1), 0)) < t_valid
    g_all = jnp.where(row_ok, -jnp.exp(alog_ref[...]) * _softplus(ab + dt_ref[...]), 0.0)
    beta_all = jnp.where(row_ok, 1.0 / (1.0 + jnp.exp(-ab)), 0.0)
    rr = lax.broadcasted_iota(jnp.int32, (rows, rows), 0)
    rc = lax.broadcasted_iota(jnp.int32, (rows, rows), 1)
    tri_chunks = ((rr >= rc) & (rr // CHUNK == rc // CHUNK)).astype(F32)
    gcum_all = _dot_hi(tri_chunks, g_all)
    sel = (lax.broadcasted_iota(jnp.int32, (8, LANES), 0) == lax.broadcasted_iota(jnp.int32, (8, LANES), 1)).astype(F32)
    gcum_rows = lax.dot_general(sel, gcum_all, _NT, preferred_element_type=F32, precision=_HI)
    ri = lax.broadcasted_iota(jnp.int32, (CHUNK, CHUNK), 0)
    ci = lax.broadcasted_iota(jnp.int32, (CHUNK, CHUNK), 1)
    tri = ri >= ci
    strict = ri > ci

    pairs = [(ck, h) for ck in range(cps) for h in range(GDN_HEADS)]
    rows_of = lambda ck: slice(ck * CHUNK, (ck + 1) * CHUNK)
    q4 = jnp.stack([act[rows_of(ck), h * GDN_DK:(h + 1) * GDN_DK] for ck, h in pairs])
    k4 = jnp.stack([act[rows_of(ck), GDN_QK + h * GDN_DK:GDN_QK + (h + 1) * GDN_DK] for ck, h in pairs])
    v4 = jnp.stack([act[rows_of(ck), 2 * GDN_QK + h * GDN_DV:2 * GDN_QK + (h + 1) * GDN_DV] for ck, h in pairs])
    q4 = q4 * lax.rsqrt(jnp.sum(q4 * q4, axis=-1, keepdims=True) + RMS_EPS) * (GDN_DK ** -0.5)
    k4 = k4 * lax.rsqrt(jnp.sum(k4 * k4, axis=-1, keepdims=True) + RMS_EPS)
    beta = jnp.stack([beta_all[rows_of(ck), GDN_HEADS + h:GDN_HEADS + h + 1] for ck, h in pairs])
    gc = jnp.stack([gcum_all[rows_of(ck), h:h + 1] for ck, h in pairs])
    gr = jnp.stack([gcum_rows[h:h + 1, rows_of(ck)] for ck, h in pairs])
    decay = jnp.exp(jnp.where(tri, gc - gr, NEG_BIG))
    kb = k4 * beta
    k4_s = _split_bf16(k4)
    m = jnp.where(strict, _bmm_nt(kb, k4_s) * decay, 0.0)
    tinv = _split_bf16(_unit_lower_inverse(m))
    eg = jnp.exp(gc)
    u = _bmm(tinv, v4 * beta)
    w = _split_bf16(_bmm(tinv, kb * eg))
    attn = _split_bf16(_bmm_nt(q4, k4_s) * decay)
    qe = _split_bf16(q4 * eg)
    g_last = gc[:, CHUNK - 1:CHUNK, :]
    k_dec = _split_bf16(k4 * jnp.exp(g_last - gc))
    s_scale = jnp.exp(g_last)

    s4 = s_ref[...]
    outs = []
    for ck in range(cps):
        sl = slice(ck * GDN_HEADS, (ck + 1) * GDN_HEADS)
        part = lambda pair: (pair[0][sl], pair[1][sl])
        s4_s = _split_bf16(s4)
        v_new_s = _split_bf16(u[sl] - _bmm(part(w), s4_s))
        outs.append(_bmm(part(qe), s4_s) + _bmm(part(attn), v_new_s))
        s4 = s4 * s_scale[sl] + _einsum3('hck,hcv->hkv', part(k_dec), v_new_s)
    s_ref[...] = s4
    for ck in range(cps):
        o = outs[ck]
        o = o * lax.rsqrt(jnp.mean(o * o, axis=-1, keepdims=True) + RMS_EPS) * gn_ref[...]
        for h in range(GDN_HEADS):
            cols = slice(h * GDN_DV, (h + 1) * GDN_DV)
            o_ref[0, rows_of(ck), cols] = (o[h] * _silu(z_ref[0, rows_of(ck), cols])).astype(o_ref.dtype)

    @pl.when(c == n_c - 1)
    def _():
        sfin_ref[0] = s_ref[...]


def gdn_heads(qkv_pre, z, ab, conv_buf, s0, conv_w, a_log, dt_bias, gnorm, t_valid, name="gdn"):
    b, t, cd = qkv_pre.shape
    cps = max(c for c in range(1, _GDN_CPS + 1) if (t // CHUNK) % c == 0)
    rows = cps * CHUNK
    assert t % rows == 0
    n_c = t // rows
    pad_l = lambda a: jnp.pad(a.astype(F32), (0, LANES - a.shape[0])).reshape(1, LANES)
    const = lambda *shape: pl.BlockSpec(shape, lambda i, j: (0,) * len(shape))
    return pl.pallas_call(
        functools.partial(_gdn_body, t_valid=t_valid, cps=cps),
        grid=(b, n_c),
        in_specs=[
            pl.BlockSpec((1, rows, cd), lambda i, j: (i, j, 0)),
            pl.BlockSpec((1, rows, GDN_V), lambda i, j: (i, j, 0)),
            pl.BlockSpec((1, rows, LANES), lambda i, j: (i, j, 0)),
            pl.BlockSpec((1, GDN_CONV - 1, cd), lambda i, j: (i, 0, 0)),
            pl.BlockSpec((1, GDN_HEADS, GDN_DK, GDN_DV), lambda i, j: (i, 0, 0, 0)),
            const(GDN_CONV, cd), const(1, LANES), const(1, LANES), const(1, GDN_DV),
        ],
        out_specs=[
            pl.BlockSpec((1, rows, GDN_V), lambda i, j: (i, j, 0)),
            pl.BlockSpec((1, GDN_HEADS, GDN_DK, GDN_DV), lambda i, j: (i, 0, 0, 0)),
        ],
        out_shape=[jax.ShapeDtypeStruct((b, t, GDN_V), BF16),
                   jax.ShapeDtypeStruct((b, GDN_HEADS, GDN_DK, GDN_DV), F32)],
        scratch_shapes=[pltpu.VMEM((_CONV_PAD + rows, cd), F32), pltpu.VMEM((GDN_HEADS, GDN_DK, GDN_DV), F32)],
        compiler_params=_cparams("parallel", "arbitrary"),
        name=name,
    )(qkv_pre, z, ab, conv_buf.astype(F32), s0.astype(F32), conv_w.astype(F32), pad_l(a_log), pad_l(dt_bias),
      gnorm.astype(F32).reshape(1, GDN_DV))


def even_mixer(h, g, w_in, w_out, conv_w, a_log, dt_bias, gnorm, sb_k_past, sb_v_past, gdn_s0, conv_buf, t_valid):
    b, t, d = h.shape
    p = sb_k_past.shape[1]
    rows = b * t
    o0 = 3 * SB_W
    w_ab = jnp.pad(w_in[:, o0 + GDN_CONV_DIM + GDN_V:], ((0, 0), (0, LANES - 2 * GDN_HEADS)))
    w_bf = jnp.concatenate([w_in[:, :o0 + GDN_CONV_DIM + GDN_V], w_ab], axis=1).astype(BF16)
    splits = (SB_W, SB_W, SB_W, GDN_CONV_DIM, GDN_V, LANES)
    tk = ROW_TILE
    if p == 0 and b == 1 and t % tk == 0:
        q_bf, k, v, k_bf, vt_bf, qkv_pre, z, ab = norm_proj(
            h.reshape(rows, d), g, w_bf, splits, rows_out=t_valid, name="even_in_proj",
            outs=((0, "bf16"), (1, "f32_rows"), (2, "f32_rows"), (1, "bf16"), (2, "bf16_t"), (3, "f32"), (4, "f32"), (5, "f32")))
        o_sb = sb_attention(q_bf[None], k_bf[None], vt_bf[None], 0)
        k = k[None]
        v = v[None]
    else:
        q, k, v, qkv_pre, z, ab = norm_proj(h.reshape(rows, d), g, w_bf, splits, name="even_in_proj")
        q = q.reshape(b, t, SB_W)
        k = k.reshape(b, t, SB_W)
        v = v.reshape(b, t, SB_W)
        tq_pad = _round_up(t, LANES)
        tk_pad = _round_up(p + t, tk)
        k_all = _pad_rows(jnp.concatenate([sb_k_past.reshape(b, p, SB_W), k], axis=1), tk_pad, 1)
        v_all = _pad_rows(jnp.concatenate([sb_v_past.reshape(b, p, SB_W), v], axis=1), tk_pad, 1)
        k_bf, vt_bf = _kv_layouts(k_all, v_all, tk)
        q_bf = _pad_rows(q, tq_pad, 1).astype(BF16)
        o_sb = sb_attention(q_bf, k_bf, vt_bf, p)[:, :t]
        k = k[:, :t_valid]
        v = v[:, :t_valid]

    t_c = _round_up(t, CHUNK)
    qkv_pre = qkv_pre.reshape(b, t, GDN_CONV_DIM)
    o_gdn, s_fin = gdn_heads(_pad_rows(qkv_pre, t_c, 1), _pad_rows(z.reshape(b, t, GDN_V), t_c, 1),
                             _pad_rows(ab.reshape(b, t, LANES), t_c, 1), conv_buf, gdn_s0,
                             conv_w, a_log, dt_bias, gnorm, t_valid)
    o_gdn = o_gdn[:, :t]
    hist = GDN_CONV - 1
    assert t_valid >= hist
    xp_tail = qkv_pre[:, t_valid - hist:t_valid]
    h_new = out_proj_residual([o_sb.reshape(rows, SB_W), o_gdn.reshape(rows, GDN_V)], h.reshape(rows, d),
                              w_out.astype(BF16), name="even_out_proj")
    return (h_new.reshape(b, t, d), k.reshape(b, t_valid, SB_HEADS, SB_DIM), v.reshape(b, t_valid, SB_HEADS, SB_DIM),
            s_fin, xp_tail)


PEER_HALF = PEER_QDIM // 2
_NSEL = PEER_TOPK + 1
_SUB = 512
_CAND = tuple((a, b) for a in range(_NSEL) for b in range(_NSEL) if (a + 1) * (b + 1) <= _NSEL)
_NCAND = _round_up(len(_CAND), 8)


_SUBLANES = 8


def _sorting_network(n):
    pairs = []

    def merge(lo, m, r):
        step = 2 * r
        if step < m:
            merge(lo, m, step)
            merge(lo + r, m, step)
            pairs.extend((i, i + r) for i in range(lo + r, lo + m - r, step))
        else:
            pairs.append((lo, lo + r))

    def sort(lo, m):
        if m > 1:
            sort(lo, m // 2)
            sort(lo + m // 2, m // 2)
            merge(lo, m, 1)

    sort(0, n)
    return pairs


def _top_values(x, n, out_ref):
    rows, tn = x.shape
    groups = rows // _SUBLANES
    width = 1 << (groups - 1).bit_length()
    minus_inf = jnp.full((_SUBLANES, tn), -jnp.inf, F32)
    lists = [x[r * _SUBLANES:(r + 1) * _SUBLANES, :] for r in range(groups)] + [minus_inf] * (width - groups)
    for i, j in _sorting_network(width):
        lists[i], lists[j] = jnp.maximum(lists[i], lists[j]), jnp.minimum(lists[i], lists[j])
    lists = lists[:groups]
    sub = lax.broadcasted_iota(jnp.int32, (_SUBLANES, 1), 0)
    for it in range(n):
        head = lists[0]
        m = jnp.max(head, axis=0, keepdims=True)
        out_ref[it:it + 1, :] = m
        still_needed = n - it - 1
        if still_needed == 0:
            break
        first = jnp.min(jnp.where(head == m, sub, _SUBLANES), axis=0, keepdims=True)
        won = sub == first
        for r in range(min(groups, still_needed)):
            below = lists[r + 1] if r + 1 < groups else minus_inf
            lists[r] = jnp.where(won, below, lists[r])


def _gelu_tanh(x):
    return 0.5 * x * (1.0 + jnp.tanh(0.7978845608028654 * (x + 0.044715 * (x * x * x))))


def _peer_body(h_ref, g_ref, wq_ref, k1_ref, k2_ref, u_ref, vt_ref, gf_ref, o_ref,
               xn_ref, q_ref, ns1_ref, s2m_ref, e1_ref, e2_ref, t1_ref, t2_ref, cand_ref, csort_ref, acc_ref,
               *, te, final_norm):
    e = pl.program_id(1)
    n_e = pl.num_programs(1)
    tn = h_ref.shape[0]

    @pl.when(e == 0)
    def _prologue():
        x = h_ref[...]
        xn = x * lax.rsqrt(jnp.mean(x * x, axis=-1, keepdims=True) + RMS_EPS) * g_ref[...]
        xb = xn.astype(BF16)
        xn_ref[...] = xb
        q = jnp.dot(xb, wq_ref[...], preferred_element_type=F32)
        for j in range(2 * PEER_HEADS):
            q_ref[j] = q[:, j * PEER_HALF:(j + 1) * PEER_HALF]
        acc_ref[...] = jnp.zeros_like(acc_ref)
        cand_ref[...] = jnp.full(cand_ref.shape, -jnp.inf, F32)

        def per_head(h, _):
            s1 = lax.dot_general(k1_ref[h], q_ref[2 * h], _NT, preferred_element_type=F32)
            s2 = lax.dot_general(k2_ref[h], q_ref[2 * h + 1], _NT, preferred_element_type=F32)
            _top_values(s1, _NSEL, t1_ref)
            _top_values(s2, _NSEL, t2_ref)
            for r, (a, b) in enumerate(_CAND):
                cand_ref[r:r + 1, :] = t1_ref[a:a + 1, :] + t2_ref[b:b + 1, :]
            _top_values(cand_ref[...], _NSEL, csort_ref)
            thr = 0.5 * (csort_ref[PEER_TOPK - 1:PEER_TOPK, :] + csort_ref[PEER_TOPK:PEER_TOPK + 1, :])
            s_max = t1_ref[0:1, :] + t2_ref[0:1, :]
            cand = cand_ref[...]
            zsum = jnp.sum(jnp.where(cand >= thr, jnp.exp(cand - s_max), 0.0), axis=0, keepdims=True)
            ns1_ref[h] = -s1
            s2m_ref[h] = s2 - thr
            e1_ref[h] = jnp.exp(s1 - t1_ref[0:1, :]) / zsum
            e2_ref[h] = jnp.exp(s2 - t2_ref[0:1, :])
            return 0

        lax.fori_loop(0, PEER_HEADS, per_head, 0)

    xb = xn_ref[...]

    w_tiles = []
    for j in range(te // _SUB):
        r0 = j * _SUB
        a_t = lax.dot_general(u_ref[r0:r0 + _SUB, :], xb, _NT, preferred_element_type=F32)
        n_i1 = _SUB // PEER_NKEYS
        i1s = [e * (te // PEER_NKEYS) + j * n_i1 + r for r in range(n_i1)]
        ns1_rows = [[ns1_ref[h, pl.ds(i1, 1), :] for h in range(PEER_HEADS)] for i1 in i1s]
        e1_rows = [[e1_ref[h, pl.ds(i1, 1), :] for h in range(PEER_HEADS)] for i1 in i1s]
        tiles = [[] for _ in i1s]
        for c0 in range(0, tn, LANES):
            gsums = [None] * n_i1
            for h in range(PEER_HEADS):
                s2m_t = s2m_ref[h, :, c0:c0 + LANES]
                e2_t = e2_ref[h, :, c0:c0 + LANES]
                for r in range(n_i1):
                    term = jnp.where(s2m_t >= ns1_rows[r][h][:, c0:c0 + LANES],
                                     e2_t * e1_rows[r][h][:, c0:c0 + LANES], 0.0)
                    gsums[r] = term if gsums[r] is None else gsums[r] + term
            for r in range(n_i1):
                tiles[r].append(gsums[r])
        gates = jnp.concatenate([jnp.concatenate(t, axis=1) for t in tiles], axis=0)
        w_tiles.append((_gelu_tanh(a_t) * gates).astype(BF16))
    acc_ref[...] += jnp.dot(vt_ref[...], jnp.concatenate(w_tiles, axis=0), preferred_element_type=F32)

    @pl.when(e == n_e - 1)
    def _epilogue():
        y = h_ref[...] + acc_ref[...].T
        if final_norm:
            y = y * lax.rsqrt(jnp.mean(y * y, axis=-1, keepdims=True) + RMS_EPS) * gf_ref[...]
        o_ref[...] = y


PEER_TN = 640


def peer_residual(h, g, wq, k1, k2, u_bf, vt_bf, layer, final_g=None, tn=PEER_TN, te=2048, name="peer"):
    m, d = h.shape
    tn = min(tn, m)
    assert m % tn == 0 and N_EXPERTS % te == 0 and te % _SUB == 0
    final_norm = final_g is not None
    gf = (final_g if final_norm else jnp.ones((d,), F32)).astype(F32).reshape(1, d)
    const = lambda *shape: pl.BlockSpec(shape, lambda i, j: (0,) * len(shape))
    big = lambda: pltpu.VMEM((PEER_HEADS, PEER_NKEYS, tn), F32)
    return pl.pallas_call(
        functools.partial(_peer_body, te=te, final_norm=final_norm),
        grid=(m // tn, N_EXPERTS // te),
        in_specs=[
            pl.BlockSpec((tn, d), lambda i, j: (i, 0)),
            const(1, d),
            const(d, PEER_HEADS * PEER_QDIM),
            const(PEER_HEADS, PEER_NKEYS, PEER_HALF),
            const(PEER_HEADS, PEER_NKEYS, PEER_HALF),
            pl.BlockSpec((None, te, d), lambda i, j: (layer, j, 0)),
            pl.BlockSpec((None, d, te), lambda i, j: (layer, 0, j)),
            const(1, d),
        ],
        out_specs=pl.BlockSpec((tn, d), lambda i, j: (i, 0)),
        out_shape=jax.ShapeDtypeStruct((m, d), F32),
        scratch_shapes=[
            pltpu.VMEM((tn, d), BF16),
            pltpu.VMEM((2 * PEER_HEADS, tn, PEER_HALF), F32),
            big(), big(), big(), big(),
            pltpu.VMEM((_round_up(_NSEL, 8), tn), F32),
            pltpu.VMEM((_round_up(_NSEL, 8), tn), F32),
            pltpu.VMEM((_NCAND, tn), F32),
            pltpu.VMEM((_round_up(_NSEL, 8), tn), F32),
            pltpu.VMEM((d, tn), F32),
        ],
        compiler_params=_cparams("parallel", "arbitrary"),
        name=name,
    )(h, g.astype(F32).reshape(1, d), wq.astype(BF16), k1.astype(F32), k2.astype(F32), u_bf, vt_bf, gf)


def _kv_layouts(k_all, v_all, tk):
    b, t, w = k_all.shape
    vt = v_all.astype(BF16).reshape(b, t // tk, tk, w).transpose(0, 1, 3, 2)
    return k_all.astype(BF16), vt


def odd_mixer(h, g, w_in, b_f, w_out, k_past, v_past, logf_past, t_valid):
    b, t, d = h.shape
    p = k_past.shape[1]
    rows = b * t
    w_f = jnp.pad(w_in[:, 3 * FOX_W:], ((0, 0), (0, LANES - FOX_HEADS)))
    w_bf = jnp.concatenate([w_in[:, :3 * FOX_W], w_f], axis=1).astype(BF16)
    bias = jnp.pad(b_f.astype(F32), (0, LANES - FOX_HEADS)).reshape(1, LANES)
    splits = (FOX_W, FOX_W, FOX_W, LANES)
    tk = ROW_TILE
    if p == 0 and b == 1 and t % tk == 0:
        q_bf, k, v, k_bf, vt_bf, logf, qq, kk, qk = norm_proj(
            h.reshape(rows, d), g, w_bf, splits, bias=bias, rows_out=t_valid, name="odd_in_proj",
            outs=((0, "bf16"), (1, "f32_rows"), (2, "f32_rows"), (1, "bf16"), (2, "bf16_t"), (3, "f32")),
            head_stats=(0, 1, FOX_DIM))
        logf = logf[None]
        f_cum = cumsum_rows(logf, name="fox_cumsum")
        stats = (qq[None, :, :FOX_HEADS], jnp.max(kk[:, :FOX_HEADS], axis=0)[None], qk[None, :, :FOX_HEADS])
        o = fox_attention(q_bf[None], k_bf[None], vt_bf[None], f_cum, f_cum[:, :, :FOX_HEADS], stats, 0, t_valid)
        k = k[None]
        v = v[None]
    else:
        q, k, v, logf = norm_proj(h.reshape(rows, d), g, w_bf, splits, bias=bias, name="odd_in_proj")
        q = q.reshape(b, t, FOX_W)
        k = k.reshape(b, t, FOX_W)
        v = v.reshape(b, t, FOX_W)
        logf = logf.reshape(b, t, LANES)
        tq_pad = _round_up(t, LANES)
        tk_pad = _round_up(p + t, tk)
        k_all = _pad_rows(jnp.concatenate([k_past.reshape(b, p, FOX_W), k], axis=1), tk_pad, 1)
        v_all = _pad_rows(jnp.concatenate([v_past.reshape(b, p, FOX_W), v], axis=1), tk_pad, 1)
        logf_past = jnp.pad(logf_past.astype(F32), ((0, 0), (0, 0), (0, LANES - FOX_HEADS)))
        logf_all = _pad_rows(jnp.concatenate([logf_past, logf], axis=1), tk_pad, 1)
        f_cum = cumsum_rows(logf_all, name="fox_cumsum")
        f_q = _pad_rows(f_cum[:, p:p + t, :FOX_HEADS], tq_pad, 1)
        k_bf, vt_bf = _kv_layouts(k_all, v_all, tk)
        q_bf = _pad_rows(q, tq_pad, 1).astype(BF16)
        o = fox_attention(q_bf, k_bf, vt_bf, f_cum, f_q, None, p, t_valid)[:, :t]
        k = k[:, :t_valid]
        v = v[:, :t_valid]
    h_new = out_proj_residual([o.reshape(rows, FOX_W)], h.reshape(rows, d), w_out.astype(BF16), name="odd_out_proj")
    return (h_new.reshape(b, t, d), k.reshape(b, t_valid, FOX_HEADS, FOX_DIM), v.reshape(b, t_valid, FOX_HEADS, FOX_DIM),
            logf[:, :t_valid, :FOX_HEADS])


def kernel(x_prompt, x_sample, cache_sb_k, cache_sb_v, state_gdn, state_gdn_conv, cache_fox_k, cache_fox_v, cache_fox_logf, meta_tokens, norm_mix, norm_ffn, norm_final, w_in_even, w_out_even, gdn_conv_w, gdn_a_log, gdn_dt_bias, gdn_norm, w_in_odd, b_forget, w_out_odd, peer_wq, peer_k1, peer_k2, peer_u, peer_v):
    bsz, seq, d = x_prompt.shape
    dec_b, dec_t, _ = x_sample.shape
    depth = norm_mix.shape[0]
    dt = x_prompt.dtype
    t_p = N_META + seq
    t_pad = _round_up(t_p, PEER_TN)

    meta = jnp.broadcast_to(meta_tokens.astype(dt)[None], (bsz, N_META, d))
    hp = _pad_rows(jnp.concatenate([meta, x_prompt], axis=1), t_pad, 1)
    hs = x_sample

    empty_sb = jnp.zeros((bsz, 0, SB_HEADS, SB_DIM), dt)
    zero_s = jnp.zeros((bsz, GDN_HEADS, GDN_DK, GDN_DV), dt)
    zero_buf = jnp.zeros((bsz, GDN_CONV - 1, GDN_CONV_DIM), dt)
    empty_fox = jnp.zeros((bsz, 0, FOX_HEADS, FOX_DIM), dt)
    empty_logf = jnp.zeros((bsz, 0, FOX_HEADS), dt)

    sbk_p, sbv_p, sbk_s, sbv_s = [], [], [], []
    gs_p, gs_s, gc_p, gc_s = [], [], [], []
    fk_p, fv_p, ff_p, fk_s, fv_s, ff_s = [], [], [], [], [], []

    u_bf = peer_u.astype(BF16)
    vt_bf = jnp.swapaxes(peer_v, 1, 2).astype(BF16)

    for layer in range(depth):
        if layer % 2 == 0:
            e = layer // 2
            w = (norm_mix[layer], w_in_even[e], w_out_even[e], gdn_conv_w[e], gdn_a_log[e], gdn_dt_bias[e], gdn_norm[e])
            hp, kp, vp, sp, bp = even_mixer(hp, *w, empty_sb, empty_sb, zero_s, zero_buf, t_p)
            hs, ks_, vs_, ss, bs = even_mixer(hs, *w, cache_sb_k[e], cache_sb_v[e], state_gdn[e], state_gdn_conv[e], dec_t)
            sbk_p.append(kp); sbv_p.append(vp); sbk_s.append(ks_); sbv_s.append(vs_)
            gs_p.append(sp); gs_s.append(ss); gc_p.append(bp); gc_s.append(bs)
        else:
            o = layer // 2
            w = (norm_mix[layer], w_in_odd[o], b_forget[o], w_out_odd[o])
            hp, kp, vp, fp = odd_mixer(hp, *w, empty_fox, empty_fox, empty_logf, t_p)
            hs, ks_, vs_, fs = odd_mixer(hs, *w, cache_fox_k[o], cache_fox_v[o], cache_fox_logf[o], dec_t)
            fk_p.append(kp); fv_p.append(vp); ff_p.append(fp)
            fk_s.append(ks_); fv_s.append(vs_); ff_s.append(fs)
        last = layer == depth - 1
        pw = (norm_ffn[layer], peer_wq[layer], peer_k1[layer], peer_k2[layer], u_bf, vt_bf, layer,
              norm_final if last else None)
        n_s = dec_b * dec_t
        if bsz == 1 and t_pad - t_p >= n_s:
            both = lax.dynamic_update_slice(hp[0], hs.reshape(n_s, d), (t_p, 0))
            both = peer_residual(both, *pw, name="peer")
            hp = both[None]
            hs = both[t_p:t_p + n_s].reshape(dec_b, dec_t, d)
        else:
            hp = peer_residual(hp.reshape(bsz * t_pad, d), *pw, name="peer_prompt").reshape(bsz, t_pad, d)
            hs = peer_residual(hs.reshape(n_s, d), *pw, name="peer_sample").reshape(dec_b, dec_t, d)

    y_prompt = hp[:, N_META:t_p]
    y_sample = hs
    return (y_prompt, y_sample,
            jnp.stack(sbk_p), jnp.stack(sbv_p), jnp.stack(sbk_s), jnp.stack(sbv_s),
            jnp.stack(gs_p), jnp.stack(gs_s), jnp.stack(gc_p), jnp.stack(gc_s),
            jnp.stack(fk_p), jnp.stack(fv_p), jnp.stack(ff_p),
            jnp.stack(fk_s), jnp.stack(fv_s), jnp.stack(ff_s))
```
